```python
import math
import jax
import jax.numpy as jnp
from jax import lax
import numpy as np

D_MODEL = 2048
BATCH = 4
SEQ = 2048
DEPTH = 2
DEC_BATCH = 128
DEC_SEQ = 4
PAST_LEN = 16384
PAGE_SIZE = 128

RW_HEAD = 64
RW_WIDTH = D_MODEL // 2
RW_HEADS = RW_WIDTH // RW_HEAD
RW_DECAY_RANK = 64
RW_ICLR_RANK = 64
RW_COLS = 3 * RW_WIDTH + RW_DECAY_RANK + RW_ICLR_RANK
RW_SPLITS = (RW_WIDTH, 2 * RW_WIDTH, 3 * RW_WIDTH, 3 * RW_WIDTH + RW_DECAY_RANK)
GN_EPS = 64e-5
ML_HEADS = 4
ML_V_WIDTH = D_MODEL // 2
ML_QK_WIDTH = ML_V_WIDTH // 2
ML_DV = ML_V_WIDTH // ML_HEADS
ML_DQK = ML_QK_WIDTH // ML_HEADS
ML_CHUNK = 64
ML_COLS = 2 * ML_QK_WIDTH + 2 * ML_V_WIDTH + 2 * ML_HEADS
ML_SPLITS = (ML_QK_WIDTH, 2 * ML_QK_WIDTH, 2 * ML_QK_WIDTH + ML_V_WIDTH,
             2 * ML_QK_WIDTH + 2 * ML_V_WIDTH, 2 * ML_QK_WIDTH + 2 * ML_V_WIDTH + ML_HEADS)
POOL_WIDTH = D_MODEL // 2
POOL_WINDOWS = (2, 4, 8, 16)
POOL_GROUPS = len(POOL_WINDOWS)
POOL_GW = POOL_WIDTH // POOL_GROUPS
POOL_BUF = max(POOL_WINDOWS) - 1
N_BRANCH = 3
GATE_COLS = N_BRANCH * D_MODEL
IN_COLS = RW_COLS + ML_COLS + POOL_WIDTH + GATE_COLS
IN_SPLITS = (RW_COLS, RW_COLS + ML_COLS, RW_COLS + ML_COLS + POOL_WIDTH)
MEM_LEN = 256
XA_HEADS = 4
XA_HEAD_DIM = 128
XA_WIDTH = XA_HEADS * XA_HEAD_DIM
MOE_GROUPS = 4
MOE_PER_GROUP = 8
MOE_EXPERTS = MOE_GROUPS * MOE_PER_GROUP
MOE_TOPK = 2
MOE_HIDDEN = D_MODEL // 8
RMS_EPS = 1e-6

kernel_name = "hybrid_rwkv7_mlstm_pool_hmoe_step"


def rmsnorm(x, g):
    xf = x.astype(jnp.float32)
    y = xf * lax.rsqrt(jnp.mean(xf * xf, axis=-1, keepdims=True) + RMS_EPS)
    return (y * g.astype(jnp.float32)).astype(x.dtype)


def rwkv7_mix(u, shift_prev, s_prev, mu, w0, w2, a0, a2, k_k, k_a, r_k, gn_w, gn_b):
    B, L, _ = u.shape
    f32 = jnp.float32
    u_prev = jnp.concatenate([shift_prev[:, None].astype(u.dtype), u[:, :-1]], axis=1)
    us = u + (u_prev - u) * mu
    r, k, v, wd, ad = jnp.split(us, RW_SPLITS, axis=-1)
    log_w = -jax.nn.softplus(-(w0 + jnp.tanh(wd) @ w2).astype(f32)) - 0.5
    decay = jnp.exp(-jnp.exp(log_w))
    a = jax.nn.sigmoid((a0 + ad @ a2).astype(f32))
    hd = lambda t: t.astype(f32).reshape(B, L, RW_HEADS, RW_HEAD)
    r, k, v, decay, a = hd(r), hd(k), hd(v), hd(decay), hd(a)
    kk = k * k_k.astype(f32).reshape(RW_HEADS, RW_HEAD)
    kk = kk / jnp.maximum(jnp.sqrt(jnp.sum(kk * kk, axis=-1, keepdims=True)), 1e-12)
    k = k * (1.0 + (a - 1.0) * k_a.astype(f32).reshape(RW_HEADS, RW_HEAD))
    b = kk * a

    def step(S, inp):
        r_t, w_t, k_t, v_t, kk_t, b_t = inp
        sa = jnp.einsum('bhvk,bhk->bhv', S, kk_t)
        S = S * w_t[:, :, None, :] - sa[..., None] * b_t[:, :, None, :] + v_t[..., None] * k_t[:, :, None, :]
        return S, jnp.einsum('bhvk,bhk->bhv', S, r_t)

    tm = lambda t: jnp.swapaxes(t, 0, 1)
    s_fin, y = lax.scan(step, s_prev.astype(f32), (tm(r), tm(decay), tm(k), tm(v), tm(kk), tm(b)))
    y = tm(y)
    mean = jnp.mean(y, axis=-1, keepdims=True)
    var = jnp.mean(jnp.square(y - mean), axis=-1, keepdims=True)
    y = (y - mean) * lax.rsqrt(var + GN_EPS) * gn_w.astype(f32).reshape(RW_HEADS, RW_HEAD) \
        + gn_b.astype(f32).reshape(RW_HEADS, RW_HEAD)
    y = y + jnp.sum(r * k * r_k.astype(f32).reshape(RW_HEADS, RW_HEAD), axis=-1, keepdims=True) * v
    return y.reshape(B, L, RW_WIDTH).astype(u.dtype), u[:, -1].astype(shift_prev.dtype), s_fin.astype(s_prev.dtype)


def mlstm_mix(u, c_prev, n_prev, m_prev, b_i, b_f, gn_w):
    B, L, _ = u.shape
    f32 = jnp.float32
    q, k, v, o, ig, fg = jnp.split(u.astype(f32), ML_SPLITS, axis=-1)
    q = q.reshape(B, L, ML_HEADS, ML_DQK)
    k = k.reshape(B, L, ML_HEADS, ML_DQK) * (ML_DQK ** -0.5)
    v = v.reshape(B, L, ML_HEADS, ML_DV)
    ig = ig + b_i.astype(f32)
    logf = jax.nn.log_sigmoid(fg + b_f.astype(f32))
    c = math.gcd(L, ML_CHUNK)
    nc = L // c
    ch = lambda t: t.reshape(B, nc, c, ML_HEADS, t.shape[-1]).transpose(1, 0, 3, 2, 4)
    chg = lambda t: t.reshape(B, nc, c, ML_HEADS).transpose(1, 0, 3, 2)
    causal = jnp.tril(jnp.ones((c, c), dtype=bool))

    def step(carry, inp):
        C, n, m = carry
        q_c, k_c, v_c, i_c, lf_c = inp
        bcum = jnp.cumsum(lf_c, axis=-1)
        dlog = jnp.where(causal, bcum[..., :, None] - bcum[..., None, :] + i_c[..., None, :], -jnp.inf)
        inter = bcum + m[..., None]
        m_t = jnp.maximum(inter, jnp.max(dlog, axis=-1))
        wts = jnp.exp(dlog - m_t[..., None]) * jnp.einsum('bhtd,bhsd->bhts', q_c, k_c)
        w_prev = jnp.exp(inter - m_t)
        num = w_prev[..., None] * jnp.einsum('bhtd,bhde->bhte', q_c, C) + jnp.einsum('bhts,bhse->bhte', wts, v_c)
        den = w_prev * jnp.einsum('bhtd,bhd->bht', q_c, n) + jnp.sum(wts, axis=-1)
        h = num / jnp.maximum(jnp.abs(den), jnp.exp(-m_t))[..., None]
        b_last = bcum[..., -1]
        s_log = b_last[..., None] - bcum + i_c
        m_new = jnp.maximum(b_last + m, jnp.max(s_log, axis=-1))
        ws = jnp.exp(s_log - m_new[..., None])
        wp = jnp.exp(b_last + m - m_new)
        C = wp[..., None, None] * C + jnp.einsum('bhs,bhsd,bhse->bhde', ws, k_c, v_c)
        n = wp[..., None] * n + jnp.einsum('bhs,bhsd->bhd', ws, k_c)
        return (C, n, m_new), h

    (c_fin, n_fin, m_fin), h = lax.scan(
        step, (c_prev.astype(f32), n_prev.astype(f32), m_prev.astype(f32)),
        (ch(q), ch(k), ch(v), chg(ig), chg(logf)))
    h = h.transpose(1, 0, 3, 2, 4).reshape(B, L, ML_HEADS, ML_DV)
    h = h * lax.rsqrt(jnp.mean(h * h, axis=-1, keepdims=True) + RMS_EPS) * gn_w.astype(f32).reshape(ML_HEADS, ML_DV)
    out = h * jax.nn.sigmoid(o.reshape(B, L, ML_HEADS, ML_DV))
    return (out.reshape(B, L, ML_V_WIDTH).astype(u.dtype), c_fin.astype(c_prev.dtype),
            n_fin.astype(n_prev.dtype), m_fin.astype(m_prev.dtype))


def pool_mix(u, buf, start, w_grp, scale):
    B, L, W = u.shape
    f32 = jnp.float32
    uf = u.astype(f32)
    z = jnp.concatenate([buf.astype(f32), uf], axis=1)
    cs = jnp.concatenate([jnp.zeros((B, 1, W), f32), jnp.cumsum(z, axis=1)], axis=1)
    pos = start + jnp.arange(L)
    means = []
    for g, win in enumerate(POOL_WINDOWS):
        sl = slice(g * POOL_GW, (g + 1) * POOL_GW)
        hi = cs[:, POOL_BUF + 1:POOL_BUF + 1 + L, sl]
        lo = cs[:, POOL_BUF + 1 - win:POOL_BUF + 1 - win + L, sl]
        cnt = jnp.minimum(win, pos + 1).astype(f32)
        means.append((hi - lo) / cnt[None, :, None])
    d = (jnp.concatenate(means, axis=-1) - uf).reshape(B, L, POOL_GROUPS, POOL_GW)
    out = jnp.einsum('blgc,gcd->blgd', d, w_grp.astype(f32)).reshape(B, L, W) * scale.astype(f32)
    return out.astype(u.dtype), z[:, -POOL_BUF:].astype(buf.dtype)


def memory_kv(mem, g_mem, wk, wv):
    B, M, _ = mem.shape
    mn = rmsnorm(mem, g_mem)
    return ((mn @ wk).reshape(B, M, XA_HEADS, XA_HEAD_DIM), (mn @ wv).reshape(B, M, XA_HEADS, XA_HEAD_DIM))


def cross_attend(xn, mem_k, mem_v, wq, wo):
    B, L, _ = xn.shape
    q = (xn @ wq).reshape(B, L, XA_HEADS, XA_HEAD_DIM)
    s = jnp.einsum('blhd,bmhd->bhlm', q, mem_k).astype(jnp.float32) * (XA_HEAD_DIM ** -0.5)
    p = jax.nn.softmax(s, axis=-1).astype(mem_v.dtype)
    return jnp.einsum('bhlm,bmhd->blhd', p, mem_v).reshape(B, L, XA_WIDTH) @ wo


def hmoe(xn, w_r1, b_r1, w_r2, b_r2, w_g, w_u, w_d):
    B, L, D = xn.shape
    f32 = jnp.float32
    t = xn.reshape(B * L, D)
    lg = (t @ w_r1).astype(f32) + b_r1.astype(f32)
    grp = jnp.argmax(lg, axis=-1)
    p_grp = jnp.take_along_axis(jax.nn.softmax(lg, axis=-1), grp[:, None], axis=-1)
    le = ((t @ w_r2).astype(f32) + b_r2.astype(f32)).reshape(-1, MOE_GROUPS, MOE_PER_GROUP)
    le = jnp.take_along_axis(le, grp[:, None, None], axis=1)[:, 0]
    top_v, top_i = lax.top_k(le, MOE_TOPK)
    gate = jax.nn.softmax(top_v, axis=-1) * p_grp
    eid = grp[:, None] * MOE_PER_GROUP + top_i
    comb = jnp.sum(jax.nn.one_hot(eid, MOE_EXPERTS, dtype=f32) * gate[..., None], axis=1)
    h = jax.nn.silu(jnp.einsum('td,edf->tef', t, w_g)) * jnp.einsum('td,edf->tef', t, w_u)
    h = h * comb.astype(h.dtype)[..., None]
    return jnp.einsum('tef,efd->td', h, w_d).reshape(B, L, D)


def trunk_layer(x, mem_k, mem_v, rw_shift, rw_s, ml_c, ml_n, ml_m, pool_buf, start, lp):
    B, L, _ = x.shape
    xn = rmsnorm(x, lp['g_mix'])
    u = xn @ lp['w_in']
    u_rw, u_ml, u_pool, u_gate = jnp.split(u, IN_SPLITS, axis=-1)
    o_rw, rw_shift, rw_s = rwkv7_mix(u_rw, rw_shift, rw_s, lp['rw_mu'], lp['rw_w0'], lp['rw_w2'], lp['rw_a0'],
                                     lp['rw_a2'], lp['rw_k_k'], lp['rw_k_a'], lp['rw_r_k'], lp['rw_gn_w'], lp['rw_gn_b'])
    o_ml, ml_c, ml_n, ml_m = mlstm_mix(u_ml, ml_c, ml_n, ml_m, lp['ml_b_i'], lp['ml_b_f'], lp['ml_gn_w'])
    o_pool, pool_buf = pool_mix(u_pool, pool_buf, start, lp['pool_w'], lp['pool_scale'])
    g = jax.nn.sigmoid(u_gate.astype(jnp.float32)).astype(x.dtype).reshape(B, L, N_BRANCH, D_MODEL)
    merged = (g[:, :, 0] * (o_rw @ lp['w_up_rwkv']) + g[:, :, 1] * (o_ml @ lp['w_up_mlstm'])
              + g[:, :, 2] * (o_pool @ lp['w_up_pool']))
    x = x + merged @ lp['w_out']
    x = x + cross_attend(rmsnorm(x, lp['g_xa']), mem_k, mem_v, lp['xa_wq'], lp['xa_wo'])
    x = x + hmoe(rmsnorm(x, lp['g_moe']), lp['moe_wr1'], lp['moe_br1'], lp['moe_wr2'], lp['moe_br2'],
                 lp['moe_wg'], lp['moe_wu'], lp['moe_wd'])
    return x, rw_shift, rw_s, ml_c, ml_n, ml_m, pool_buf


def setup_inputs(seed: int = 0) -> dict:
    key = jax.random.key(seed)
    ks = iter(jax.random.split(key, 64))
    f32 = jnp.float32
    nrm = lambda shape, s: s * jax.random.normal(next(ks), shape, f32)
    gain = lambda shape: 1.0 + nrm(shape, 0.02)
    return {
        'x_prompt': nrm((BATCH, SEQ, D_MODEL), 1.0),
        'x_sample': nrm((DEC_BATCH, DEC_SEQ, D_MODEL), 1.0),
        'cache_mem_k': nrm((DEPTH, DEC_BATCH, MEM_LEN, XA_HEADS, XA_HEAD_DIM), 1.0),
        'cache_mem_v': nrm((DEPTH, DEC_BATCH, MEM_LEN, XA_HEADS, XA_HEAD_DIM), 1.0),
        'state_rwkv_s': nrm((DEPTH, DEC_BATCH, RW_HEADS, RW_HEAD, RW_HEAD), 0.1),
        'state_rwkv_shift': nrm((DEPTH, DEC_BATCH, RW_COLS), 1.0),
        'state_mlstm_c': nrm((DEPTH, DEC_BATCH, ML_HEADS, ML_DQK, ML_DV), 0.1),
        'state_mlstm_n': nrm((DEPTH, DEC_BATCH, ML_HEADS, ML_DQK), 0.1),
        'state_mlstm_m': nrm((DEPTH, DEC_BATCH, ML_HEADS), 1.0),
        'state_pool': nrm((DEPTH, DEC_BATCH, POOL_BUF, POOL_WIDTH), 1.0),
        'mem_prompt': nrm((BATCH, MEM_LEN, D_MODEL), 1.0),
        'g_mix': gain((DEPTH, D_MODEL)),
        'w_in': nrm((DEPTH, D_MODEL, IN_COLS), D_MODEL ** -0.5),
        'rw_mu': jax.random.uniform(next(ks), (DEPTH, RW_COLS), f32),
        'rw_w0': jax.random.uniform(next(ks), (DEPTH, RW_WIDTH), f32, minval=-6.0, maxval=-1.0),
        'rw_w2': nrm((DEPTH, RW_DECAY_RANK, RW_WIDTH), 0.1 * RW_DECAY_RANK ** -0.5),
        'rw_a0': nrm((DEPTH, RW_WIDTH), 0.1),
        'rw_a2': nrm((DEPTH, RW_ICLR_RANK, RW_WIDTH), 0.1 * RW_ICLR_RANK ** -0.5),
        'rw_k_k': 0.85 + nrm((DEPTH, RW_WIDTH), 0.02),
        'rw_k_a': gain((DEPTH, RW_WIDTH)),
        'rw_r_k': nrm((DEPTH, RW_WIDTH), 0.1),
        'rw_gn_w': gain((DEPTH, RW_WIDTH)),
        'rw_gn_b': nrm((DEPTH, RW_WIDTH), 0.02),
        'ml_b_i': nrm((DEPTH, ML_HEADS), 0.5),
        'ml_b_f': 3.0 + nrm((DEPTH, ML_HEADS), 0.5),
        'ml_gn_w': gain((DEPTH, ML_V_WIDTH)),
        'pool_w': nrm((DEPTH, POOL_GROUPS, POOL_GW, POOL_GW), POOL_GW ** -0.5),
        'pool_scale': 1.0 + nrm((DEPTH, POOL_WIDTH), 0.1),
        'w_up_rwkv': nrm((DEPTH, RW_WIDTH, D_MODEL), RW_WIDTH ** -0.5),
        'w_up_mlstm': nrm((DEPTH, ML_V_WIDTH, D_MODEL), ML_V_WIDTH ** -0.5),
        'w_up_pool': nrm((DEPTH, POOL_WIDTH, D_MODEL), POOL_WIDTH ** -0.5),
        'w_out': nrm((DEPTH, D_MODEL, D_MODEL), D_MODEL ** -0.5),
        'g_xa': gain((DEPTH, D_MODEL)),
        'g_mem': gain((DEPTH, D_MODEL)),
        'xa_wq': nrm((DEPTH, D_MODEL, XA_WIDTH), D_MODEL ** -0.5),
        'xa_wk': nrm((DEPTH, D_MODEL, XA_WIDTH), D_MODEL ** -0.5),
        'xa_wv': nrm((DEPTH, D_MODEL, XA_WIDTH), D_MODEL ** -0.5),
        'xa_wo': nrm((DEPTH, XA_WIDTH, D_MODEL), XA_WIDTH ** -0.5),
        'g_moe': gain((DEPTH, D_MODEL)),
        'moe_wr1': nrm((DEPTH, D_MODEL, MOE_GROUPS), D_MODEL ** -0.5),
        'moe_br1': nrm((DEPTH, MOE_GROUPS), 0.01),
        'moe_wr2': nrm((DEPTH, D_MODEL, MOE_EXPERTS), D_MODEL ** -0.5),
        'moe_br2': nrm((DEPTH, MOE_EXPERTS), 0.01),
        'moe_wg': nrm((DEPTH, MOE_EXPERTS, D_MODEL, MOE_HIDDEN), D_MODEL ** -0.5),
        'moe_wu': nrm((DEPTH, MOE_EXPERTS, D_MODEL, MOE_HIDDEN), D_MODEL ** -0.5),
        'moe_wd': nrm((DEPTH, MOE_EXPERTS, MOE_HIDDEN, D_MODEL), MOE_HIDDEN ** -0.5),
        'g_final': gain((D_MODEL,)),
    }


def reference(x_prompt, x_sample, cache_mem_k, cache_mem_v, state_rwkv_s, state_rwkv_shift, state_mlstm_c,
              state_mlstm_n, state_mlstm_m, state_pool, mem_prompt, g_mix, w_in, rw_mu, rw_w0, rw_w2, rw_a0, rw_a2,
              rw_k_k, rw_k_a, rw_r_k, rw_gn_w, rw_gn_b, ml_b_i, ml_b_f, ml_gn_w, pool_w, pool_scale, w_up_rwkv,
              w_up_mlstm, w_up_pool, w_out, g_xa, g_mem, xa_wq, xa_wk, xa_wv, xa_wo, g_moe, moe_wr1, moe_br1,
              moe_wr2, moe_br2, moe_wg, moe_wu, moe_wd, g_final):
    def layer_params(l):
        return dict(g_mix=g_mix[l], w_in=w_in[l], rw_mu=rw_mu[l], rw_w0=rw_w0[l], rw_w2=rw_w2[l], rw_a0=rw_a0[l],
                    rw_a2=rw_a2[l], rw_k_k=rw_k_k[l], rw_k_a=rw_k_a[l], rw_r_k=rw_r_k[l], rw_gn_w=rw_gn_w[l],
                    rw_gn_b=rw_gn_b[l], ml_b_i=ml_b_i[l], ml_b_f=ml_b_f[l], ml_gn_w=ml_gn_w[l], pool_w=pool_w[l],
                    pool_scale=pool_scale[l], w_up_rwkv=w_up_rwkv[l], w_up_mlstm=w_up_mlstm[l],
                    w_up_pool=w_up_pool[l], w_out=w_out[l], g_xa=g_xa[l], xa_wq=xa_wq[l], xa_wo=xa_wo[l],
                    g_moe=g_moe[l], moe_wr1=moe_wr1[l], moe_br1=moe_br1[l], moe_wr2=moe_wr2[l],
                    moe_br2=moe_br2[l], moe_wg=moe_wg[l], moe_wu=moe_wu[l], moe_wd=moe_wd[l])

    B = x_prompt.shape[0]
    dt = x_prompt.dtype
    yp = x_prompt
    p_rs, p_sh, p_c, p_n, p_m, p_pl, p_mk, p_mv = [], [], [], [], [], [], [], []
    for l in range(DEPTH):
        lp = layer_params(l)
        mk, mv = memory_kv(mem_prompt, g_mem[l], xa_wk[l], xa_wv[l])
        yp, sh, rs, c, n, m, pl = trunk_layer(
            yp, mk, mv, jnp.zeros((B, RW_COLS), dt), jnp.zeros((B, RW_HEADS, RW_HEAD, RW_HEAD), dt),
            jnp.zeros((B, ML_HEADS, ML_DQK, ML_DV), dt), jnp.zeros((B, ML_HEADS, ML_DQK), dt),
            jnp.zeros((B, ML_HEADS), dt), jnp.zeros((B, POOL_BUF, POOL_WIDTH), dt), 0, lp)
        p_rs.append(rs); p_sh.append(sh); p_c.append(c); p_n.append(n); p_m.append(m); p_pl.append(pl)
        p_mk.append(mk); p_mv.append(mv)
    y_prompt = rmsnorm(yp, g_final)

    ys = x_sample
    s_rs, s_sh, s_c, s_n, s_m, s_pl = [], [], [], [], [], []
    for l in range(DEPTH):
        lp = layer_params(l)
        ys, sh, rs, c, n, m, pl = trunk_layer(
            ys, cache_mem_k[l], cache_mem_v[l], state_rwkv_shift[l], state_rwkv_s[l], state_mlstm_c[l],
            state_mlstm_n[l], state_mlstm_m[l], state_pool[l], PAST_LEN, lp)
        s_rs.append(rs); s_sh.append(sh); s_c.append(c); s_n.append(n); s_m.append(m); s_pl.append(pl)
    y_sample = rmsnorm(ys, g_final)

    return (y_prompt, y_sample,
            jnp.stack(p_rs), jnp.stack(p_sh), jnp.stack(p_c), jnp.stack(p_n), jnp.stack(p_m), jnp.stack(p_pl),
            jnp.stack(p_mk), jnp.stack(p_mv),
            jnp.stack(s_rs), jnp.stack(s_sh), jnp.stack(s_c), jnp.stack(s_n), jnp.stack(s_m), jnp.stack(s_pl))
```

```python
import functools
import math

import jax
import jax.numpy as jnp
from jax import lax
from jax.experimental import pallas as pl
from jax.experimental.pallas import tpu as pltpu

F32 = jnp.float32
BF16 = jnp.bfloat16
HI = lax.Precision.HIGHEST

D_MODEL = 2048
DEPTH = 2
PAST_LEN = 16384
RW_HEAD = 64
RW_WIDTH = 1024
RW_HEADS = 16
RW_RANK = 64
RW_COLS = 3 * RW_WIDTH + 2 * RW_RANK
GN_EPS = 64e-5
ML_HEADS = 4
ML_DQK = 128
ML_DV = 256
ML_QK_WIDTH = 512
ML_V_WIDTH = 1024
ML_MAIN = 2 * ML_QK_WIDTH + 2 * ML_V_WIDTH
ML_COLS = ML_MAIN + 2 * ML_HEADS
POOL_WIDTH = 1024
POOL_WINDOWS = (2, 4, 8, 16)
POOL_GW = 256
POOL_BUF = 15
POOL_HIST = 16
POOL_TAIL = 24
GATE_COLS = 3 * D_MODEL
MEM_LEN = 256
XA_HEADS = 4
XA_HEAD_DIM = 128
XA_WIDTH = 512
MOE_GROUPS = 4
MOE_PER_GROUP = 8
MOE_EXPERTS = 32
MOE_HIDDEN = 256
RMS_EPS = 1e-6
LANES = 128
SUBLANES = 8
VMEM_LIMIT = 56 * 1024 * 1024


def _cparams(n_axes):
    return pltpu.CompilerParams(dimension_semantics=("arbitrary",) * n_axes, vmem_limit_bytes=VMEM_LIMIT)


def _dot(a, b, precision=None):
    return jnp.dot(a, b, preferred_element_type=F32, precision=precision)


def _dot_nt(a, b, precision=None):
    return lax.dot_general(a, b, (((1,), (1,)), ((), ())), preferred_element_type=F32, precision=precision)


def _dot_tn(a, b, precision=None):
    return lax.dot_general(a, b, (((0,), (0,)), ((), ())), preferred_element_type=F32, precision=precision)


def _rms(x, g):
    return x * lax.rsqrt(jnp.mean(x * x, axis=-1, keepdims=True) + RMS_EPS) * g


def _log_sigmoid(x):
    return jnp.minimum(x, 0.0) - jnp.log1p(jnp.exp(-jnp.abs(x)))


def _softplus(x):
    return jnp.maximum(x, 0.0) + jnp.log1p(jnp.exp(-jnp.abs(x)))


def _lower_tri(n, strict=False):
    r = lax.broadcasted_iota(jnp.int32, (n, n), 0)
    c = lax.broadcasted_iota(jnp.int32, (n, n), 1)
    return (c < r) if strict else (c <= r)


def _norm_matmul_kernel(x_ref, g_ref, w_ref, o_ref, xn_ref, *, act):
    @pl.when(pl.program_id(1) == 0)
    def _():
        xn_ref[...] = _rms(x_ref[...], g_ref[...]).astype(BF16)

    acc = _dot(xn_ref[...], w_ref[...])
    if act == "sigmoid":
        acc = jax.nn.sigmoid(acc)
    o_ref[...] = acc.astype(o_ref.dtype)


def norm_matmul(x, g, w, *, tn, out_dtype=F32, act=None):
    T, D = x.shape
    N = w.shape[1]
    tm = min(T, 1024)
    return pl.pallas_call(
        functools.partial(_norm_matmul_kernel, act=act),
        grid=(T // tm, N // tn),
        in_specs=[pl.BlockSpec((tm, D), lambda i, j: (i, 0)),
                  pl.BlockSpec((1, D), lambda i, j: (0, 0)),
                  pl.BlockSpec((D, tn), lambda i, j: (0, j))],
        out_specs=pl.BlockSpec((tm, tn), lambda i, j: (i, j)),
        out_shape=jax.ShapeDtypeStruct((T, N), out_dtype),
        scratch_shapes=[pltpu.VMEM((tm, D), BF16)],
        compiler_params=_cparams(2),
    )(x, g.reshape(1, D), w)


def _matmul_residual_kernel(a_ref, w_ref, r_ref, o_ref):
    o_ref[...] = r_ref[...] + _dot(a_ref[...], w_ref[...])


def matmul_residual(a, w, res, *, tn=1024):
    T, K = a.shape
    N = w.shape[1]
    tm = min(T, 1024)
    return pl.pallas_call(
        _matmul_residual_kernel,
        grid=(T // tm, N // tn),
        in_specs=[pl.BlockSpec((tm, K), lambda i, j: (i, 0)),
                  pl.BlockSpec((K, tn), lambda i, j: (0, j)),
                  pl.BlockSpec((tm, tn), lambda i, j: (i, j))],
        out_specs=pl.BlockSpec((tm, tn), lambda i, j: (i, j)),
        out_shape=jax.ShapeDtypeStruct((T, N), F32),
        compiler_params=_cparams(2),
    )(a, w, res)


def _merge_kernel(orw_ref, oml_ref, opl_ref, g0_ref, g1_ref, g2_ref, wr_ref, wm_ref, wp_ref, o_ref):
    m = (g0_ref[...].astype(F32) * _dot(orw_ref[...], wr_ref[...])
         + g1_ref[...].astype(F32) * _dot(oml_ref[...], wm_ref[...])
         + g2_ref[...].astype(F32) * _dot(opl_ref[...], wp_ref[...]))
    o_ref[...] = m.astype(BF16)


def merge_branches(o_rw, o_ml, o_pl, gates, w_rw, w_ml, w_pl, *, tn=1024):
    T, K = o_rw.shape
    tm = min(T, 1024)
    nj = D_MODEL // tn
    act = pl.BlockSpec((tm, K), lambda i, j: (i, 0))
    wsp = pl.BlockSpec((K, tn), lambda i, j: (0, j))
    gate = lambda b: pl.BlockSpec((tm, tn), lambda i, j, b=b: (i, b * nj + j))
    return pl.pallas_call(
        _merge_kernel,
        grid=(T // tm, nj),
        in_specs=[act, act, act, gate(0), gate(1), gate(2), wsp, wsp, wsp],
        out_specs=pl.BlockSpec((tm, tn), lambda i, j: (i, j)),
        out_shape=jax.ShapeDtypeStruct((T, D_MODEL), BF16),
        compiler_params=_cparams(2),
    )(o_rw, o_ml, o_pl, gates, gates, gates, w_rw, w_ml, w_pl)


def _pool_kernel(hist_ref, u_ref, w_ref, sc_ref, o_ref, tail_ref, e_ref, *, tl, start):
    li = pl.program_id(1)

    @pl.when(li == 0)
    def _():
        e_ref[0:POOL_HIST, :] = hist_ref[0]

    u = u_ref[0]
    e_ref[POOL_HIST:, :] = u
    e = e_ref[...]
    s2 = e + pltpu.roll(e, 1, 0)
    s4 = s2[:, POOL_GW:] + pltpu.roll(s2[:, POOL_GW:], 2, 0)
    s8 = s4[:, POOL_GW:] + pltpu.roll(s4[:, POOL_GW:], 4, 0)
    s16 = s8[:, POOL_GW:] + pltpu.roll(s8[:, POOL_GW:], 8, 0)
    sums = (s2[:, :POOL_GW], s4[:, :POOL_GW], s8[:, :POOL_GW], s16)
    pos = start + li * tl + lax.broadcasted_iota(jnp.int32, (tl, 1), 0)
    for g, win in enumerate(POOL_WINDOWS):
        cols = slice(g * POOL_GW, (g + 1) * POOL_GW)
        cnt = jnp.minimum(win, pos + 1).astype(F32)
        d = sums[g][POOL_HIST:, :] / cnt - u[:, cols]
        out = _dot(d.astype(BF16), w_ref[g]) * sc_ref[:, cols]
        o_ref[0, :, cols] = out.astype(o_ref.dtype)
    tail_ref[0] = e[tl + POOL_HIST - POOL_TAIL:, :]
    e_ref[0:POOL_HIST, :] = e[tl:, :]


def pool_mix(u_pl, buf, w_grp, scale, *, B, L, l_valid, start):
    tl = min(L, 256)
    hist = jnp.pad(buf, ((0, 0), (POOL_HIST - POOL_BUF, 0), (0, 0)))
    u3 = u_pl.reshape(B, L, u_pl.shape[-1])
    out, tail = pl.pallas_call(
        functools.partial(_pool_kernel, tl=tl, start=start),
        grid=(B, L // tl),
        in_specs=[pl.BlockSpec((1, POOL_HIST, POOL_WIDTH), lambda b, l: (b, 0, 0)),
                  pl.BlockSpec((1, tl, POOL_WIDTH), lambda b, l: (b, l, 0)),
                  pl.BlockSpec((4, POOL_GW, POOL_GW), lambda b, l: (0, 0, 0)),
                  pl.BlockSpec((1, POOL_WIDTH), lambda b, l: (0, 0))],
        out_specs=[pl.BlockSpec((1, tl, POOL_WIDTH), lambda b, l: (b, l, 0)),
                   pl.BlockSpec((1, POOL_TAIL, POOL_WIDTH), lambda b, l: (b, 0, 0))],
        out_shape=[jax.ShapeDtypeStruct((B, L, POOL_WIDTH), BF16),
                   jax.ShapeDtypeStruct((B, POOL_TAIL, POOL_WIDTH), F32)],
        scratch_shapes=[pltpu.VMEM((tl + POOL_HIST, POOL_WIDTH), F32)],
        compiler_params=_cparams(2),
    )(hist, u3, w_grp, scale.reshape(1, POOL_WIDTH))
    pad = L - l_valid
    new_buf = tail[:, POOL_TAIL - pad - POOL_BUF:POOL_TAIL - pad, :]
    return out.reshape(B * L, POOL_WIDTH), new_buf


def _mlstm_kernel(u_ref, gif_ref, c0_ref, n0_ref, m0_ref, bi_ref, bf_ref, gnw_ref,
                  o_ref, cN_ref, nN_ref, mN_ref, c_s, n_s, m_s, *, c, l_valid):
    ci = pl.program_id(1)

    @pl.when(ci == 0)
    def _():
        c_s[...] = c0_ref[0]
        n_s[...] = n0_ref[0]
        m_s[...] = m0_ref[0]

    u = u_ref[0]
    gif = gif_ref[0]
    row = lax.broadcasted_iota(jnp.int32, (c, 1), 0)
    valid = row < l_valid
    ig_all = jnp.where(valid, gif + bi_ref[...], -jnp.inf)
    lf_all = jnp.where(valid, _log_sigmoid(gif + bf_ref[...]), 0.0)
    bcum_all = _dot(_lower_tri(c).astype(F32), lf_all, HI)
    ig_t = ig_all.T
    bcum_t = bcum_all.T
    causal = _lower_tri(c)
    for h in range(ML_HEADS):
        q = u[:, h * ML_DQK:(h + 1) * ML_DQK]
        k = u[:, ML_QK_WIDTH + h * ML_DQK:ML_QK_WIDTH + (h + 1) * ML_DQK] * (ML_DQK ** -0.5)
        v = u[:, 2 * ML_QK_WIDTH + h * ML_DV:2 * ML_QK_WIDTH + (h + 1) * ML_DV]
        og = u[:, 2 * ML_QK_WIDTH + ML_V_WIDTH + h * ML_DV:2 * ML_QK_WIDTH + ML_V_WIDTH + (h + 1) * ML_DV]
        i_c = ig_all[:, h:h + 1]
        b_c = bcum_all[:, ML_HEADS + h:ML_HEADS + h + 1]
        i_r = ig_t[h:h + 1, :]
        b_r = bcum_t[ML_HEADS + h:ML_HEADS + h + 1, :]
        c_prev = c_s[h]
        n_prev = n_s[h]
        m_prev = m_s[h]
        qb = q.astype(BF16)
        vb = v.astype(BF16)
        dlog = jnp.where(causal, b_c - b_r + i_r, -jnp.inf)
        inter = b_c + m_prev
        m_t = jnp.maximum(inter, jnp.max(dlog, axis=-1, keepdims=True))
        wts = jnp.exp(dlog - m_t) * _dot_nt(qb, k.astype(BF16))
        w_prev = jnp.exp(inter - m_t)
        num = w_prev * _dot(qb, c_prev.astype(BF16)) + _dot(wts.astype(BF16), vb)
        den = w_prev * jnp.sum(q * n_prev, axis=-1, keepdims=True) + jnp.sum(wts, axis=-1, keepdims=True)
        hh = num / jnp.maximum(jnp.abs(den), jnp.exp(-m_t))
        b_last = b_c[c - 1:c, :]
        s_log = b_last - b_c + i_c
        m_new = jnp.maximum(b_last + m_prev, jnp.max(s_log, axis=0, keepdims=True))
        kw = k * jnp.exp(s_log - m_new)
        wp = jnp.exp(b_last + m_prev - m_new)
        c_s[h] = wp * c_prev + _dot_tn(kw.astype(BF16), vb)
        n_s[h] = wp * n_prev + jnp.sum(kw, axis=0, keepdims=True)
        m_s[h] = m_new
        hn = hh * lax.rsqrt(jnp.mean(hh * hh, axis=-1, keepdims=True) + RMS_EPS) * gnw_ref[:, h * ML_DV:(h + 1) * ML_DV]
        o_ref[0, :, h * ML_DV:(h + 1) * ML_DV] = (hn * jax.nn.sigmoid(og)).astype(o_ref.dtype)

    @pl.when(ci == pl.num_programs(1) - 1)
    def _():
        cN_ref[0] = c_s[...]
        nN_ref[0] = n_s[...]
        mN_ref[0] = m_s[...]


def mlstm_mix(u_ml, u_pl, c0, n0, m0, b_i, b_f, gn_w, *, B, L, l_valid):
    c = min(L, 64)
    u3 = u_ml.reshape(B, L, ML_MAIN)
    g3 = u_pl.reshape(B, L, u_pl.shape[-1])
    gate_blk = POOL_WIDTH // LANES
    zeros = jnp.zeros((LANES - 2 * ML_HEADS,), F32)
    bi = jnp.concatenate([b_i, jnp.zeros((ML_HEADS,), F32), zeros]).reshape(1, LANES)
    bf = jnp.concatenate([jnp.zeros((ML_HEADS,), F32), b_f, zeros]).reshape(1, LANES)
    out, cN, nN, mN = pl.pallas_call(
        functools.partial(_mlstm_kernel, c=c, l_valid=l_valid),
        grid=(B, L // c),
        in_specs=[pl.BlockSpec((1, c, ML_MAIN), lambda b, i: (b, i, 0)),
                  pl.BlockSpec((1, c, LANES), lambda b, i: (b, i, gate_blk)),
                  pl.BlockSpec((1, ML_HEADS, ML_DQK, ML_DV), lambda b, i: (b, 0, 0, 0)),
                  pl.BlockSpec((1, ML_HEADS, 1, ML_DQK), lambda b, i: (b, 0, 0, 0)),
                  pl.BlockSpec((1, ML_HEADS, 1, 1), lambda b, i: (b, 0, 0, 0)),
                  pl.BlockSpec((1, LANES), lambda b, i: (0, 0)),
                  pl.BlockSpec((1, LANES), lambda b, i: (0, 0)),
                  pl.BlockSpec((1, ML_V_WIDTH), lambda b, i: (0, 0))],
        out_specs=[pl.BlockSpec((1, c, ML_V_WIDTH), lambda b, i: (b, i, 0)),
                   pl.BlockSpec((1, ML_HEADS, ML_DQK, ML_DV), lambda b, i: (b, 0, 0, 0)),
                   pl.BlockSpec((1, ML_HEADS, 1, ML_DQK), lambda b, i: (b, 0, 0, 0)),
                   pl.BlockSpec((1, ML_HEADS, 1, 1), lambda b, i: (b, 0, 0, 0))],
        out_shape=[jax.ShapeDtypeStruct((B, L, ML_V_WIDTH), BF16),
                   jax.ShapeDtypeStruct((B, ML_HEADS, ML_DQK, ML_DV), F32),
                   jax.ShapeDtypeStruct((B, ML_HEADS, 1, ML_DQK), F32),
                   jax.ShapeDtypeStruct((B, ML_HEADS, 1, 1), F32)],
        scratch_shapes=[pltpu.VMEM((ML_HEADS, ML_DQK, ML_DV), F32),
                        pltpu.VMEM((ML_HEADS, 1, ML_DQK), F32),
                        pltpu.VMEM((ML_HEADS, 1, 1), F32)],
        compiler_params=_cparams(2),
    )(u3, g3, c0, n0.reshape(B, ML_HEADS, 1, ML_DQK), m0.reshape(B, ML_HEADS, 1, 1), bi, bf,
      gn_w.reshape(1, ML_V_WIDTH))
    return (out.reshape(B * L, ML_V_WIDTH), cN, nN.reshape(B, ML_HEADS, ML_DQK), mN.reshape(B, ML_HEADS))


def _rwkv_kernel(u_ref, sh_ref, s0_ref, mu_ref, w0_ref, w2_ref, a0_ref, a2_ref, kk_ref, ka_ref, rk_ref,
                 gnw_ref, gnb_ref, o_ref, sN_ref, prev_s, s_s, *, C, l_valid):
    ci = pl.program_id(1)

    @pl.when(ci == 0)
    def _():
        prev_s[...] = sh_ref[0]
        s_s[...] = s0_ref[0]

    u = u_ref[0]
    row = lax.broadcasted_iota(jnp.int32, (C, 1), 0)
    valid = row < l_valid
    u_prev = jnp.where(row == 0, prev_s[...], pltpu.roll(u, 1, 0))
    prev_s[...] = u[C - 1:C, :]
    us = u + (u_prev - u) * mu_ref[...]
    W = RW_WIDTH
    r = us[:, 0:W]
    k = us[:, W:2 * W]
    v = jnp.where(valid, us[:, 2 * W:3 * W], 0.0)
    wd = us[:, 3 * W:3 * W + RW_RANK]
    ad = us[:, 3 * W + RW_RANK:3 * W + 2 * RW_RANK]
    xw = w0_ref[...] + _dot(jnp.tanh(wd), w2_ref[...], HI)
    log_w = -_softplus(-xw) - 0.5
    ld = jnp.where(valid, -jnp.exp(log_w), 0.0)
    a = jax.nn.sigmoid(a0_ref[...] + _dot(ad, a2_ref[...], HI))
    cum = _dot(_lower_tri(C).astype(F32), ld, HI)
    g = jnp.exp(cum)
    gi = jnp.exp(-cum)
    gp = jnp.exp(cum - ld)
    g_end = g[C - 1:C, :]
    kkv = k * kk_ref[...]
    kmod = jnp.where(valid, k * (1.0 + (a - 1.0) * ka_ref[...]), 0.0)
    strict = _lower_tri(C, strict=True)
    incl = _lower_tri(C)
    n_sq = max(int(math.log2(C)), 1)
    for h in range(RW_HEADS):
        sl = slice(h * RW_HEAD, (h + 1) * RW_HEAD)
        kk_h = kkv[:, sl]
        nrm = jnp.sqrt(jnp.sum(kk_h * kk_h, axis=-1, keepdims=True))
        kap = jnp.where(valid, kk_h / jnp.maximum(nrm, 1e-12), 0.0)
        b_h = kap * a[:, sl]
        k_h = kmod[:, sl]
        v_h = v[:, sl]
        r_h = r[:, sl]
        a_t = -kap * gp[:, sl]
        r_t = r_h * g[:, sl]
        b_t = b_h * gi[:, sl]
        k_t = k_h * gi[:, sl]
        s0 = s_s[h]
        ar = jnp.concatenate([a_t, r_t], axis=0)
        bk = jnp.concatenate([b_t, k_t], axis=0)
        m = _dot_nt(ar, bk, HI)
        a_ab = jnp.where(strict, m[:C, :C], 0.0)
        a_ak = jnp.where(strict, m[:C, C:], 0.0)
        a_rb = jnp.where(incl, m[C:, :C], 0.0)
        a_rk = jnp.where(incl, m[C:, C:], 0.0)
        ars = _dot_nt(ar, s0, HI)
        x = ars[:C] + _dot(a_ak, v_h, HI)
        p = a_ab
        for j in range(n_sq):
            x = x + _dot(p, x, HI)
            if j + 1 < n_sq:
                p = _dot(p, p, HI)
        y = ars[C:] + _dot(a_rb, x, HI) + _dot(a_rk, v_h, HI)
        ge = g_end[:, sl]
        s_s[h] = s0 * ge + _dot_tn(x, b_t * ge, HI) + _dot_tn(v_h, k_t * ge, HI)
        mean = jnp.mean(y, axis=-1, keepdims=True)
        yc = y - mean
        var = jnp.mean(yc * yc, axis=-1, keepdims=True)
        yn = yc * lax.rsqrt(var + GN_EPS) * gnw_ref[:, sl] + gnb_ref[:, sl]
        bonus = jnp.sum(r_h * k_h * rk_ref[:, sl], axis=-1, keepdims=True) * v_h
        o_ref[0, :, sl] = (yn + bonus).astype(o_ref.dtype)

    @pl.when(ci == pl.num_programs(1) - 1)
    def _():
        sN_ref[0] = s_s[...]


def rwkv7_mix(u_rw, shift_prev, s_prev, mu, w0, w2, a0, a2, k_k, k_a, r_k, gn_w, gn_b, *, B, L, l_valid):
    C = min(L, 64)
    u3 = u_rw.reshape(B, L, RW_COLS)
    vec = lambda n: pl.BlockSpec((1, n), lambda b, i: (0, 0))
    row = lambda t: t.reshape(1, -1)
    out, sN = pl.pallas_call(
        functools.partial(_rwkv_kernel, C=C, l_valid=l_valid),
        grid=(B, L // C),
        in_specs=[pl.BlockSpec((1, C, RW_COLS), lambda b, i: (b, i, 0)),
                  pl.BlockSpec((1, 1, RW_COLS), lambda b, i: (b, 0, 0)),
                  pl.BlockSpec((1, RW_HEADS, RW_HEAD, RW_HEAD), lambda b, i: (b, 0, 0, 0)),
                  vec(RW_COLS), vec(RW_WIDTH),
                  pl.BlockSpec((RW_RANK, RW_WIDTH), lambda b, i: (0, 0)),
                  vec(RW_WIDTH),
                  pl.BlockSpec((RW_RANK, RW_WIDTH), lambda b, i: (0, 0)),
                  vec(RW_WIDTH), vec(RW_WIDTH), vec(RW_WIDTH), vec(RW_WIDTH), vec(RW_WIDTH)],
        out_specs=[pl.BlockSpec((1, C, RW_WIDTH), lambda b, i: (b, i, 0)),
                   pl.BlockSpec((1, RW_HEADS, RW_HEAD, RW_HEAD), lambda b, i: (b, 0, 0, 0))],
        out_shape=[jax.ShapeDtypeStruct((B, L, RW_WIDTH), BF16),
                   jax.ShapeDtypeStruct((B, RW_HEADS, RW_HEAD, RW_HEAD), F32)],
        scratch_shapes=[pltpu.VMEM((1, RW_COLS), F32),
                        pltpu.VMEM((RW_HEADS, RW_HEAD, RW_HEAD), F32)],
        compiler_params=_cparams(2),
    )(u3, shift_prev.reshape(B, 1, RW_COLS), s_prev, row(mu), row(w0), w2, row(a0), a2, row(k_k), row(k_a),
      row(r_k), row(gn_w), row(gn_b))
    return out.reshape(B * L, RW_WIDTH), sN


def _attn_kernel(q_ref, k_ref, v_ref, o_ref, *, bt):
    for b in range(bt):
        for h in range(XA_HEADS):
            sl = slice(h * XA_HEAD_DIM, (h + 1) * XA_HEAD_DIM)
            s = _dot_nt(q_ref[b, :, sl], k_ref[b, :, sl].astype(BF16)) * (XA_HEAD_DIM ** -0.5)
            e = jnp.exp(s - jnp.max(s, axis=-1, keepdims=True))
            p = e / jnp.sum(e, axis=-1, keepdims=True)
            o_ref[b, :, sl] = _dot(p.astype(BF16), v_ref[b, :, sl].astype(BF16)).astype(o_ref.dtype)


def cross_attention(q, mem_k, mem_v, *, B, L):
    tq = min(L, 512)
    bt = 1 if L >= 64 else 8
    q3 = q.reshape(B, L, XA_WIDTH)
    kv = pl.BlockSpec((bt, MEM_LEN, XA_WIDTH), lambda b, i: (b, 0, 0))
    out = pl.pallas_call(
        functools.partial(_attn_kernel, bt=bt),
        grid=(B // bt, L // tq),
        in_specs=[pl.BlockSpec((bt, tq, XA_WIDTH), lambda b, i: (b, i, 0)), kv, kv],
        out_specs=pl.BlockSpec((bt, tq, XA_WIDTH), lambda b, i: (b, i, 0)),
        out_shape=jax.ShapeDtypeStruct((B, L, XA_WIDTH), BF16),
        compiler_params=_cparams(2),
    )(q3, mem_k.reshape(B, MEM_LEN, XA_WIDTH), mem_v.reshape(B, MEM_LEN, XA_WIDTH))
    return out.reshape(B * L, XA_WIDTH)


ROUTER_GROUP_LANE = MOE_EXPERTS


def _router_kernel(x_ref, g_ref, w_ref, b_ref, xn_ref, comb_ref):
    xn = _rms(x_ref[...], g_ref[...])
    xn_ref[...] = xn.astype(BF16)
    z = _dot(xn, w_ref[...], HI) + b_ref[...]
    lane = lax.broadcasted_iota(jnp.int32, z.shape, 1).astype(F32)
    big = float(LANES)
    neg = -jnp.inf
    first = lambda mask: jnp.min(jnp.where(mask, lane, big), axis=-1, keepdims=True)
    is_g = jnp.logical_and(lane >= ROUTER_GROUP_LANE, lane < ROUTER_GROUP_LANE + MOE_GROUPS)
    zg = jnp.where(is_g, z, neg)
    mg = jnp.max(zg, axis=-1, keepdims=True)
    grp = first(zg == mg) - ROUTER_GROUP_LANE
    p_grp = 1.0 / jnp.sum(jnp.exp(zg - mg), axis=-1, keepdims=True)
    lo = grp * MOE_PER_GROUP
    ze = jnp.where(jnp.logical_and(lane >= lo, lane < lo + MOE_PER_GROUP), z, neg)
    t1 = jnp.max(ze, axis=-1, keepdims=True)
    i1 = first(ze == t1)
    ze2 = jnp.where(lane == i1, neg, ze)
    t2 = jnp.max(ze2, axis=-1, keepdims=True)
    i2 = first(ze2 == t2)
    e2 = jnp.exp(t2 - t1)
    g1 = p_grp / (1.0 + e2)
    comb_ref[...] = jnp.where(lane == i1, g1, 0.0) + jnp.where(lane == i2, g1 * e2, 0.0)


def moe_router(x, g, w_r1, b_r1, w_r2, b_r2):
    T, D = x.shape
    tm = min(T, 512)
    pad = LANES - MOE_EXPERTS - MOE_GROUPS
    w = jnp.concatenate([w_r2, w_r1, jnp.zeros((D, pad), F32)], axis=1)
    b = jnp.concatenate([b_r2, b_r1, jnp.zeros((pad,), F32)]).reshape(1, LANES)
    return pl.pallas_call(
        _router_kernel,
        grid=(T // tm,),
        in_specs=[pl.BlockSpec((tm, D), lambda i: (i, 0)),
                  pl.BlockSpec((1, D), lambda i: (0, 0)),
                  pl.BlockSpec((D, LANES), lambda i: (0, 0)),
                  pl.BlockSpec((1, LANES), lambda i: (0, 0))],
        out_specs=[pl.BlockSpec((tm, D), lambda i: (i, 0)),
                   pl.BlockSpec((tm, LANES), lambda i: (i, 0))],
        out_shape=[jax.ShapeDtypeStruct((T, D), BF16), jax.ShapeDtypeStruct((T, LANES), F32)],
        compiler_params=_cparams(1),
    )(x, g.reshape(1, D), w, b)


def _experts_kernel(xn_ref, comb_ref, x_ref, wg_ref, wu_ref, wd_ref, gf_ref, o_ref, *, final_norm):
    e = pl.program_id(1)

    @pl.when(e == 0)
    def _():
        o_ref[...] = x_ref[...]

    xn = xn_ref[...]
    comb = comb_ref[...]
    lane = lax.broadcasted_iota(jnp.int32, comb.shape, 1)
    ce = jnp.sum(jnp.where(lane == e, comb, 0.0), axis=-1, keepdims=True)
    hid = jax.nn.silu(_dot(xn, wg_ref[0])) * _dot(xn, wu_ref[0]) * ce
    o_ref[...] += _dot(hid.astype(BF16), wd_ref[0])

    if final_norm:
        @pl.when(e == pl.num_programs(1) - 1)
        def _():
            o_ref[...] = _rms(o_ref[...], gf_ref[...])


def moe_experts(xn, comb, x, wg, wu, wd, g_final, *, final_norm):
    T, D = x.shape
    tm = min(T, 512)
    row = pl.BlockSpec((tm, D), lambda i, e: (i, 0))
    return pl.pallas_call(
        functools.partial(_experts_kernel, final_norm=final_norm),
        grid=(T // tm, MOE_EXPERTS),
        in_specs=[row,
                  pl.BlockSpec((tm, LANES), lambda i, e: (i, 0)),
                  row,
                  pl.BlockSpec((1, D, MOE_HIDDEN), lambda i, e: (e, 0, 0)),
                  pl.BlockSpec((1, D, MOE_HIDDEN), lambda i, e: (e, 0, 0)),
                  pl.BlockSpec((1, MOE_HIDDEN, D), lambda i, e: (e, 0, 0)),
                  pl.BlockSpec((1, D), lambda i, e: (0, 0))],
        out_specs=row,
        out_shape=jax.ShapeDtypeStruct((T, D), F32),
        compiler_params=_cparams(2),
    )(xn, comb, x, wg, wu, wd, g_final.reshape(1, D))


def _layer_weights(l, p):
    w_in = p["w_in"][l]
    c0 = RW_COLS
    c1 = c0 + ML_MAIN
    c2 = c1 + 2 * ML_HEADS
    c3 = c2 + POOL_WIDTH
    w_pl = jnp.concatenate([w_in[:, c2:c3], w_in[:, c1:c2],
                            jnp.zeros((D_MODEL, LANES - 2 * ML_HEADS), F32)], axis=1)
    bf = lambda t: t.astype(BF16)
    return dict(
        w_rw=bf(w_in[:, :c0]), w_ml=bf(w_in[:, c0:c1]), w_pl=bf(w_pl), w_gate=bf(w_in[:, c3:]),
        w_up_rwkv=bf(p["w_up_rwkv"][l]), w_up_mlstm=bf(p["w_up_mlstm"][l]), w_up_pool=bf(p["w_up_pool"][l]),
        w_out=bf(p["w_out"][l]), pool_w=bf(p["pool_w"][l]),
        xa_wq=bf(p["xa_wq"][l]), xa_wo=bf(p["xa_wo"][l]),
        xa_wkv=bf(jnp.concatenate([p["xa_wk"][l], p["xa_wv"][l]], axis=1)),
        moe_wg=bf(p["moe_wg"][l]), moe_wu=bf(p["moe_wu"][l]), moe_wd=bf(p["moe_wd"][l]),
    )


def _trunk_layer(x, mem_k, mem_v, rw_shift, rw_s, ml_c, ml_n, ml_m, pool_buf, *, l, p, lw, B, L, l_valid, start,
                 final_norm):
    g_mix = p["g_mix"][l]
    u_rw = norm_matmul(x, g_mix, lw["w_rw"], tn=640)
    u_ml = norm_matmul(x, g_mix, lw["w_ml"], tn=1024)
    u_pl = norm_matmul(x, g_mix, lw["w_pl"], tn=POOL_WIDTH + LANES)
    gates = norm_matmul(x, g_mix, lw["w_gate"], tn=1024, out_dtype=BF16, act="sigmoid")

    o_rw, rw_s = rwkv7_mix(u_rw, rw_shift, rw_s, p["rw_mu"][l], p["rw_w0"][l], p["rw_w2"][l], p["rw_a0"][l],
                           p["rw_a2"][l], p["rw_k_k"][l], p["rw_k_a"][l], p["rw_r_k"][l], p["rw_gn_w"][l],
                           p["rw_gn_b"][l], B=B, L=L, l_valid=l_valid)
    rw_shift = u_rw.reshape(B, L, RW_COLS)[:, l_valid - 1]
    o_ml, ml_c, ml_n, ml_m = mlstm_mix(u_ml, u_pl, ml_c, ml_n, ml_m, p["ml_b_i"][l], p["ml_b_f"][l],
                                       p["ml_gn_w"][l], B=B, L=L, l_valid=l_valid)
    o_pl, pool_buf = pool_mix(u_pl, pool_buf, lw["pool_w"], p["pool_scale"][l], B=B, L=L, l_valid=l_valid,
                              start=start)
    merged = merge_branches(o_rw, o_ml, o_pl, gates, lw["w_up_rwkv"], lw["w_up_mlstm"], lw["w_up_pool"])
    x = matmul_residual(merged, lw["w_out"], x)

    q = norm_matmul(x, p["g_xa"][l], lw["xa_wq"], tn=XA_WIDTH, out_dtype=BF16)
    att = cross_attention(q, mem_k, mem_v, B=B, L=L)
    x = matmul_residual(att, lw["xa_wo"], x)

    xn, comb = moe_router(x, p["g_moe"][l], p["moe_wr1"][l], p["moe_br1"][l], p["moe_wr2"][l], p["moe_br2"][l])
    x = moe_experts(xn, comb, x, lw["moe_wg"], lw["moe_wu"], lw["moe_wd"], p["g_final"], final_norm=final_norm)
    return x, rw_shift, rw_s, ml_c, ml_n, ml_m, pool_buf


def kernel(x_prompt, x_sample, cache_mem_k, cache_mem_v, state_rwkv_s, state_rwkv_shift, state_mlstm_c, state_mlstm_n, state_mlstm_m, state_pool, mem_prompt, g_mix, w_in, rw_mu, rw_w0, rw_w2, rw_a0, rw_a2, rw_k_k, rw_k_a, rw_r_k, rw_gn_w, rw_gn_b, ml_b_i, ml_b_f, ml_gn_w, pool_w, pool_scale, w_up_rwkv, w_up_mlstm, w_up_pool, w_out, g_xa, g_mem, xa_wq, xa_wk, xa_wv, xa_wo, g_moe, moe_wr1, moe_br1, moe_wr2, moe_br2, moe_wg, moe_wu, moe_wd, g_final):
    p = dict(g_mix=g_mix, w_in=w_in, rw_mu=rw_mu, rw_w0=rw_w0, rw_w2=rw_w2, rw_a0=rw_a0, rw_a2=rw_a2, rw_k_k=rw_k_k,
             rw_k_a=rw_k_a, rw_r_k=rw_r_k, rw_gn_w=rw_gn_w, rw_gn_b=rw_gn_b, ml_b_i=ml_b_i, ml_b_f=ml_b_f,
             ml_gn_w=ml_gn_w, pool_w=pool_w, pool_scale=pool_scale, w_up_rwkv=w_up_rwkv, w_up_mlstm=w_up_mlstm,
             w_up_pool=w_up_pool, w_out=w_out, g_xa=g_xa, g_mem=g_mem, xa_wq=xa_wq, xa_wk=xa_wk, xa_wv=xa_wv,
             xa_wo=xa_wo, g_moe=g_moe, moe_wr1=moe_wr1, moe_br1=moe_br1, moe_wr2=moe_wr2, moe_br2=moe_br2,
             moe_wg=moe_wg, moe_wu=moe_wu, moe_wd=moe_wd, g_final=g_final)
    Bp, Lp, D = x_prompt.shape
    Bs, Ls, _ = x_sample.shape
    Ls_pad = -(-Ls // SUBLANES) * SUBLANES
    yp = x_prompt.reshape(Bp * Lp, D)
    ys = jnp.pad(x_sample, ((0, 0), (0, Ls_pad - Ls), (0, 0))).reshape(Bs * Ls_pad, D)
    zeros = lambda *s: jnp.zeros(s, F32)
    outs_p = [[] for _ in range(8)]
    outs_s = [[] for _ in range(6)]
    for l in range(DEPTH):
        lw = _layer_weights(l, p)
        final = l == DEPTH - 1
        kv = norm_matmul(mem_prompt.reshape(Bp * MEM_LEN, D), g_mem[l], lw["xa_wkv"], tn=1024)
        mk = kv[:, :XA_WIDTH].reshape(Bp, MEM_LEN, XA_HEADS, XA_HEAD_DIM)
        mv = kv[:, XA_WIDTH:].reshape(Bp, MEM_LEN, XA_HEADS, XA_HEAD_DIM)
        yp, sh, rs, c, n, m, pb = _trunk_layer(
            yp, mk, mv, zeros(Bp, RW_COLS), zeros(Bp, RW_HEADS, RW_HEAD, RW_HEAD),
            zeros(Bp, ML_HEADS, ML_DQK, ML_DV), zeros(Bp, ML_HEADS, ML_DQK), zeros(Bp, ML_HEADS),
            zeros(Bp, POOL_BUF, POOL_WIDTH), l=l, p=p, lw=lw, B=Bp, L=Lp, l_valid=Lp, start=0, final_norm=final)
        for acc, t in zip(outs_p, (rs, sh, c, n, m, pb, mk, mv)):
            acc.append(t)
        ys, sh, rs, c, n, m, pb = _trunk_layer(
            ys, cache_mem_k[l], cache_mem_v[l], state_rwkv_shift[l], state_rwkv_s[l], state_mlstm_c[l],
            state_mlstm_n[l], state_mlstm_m[l], state_pool[l], l=l, p=p, lw=lw, B=Bs, L=Ls_pad, l_valid=Ls,
            start=PAST_LEN, final_norm=final)
        for acc, t in zip(outs_s, (rs, sh, c, n, m, pb)):
            acc.append(t)
    y_prompt = yp.reshape(Bp, Lp, D)
    y_sample = ys.reshape(Bs, Ls_pad, D)[:, :Ls]
    return (y_prompt, y_sample, *[jnp.stack(t) for t in outs_p], *[jnp.stack(t) for t in outs_s])
```

```python
import functools
import math

import jax
import jax.numpy as jnp
from jax import lax
from jax.experimental import pallas as pl
from jax.experimental.pallas import tpu as pltpu

F32 = jnp.float32
BF16 = jnp.bfloat16
HI = lax.Precision.HIGHEST

D_MODEL = 2048
DEPTH = 2
PAST_LEN = 16384
RW_HEAD = 64
RW_WIDTH = 1024
RW_HEADS = 16
RW_RANK = 64
RW_COLS = 3 * RW_WIDTH + 2 * RW_RANK
GN_EPS = 64e-5
ML_HEADS = 4
ML_DQK = 128
ML_DV = 256
ML_QK_WIDTH = 512
ML_V_WIDTH = 1024
ML_MAIN = 2 * ML_QK_WIDTH + 2 * ML_V_WIDTH
ML_COLS = ML_MAIN + 2 * ML_HEADS
POOL_WIDTH = 1024
POOL_WINDOWS = (2, 4, 8, 16)
POOL_GW = 256
POOL_BUF = 15
POOL_HIST = 16
POOL_TAIL = 24
GATE_COLS = 3 * D_MODEL
MEM_LEN = 256
XA_HEADS = 4
XA_HEAD_DIM = 128
XA_WIDTH = 512
MOE_GROUPS = 4
MOE_PER_GROUP = 8
MOE_EXPERTS = 32
MOE_HIDDEN = 256
RMS_EPS = 1e-6
LANES = 128
SUBLANES = 8
VMEM_LIMIT = 56 * 1024 * 1024


def _cparams(n_axes):
    return pltpu.CompilerParams(dimension_semantics=("arbitrary",) * n_axes, vmem_limit_bytes=VMEM_LIMIT)


def _dot(a, b, precision=None):
    return jnp.dot(a, b, preferred_element_type=F32, precision=precision)


def _dot_nt(a, b, precision=None):
    return lax.dot_general(a, b, (((1,), (1,)), ((), ())), preferred_element_type=F32, precision=precision)


def _dot_tn(a, b, precision=None):
    return lax.dot_general(a, b, (((0,), (0,)), ((), ())), preferred_element_type=F32, precision=precision)


def _rms(x, g):
    return x * lax.rsqrt(jnp.mean(x * x, axis=-1, keepdims=True) + RMS_EPS) * g


def _log_sigmoid(x):
    return jnp.minimum(x, 0.0) - jnp.log1p(jnp.exp(-jnp.abs(x)))


def _softplus(x):
    return jnp.maximum(x, 0.0) + jnp.log1p(jnp.exp(-jnp.abs(x)))


def _lower_tri(n, strict=False):
    r = lax.broadcasted_iota(jnp.int32, (n, n), 0)
    c = lax.broadcasted_iota(jnp.int32, (n, n), 1)
    return (c < r) if strict else (c <= r)


def _norm_matmul_kernel(x_ref, g_ref, w_ref, o_ref, xn_ref, *, act):
    @pl.when(pl.program_id(1) == 0)
    def _():
        xn_ref[...] = _rms(x_ref[...], g_ref[...]).astype(BF16)

    acc = _dot(xn_ref[...], w_ref[...])
    if act == "sigmoid":
        acc = jax.nn.sigmoid(acc)
    o_ref[...] = acc.astype(o_ref.dtype)


def norm_matmul(x, g, w, *, tn, name, out_dtype=F32, act=None):
    T, D = x.shape
    N = w.shape[1]
    tm = min(T, 1024)
    return pl.pallas_call(
        functools.partial(_norm_matmul_kernel, act=act),
        grid=(T // tm, N // tn),
        in_specs=[pl.BlockSpec((tm, D), lambda i, j: (i, 0)),
                  pl.BlockSpec((1, D), lambda i, j: (0, 0)),
                  pl.BlockSpec((D, tn), lambda i, j: (0, j))],
        out_specs=pl.BlockSpec((tm, tn), lambda i, j: (i, j)),
        out_shape=jax.ShapeDtypeStruct((T, N), out_dtype),
        scratch_shapes=[pltpu.VMEM((tm, D), BF16)],
        compiler_params=_cparams(2),
        name=name,
    )(x, g.reshape(1, D), w)


def _matmul_residual_kernel(a_ref, w_ref, r_ref, o_ref):
    o_ref[...] = r_ref[...] + _dot(a_ref[...], w_ref[...])


def matmul_residual(a, w, res, *, tn=1024):
    T, K = a.shape
    N = w.shape[1]
    tm = min(T, 1024)
    return pl.pallas_call(
        _matmul_residual_kernel,
        grid=(T // tm, N // tn),
        in_specs=[pl.BlockSpec((tm, K), lambda i, j: (i, 0)),
                  pl.BlockSpec((K, tn), lambda i, j: (0, j)),
                  pl.BlockSpec((tm, tn), lambda i, j: (i, j))],
        out_specs=pl.BlockSpec((tm, tn), lambda i, j: (i, j)),
        out_shape=jax.ShapeDtypeStruct((T, N), F32),
        compiler_params=_cparams(2),
        name="matmul_residual",
    )(a, w, res)


def _merge_kernel(orw_ref, oml_ref, opl_ref, g0_ref, g1_ref, g2_ref, wr_ref, wm_ref, wp_ref, o_ref):
    m = (g0_ref[...].astype(F32) * _dot(orw_ref[...], wr_ref[...])
         + g1_ref[...].astype(F32) * _dot(oml_ref[...], wm_ref[...])
         + g2_ref[...].astype(F32) * _dot(opl_ref[...], wp_ref[...]))
    o_ref[...] = m.astype(BF16)


def merge_branches(o_rw, o_ml, o_pl, gates, w_rw, w_ml, w_pl, *, tn=1024):
    T, K = o_rw.shape
    tm = min(T, 1024)
    nj = D_MODEL // tn
    act = pl.BlockSpec((tm, K), lambda i, j: (i, 0))
    wsp = pl.BlockSpec((K, tn), lambda i, j: (0, j))
    gate = lambda b: pl.BlockSpec((tm, tn), lambda i, j, b=b: (i, b * nj + j))
    return pl.pallas_call(
        _merge_kernel,
        grid=(T // tm, nj),
        in_specs=[act, act, act, gate(0), gate(1), gate(2), wsp, wsp, wsp],
        out_specs=pl.BlockSpec((tm, tn), lambda i, j: (i, j)),
        out_shape=jax.ShapeDtypeStruct((T, D_MODEL), BF16),
        compiler_params=_cparams(2),
        name="merge_branches",
    )(o_rw, o_ml, o_pl, gates, gates, gates, w_rw, w_ml, w_pl)


def _pool_kernel(hist_ref, u_ref, w_ref, sc_ref, o_ref, tail_ref, e_ref, *, tl, start):
    li = pl.program_id(1)

    @pl.when(li == 0)
    def _():
        e_ref[0:POOL_HIST, :] = hist_ref[0]

    u = u_ref[0]
    e_ref[POOL_HIST:, :] = u
    e = e_ref[...]
    s2 = e + pltpu.roll(e, 1, 0)
    s4 = s2[:, POOL_GW:] + pltpu.roll(s2[:, POOL_GW:], 2, 0)
    s8 = s4[:, POOL_GW:] + pltpu.roll(s4[:, POOL_GW:], 4, 0)
    s16 = s8[:, POOL_GW:] + pltpu.roll(s8[:, POOL_GW:], 8, 0)
    sums = (s2[:, :POOL_GW], s4[:, :POOL_GW], s8[:, :POOL_GW], s16)
    pos = start + li * tl + lax.broadcasted_iota(jnp.int32, (tl, 1), 0)
    for g, win in enumerate(POOL_WINDOWS):
        cols = slice(g * POOL_GW, (g + 1) * POOL_GW)
        cnt = jnp.minimum(win, pos + 1).astype(F32)
        d = sums[g][POOL_HIST:, :] / cnt - u[:, cols]
        out = _dot(d.astype(BF16), w_ref[g]) * sc_ref[:, cols]
        o_ref[0, :, cols] = out.astype(o_ref.dtype)
    tail_ref[0] = e[tl + POOL_HIST - POOL_TAIL:, :]
    e_ref[0:POOL_HIST, :] = e[tl:, :]


def pool_mix(u_pl, buf, w_grp, scale, *, B, L, l_valid, start):
    tl = min(L, 256)
    hist = jnp.pad(buf, ((0, 0), (POOL_HIST - POOL_BUF, 0), (0, 0)))
    u3 = u_pl.reshape(B, L, u_pl.shape[-1])
    out, tail = pl.pallas_call(
        functools.partial(_pool_kernel, tl=tl, start=start),
        grid=(B, L // tl),
        in_specs=[pl.BlockSpec((1, POOL_HIST, POOL_WIDTH), lambda b, l: (b, 0, 0)),
                  pl.BlockSpec((1, tl, POOL_WIDTH), lambda b, l: (b, l, 0)),
                  pl.BlockSpec((4, POOL_GW, POOL_GW), lambda b, l: (0, 0, 0)),
                  pl.BlockSpec((1, POOL_WIDTH), lambda b, l: (0, 0))],
        out_specs=[pl.BlockSpec((1, tl, POOL_WIDTH), lambda b, l: (b, l, 0)),
                   pl.BlockSpec((1, POOL_TAIL, POOL_WIDTH), lambda b, l: (b, 0, 0))],
        out_shape=[jax.ShapeDtypeStruct((B, L, POOL_WIDTH), BF16),
                   jax.ShapeDtypeStruct((B, POOL_TAIL, POOL_WIDTH), F32)],
        scratch_shapes=[pltpu.VMEM((tl + POOL_HIST, POOL_WIDTH), F32)],
        compiler_params=_cparams(2),
        name="pool_mix",
    )(hist, u3, w_grp, scale.reshape(1, POOL_WIDTH))
    pad = L - l_valid
    new_buf = tail[:, POOL_TAIL - pad - POOL_BUF:POOL_TAIL - pad, :]
    return out.reshape(B * L, POOL_WIDTH), new_buf


def _mlstm_kernel(u_ref, gif_ref, c0_ref, n0_ref, m0_ref, bi_ref, bf_ref, gnw_ref,
                  o_ref, cN_ref, nN_ref, mN_ref, c_s, n_s, m_s, *, c, l_valid):
    ci = pl.program_id(1)

    @pl.when(ci == 0)
    def _():
        c_s[...] = c0_ref[0]
        n_s[...] = n0_ref[0]
        m_s[...] = m0_ref[0]

    u = u_ref[0]
    gif = gif_ref[0]
    row = lax.broadcasted_iota(jnp.int32, (c, 1), 0)
    valid = row < l_valid
    ig_all = jnp.where(valid, gif + bi_ref[...], -jnp.inf)
    lf_all = jnp.where(valid, _log_sigmoid(gif + bf_ref[...]), 0.0)
    bcum_all = _dot(_lower_tri(c).astype(F32), lf_all, HI)
    ig_t = ig_all.T
    bcum_t = bcum_all.T
    causal = _lower_tri(c)
    for h in range(ML_HEADS):
        q = u[:, h * ML_DQK:(h + 1) * ML_DQK]
        k = u[:, ML_QK_WIDTH + h * ML_DQK:ML_QK_WIDTH + (h + 1) * ML_DQK] * (ML_DQK ** -0.5)
        v = u[:, 2 * ML_QK_WIDTH + h * ML_DV:2 * ML_QK_WIDTH + (h + 1) * ML_DV]
        og = u[:, 2 * ML_QK_WIDTH + ML_V_WIDTH + h * ML_DV:2 * ML_QK_WIDTH + ML_V_WIDTH + (h + 1) * ML_DV]
        i_c = ig_all[:, h:h + 1]
        b_c = bcum_all[:, ML_HEADS + h:ML_HEADS + h + 1]
        i_r = ig_t[h:h + 1, :]
        b_r = bcum_t[ML_HEADS + h:ML_HEADS + h + 1, :]
        c_prev = c_s[h]
        n_prev = n_s[h]
        m_prev = m_s[h]
        qb = q.astype(BF16)
        vb = v.astype(BF16)
        dlog = jnp.where(causal, b_c - b_r + i_r, -jnp.inf)
        inter = b_c + m_prev
        m_t = jnp.maximum(inter, jnp.max(dlog, axis=-1, keepdims=True))
        wts = jnp.exp(dlog - m_t) * _dot_nt(qb, k.astype(BF16))
        w_prev = jnp.exp(inter - m_t)
        num = w_prev * _dot(qb, c_prev.astype(BF16)) + _dot(wts.astype(BF16), vb)
        den = w_prev * jnp.sum(q * n_prev, axis=-1, keepdims=True) + jnp.sum(wts, axis=-1, keepdims=True)
        hh = num / jnp.maximum(jnp.abs(den), jnp.exp(-m_t))
        b_last = b_c[c - 1:c, :]
        s_log = b_last - b_c + i_c
        m_new = jnp.maximum(b_last + m_prev, jnp.max(s_log, axis=0, keepdims=True))
        kw = k * jnp.exp(s_log - m_new)
        wp = jnp.exp(b_last + m_prev - m_new)
        c_s[h] = wp * c_prev + _dot_tn(kw.astype(BF16), vb)
        n_s[h] = wp * n_prev + jnp.sum(kw, axis=0, keepdims=True)
        m_s[h] = m_new
        hn = hh * lax.rsqrt(jnp.mean(hh * hh, axis=-1, keepdims=True) + RMS_EPS) * gnw_ref[:, h * ML_DV:(h + 1) * ML_DV]
        o_ref[0, :, h * ML_DV:(h + 1) * ML_DV] = (hn * jax.nn.sigmoid(og)).astype(o_ref.dtype)

    @pl.when(ci == pl.num_programs(1) - 1)
    def _():
        cN_ref[0] = c_s[...]
        nN_ref[0] = n_s[...]
        mN_ref[0] = m_s[...]


def mlstm_mix(u_ml, u_pl, c0, n0, m0, b_i, b_f, gn_w, *, B, L, l_valid):
    c = min(L, 64)
    u3 = u_ml.reshape(B, L, ML_MAIN)
    g3 = u_pl.reshape(B, L, u_pl.shape[-1])
    gate_blk = POOL_WIDTH // LANES
    zeros = jnp.zeros((LANES - 2 * ML_HEADS,), F32)
    bi = jnp.concatenate([b_i, jnp.zeros((ML_HEADS,), F32), zeros]).reshape(1, LANES)
    bf = jnp.concatenate([jnp.zeros((ML_HEADS,), F32), b_f, zeros]).reshape(1, LANES)
    out, cN, nN, mN = pl.pallas_call(
        functools.partial(_mlstm_kernel, c=c, l_valid=l_valid),
        grid=(B, L // c),
        in_specs=[pl.BlockSpec((1, c, ML_MAIN), lambda b, i: (b, i, 0)),
                  pl.BlockSpec((1, c, LANES), lambda b, i: (b, i, gate_blk)),
                  pl.BlockSpec((1, ML_HEADS, ML_DQK, ML_DV), lambda b, i: (b, 0, 0, 0)),
                  pl.BlockSpec((1, ML_HEADS, 1, ML_DQK), lambda b, i: (b, 0, 0, 0)),
                  pl.BlockSpec((1, ML_HEADS, 1, 1), lambda b, i: (b, 0, 0, 0)),
                  pl.BlockSpec((1, LANES), lambda b, i: (0, 0)),
                  pl.BlockSpec((1, LANES), lambda b, i: (0, 0)),
                  pl.BlockSpec((1, ML_V_WIDTH), lambda b, i: (0, 0))],
        out_specs=[pl.BlockSpec((1, c, ML_V_WIDTH), lambda b, i: (b, i, 0)),
                   pl.BlockSpec((1, ML_HEADS, ML_DQK, ML_DV), lambda b, i: (b, 0, 0, 0)),
                   pl.BlockSpec((1, ML_HEADS, 1, ML_DQK), lambda b, i: (b, 0, 0, 0)),
                   pl.BlockSpec((1, ML_HEADS, 1, 1), lambda b, i: (b, 0, 0, 0))],
        out_shape=[jax.ShapeDtypeStruct((B, L, ML_V_WIDTH), BF16),
                   jax.ShapeDtypeStruct((B, ML_HEADS, ML_DQK, ML_DV), F32),
                   jax.ShapeDtypeStruct((B, ML_HEADS, 1, ML_DQK), F32),
                   jax.ShapeDtypeStruct((B, ML_HEADS, 1, 1), F32)],
        scratch_shapes=[pltpu.VMEM((ML_HEADS, ML_DQK, ML_DV), F32),
                        pltpu.VMEM((ML_HEADS, 1, ML_DQK), F32),
                        pltpu.VMEM((ML_HEADS, 1, 1), F32)],
        compiler_params=_cparams(2),
        name="mlstm_mix",
    )(u3, g3, c0, n0.reshape(B, ML_HEADS, 1, ML_DQK), m0.reshape(B, ML_HEADS, 1, 1), bi, bf,
      gn_w.reshape(1, ML_V_WIDTH))
    return (out.reshape(B * L, ML_V_WIDTH), cN, nN.reshape(B, ML_HEADS, ML_DQK), mN.reshape(B, ML_HEADS))


def _bf(x):
    return x.astype(BF16)


RW_CHUNK_ROWS = 64
RW_CHAIN_GROUP = 16


def _rwkv_chain_group(chains, seq, s_s, o_ref, rk_ref, gnw_ref, gnb_ref, *, C):
    rowi = lax.broadcasted_iota(jnp.int32, (C, C), 0)
    coli = lax.broadcasted_iota(jnp.int32, (C, C), 1)
    upper = rowi < coli
    col2 = lax.broadcasted_iota(jnp.int32, (C, 2 * C), 1)
    incl2 = jnp.where(col2 >= C, col2 - C, col2) <= lax.broadcasted_iota(jnp.int32, (C, 2 * C), 0)
    n_sq = max(int(math.log2(C)), 1)
    st = []
    for b, h in chains:
        rs = slice(b * C, (b + 1) * C)
        sl = slice(h * RW_HEAD, (h + 1) * RW_HEAD)
        cut = lambda t, rs=rs, sl=sl: t[rs, sl]
        kk_h = cut(seq["kkv"])
        nrm = jnp.sqrt(jnp.sum(kk_h * kk_h, axis=-1, keepdims=True))
        kap = jnp.where(seq["valid"][rs], kk_h / jnp.maximum(nrm, 1e-12), 0.0)
        k_h, v_h, r_h = cut(seq["kmod"]), cut(seq["v"]), cut(seq["r"])
        gi = cut(seq["gi"])
        b_t = kap * cut(seq["a"]) * gi
        k_t = k_h * gi
        st.append(dict(b=b, h=h, sl=sl, k_h=k_h, v_h=v_h, r_h=r_h,
                       a_t=_bf(-kap * cut(seq["gp"])), r_t=_bf(r_h * cut(seq["g"])),
                       bk=jnp.concatenate([b_t, k_t], axis=0),
                       ge=seq["g"][(b + 1) * C - 1:(b + 1) * C, sl], s0=s_s[b, h]))
    for c in st:
        bkb = _bf(c["bk"])
        mt = _dot_nt(bkb, c["a_t"])
        c["pt"] = jnp.where(upper, mt[:C], 0.0)
        c["akt"] = _bf(jnp.where(upper, mt[C:], 0.0))
        c["a_r"] = jnp.where(incl2, _dot_nt(c["r_t"], bkb), 0.0)
        c["s0b"] = _bf(c["s0"])
        c["vb"] = _bf(c["v_h"])
    for c in st:
        c["xt"] = _dot_nt(c["s0b"], c["a_t"]) + _dot_tn(c["vb"], c["akt"])
    for _ in range(n_sq - 1):
        for c in st:
            z = _dot(_bf(jnp.concatenate([c["pt"], c["xt"]], axis=0)), _bf(c["pt"]))
            c["pt"] = z[:C]
            c["xt"] = c["xt"] + z[C:]
    for c in st:
        c["ut"] = _bf(c["xt"] + _dot(_bf(c["xt"]), _bf(c["pt"])))
    for c in st:
        a_r = c["a_r"]
        c["y"] = (_dot_nt(c["r_t"], c["s0b"]) + _dot_nt(_bf(a_r[:, :C]), c["ut"])
                  + _dot(_bf(a_r[:, C:]), c["vb"]))
        bkg = c["bk"] * c["ge"]
        s_s[c["b"], c["h"]] = (c["s0"] * c["ge"] + _dot(c["ut"], _bf(bkg[:C]))
                               + _dot_tn(c["vb"], _bf(bkg[C:])))
    for c in st:
        y, sl = c["y"], c["sl"]
        mean = jnp.mean(y, axis=-1, keepdims=True)
        yc = y - mean
        var = jnp.mean(yc * yc, axis=-1, keepdims=True)
        yn = yc * lax.rsqrt(var + GN_EPS) * gnw_ref[:, sl] + gnb_ref[:, sl]
        bonus = jnp.sum(c["r_h"] * c["k_h"] * rk_ref[:, sl], axis=-1, keepdims=True) * c["v_h"]
        o_ref[c["b"], :, sl] = (yn + bonus).astype(o_ref.dtype)


def _rwkv_kernel(u_ref, sh_ref, s0_ref, mu_ref, w0_ref, w2_ref, a0_ref, a2_ref, kk_ref, ka_ref, rk_ref,
                 gnw_ref, gnb_ref, o_ref, sN_ref, prev_s, s_s, *, C, bt, l_valid):
    ci = pl.program_id(1)

    @pl.when(ci == 0)
    def _():
        prev_s[...] = sh_ref[...]
        s_s[...] = s0_ref[...]

    N = bt * C
    u = u_ref[...].reshape(N, RW_COLS)
    row = lax.broadcasted_iota(jnp.int32, (N, 1), 0)
    tok = row % C
    valid = tok < l_valid
    prev = jnp.concatenate([jnp.broadcast_to(prev_s[b], (C, RW_COLS)) for b in range(bt)], axis=0)
    u_prev = jnp.where(tok == 0, prev, pltpu.roll(u, 1, 0))
    for b in range(bt):
        prev_s[b] = u[(b + 1) * C - 1:(b + 1) * C, :]
    us = u + (u_prev - u) * mu_ref[...]
    W = RW_WIDTH
    k = us[:, W:2 * W]
    wd = us[:, 3 * W:3 * W + RW_RANK]
    ad = us[:, 3 * W + RW_RANK:3 * W + 2 * RW_RANK]
    xw = w0_ref[...] + _dot(jnp.tanh(wd), w2_ref[...], HI)
    log_w = -_softplus(-xw) - 0.5
    ld = jnp.where(valid, -jnp.exp(log_w), 0.0)
    a = jax.nn.sigmoid(a0_ref[...] + _dot(ad, a2_ref[...], HI))
    ri = lax.broadcasted_iota(jnp.int32, (N, N), 0)
    cj = lax.broadcasted_iota(jnp.int32, (N, N), 1)
    same_seq_before = jnp.logical_and(cj <= ri, cj // C == ri // C)
    cum = _dot(same_seq_before.astype(F32), ld, HI)
    seq = dict(valid=valid, r=us[:, 0:W], v=jnp.where(valid, us[:, 2 * W:3 * W], 0.0), a=a,
               g=jnp.exp(cum), gi=jnp.exp(-cum), gp=jnp.exp(cum - ld), kkv=k * kk_ref[...],
               kmod=jnp.where(valid, k * (1.0 + (a - 1.0) * ka_ref[...]), 0.0))
    chains = [(b, h) for b in range(bt) for h in range(RW_HEADS)]
    group = RW_CHAIN_GROUP * bt
    for i in range(0, len(chains), group):
        _rwkv_chain_group(chains[i:i + group], seq, s_s, o_ref, rk_ref, gnw_ref, gnb_ref, C=C)

    @pl.when(ci == pl.num_programs(1) - 1)
    def _():
        sN_ref[...] = s_s[...]


def rwkv7_mix(u_rw, shift_prev, s_prev, mu, w0, w2, a0, a2, k_k, k_a, r_k, gn_w, gn_b, *, B, L, l_valid):
    C = min(L, RW_CHUNK_ROWS)
    bt = RW_CHUNK_ROWS // C
    u3 = u_rw.reshape(B, L, RW_COLS)
    vec = lambda n: pl.BlockSpec((1, n), lambda b, i: (0, 0))
    row = lambda t: t.reshape(1, -1)
    state = pl.BlockSpec((bt, RW_HEADS, RW_HEAD, RW_HEAD), lambda b, i: (b, 0, 0, 0))
    out, sN = pl.pallas_call(
        functools.partial(_rwkv_kernel, C=C, bt=bt, l_valid=l_valid),
        grid=(B // bt, L // C),
        in_specs=[pl.BlockSpec((bt, C, RW_COLS), lambda b, i: (b, i, 0)),
                  pl.BlockSpec((bt, 1, RW_COLS), lambda b, i: (b, 0, 0)),
                  state,
                  vec(RW_COLS), vec(RW_WIDTH),
                  pl.BlockSpec((RW_RANK, RW_WIDTH), lambda b, i: (0, 0)),
                  vec(RW_WIDTH),
                  pl.BlockSpec((RW_RANK, RW_WIDTH), lambda b, i: (0, 0)),
                  vec(RW_WIDTH), vec(RW_WIDTH), vec(RW_WIDTH), vec(RW_WIDTH), vec(RW_WIDTH)],
        out_specs=[pl.BlockSpec((bt, C, RW_WIDTH), lambda b, i: (b, i, 0)), state],
        out_shape=[jax.ShapeDtypeStruct((B, L, RW_WIDTH), BF16),
                   jax.ShapeDtypeStruct((B, RW_HEADS, RW_HEAD, RW_HEAD), F32)],
        scratch_shapes=[pltpu.VMEM((bt, 1, RW_COLS), F32),
                        pltpu.VMEM((bt, RW_HEADS, RW_HEAD, RW_HEAD), F32)],
        compiler_params=_cparams(2),
        name="rwkv7_mix",
    )(u3, shift_prev.reshape(B, 1, RW_COLS), s_prev, row(mu), row(w0), w2, row(a0), a2, row(k_k), row(k_a),
      row(r_k), row(gn_w), row(gn_b))
    return out.reshape(B * L, RW_WIDTH), sN


def _attn_kernel(q_ref, k_ref, v_ref, o_ref, *, bt):
    for b in range(bt):
        for h in range(XA_HEADS):
            sl = slice(h * XA_HEAD_DIM, (h + 1) * XA_HEAD_DIM)
            s = _dot_nt(q_ref[b, :, sl], k_ref[b, :, sl].astype(BF16)) * (XA_HEAD_DIM ** -0.5)
            e = jnp.exp(s - jnp.max(s, axis=-1, keepdims=True))
            p = e / jnp.sum(e, axis=-1, keepdims=True)
            o_ref[b, :, sl] = _dot(p.astype(BF16), v_ref[b, :, sl].astype(BF16)).astype(o_ref.dtype)


def cross_attention(q, mem_k, mem_v, *, B, L):
    tq = min(L, 512)
    bt = 1 if L >= 64 else 8
    q3 = q.reshape(B, L, XA_WIDTH)
    kv = pl.BlockSpec((bt, MEM_LEN, XA_WIDTH), lambda b, i: (b, 0, 0))
    out = pl.pallas_call(
        functools.partial(_attn_kernel, bt=bt),
        grid=(B // bt, L // tq),
        in_specs=[pl.BlockSpec((bt, tq, XA_WIDTH), lambda b, i: (b, i, 0)), kv, kv],
        out_specs=pl.BlockSpec((bt, tq, XA_WIDTH), lambda b, i: (b, i, 0)),
        out_shape=jax.ShapeDtypeStruct((B, L, XA_WIDTH), BF16),
        compiler_params=_cparams(2),
        name="cross_attention",
    )(q3, mem_k.reshape(B, MEM_LEN, XA_WIDTH), mem_v.reshape(B, MEM_LEN, XA_WIDTH))
    return out.reshape(B * L, XA_WIDTH)


ROUTER_GROUP_LANE = MOE_EXPERTS


def _router_kernel(x_ref, g_ref, w_ref, b_ref, xn_ref, comb_ref):
    xn = _rms(x_ref[...], g_ref[...])
    xn_ref[...] = xn.astype(BF16)
    z = _dot(xn, w_ref[...], HI) + b_ref[...]
    lane = lax.broadcasted_iota(jnp.int32, z.shape, 1).astype(F32)
    big = float(LANES)
    neg = -jnp.inf
    first = lambda mask: jnp.min(jnp.where(mask, lane, big), axis=-1, keepdims=True)
    is_g = jnp.logical_and(lane >= ROUTER_GROUP_LANE, lane < ROUTER_GROUP_LANE + MOE_GROUPS)
    zg = jnp.where(is_g, z, neg)
    mg = jnp.max(zg, axis=-1, keepdims=True)
    grp = first(zg == mg) - ROUTER_GROUP_LANE
    p_grp = 1.0 / jnp.sum(jnp.exp(zg - mg), axis=-1, keepdims=True)
    lo = grp * MOE_PER_GROUP
    ze = jnp.where(jnp.logical_and(lane >= lo, lane < lo + MOE_PER_GROUP), z, neg)
    t1 = jnp.max(ze, axis=-1, keepdims=True)
    i1 = first(ze == t1)
    ze2 = jnp.where(lane == i1, neg, ze)
    t2 = jnp.max(ze2, axis=-1, keepdims=True)
    i2 = first(ze2 == t2)
    e2 = jnp.exp(t2 - t1)
    g1 = p_grp / (1.0 + e2)
    comb_ref[...] = jnp.where(lane == i1, g1, 0.0) + jnp.where(lane == i2, g1 * e2, 0.0)


def moe_router(x, g, w_r1, b_r1, w_r2, b_r2):
    T, D = x.shape
    tm = min(T, 512)
    pad = LANES - MOE_EXPERTS - MOE_GROUPS
    w = jnp.concatenate([w_r2, w_r1, jnp.zeros((D, pad), F32)], axis=1)
    b = jnp.concatenate([b_r2, b_r1, jnp.zeros((pad,), F32)]).reshape(1, LANES)
    return pl.pallas_call(
        _router_kernel,
        grid=(T // tm,),
        in_specs=[pl.BlockSpec((tm, D), lambda i: (i, 0)),
                  pl.BlockSpec((1, D), lambda i: (0, 0)),
                  pl.BlockSpec((D, LANES), lambda i: (0, 0)),
                  pl.BlockSpec((1, LANES), lambda i: (0, 0))],
        out_specs=[pl.BlockSpec((tm, D), lambda i: (i, 0)),
                   pl.BlockSpec((tm, LANES), lambda i: (i, 0))],
        out_shape=[jax.ShapeDtypeStruct((T, D), BF16), jax.ShapeDtypeStruct((T, LANES), F32)],
        compiler_params=_cparams(1),
        name="moe_router",
    )(x, g.reshape(1, D), w, b)


def _experts_kernel(xn_ref, comb_ref, x_ref, wg_ref, wu_ref, wd_ref, gf_ref, o_ref, *, final_norm):
    e = pl.program_id(1)

    @pl.when(e == 0)
    def _():
        o_ref[...] = x_ref[...]

    xn = xn_ref[...]
    comb = comb_ref[...]
    lane = lax.broadcasted_iota(jnp.int32, comb.shape, 1)
    ce = jnp.sum(jnp.where(lane == e, comb, 0.0), axis=-1, keepdims=True)
    hid = jax.nn.silu(_dot(xn, wg_ref[0])) * _dot(xn, wu_ref[0]) * ce
    o_ref[...] += _dot(hid.astype(BF16), wd_ref[0])

    if final_norm:
        @pl.when(e == pl.num_programs(1) - 1)
        def _():
            o_ref[...] = _rms(o_ref[...], gf_ref[...])


def moe_experts(xn, comb, x, wg, wu, wd, g_final, *, final_norm):
    T, D = x.shape
    tm = min(T, 512)
    row = pl.BlockSpec((tm, D), lambda i, e: (i, 0))
    return pl.pallas_call(
        functools.partial(_experts_kernel, final_norm=final_norm),
        grid=(T // tm, MOE_EXPERTS),
        in_specs=[row,
                  pl.BlockSpec((tm, LANES), lambda i, e: (i, 0)),
                  row,
                  pl.BlockSpec((1, D, MOE_HIDDEN), lambda i, e: (e, 0, 0)),
                  pl.BlockSpec((1, D, MOE_HIDDEN), lambda i, e: (e, 0, 0)),
                  pl.BlockSpec((1, MOE_HIDDEN, D), lambda i, e: (e, 0, 0)),
                  pl.BlockSpec((1, D), lambda i, e: (0, 0))],
        out_specs=row,
        out_shape=jax.ShapeDtypeStruct((T, D), F32),
        compiler_params=_cparams(2),
        name="moe_experts",
    )(xn, comb, x, wg, wu, wd, g_final.reshape(1, D))


def _layer_weights(l, p):
    w_in = p["w_in"][l]
    c0 = RW_COLS
    c1 = c0 + ML_MAIN
    c2 = c1 + 2 * ML_HEADS
    c3 = c2 + POOL_WIDTH
    w_pl = jnp.concatenate([w_in[:, c2:c3], w_in[:, c1:c2],
                            jnp.zeros((D_MODEL, LANES - 2 * ML_HEADS), F32)], axis=1)
    bf = lambda t: t.astype(BF16)
    return dict(
        w_rw=bf(w_in[:, :c0]), w_ml=bf(w_in[:, c0:c1]), w_pl=bf(w_pl), w_gate=bf(w_in[:, c3:]),
        w_up_rwkv=bf(p["w_up_rwkv"][l]), w_up_mlstm=bf(p["w_up_mlstm"][l]), w_up_pool=bf(p["w_up_pool"][l]),
        w_out=bf(p["w_out"][l]), pool_w=bf(p["pool_w"][l]),
        xa_wq=bf(p["xa_wq"][l]), xa_wo=bf(p["xa_wo"][l]),
        xa_wkv=bf(jnp.concatenate([p["xa_wk"][l], p["xa_wv"][l]], axis=1)),
        moe_wg=bf(p["moe_wg"][l]), moe_wu=bf(p["moe_wu"][l]), moe_wd=bf(p["moe_wd"][l]),
    )


def _trunk_layer(x, mem_k, mem_v, rw_shift, rw_s, ml_c, ml_n, ml_m, pool_buf, *, l, p, lw, B, L, l_valid, start,
                 final_norm):
    g_mix = p["g_mix"][l]
    u_rw = norm_matmul(x, g_mix, lw["w_rw"], tn=640, name="in_rwkv")
    u_ml = norm_matmul(x, g_mix, lw["w_ml"], tn=1024, name="in_mlstm")
    u_pl = norm_matmul(x, g_mix, lw["w_pl"], tn=POOL_WIDTH + LANES, name="in_pool")
    gates = norm_matmul(x, g_mix, lw["w_gate"], tn=1024, out_dtype=BF16, act="sigmoid", name="in_gates")

    o_rw, rw_s = rwkv7_mix(u_rw, rw_shift, rw_s, p["rw_mu"][l], p["rw_w0"][l], p["rw_w2"][l], p["rw_a0"][l],
                           p["rw_a2"][l], p["rw_k_k"][l], p["rw_k_a"][l], p["rw_r_k"][l], p["rw_gn_w"][l],
                           p["rw_gn_b"][l], B=B, L=L, l_valid=l_valid)
    rw_shift = u_rw.reshape(B, L, RW_COLS)[:, l_valid - 1]
    o_ml, ml_c, ml_n, ml_m = mlstm_mix(u_ml, u_pl, ml_c, ml_n, ml_m, p["ml_b_i"][l], p["ml_b_f"][l],
                                       p["ml_gn_w"][l], B=B, L=L, l_valid=l_valid)
    o_pl, pool_buf = pool_mix(u_pl, pool_buf, lw["pool_w"], p["pool_scale"][l], B=B, L=L, l_valid=l_valid,
                              start=start)
    merged = merge_branches(o_rw, o_ml, o_pl, gates, lw["w_up_rwkv"], lw["w_up_mlstm"], lw["w_up_pool"])
    x = matmul_residual(merged, lw["w_out"], x)

    q = norm_matmul(x, p["g_xa"][l], lw["xa_wq"], tn=XA_WIDTH, out_dtype=BF16, name="xa_q")
    att = cross_attention(q, mem_k, mem_v, B=B, L=L)
    x = matmul_residual(att, lw["xa_wo"], x)

    xn, comb = moe_router(x, p["g_moe"][l], p["moe_wr1"][l], p["moe_br1"][l], p["moe_wr2"][l], p["moe_br2"][l])
    x = moe_experts(xn, comb, x, lw["moe_wg"], lw["moe_wu"], lw["moe_wd"], p["g_final"], final_norm=final_norm)
    return x, rw_shift, rw_s, ml_c, ml_n, ml_m, pool_buf


def kernel(x_prompt, x_sample, cache_mem_k, cache_mem_v, state_rwkv_s, state_rwkv_shift, state_mlstm_c, state_mlstm_n, state_mlstm_m, state_pool, mem_prompt, g_mix, w_in, rw_mu, rw_w0, rw_w2, rw_a0, rw_a2, rw_k_k, rw_k_a, rw_r_k, rw_gn_w, rw_gn_b, ml_b_i, ml_b_f, ml_gn_w, pool_w, pool_scale, w_up_rwkv, w_up_mlstm, w_up_pool, w_out, g_xa, g_mem, xa_wq, xa_wk, xa_wv, xa_wo, g_moe, moe_wr1, moe_br1, moe_wr2, moe_br2, moe_wg, moe_wu, moe_wd, g_final):
    p = dict(g_mix=g_mix, w_in=w_in, rw_mu=rw_mu, rw_w0=rw_w0, rw_w2=rw_w2, rw_a0=rw_a0, rw_a2=rw_a2, rw_k_k=rw_k_k,
             rw_k_a=rw_k_a, rw_r_k=rw_r_k, rw_gn_w=rw_gn_w, rw_gn_b=rw_gn_b, ml_b_i=ml_b_i, ml_b_f=ml_b_f,
             ml_gn_w=ml_gn_w, pool_w=pool_w, pool_scale=pool_scale, w_up_rwkv=w_up_rwkv, w_up_mlstm=w_up_mlstm,
             w_up_pool=w_up_pool, w_out=w_out, g_xa=g_xa, g_mem=g_mem, xa_wq=xa_wq, xa_wk=xa_wk, xa_wv=xa_wv,
             xa_wo=xa_wo, g_moe=g_moe, moe_wr1=moe_wr1, moe_br1=moe_br1, moe_wr2=moe_wr2, moe_br2=moe_br2,
             moe_wg=moe_wg, moe_wu=moe_wu, moe_wd=moe_wd, g_final=g_final)
    Bp, Lp, D = x_prompt.shape
    Bs, Ls, _ = x_sample.shape
    Ls_pad = -(-Ls // SUBLANES) * SUBLANES
    yp = x_prompt.reshape(Bp * Lp, D)
    ys = jnp.pad(x_sample, ((0, 0), (0, Ls_pad - Ls), (0, 0))).reshape(Bs * Ls_pad, D)
    zeros = lambda *s: jnp.zeros(s, F32)
    outs_p = [[] for _ in range(8)]
    outs_s = [[] for _ in range(6)]
    for l in range(DEPTH):
        lw = _layer_weights(l, p)
        final = l == DEPTH - 1
        kv = norm_matmul(mem_prompt.reshape(Bp * MEM_LEN, D), g_mem[l], lw["xa_wkv"], tn=1024, name="memory_kv")
        mk = kv[:, :XA_WIDTH].reshape(Bp, MEM_LEN, XA_HEADS, XA_HEAD_DIM)
        mv = kv[:, XA_WIDTH:].reshape(Bp, MEM_LEN, XA_HEADS, XA_HEAD_DIM)
        yp, sh, rs, c, n, m, pb = _trunk_layer(
            yp, mk, mv, zeros(Bp, RW_COLS), zeros(Bp, RW_HEADS, RW_HEAD, RW_HEAD),
            zeros(Bp, ML_HEADS, ML_DQK, ML_DV), zeros(Bp, ML_HEADS, ML_DQK), zeros(Bp, ML_HEADS),
            zeros(Bp, POOL_BUF, POOL_WIDTH), l=l, p=p, lw=lw, B=Bp, L=Lp, l_valid=Lp, start=0, final_norm=final)
        for acc, t in zip(outs_p, (rs, sh, c, n, m, pb, mk, mv)):
            acc.append(t)
        ys, sh, rs, c, n, m, pb = _trunk_layer(
            ys, cache_mem_k[l], cache_mem_v[l], state_rwkv_shift[l], state_rwkv_s[l], state_mlstm_c[l],
            state_mlstm_n[l], state_mlstm_m[l], state_pool[l], l=l, p=p, lw=lw, B=Bs, L=Ls_pad, l_valid=Ls,
            start=PAST_LEN, final_norm=final)
        for acc, t in zip(outs_s, (rs, sh, c, n, m, pb)):
            acc.append(t)
    y_prompt = yp.reshape(Bp, Lp, D)
    y_sample = ys.reshape(Bs, Ls_pad, D)[:, :Ls]
    return (y_prompt, y_sample, *[jnp.stack(t) for t in outs_p], *[jnp.stack(t) for t in outs_s])
```

```python
import functools
import math

import jax
import jax.numpy as jnp
from jax import lax
from jax.experimental import pallas as pl
from jax.experimental.pallas import tpu as pltpu

F32 = jnp.float32
BF16 = jnp.bfloat16
HI = lax.Precision.HIGHEST

D_MODEL = 2048
DEPTH = 2
PAST_LEN = 16384
RW_HEAD = 64
RW_WIDTH = 1024
RW_HEADS = 16
RW_RANK = 64
RW_COLS = 3 * RW_WIDTH + 2 * RW_RANK
GN_EPS = 64e-5
ML_HEADS = 4
ML_DQK = 128
ML_DV = 256
ML_QK_WIDTH = 512
ML_V_WIDTH = 1024
ML_MAIN = 2 * ML_QK_WIDTH + 2 * ML_V_WIDTH
POOL_WIDTH = 1024
POOL_WINDOWS = (2, 4, 8, 16)
POOL_GW = 256
POOL_BUF = 15
POOL_HIST = 16
POOL_TAIL = 24
MEM_LEN = 256
XA_HEADS = 4
XA_HEAD_DIM = 128
XA_WIDTH = 512
MOE_GROUPS = 4
MOE_PER_GROUP = 8
MOE_EXPERTS = 32
MOE_HIDDEN = 256
RMS_EPS = 1e-6
LANES = 128
SUBLANES = 8
VMEM_LIMIT = 56 * 1024 * 1024
ROW_TILE = 1024
CHUNK_ROWS = 64
TN_RWKV = 640
TN_MLSTM = 1024
TN_GATES = 1024


def _cparams(n_axes):
    return pltpu.CompilerParams(dimension_semantics=("arbitrary",) * n_axes, vmem_limit_bytes=VMEM_LIMIT)


def _dot(a, b, precision=None):
    return jnp.dot(a, b, preferred_element_type=F32, precision=precision)


def _dot_nt(a, b, precision=None):
    return lax.dot_general(a, b, (((1,), (1,)), ((), ())), preferred_element_type=F32, precision=precision)


def _dot_tn(a, b, precision=None):
    return lax.dot_general(a, b, (((0,), (0,)), ((), ())), preferred_element_type=F32, precision=precision)


def _bf(x):
    return x.astype(BF16)


def _rms(x, g):
    return x * lax.rsqrt(jnp.mean(x * x, axis=-1, keepdims=True) + RMS_EPS) * g


def _log_sigmoid(x):
    return jnp.minimum(x, 0.0) - jnp.log1p(jnp.exp(-jnp.abs(x)))


def _softplus(x):
    return jnp.maximum(x, 0.0) + jnp.log1p(jnp.exp(-jnp.abs(x)))


def _seq_prefix_mask(n, c):
    r = lax.broadcasted_iota(jnp.int32, (n, n), 0)
    q = lax.broadcasted_iota(jnp.int32, (n, n), 1)
    return jnp.logical_and(q <= r, q // c == r // c)


def _layer_block(l, blk, idx):
    return pl.BlockSpec((None,) + tuple(blk), lambda b, i: (l,) + tuple(idx(b, i)))


def _norm_matmul_kernel(x_ref, g_ref, w_ref, o_ref, xn_ref, *, act):
    @pl.when(pl.program_id(1) == 0)
    def _():
        xn_ref[...] = _bf(_rms(x_ref[...], g_ref[...]))

    acc = _dot(xn_ref[...], w_ref[...])
    if act == "sigmoid":
        acc = jax.nn.sigmoid(acc)
    o_ref[...] = acc.astype(o_ref.dtype)


def norm_matmul(x, g, w, *, tn, name, out_dtype=F32, act=None):
    T, D = x.shape
    N = w.shape[1]
    tm = min(T, ROW_TILE)
    return pl.pallas_call(
        functools.partial(_norm_matmul_kernel, act=act),
        grid=(T // tm, N // tn),
        in_specs=[pl.BlockSpec((tm, D), lambda i, j: (i, 0)),
                  pl.BlockSpec((1, D), lambda i, j: (0, 0)),
                  pl.BlockSpec((D, tn), lambda i, j: (0, j))],
        out_specs=pl.BlockSpec((tm, tn), lambda i, j: (i, j)),
        out_shape=jax.ShapeDtypeStruct((T, N), out_dtype),
        scratch_shapes=[pltpu.VMEM((tm, D), BF16)],
        compiler_params=_cparams(2),
        name=name,
    )(x, g.reshape(1, D), w)


def _in_pool_kernel(x_ref, g_ref, w_ref, wif_ref, o_ref, gif_ref):
    xn = _rms(x_ref[...], g_ref[...])
    o_ref[...] = _dot(_bf(xn), w_ref[...])
    gif_ref[...] = _dot(xn, wif_ref[...], HI)


def in_pool_gates(x, g, w_pool, w_if):
    T, D = x.shape
    tm = min(T, ROW_TILE)
    return pl.pallas_call(
        _in_pool_kernel,
        grid=(T // tm,),
        in_specs=[pl.BlockSpec((tm, D), lambda i: (i, 0)),
                  pl.BlockSpec((1, D), lambda i: (0, 0)),
                  pl.BlockSpec((D, POOL_WIDTH), lambda i: (0, 0)),
                  pl.BlockSpec((D, LANES), lambda i: (0, 0))],
        out_specs=[pl.BlockSpec((tm, POOL_WIDTH), lambda i: (i, 0)),
                   pl.BlockSpec((tm, LANES), lambda i: (i, 0))],
        out_shape=[jax.ShapeDtypeStruct((T, POOL_WIDTH), F32), jax.ShapeDtypeStruct((T, LANES), F32)],
        compiler_params=_cparams(1),
        name="in_pool_gates",
    )(x, g.reshape(1, D), w_pool, w_if)


def _matmul_residual_kernel(a_ref, w_ref, r_ref, o_ref):
    o_ref[...] = r_ref[...] + _dot(a_ref[...], w_ref[...])


def matmul_residual(a, w, res, *, tn=1024):
    T, K = a.shape
    N = w.shape[1]
    tm = min(T, ROW_TILE)
    return pl.pallas_call(
        _matmul_residual_kernel,
        grid=(T // tm, N // tn),
        in_specs=[pl.BlockSpec((tm, K), lambda i, j: (i, 0)),
                  pl.BlockSpec((K, tn), lambda i, j: (0, j)),
                  pl.BlockSpec((tm, tn), lambda i, j: (i, j))],
        out_specs=pl.BlockSpec((tm, tn), lambda i, j: (i, j)),
        out_shape=jax.ShapeDtypeStruct((T, N), F32),
        compiler_params=_cparams(2),
        name="matmul_residual",
    )(a, w, res)


def _merge_kernel(orw_ref, oml_ref, opl_ref, g0_ref, g1_ref, g2_ref, wr_ref, wm_ref, wp_ref, o_ref):
    m = (g0_ref[...].astype(F32) * _dot(orw_ref[...], wr_ref[...])
         + g1_ref[...].astype(F32) * _dot(oml_ref[...], wm_ref[...])
         + g2_ref[...].astype(F32) * _dot(opl_ref[...], wp_ref[...]))
    o_ref[...] = _bf(m)


def merge_branches(o_rw, o_ml, o_pl, gates, w_rw, w_ml, w_pl, *, tn=1024):
    T, K = o_rw.shape
    tm = min(T, ROW_TILE)
    nj = D_MODEL // tn
    act = pl.BlockSpec((tm, K), lambda i, j: (i, 0))
    wsp = pl.BlockSpec((K, tn), lambda i, j: (0, j))
    gate = lambda b: pl.BlockSpec((tm, tn), lambda i, j, b=b: (i, b * nj + j))
    return pl.pallas_call(
        _merge_kernel,
        grid=(T // tm, nj),
        in_specs=[act, act, act, gate(0), gate(1), gate(2), wsp, wsp, wsp],
        out_specs=pl.BlockSpec((tm, tn), lambda i, j: (i, j)),
        out_shape=jax.ShapeDtypeStruct((T, D_MODEL), BF16),
        compiler_params=_cparams(2),
        name="merge_branches",
    )(o_rw, o_ml, o_pl, gates, gates, gates, w_rw, w_ml, w_pl)


def _pool_kernel(hist_ref, u_ref, w_ref, sc_ref, o_ref, tail_ref, e_ref, *, tl, bt, start):
    li = pl.program_id(1)

    @pl.when(li == 0)
    def _():
        e_ref[:, 0:POOL_HIST, :] = hist_ref[...]

    pos = start + li * tl + lax.broadcasted_iota(jnp.int32, (tl, 1), 0)
    for b in range(bt):
        u = u_ref[b]
        e_ref[b, POOL_HIST:, :] = u
        e = e_ref[b]
        s2 = e + pltpu.roll(e, 1, 0)
        s4 = s2[:, POOL_GW:] + pltpu.roll(s2[:, POOL_GW:], 2, 0)
        s8 = s4[:, POOL_GW:] + pltpu.roll(s4[:, POOL_GW:], 4, 0)
        s16 = s8[:, POOL_GW:] + pltpu.roll(s8[:, POOL_GW:], 8, 0)
        sums = (s2[:, :POOL_GW], s4[:, :POOL_GW], s8[:, :POOL_GW], s16)
        for g, win in enumerate(POOL_WINDOWS):
            cols = slice(g * POOL_GW, (g + 1) * POOL_GW)
            cnt = jnp.minimum(win, pos + 1).astype(F32)
            d = sums[g][POOL_HIST:, :] / cnt - u[:, cols]
            out = _dot(_bf(d), w_ref[g]) * sc_ref[:, cols]
            o_ref[b, :, cols] = out.astype(o_ref.dtype)
        tail_ref[b] = e[tl + POOL_HIST - POOL_TAIL:, :]
        e_ref[b, 0:POOL_HIST, :] = e[tl:, :]


def pool_mix(u_pl, buf, l, w_grp, scale, *, B, L, l_valid, start):
    tl = min(L, 256)
    bt = max(CHUNK_ROWS // L, 1)
    hist = jnp.pad(buf[l], ((0, 0), (POOL_HIST - POOL_BUF, 0), (0, 0)))
    u3 = u_pl.reshape(B, L, POOL_WIDTH)
    out, tail = pl.pallas_call(
        functools.partial(_pool_kernel, tl=tl, bt=bt, start=start),
        grid=(B // bt, L // tl),
        in_specs=[pl.BlockSpec((bt, POOL_HIST, POOL_WIDTH), lambda b, i: (b, 0, 0)),
                  pl.BlockSpec((bt, tl, POOL_WIDTH), lambda b, i: (b, i, 0)),
                  pl.BlockSpec((4, POOL_GW, POOL_GW), lambda b, i: (0, 0, 0)),
                  pl.BlockSpec((1, POOL_WIDTH), lambda b, i: (0, 0))],
        out_specs=[pl.BlockSpec((bt, tl, POOL_WIDTH), lambda b, i: (b, i, 0)),
                   pl.BlockSpec((bt, POOL_TAIL, POOL_WIDTH), lambda b, i: (b, 0, 0))],
        out_shape=[jax.ShapeDtypeStruct((B, L, POOL_WIDTH), BF16),
                   jax.ShapeDtypeStruct((B, POOL_TAIL, POOL_WIDTH), F32)],
        scratch_shapes=[pltpu.VMEM((bt, tl + POOL_HIST, POOL_WIDTH), F32)],
        compiler_params=_cparams(2),
        name="pool_mix",
    )(hist, u3, w_grp, scale.reshape(1, POOL_WIDTH))
    pad = L - l_valid
    new_buf = tail[:, POOL_TAIL - pad - POOL_BUF:POOL_TAIL - pad, :]
    return out.reshape(B * L, POOL_WIDTH), new_buf


def _mlstm_kernel(*refs, c, bt, l_valid, has_carry):
    u_ref, gif_ref, c0_ref, n0_ref, m0_ref, bi_ref, bf_ref, gnw_ref = refs[:8]
    o_ref, cN_ref, nN_ref, mN_ref, c_s, n_s, m_s = refs[8 + has_carry:]
    ci = pl.program_id(1)

    @pl.when(ci == 0)
    def _():
        c_s[...] = c0_ref[...]
        n_s[...] = n0_ref[...]
        m_s[...] = m0_ref[...]

    N = bt * c
    gif = gif_ref[...].reshape(N, LANES)
    tok = lax.broadcasted_iota(jnp.int32, (N, 1), 0) % c
    valid = tok < l_valid
    ig_all = jnp.where(valid, gif + bi_ref[...], -jnp.inf)
    lf_all = jnp.where(valid, _log_sigmoid(gif + bf_ref[...]), 0.0)
    bcum_all = _dot(_seq_prefix_mask(N, c).astype(F32), lf_all, HI)
    ig_t = ig_all.T
    bcum_t = bcum_all.T
    ri = lax.broadcasted_iota(jnp.int32, (c, c), 0)
    causal = lax.broadcasted_iota(jnp.int32, (c, c), 1) <= ri
    st = []
    for b in range(bt):
        rs = slice(b * c, (b + 1) * c)
        for h in range(ML_HEADS):
            qs = slice(h * ML_DQK, (h + 1) * ML_DQK)
            ks = slice(ML_QK_WIDTH + h * ML_DQK, ML_QK_WIDTH + (h + 1) * ML_DQK)
            vs = slice(2 * ML_QK_WIDTH + h * ML_DV, 2 * ML_QK_WIDTH + (h + 1) * ML_DV)
            os_ = slice(2 * ML_QK_WIDTH + ML_V_WIDTH + h * ML_DV, 2 * ML_QK_WIDTH + ML_V_WIDTH + (h + 1) * ML_DV)
            i_c = ig_all[rs, h:h + 1]
            b_c = bcum_all[rs, ML_HEADS + h:ML_HEADS + h + 1]
            i_r = ig_t[h:h + 1, rs]
            b_r = bcum_t[ML_HEADS + h:ML_HEADS + h + 1, rs]
            m_prev = m_s[b, h]
            dlog = jnp.where(causal, b_c - b_r + i_r, -jnp.inf)
            inter = b_c + m_prev
            m_t = jnp.maximum(inter, jnp.max(dlog, axis=-1, keepdims=True))
            b_last = b_c[c - 1:c, :]
            s_log = b_last - b_c + i_c
            m_new = jnp.maximum(b_last + m_prev, jnp.max(s_log, axis=0, keepdims=True))
            q = u_ref[b, :, qs]
            k = u_ref[b, :, ks] * (ML_DQK ** -0.5)
            st.append(dict(b=b, h=h, hs=slice(h * ML_DV, (h + 1) * ML_DV), os=os_, q=q, k=k, qb=_bf(q), vb=_bf(u_ref[b, :, vs]),
                           dlog=dlog, m_t=m_t, w_prev=jnp.exp(inter - m_t), m_new=m_new,
                           kw=k * jnp.exp(s_log - m_new), wp=jnp.exp(b_last + m_prev - m_new)))
    for s in st:
        s["wts"] = jnp.exp(s["dlog"] - s["m_t"]) * _dot_nt(s["qb"], _bf(s["k"]))
    for s in st:
        b, h = s["b"], s["h"]
        c_prev = c_s[b, h]
        n_prev = n_s[b, h]
        num = s["w_prev"] * _dot(s["qb"], _bf(c_prev)) + _dot(_bf(s["wts"]), s["vb"])
        den = (s["w_prev"] * jnp.sum(s["q"] * n_prev, axis=-1, keepdims=True)
               + jnp.sum(s["wts"], axis=-1, keepdims=True))
        s["hh"] = num / jnp.maximum(jnp.abs(den), jnp.exp(-s["m_t"]))
        c_s[b, h] = s["wp"] * c_prev + _dot_tn(_bf(s["kw"]), s["vb"])
        n_s[b, h] = s["wp"] * n_prev + jnp.sum(s["kw"], axis=0, keepdims=True)
        m_s[b, h] = s["m_new"]
    for s in st:
        hh = s["hh"]
        hn = hh * lax.rsqrt(jnp.mean(hh * hh, axis=-1, keepdims=True) + RMS_EPS) * gnw_ref[:, s["hs"]]
        o_ref[s["b"], :, s["hs"]] =(hn * jax.nn.sigmoid(u_ref[s["b"], :, s["os"]])).astype(o_ref.dtype)

    @pl.when(ci == pl.num_programs(1) - 1)
    def _():
        cN_ref[...] = c_s[...]
        nN_ref[...] = n_s[...]
        mN_ref[...] = m_s[...]


def mlstm_mix(u_ml, gif, c0, n0, m0, l, l_out, c_carry, b_i, b_f, gn_w, *, B, L, l_valid):
    c = min(L, CHUNK_ROWS)
    bt = CHUNK_ROWS // c
    u3 = u_ml.reshape(B, L, ML_MAIN)
    g3 = gif.reshape(B, L, LANES)
    zeros = jnp.zeros((LANES - 2 * ML_HEADS,), F32)
    bi = jnp.concatenate([b_i, jnp.zeros((ML_HEADS,), F32), zeros]).reshape(1, LANES)
    bf = jnp.concatenate([jnp.zeros((ML_HEADS,), F32), b_f, zeros]).reshape(1, LANES)
    nl = n0.shape[0]
    c_blk, n_blk, m_blk = (bt, ML_HEADS, ML_DQK, ML_DV), (bt, ML_HEADS, 1, ML_DQK), (bt, ML_HEADS, 1, 1)
    at_b = lambda b, i: (b, 0, 0, 0)
    in_specs = [pl.BlockSpec((bt, c, ML_MAIN), lambda b, i: (b, i, 0)),
                pl.BlockSpec((bt, c, LANES), lambda b, i: (b, i, 0)),
                _layer_block(l, c_blk, at_b), _layer_block(l, n_blk, at_b), _layer_block(l, m_blk, at_b),
                pl.BlockSpec((1, LANES), lambda b, i: (0, 0)),
                pl.BlockSpec((1, LANES), lambda b, i: (0, 0)),
                pl.BlockSpec((1, ML_V_WIDTH), lambda b, i: (0, 0))]
    args = [u3, g3, c0, n0.reshape(nl, B, ML_HEADS, 1, ML_DQK), m0.reshape(nl, B, ML_HEADS, 1, 1), bi, bf,
            gn_w.reshape(1, ML_V_WIDTH)]
    aliases = {}
    if c_carry is not None:
        in_specs.append(pl.BlockSpec(memory_space=pl.ANY))
        args.append(c_carry)
        aliases = {len(args) - 1: 1}
    out, cN, nN, mN = pl.pallas_call(
        functools.partial(_mlstm_kernel, c=c, bt=bt, l_valid=l_valid, has_carry=c_carry is not None),
        grid=(B // bt, L // c),
        in_specs=in_specs,
        out_specs=[pl.BlockSpec((bt, c, ML_V_WIDTH), lambda b, i: (b, i, 0)),
                   _layer_block(l_out, c_blk, at_b),
                   pl.BlockSpec(n_blk, at_b), pl.BlockSpec(m_blk, at_b)],
        out_shape=[jax.ShapeDtypeStruct((B, L, ML_V_WIDTH), BF16),
                   jax.ShapeDtypeStruct((DEPTH, B, ML_HEADS, ML_DQK, ML_DV), F32),
                   jax.ShapeDtypeStruct((B, ML_HEADS, 1, ML_DQK), F32),
                   jax.ShapeDtypeStruct((B, ML_HEADS, 1, 1), F32)],
        scratch_shapes=[pltpu.VMEM(c_blk, F32), pltpu.VMEM(n_blk, F32), pltpu.VMEM(m_blk, F32)],
        input_output_aliases=aliases,
        compiler_params=_cparams(2),
        name="mlstm_mix",
    )(*args)
    return (out.reshape(B * L, ML_V_WIDTH), cN, nN.reshape(B, ML_HEADS, ML_DQK), mN.reshape(B, ML_HEADS))


RW_CHAIN_GROUP = 16


def _rwkv_chain_group(chains, seq, s_s, o_ref, rk_ref, gnw_ref, gnb_ref, *, C):
    rowi = lax.broadcasted_iota(jnp.int32, (C, C), 0)
    coli = lax.broadcasted_iota(jnp.int32, (C, C), 1)
    upper = rowi < coli
    col2 = lax.broadcasted_iota(jnp.int32, (C, 2 * C), 1)
    incl2 = jnp.where(col2 >= C, col2 - C, col2) <= lax.broadcasted_iota(jnp.int32, (C, 2 * C), 0)
    n_sq = max(int(math.log2(C)), 1)
    st = []
    for b, h in chains:
        rs = slice(b * C, (b + 1) * C)
        sl = slice(h * RW_HEAD, (h + 1) * RW_HEAD)
        cut = lambda t, rs=rs, sl=sl: t[rs, sl]
        kk_h = cut(seq["kkv"])
        nrm = jnp.sqrt(jnp.sum(kk_h * kk_h, axis=-1, keepdims=True))
        kap = jnp.where(seq["valid"][rs], kk_h / jnp.maximum(nrm, 1e-12), 0.0)
        k_h, v_h, r_h = cut(seq["kmod"]), cut(seq["v"]), cut(seq["r"])
        gi = cut(seq["gi"])
        b_t = kap * cut(seq["a"]) * gi
        k_t = k_h * gi
        st.append(dict(b=b, h=h, sl=sl, k_h=k_h, v_h=v_h, r_h=r_h,
                       a_t=_bf(-kap * cut(seq["gp"])), r_t=_bf(r_h * cut(seq["g"])),
                       bk=jnp.concatenate([b_t, k_t], axis=0),
                       ge=seq["g"][(b + 1) * C - 1:(b + 1) * C, sl], s0=s_s[b, h]))
    for c in st:
        bkb = _bf(c["bk"])
        mt = _dot_nt(bkb, c["a_t"])
        c["pt"] = jnp.where(upper, mt[:C], 0.0)
        c["akt"] = _bf(jnp.where(upper, mt[C:], 0.0))
        c["a_r"] = jnp.where(incl2, _dot_nt(c["r_t"], bkb), 0.0)
        c["s0b"] = _bf(c["s0"])
        c["vb"] = _bf(c["v_h"])
    for c in st:
        c["xt"] = _dot_nt(c["s0b"], c["a_t"]) + _dot_tn(c["vb"], c["akt"])
    for _ in range(n_sq - 1):
        for c in st:
            z = _dot(_bf(jnp.concatenate([c["pt"], c["xt"]], axis=0)), _bf(c["pt"]))
            c["pt"] = z[:C]
            c["xt"] = c["xt"] + z[C:]
    for c in st:
        c["ut"] = _bf(c["xt"] + _dot(_bf(c["xt"]), _bf(c["pt"])))
    for c in st:
        a_r = c["a_r"]
        c["y"] = (_dot_nt(c["r_t"], c["s0b"]) + _dot_nt(_bf(a_r[:, :C]), c["ut"])
                  + _dot(_bf(a_r[:, C:]), c["vb"]))
        bkg = c["bk"] * c["ge"]
        s_s[c["b"], c["h"]] = (c["s0"] * c["ge"] + _dot(c["ut"], _bf(bkg[:C]))
                               + _dot_tn(c["vb"], _bf(bkg[C:])))
    for c in st:
        y, sl = c["y"], c["sl"]
        mean = jnp.mean(y, axis=-1, keepdims=True)
        yc = y - mean
        var = jnp.mean(yc * yc, axis=-1, keepdims=True)
        yn = yc * lax.rsqrt(var + GN_EPS) * gnw_ref[:, sl] + gnb_ref[:, sl]
        bonus = jnp.sum(c["r_h"] * c["k_h"] * rk_ref[:, sl], axis=-1, keepdims=True) * c["v_h"]
        o_ref[c["b"], :, sl] = (yn + bonus).astype(o_ref.dtype)


def _rwkv_kernel(*refs, C, bt, l_valid, has_carry):
    (u_ref, sh_ref, s0_ref, mu_ref, w0_ref, w2_ref, a0_ref, a2_ref, kk_ref, ka_ref, rk_ref, gnw_ref,
     gnb_ref) = refs[:13]
    o_ref, sN_ref, prev_s, s_s = refs[13 + has_carry:]
    ci = pl.program_id(1)

    @pl.when(ci == 0)
    def _():
        prev_s[...] = sh_ref[...]
        s_s[...] = s0_ref[...]

    N = bt * C
    u = u_ref[...].reshape(N, RW_COLS)
    tok = lax.broadcasted_iota(jnp.int32, (N, 1), 0) % C
    valid = tok < l_valid
    prev = jnp.concatenate([jnp.broadcast_to(prev_s[b], (C, RW_COLS)) for b in range(bt)], axis=0)
    u_prev = jnp.where(tok == 0, prev, pltpu.roll(u, 1, 0))
    for b in range(bt):
        prev_s[b] = u[(b + 1) * C - 1:(b + 1) * C, :]
    us = u + (u_prev - u) * mu_ref[...]
    W = RW_WIDTH
    k = us[:, W:2 * W]
    wd = us[:, 3 * W:3 * W + RW_RANK]
    ad = us[:, 3 * W + RW_RANK:3 * W + 2 * RW_RANK]
    xw = w0_ref[...] + _dot(jnp.tanh(wd), w2_ref[...], HI)
    log_w = -_softplus(-xw) - 0.5
    ld = jnp.where(valid, -jnp.exp(log_w), 0.0)
    a = jax.nn.sigmoid(a0_ref[...] + _dot(ad, a2_ref[...], HI))
    cum = _dot(_seq_prefix_mask(N, C).astype(F32), ld, HI)
    seq = dict(valid=valid, r=us[:, 0:W], v=jnp.where(valid, us[:, 2 * W:3 * W], 0.0), a=a,
               g=jnp.exp(cum), gi=jnp.exp(-cum), gp=jnp.exp(cum - ld), kkv=k * kk_ref[...],
               kmod=jnp.where(valid, k * (1.0 + (a - 1.0) * ka_ref[...]), 0.0))
    chains = [(b, h) for b in range(bt) for h in range(RW_HEADS)]
    group = RW_CHAIN_GROUP * bt
    for i in range(0, len(chains), group):
        _rwkv_chain_group(chains[i:i + group], seq, s_s, o_ref, rk_ref, gnw_ref, gnb_ref, C=C)

    @pl.when(ci == pl.num_programs(1) - 1)
    def _():
        sN_ref[...] = s_s[...]


def rwkv7_mix(u_rw, shift_prev, s_prev, l, l_out, s_carry, mu, w0, w2, a0, a2, k_k, k_a, r_k, gn_w, gn_b, *, B, L,
              l_valid):
    C = min(L, CHUNK_ROWS)
    bt = CHUNK_ROWS // C
    u3 = u_rw.reshape(B, L, RW_COLS)
    vec = lambda n: pl.BlockSpec((1, n), lambda b, i: (0, 0))
    row = lambda t: t.reshape(1, -1)
    s_blk = (bt, RW_HEADS, RW_HEAD, RW_HEAD)
    at_b = lambda b, i: (b, 0, 0, 0)
    in_specs = [pl.BlockSpec((bt, C, RW_COLS), lambda b, i: (b, i, 0)),
                _layer_block(l, (bt, 1, RW_COLS), lambda b, i: (b, 0, 0)),
                _layer_block(l, s_blk, at_b),
                vec(RW_COLS), vec(RW_WIDTH),
                pl.BlockSpec((RW_RANK, RW_WIDTH), lambda b, i: (0, 0)),
                vec(RW_WIDTH),
                pl.BlockSpec((RW_RANK, RW_WIDTH), lambda b, i: (0, 0)),
                vec(RW_WIDTH), vec(RW_WIDTH), vec(RW_WIDTH), vec(RW_WIDTH), vec(RW_WIDTH)]
    args = [u3, shift_prev.reshape(shift_prev.shape[0], B, 1, RW_COLS), s_prev, row(mu), row(w0), w2, row(a0), a2,
            row(k_k), row(k_a), row(r_k), row(gn_w), row(gn_b)]
    aliases = {}
    if s_carry is not None:
        in_specs.append(pl.BlockSpec(memory_space=pl.ANY))
        args.append(s_carry)
        aliases = {len(args) - 1: 1}
    out, sN = pl.pallas_call(
        functools.partial(_rwkv_kernel, C=C, bt=bt, l_valid=l_valid, has_carry=s_carry is not None),
        grid=(B // bt, L // C),
        in_specs=in_specs,
        out_specs=[pl.BlockSpec((bt, C, RW_WIDTH), lambda b, i: (b, i, 0)), _layer_block(l_out, s_blk, at_b)],
        out_shape=[jax.ShapeDtypeStruct((B, L, RW_WIDTH), BF16),
                   jax.ShapeDtypeStruct((DEPTH, B, RW_HEADS, RW_HEAD, RW_HEAD), F32)],
        scratch_shapes=[pltpu.VMEM((bt, 1, RW_COLS), F32), pltpu.VMEM(s_blk, F32)],
        input_output_aliases=aliases,
        compiler_params=_cparams(2),
        name="rwkv7_mix",
    )(*args)
    return out.reshape(B * L, RW_WIDTH), sN


ATTN_SEQS_PER_STEP = 8


def _attn_kernel(q_ref, k_ref, v_ref, o_ref, *, bt, heads):
    if heads == 1:
        pairs = [(b, slice(None), (b,)) for b in range(bt)]
    else:
        pairs = [(b, slice(h * XA_HEAD_DIM, (h + 1) * XA_HEAD_DIM), (b, slice(None), h))
                 for b in range(bt) for h in range(heads)]
    s = [_dot_nt(q_ref[b, :, qs], _bf(k_ref[ki])) * (XA_HEAD_DIM ** -0.5) for b, qs, ki in pairs]
    e = [jnp.exp(t - jnp.max(t, axis=-1, keepdims=True)) for t in s]
    p = [t / jnp.sum(t, axis=-1, keepdims=True) for t in e]
    for (b, qs, ki), pr in zip(pairs, p):
        o_ref[b, :, qs] = _dot(_bf(pr), _bf(v_ref[ki])).astype(o_ref.dtype)


def cross_attention(q, mem_k, mem_v, l, *, B, L):
    q3 = q.reshape(B, L, XA_WIDTH)
    if l is None:
        bt, tq, heads = 1, min(L, 512), 1
        grid = (B, L // tq, XA_HEADS)
        qo_spec = pl.BlockSpec((bt, tq, XA_HEAD_DIM), lambda b, i, h: (b, i, h))
        k_spec = pl.BlockSpec((bt, MEM_LEN, XA_HEAD_DIM), lambda b, i, h: (b, 0, h))
        v_spec = pl.BlockSpec((bt, MEM_LEN, XA_HEAD_DIM), lambda b, i, h: (b, 0, XA_HEADS + h))
    else:
        bt, tq, heads = ATTN_SEQS_PER_STEP, L, XA_HEADS
        grid = (B // bt, 1, 1)
        qo_spec = pl.BlockSpec((bt, tq, XA_WIDTH), lambda b, i, h: (b, 0, 0))
        k_spec = v_spec = pl.BlockSpec((None, bt, MEM_LEN, XA_HEADS, XA_HEAD_DIM),
                                       lambda b, i, h: (l, b, 0, 0, 0))
    out = pl.pallas_call(
        functools.partial(_attn_kernel, bt=bt, heads=heads),
        grid=grid,
        in_specs=[qo_spec, k_spec, v_spec],
        out_specs=qo_spec,
        out_shape=jax.ShapeDtypeStruct((B, L, XA_WIDTH), BF16),
        compiler_params=_cparams(3),
        name="cross_attention",
    )(q3, mem_k, mem_v)
    return out.reshape(B * L, XA_WIDTH)


ROUTER_GROUP_LANE = MOE_EXPERTS


def _router_kernel(x_ref, g_ref, w_ref, b_ref, xn_ref, comb_ref):
    xn = _rms(x_ref[...], g_ref[...])
    xn_ref[...] = _bf(xn)
    z = _dot(xn, w_ref[...], HI) + b_ref[...]
    lane = lax.broadcasted_iota(jnp.int32, z.shape, 1).astype(F32)
    big = float(LANES)
    neg = -jnp.inf
    first = lambda mask: jnp.min(jnp.where(mask, lane, big), axis=-1, keepdims=True)
    is_g = jnp.logical_and(lane >= ROUTER_GROUP_LANE, lane < ROUTER_GROUP_LANE + MOE_GROUPS)
    zg = jnp.where(is_g, z, neg)
    mg = jnp.max(zg, axis=-1, keepdims=True)
    grp = first(zg == mg) - ROUTER_GROUP_LANE
    p_grp = 1.0 / jnp.sum(jnp.exp(zg - mg), axis=-1, keepdims=True)
    lo = grp * MOE_PER_GROUP
    ze = jnp.where(jnp.logical_and(lane >= lo, lane < lo + MOE_PER_GROUP), z, neg)
    t1 = jnp.max(ze, axis=-1, keepdims=True)
    i1 = first(ze == t1)
    ze2 = jnp.where(lane == i1, neg, ze)
    t2 = jnp.max(ze2, axis=-1, keepdims=True)
    i2 = first(ze2 == t2)
    e2 = jnp.exp(t2 - t1)
    g1 = p_grp / (1.0 + e2)
    comb_ref[...] = jnp.where(lane == i1, g1, 0.0) + jnp.where(lane == i2, g1 * e2, 0.0)


def moe_router(x, g, w_r1, b_r1, w_r2, b_r2):
    T, D = x.shape
    tm = min(T, 512)
    pad = LANES - MOE_EXPERTS - MOE_GROUPS
    w = jnp.concatenate([w_r2, w_r1, jnp.zeros((D, pad), F32)], axis=1)
    b = jnp.concatenate([b_r2, b_r1, jnp.zeros((pad,), F32)]).reshape(1, LANES)
    return pl.pallas_call(
        _router_kernel,
        grid=(T // tm,),
        in_specs=[pl.BlockSpec((tm, D), lambda i: (i, 0)),
                  pl.BlockSpec((1, D), lambda i: (0, 0)),
                  pl.BlockSpec((D, LANES), lambda i: (0, 0)),
                  pl.BlockSpec((1, LANES), lambda i: (0, 0))],
        out_specs=[pl.BlockSpec((tm, D), lambda i: (i, 0)),
                   pl.BlockSpec((tm, LANES), lambda i: (i, 0))],
        out_shape=[jax.ShapeDtypeStruct((T, D), BF16), jax.ShapeDtypeStruct((T, LANES), F32)],
        compiler_params=_cparams(1),
        name="moe_router",
    )(x, g.reshape(1, D), w, b)


def _experts_kernel(xn_ref, comb_ref, x_ref, wg_ref, wu_ref, wd_ref, gf_ref, o_ref, *, final_norm):
    e = pl.program_id(1)

    @pl.when(e == 0)
    def _():
        o_ref[...] = x_ref[...]

    xn = xn_ref[...]
    comb = comb_ref[...]
    lane = lax.broadcasted_iota(jnp.int32, comb.shape, 1)
    ce = jnp.sum(jnp.where(lane == e, comb, 0.0), axis=-1, keepdims=True)
    hid = jax.nn.silu(_dot(xn, wg_ref[...])) * _dot(xn, wu_ref[...]) * ce
    o_ref[...] += _dot(_bf(hid), wd_ref[...])

    if final_norm:
        @pl.when(e == pl.num_programs(1) - 1)
        def _():
            o_ref[...] = _rms(o_ref[...], gf_ref[...])


def moe_experts(xn, comb, x, wg, wu, wd, l, g_final, *, final_norm):
    T, D = x.shape
    tm = min(T, 512)
    row = pl.BlockSpec((tm, D), lambda i, e: (i, 0))
    w_in = pl.BlockSpec((None, None, D, MOE_HIDDEN), lambda i, e: (l, e, 0, 0))
    return pl.pallas_call(
        functools.partial(_experts_kernel, final_norm=final_norm),
        grid=(T // tm, MOE_EXPERTS),
        in_specs=[row,
                  pl.BlockSpec((tm, LANES), lambda i, e: (i, 0)),
                  row, w_in, w_in,
                  pl.BlockSpec((None, None, MOE_HIDDEN, D), lambda i, e: (l, e, 0, 0)),
                  pl.BlockSpec((1, D), lambda i, e: (0, 0))],
        out_specs=row,
        out_shape=jax.ShapeDtypeStruct((T, D), F32),
        compiler_params=_cparams(2),
        name="moe_experts",
    )(xn, comb, x, wg, wu, wd, g_final.reshape(1, D))


def _layer_weights(l, p):
    w_in = p["w_in"][l]
    c0 = RW_COLS
    c1 = c0 + ML_MAIN
    c2 = c1 + 2 * ML_HEADS
    c3 = c2 + POOL_WIDTH
    w_if = jnp.pad(w_in[:, c1:c2], ((0, 0), (0, LANES - 2 * ML_HEADS)))
    return dict(
        w_rw=_bf(w_in[:, :c0]), w_ml=_bf(w_in[:, c0:c1]), w_if=w_if, w_pool=_bf(w_in[:, c2:c3]),
        w_gate=_bf(w_in[:, c3:]),
        w_up_rwkv=_bf(p["w_up_rwkv"][l]), w_up_mlstm=_bf(p["w_up_mlstm"][l]), w_up_pool=_bf(p["w_up_pool"][l]),
        w_out=_bf(p["w_out"][l]), pool_w=_bf(p["pool_w"][l]),
        xa_wq=_bf(p["xa_wq"][l]), xa_wo=_bf(p["xa_wo"][l]),
        xa_wkv=_bf(jnp.concatenate([p["xa_wk"][l], p["xa_wv"][l]], axis=1)),
    )


def _trunk_layer(x, mem_k, mem_v, mem_l, st, carry, *, l, sl, p, lw, B, L, l_valid, start, final_norm):
    g_mix = p["g_mix"][l]
    u_rw = norm_matmul(x, g_mix, lw["w_rw"], tn=TN_RWKV, name="in_rwkv")
    u_ml = norm_matmul(x, g_mix, lw["w_ml"], tn=TN_MLSTM, name="in_mlstm")
    u_pl, gif = in_pool_gates(x, g_mix, lw["w_pool"], lw["w_if"])
    gates = norm_matmul(x, g_mix, lw["w_gate"], tn=TN_GATES, out_dtype=BF16, act="sigmoid", name="in_gates")

    o_rw, rw_s = rwkv7_mix(u_rw, st["rw_shift"], st["rw_s"], sl, l, carry["rw_s"], p["rw_mu"][l], p["rw_w0"][l],
                           p["rw_w2"][l], p["rw_a0"][l], p["rw_a2"][l], p["rw_k_k"][l], p["rw_k_a"][l],
                           p["rw_r_k"][l], p["rw_gn_w"][l], p["rw_gn_b"][l], B=B, L=L, l_valid=l_valid)
    rw_shift = u_rw.reshape(B, L, RW_COLS)[:, l_valid - 1]
    o_ml, ml_c, ml_n, ml_m = mlstm_mix(u_ml, gif, st["ml_c"], st["ml_n"], st["ml_m"], sl, l, carry["ml_c"],
                                       p["ml_b_i"][l], p["ml_b_f"][l], p["ml_gn_w"][l], B=B, L=L, l_valid=l_valid)
    o_pl, pool_buf = pool_mix(u_pl, st["pool"], sl, lw["pool_w"], p["pool_scale"][l], B=B, L=L, l_valid=l_valid,
                              start=start)
    merged = merge_branches(o_rw, o_ml, o_pl, gates, lw["w_up_rwkv"], lw["w_up_mlstm"], lw["w_up_pool"])
    x = matmul_residual(merged, lw["w_out"], x)

    q = norm_matmul(x, p["g_xa"][l], lw["xa_wq"], tn=XA_WIDTH, out_dtype=BF16, name="xa_q")
    att = cross_attention(q, mem_k, mem_v, mem_l, B=B, L=L)
    x = matmul_residual(att, lw["xa_wo"], x)

    xn, comb = moe_router(x, p["g_moe"][l], p["moe_wr1"][l], p["moe_br1"][l], p["moe_wr2"][l], p["moe_br2"][l])
    x = moe_experts(xn, comb, x, p["moe_wg_bf"], p["moe_wu_bf"], p["moe_wd_bf"], l, p["g_final"],
                    final_norm=final_norm)
    return x, dict(rw_s=rw_s, ml_c=ml_c), (rw_shift, ml_n, ml_m, pool_buf)


def kernel(x_prompt, x_sample, cache_mem_k, cache_mem_v, state_rwkv_s, state_rwkv_shift, state_mlstm_c, state_mlstm_n, state_mlstm_m, state_pool, mem_prompt, g_mix, w_in, rw_mu, rw_w0, rw_w2, rw_a0, rw_a2, rw_k_k, rw_k_a, rw_r_k, rw_gn_w, rw_gn_b, ml_b_i, ml_b_f, ml_gn_w, pool_w, pool_scale, w_up_rwkv, w_up_mlstm, w_up_pool, w_out, g_xa, g_mem, xa_wq, xa_wk, xa_wv, xa_wo, g_moe, moe_wr1, moe_br1, moe_wr2, moe_br2, moe_wg, moe_wu, moe_wd, g_final):
    p = dict(g_mix=g_mix, w_in=w_in, rw_mu=rw_mu, rw_w0=rw_w0, rw_w2=rw_w2, rw_a0=rw_a0, rw_a2=rw_a2, rw_k_k=rw_k_k,
             rw_k_a=rw_k_a, rw_r_k=rw_r_k, rw_gn_w=rw_gn_w, rw_gn_b=rw_gn_b, ml_b_i=ml_b_i, ml_b_f=ml_b_f,
             ml_gn_w=ml_gn_w, pool_w=pool_w, pool_scale=pool_scale, w_up_rwkv=w_up_rwkv, w_up_mlstm=w_up_mlstm,
             w_up_pool=w_up_pool, w_out=w_out, g_xa=g_xa, g_mem=g_mem, xa_wq=xa_wq, xa_wk=xa_wk, xa_wv=xa_wv,
             xa_wo=xa_wo, g_moe=g_moe, moe_wr1=moe_wr1, moe_br1=moe_br1, moe_wr2=moe_wr2, moe_br2=moe_br2,
             moe_wg_bf=_bf(moe_wg), moe_wu_bf=_bf(moe_wu), moe_wd_bf=_bf(moe_wd), g_final=g_final)
    Bp, Lp, D = x_prompt.shape
    Bs, Ls, _ = x_sample.shape
    Ls_pad = -(-Ls // SUBLANES) * SUBLANES
    yp = x_prompt.reshape(Bp * Lp, D)
    ys = jnp.pad(x_sample, ((0, 0), (0, Ls_pad - Ls), (0, 0))).reshape(Bs * Ls_pad, D)
    zeros = lambda *s: jnp.zeros((1,) + s, F32)
    st_p = dict(rw_shift=zeros(Bp, RW_COLS), rw_s=zeros(Bp, RW_HEADS, RW_HEAD, RW_HEAD),
                ml_c=zeros(Bp, ML_HEADS, ML_DQK, ML_DV), ml_n=zeros(Bp, ML_HEADS, ML_DQK), ml_m=zeros(Bp, ML_HEADS),
                pool=zeros(Bp, POOL_BUF, POOL_WIDTH))
    st_s = dict(rw_shift=state_rwkv_shift, rw_s=state_rwkv_s, ml_c=state_mlstm_c, ml_n=state_mlstm_n,
                ml_m=state_mlstm_m, pool=state_pool)
    carry_p = carry_s = dict(rw_s=None, ml_c=None)
    small_p = [[] for _ in range(6)]
    small_s = [[] for _ in range(4)]
    for l in range(DEPTH):
        lw = _layer_weights(l, p)
        final = l == DEPTH - 1
        kv = norm_matmul(mem_prompt.reshape(Bp * MEM_LEN, D), g_mem[l], lw["xa_wkv"], tn=2 * XA_WIDTH,
                         name="memory_kv")
        kv3 = kv.reshape(Bp, MEM_LEN, 2 * XA_WIDTH)
        yp, carry_p, small = _trunk_layer(yp, kv3, kv3, None, st_p, carry_p, l=l, sl=0, p=p, lw=lw, B=Bp, L=Lp,
                                          l_valid=Lp, start=0, final_norm=final)
        mk = kv[:, :XA_WIDTH].reshape(Bp, MEM_LEN, XA_HEADS, XA_HEAD_DIM)
        mv = kv[:, XA_WIDTH:].reshape(Bp, MEM_LEN, XA_HEADS, XA_HEAD_DIM)
        for acc, t in zip(small_p, small + (mk, mv)):
            acc.append(t)
        ys, carry_s, small = _trunk_layer(ys, cache_mem_k, cache_mem_v, l, st_s, carry_s, l=l, sl=l, p=p, lw=lw,
                                          B=Bs, L=Ls_pad, l_valid=Ls, start=PAST_LEN, final_norm=final)
        for acc, t in zip(small_s, small):
            acc.append(t)
    y_prompt = yp.reshape(Bp, Lp, D)
    y_sample = ys.reshape(Bs, Ls_pad, D)[:, :Ls]
    p_sh, p_n, p_m, p_pool, p_mk, p_mv = [jnp.stack(t) for t in small_p]
    s_sh, s_n, s_m, s_pool = [jnp.stack(t) for t in small_s]
    return (y_prompt, y_sample, carry_p["rw_s"], p_sh, carry_p["ml_c"], p_n, p_m, p_pool, p_mk, p_mv,
            carry_s["rw_s"], s_sh, carry_s["ml_c"], s_n, s_m, s_pool)
```

```python
import functools
import math

import jax
import jax.numpy as jnp
from jax import lax
from jax.experimental import pallas as pl
from jax.experimental.pallas import tpu as pltpu

F32 = jnp.float32
BF16 = jnp.bfloat16
HI = lax.Precision.HIGHEST

D_MODEL = 2048
DEPTH = 2
PAST_LEN = 16384
RW_HEAD = 64
RW_WIDTH = 1024
RW_HEADS = 16
RW_RANK = 64
RW_COLS = 3 * RW_WIDTH + 2 * RW_RANK
GN_EPS = 64e-5
ML_HEADS = 4
ML_DQK = 128
ML_DV = 256
ML_QK_WIDTH = 512
ML_V_WIDTH = 1024
ML_MAIN = 2 * ML_QK_WIDTH + 2 * ML_V_WIDTH
POOL_WIDTH = 1024
POOL_WINDOWS = (2, 4, 8, 16)
POOL_GW = 256
POOL_BUF = 15
POOL_HIST = 16
POOL_TAIL = 24
MEM_LEN = 256
XA_HEADS = 4
XA_HEAD_DIM = 128
XA_WIDTH = 512
MOE_GROUPS = 4
MOE_PER_GROUP = 8
MOE_EXPERTS = 32
MOE_HIDDEN = 256
RMS_EPS = 1e-6
LANES = 128
SUBLANES = 8
VMEM_LIMIT = 56 * 1024 * 1024
ROW_TILE = 1024
CHUNK_ROWS = 64
TN_RWKV = 640
TN_MLSTM = 1024
TN_GATES = 1024


def _cparams(n_axes):
    return pltpu.CompilerParams(dimension_semantics=("arbitrary",) * n_axes, vmem_limit_bytes=VMEM_LIMIT)


def _dot(a, b, precision=None):
    return jnp.dot(a, b, preferred_element_type=F32, precision=precision)


def _dot_nt(a, b, precision=None):
    return lax.dot_general(a, b, (((1,), (1,)), ((), ())), preferred_element_type=F32, precision=precision)


def _dot_tn(a, b, precision=None):
    return lax.dot_general(a, b, (((0,), (0,)), ((), ())), preferred_element_type=F32, precision=precision)


def _bf(x):
    return x.astype(BF16)


def _rms(x, g):
    return x * lax.rsqrt(jnp.mean(x * x, axis=-1, keepdims=True) + RMS_EPS) * g


def _log_sigmoid(x):
    return jnp.minimum(x, 0.0) - jnp.log1p(jnp.exp(-jnp.abs(x)))


def _softplus(x):
    return jnp.maximum(x, 0.0) + jnp.log1p(jnp.exp(-jnp.abs(x)))


def _seq_prefix_mask(n, c):
    r = lax.broadcasted_iota(jnp.int32, (n, n), 0)
    q = lax.broadcasted_iota(jnp.int32, (n, n), 1)
    return jnp.logical_and(q <= r, q // c == r // c)


def _layer_block(l, blk, idx):
    return pl.BlockSpec((None,) + tuple(blk), lambda b, i: (l,) + tuple(idx(b, i)))


def _norm_matmul_kernel(x_ref, g_ref, w_ref, o_ref, xn_ref, *, act):
    @pl.when(pl.program_id(1) == 0)
    def _():
        xn_ref[...] = _bf(_rms(x_ref[...], g_ref[...]))

    acc = _dot(xn_ref[...], w_ref[...])
    if act == "sigmoid":
        acc = jax.nn.sigmoid(acc)
    o_ref[...] = acc.astype(o_ref.dtype)


def norm_matmul(x, g, w, *, tn, name, out_dtype=F32, act=None):
    T, D = x.shape
    N = w.shape[1]
    tm = min(T, ROW_TILE)
    return pl.pallas_call(
        functools.partial(_norm_matmul_kernel, act=act),
        grid=(T // tm, N // tn),
        in_specs=[pl.BlockSpec((tm, D), lambda i, j: (i, 0)),
                  pl.BlockSpec((1, D), lambda i, j: (0, 0)),
                  pl.BlockSpec((D, tn), lambda i, j: (0, j))],
        out_specs=pl.BlockSpec((tm, tn), lambda i, j: (i, j)),
        out_shape=jax.ShapeDtypeStruct((T, N), out_dtype),
        scratch_shapes=[pltpu.VMEM((tm, D), BF16)],
        compiler_params=_cparams(2),
        name=name,
    )(x, g.reshape(1, D), w)


def _matmul_residual_kernel(a_ref, w_ref, r_ref, o_ref):
    o_ref[...] = r_ref[...] + _dot(a_ref[...], w_ref[...])


def matmul_residual(a, w, res, *, tn=1024):
    T, K = a.shape
    N = w.shape[1]
    tm = min(T, ROW_TILE)
    return pl.pallas_call(
        _matmul_residual_kernel,
        grid=(T // tm, N // tn),
        in_specs=[pl.BlockSpec((tm, K), lambda i, j: (i, 0)),
                  pl.BlockSpec((K, tn), lambda i, j: (0, j)),
                  pl.BlockSpec((tm, tn), lambda i, j: (i, j))],
        out_specs=pl.BlockSpec((tm, tn), lambda i, j: (i, j)),
        out_shape=jax.ShapeDtypeStruct((T, N), F32),
        compiler_params=_cparams(2),
        name="matmul_residual",
    )(a, w, res)


def _merge_kernel(orw_ref, oml_ref, opl_ref, g0_ref, g1_ref, g2_ref, wr_ref, wm_ref, wp_ref, o_ref):
    m = (g0_ref[...].astype(F32) * _dot(orw_ref[...], wr_ref[...])
         + g1_ref[...].astype(F32) * _dot(oml_ref[...], wm_ref[...])
         + g2_ref[...].astype(F32) * _dot(opl_ref[...], wp_ref[...]))
    o_ref[...] = _bf(m)


def merge_branches(o_rw, o_ml, o_pl, gates, w_rw, w_ml, w_pl, *, tn=1024):
    T, K = o_rw.shape
    tm = min(T, ROW_TILE)
    nj = D_MODEL // tn
    act = pl.BlockSpec((tm, K), lambda i, j: (i, 0))
    wsp = pl.BlockSpec((K, tn), lambda i, j: (0, j))
    gate = lambda b: pl.BlockSpec((tm, tn), lambda i, j, b=b: (i, b * nj + j))
    return pl.pallas_call(
        _merge_kernel,
        grid=(T // tm, nj),
        in_specs=[act, act, act, gate(0), gate(1), gate(2), wsp, wsp, wsp],
        out_specs=pl.BlockSpec((tm, tn), lambda i, j: (i, j)),
        out_shape=jax.ShapeDtypeStruct((T, D_MODEL), BF16),
        compiler_params=_cparams(2),
        name="merge_branches",
    )(o_rw, o_ml, o_pl, gates, gates, gates, w_rw, w_ml, w_pl)


def _pool_kernel(hist_ref, u_ref, w_ref, sc_ref, o_ref, tail_ref, e_ref, *, tl, bt, start):
    li = pl.program_id(1)

    @pl.when(li == 0)
    def _():
        e_ref[:, 0:POOL_HIST, :] = hist_ref[...]

    pos = start + li * tl + lax.broadcasted_iota(jnp.int32, (tl, 1), 0)
    for b in range(bt):
        u = u_ref[b]
        e_ref[b, POOL_HIST:, :] = u
        e = e_ref[b]
        s2 = e + pltpu.roll(e, 1, 0)
        s4 = s2[:, POOL_GW:] + pltpu.roll(s2[:, POOL_GW:], 2, 0)
        s8 = s4[:, POOL_GW:] + pltpu.roll(s4[:, POOL_GW:], 4, 0)
        s16 = s8[:, POOL_GW:] + pltpu.roll(s8[:, POOL_GW:], 8, 0)
        sums = (s2[:, :POOL_GW], s4[:, :POOL_GW], s8[:, :POOL_GW], s16)
        for g, win in enumerate(POOL_WINDOWS):
            cols = slice(g * POOL_GW, (g + 1) * POOL_GW)
            cnt = jnp.minimum(win, pos + 1).astype(F32)
            d = sums[g][POOL_HIST:, :] / cnt - u[:, cols]
            out = _dot(_bf(d), w_ref[g]) * sc_ref[:, cols]
            o_ref[b, :, cols] = out.astype(o_ref.dtype)
        tail_ref[b] = e[tl + POOL_HIST - POOL_TAIL:, :]
        e_ref[b, 0:POOL_HIST, :] = e[tl:, :]


def pool_mix(u_pl, buf, l, w_grp, scale, *, B, L, l_valid, start):
    tl = min(L, 256)
    bt = max(CHUNK_ROWS // L, 1)
    hist = jnp.pad(buf[l], ((0, 0), (POOL_HIST - POOL_BUF, 0), (0, 0)))
    u3 = u_pl.reshape(B, L, u_pl.shape[-1])
    out, tail = pl.pallas_call(
        functools.partial(_pool_kernel, tl=tl, bt=bt, start=start),
        grid=(B // bt, L // tl),
        in_specs=[pl.BlockSpec((bt, POOL_HIST, POOL_WIDTH), lambda b, i: (b, 0, 0)),
                  pl.BlockSpec((bt, tl, POOL_WIDTH), lambda b, i: (b, i, 0)),
                  pl.BlockSpec((4, POOL_GW, POOL_GW), lambda b, i: (0, 0, 0)),
                  pl.BlockSpec((1, POOL_WIDTH), lambda b, i: (0, 0))],
        out_specs=[pl.BlockSpec((bt, tl, POOL_WIDTH), lambda b, i: (b, i, 0)),
                   pl.BlockSpec((bt, POOL_TAIL, POOL_WIDTH), lambda b, i: (b, 0, 0))],
        out_shape=[jax.ShapeDtypeStruct((B, L, POOL_WIDTH), BF16),
                   jax.ShapeDtypeStruct((B, POOL_TAIL, POOL_WIDTH), F32)],
        scratch_shapes=[pltpu.VMEM((bt, tl + POOL_HIST, POOL_WIDTH), F32)],
        compiler_params=_cparams(2),
        name="pool_mix",
    )(hist, u3, w_grp, scale.reshape(1, POOL_WIDTH))
    pad = L - l_valid
    new_buf = tail[:, POOL_TAIL - pad - POOL_BUF:POOL_TAIL - pad, :]
    return out.reshape(B * L, POOL_WIDTH), new_buf


def _mlstm_kernel(*refs, c, bt, l_valid, has_carry):
    u_ref, gif_ref, c0_ref, n0_ref, m0_ref, bi_ref, bf_ref, gnw_ref = refs[:8]
    o_ref, cN_ref, nN_ref, mN_ref, c_s, n_s, m_s = refs[8 + has_carry:]
    ci = pl.program_id(1)

    @pl.when(ci == 0)
    def _():
        c_s[...] = c0_ref[...]
        n_s[...] = n0_ref[...]
        m_s[...] = m0_ref[...]

    N = bt * c
    gif = gif_ref[...].reshape(N, LANES)
    tok = lax.broadcasted_iota(jnp.int32, (N, 1), 0) % c
    valid = tok < l_valid
    ig_all = jnp.where(valid, gif + bi_ref[...], -jnp.inf)
    lf_all = jnp.where(valid, _log_sigmoid(gif + bf_ref[...]), 0.0)
    bcum_all = _dot(_seq_prefix_mask(N, c).astype(F32), lf_all, HI)
    ig_t = ig_all.T
    bcum_t = bcum_all.T
    ri = lax.broadcasted_iota(jnp.int32, (c, c), 0)
    causal = lax.broadcasted_iota(jnp.int32, (c, c), 1) <= ri
    st = []
    for b in range(bt):
        rs = slice(b * c, (b + 1) * c)
        for h in range(ML_HEADS):
            qs = slice(h * ML_DQK, (h + 1) * ML_DQK)
            ks = slice(ML_QK_WIDTH + h * ML_DQK, ML_QK_WIDTH + (h + 1) * ML_DQK)
            vs = slice(2 * ML_QK_WIDTH + h * ML_DV, 2 * ML_QK_WIDTH + (h + 1) * ML_DV)
            os_ = slice(2 * ML_QK_WIDTH + ML_V_WIDTH + h * ML_DV, 2 * ML_QK_WIDTH + ML_V_WIDTH + (h + 1) * ML_DV)
            i_c = ig_all[rs, h:h + 1]
            b_c = bcum_all[rs, ML_HEADS + h:ML_HEADS + h + 1]
            i_r = ig_t[h:h + 1, rs]
            b_r = bcum_t[ML_HEADS + h:ML_HEADS + h + 1, rs]
            m_prev = m_s[b, h]
            dlog = jnp.where(causal, b_c - b_r + i_r, -jnp.inf)
            inter = b_c + m_prev
            m_t = jnp.maximum(inter, jnp.max(dlog, axis=-1, keepdims=True))
            b_last = b_c[c - 1:c, :]
            s_log = b_last - b_c + i_c
            m_new = jnp.maximum(b_last + m_prev, jnp.max(s_log, axis=0, keepdims=True))
            q = u_ref[b, :, qs]
            k = u_ref[b, :, ks] * (ML_DQK ** -0.5)
            st.append(dict(b=b, h=h, hs=slice(h * ML_DV, (h + 1) * ML_DV), os=os_, q=q, k=k, qb=_bf(q), vb=_bf(u_ref[b, :, vs]),
                           dlog=dlog, m_t=m_t, w_prev=jnp.exp(inter - m_t), m_new=m_new,
                           kw=k * jnp.exp(s_log - m_new), wp=jnp.exp(b_last + m_prev - m_new)))
    for s in st:
        s["wts"] = jnp.exp(s["dlog"] - s["m_t"]) * _dot_nt(s["qb"], _bf(s["k"]))
    for s in st:
        b, h = s["b"], s["h"]
        c_prev = c_s[b, h]
        n_prev = n_s[b, h]
        num = s["w_prev"] * _dot(s["qb"], _bf(c_prev)) + _dot(_bf(s["wts"]), s["vb"])
        den = (s["w_prev"] * jnp.sum(s["q"] * n_prev, axis=-1, keepdims=True)
               + jnp.sum(s["wts"], axis=-1, keepdims=True))
        s["hh"] = num / jnp.maximum(jnp.abs(den), jnp.exp(-s["m_t"]))
        c_s[b, h] = s["wp"] * c_prev + _dot_tn(_bf(s["kw"]), s["vb"])
        n_s[b, h] = s["wp"] * n_prev + jnp.sum(s["kw"], axis=0, keepdims=True)
        m_s[b, h] = s["m_new"]
    for s in st:
        hh = s["hh"]
        hn = hh * lax.rsqrt(jnp.mean(hh * hh, axis=-1, keepdims=True) + RMS_EPS) * gnw_ref[:, s["hs"]]
        o_ref[s["b"], :, s["hs"]] =(hn * jax.nn.sigmoid(u_ref[s["b"], :, s["os"]])).astype(o_ref.dtype)

    @pl.when(ci == pl.num_programs(1) - 1)
    def _():
        cN_ref[...] = c_s[...]
        nN_ref[...] = n_s[...]
        mN_ref[...] = m_s[...]


def mlstm_mix(u_ml, u_pl, c0, n0, m0, l, l_out, c_carry, b_i, b_f, gn_w, *, B, L, l_valid):
    c = min(L, CHUNK_ROWS)
    bt = CHUNK_ROWS // c
    u3 = u_ml.reshape(B, L, ML_MAIN)
    g3 = u_pl.reshape(B, L, POOL_WIDTH + LANES)
    zeros = jnp.zeros((LANES - 2 * ML_HEADS,), F32)
    bi = jnp.concatenate([b_i, jnp.zeros((ML_HEADS,), F32), zeros]).reshape(1, LANES)
    bf = jnp.concatenate([jnp.zeros((ML_HEADS,), F32), b_f, zeros]).reshape(1, LANES)
    nl = n0.shape[0]
    c_blk, n_blk, m_blk = (bt, ML_HEADS, ML_DQK, ML_DV), (bt, ML_HEADS, 1, ML_DQK), (bt, ML_HEADS, 1, 1)
    at_b = lambda b, i: (b, 0, 0, 0)
    in_specs = [pl.BlockSpec((bt, c, ML_MAIN), lambda b, i: (b, i, 0)),
                pl.BlockSpec((bt, c, LANES), lambda b, i: (b, i, POOL_WIDTH // LANES)),
                _layer_block(l, c_blk, at_b), _layer_block(l, n_blk, at_b), _layer_block(l, m_blk, at_b),
                pl.BlockSpec((1, LANES), lambda b, i: (0, 0)),
                pl.BlockSpec((1, LANES), lambda b, i: (0, 0)),
                pl.BlockSpec((1, ML_V_WIDTH), lambda b, i: (0, 0))]
    args = [u3, g3, c0, n0.reshape(nl, B, ML_HEADS, 1, ML_DQK), m0.reshape(nl, B, ML_HEADS, 1, 1), bi, bf,
            gn_w.reshape(1, ML_V_WIDTH)]
    aliases = {}
    if c_carry is not None:
        in_specs.append(pl.BlockSpec(memory_space=pl.ANY))
        args.append(c_carry)
        aliases = {len(args) - 1: 1}
    out, cN, nN, mN = pl.pallas_call(
        functools.partial(_mlstm_kernel, c=c, bt=bt, l_valid=l_valid, has_carry=c_carry is not None),
        grid=(B // bt, L // c),
        in_specs=in_specs,
        out_specs=[pl.BlockSpec((bt, c, ML_V_WIDTH), lambda b, i: (b, i, 0)),
                   _layer_block(l_out, c_blk, at_b),
                   pl.BlockSpec(n_blk, at_b), pl.BlockSpec(m_blk, at_b)],
        out_shape=[jax.ShapeDtypeStruct((B, L, ML_V_WIDTH), BF16),
                   jax.ShapeDtypeStruct((DEPTH, B, ML_HEADS, ML_DQK, ML_DV), F32),
                   jax.ShapeDtypeStruct((B, ML_HEADS, 1, ML_DQK), F32),
                   jax.ShapeDtypeStruct((B, ML_HEADS, 1, 1), F32)],
        scratch_shapes=[pltpu.VMEM(c_blk, F32), pltpu.VMEM(n_blk, F32), pltpu.VMEM(m_blk, F32)],
        input_output_aliases=aliases,
        compiler_params=_cparams(2),
        name="mlstm_mix",
    )(*args)
    return (out.reshape(B * L, ML_V_WIDTH), cN, nN.reshape(B, ML_HEADS, ML_DQK), mN.reshape(B, ML_HEADS))


RW_CHAIN_GROUP = 16


def _rwkv_chain_group(chains, seq, s_s, o_ref, rk_ref, gnw_ref, gnb_ref, *, C):
    rowi = lax.broadcasted_iota(jnp.int32, (C, C), 0)
    coli = lax.broadcasted_iota(jnp.int32, (C, C), 1)
    upper = rowi < coli
    col2 = lax.broadcasted_iota(jnp.int32, (C, 2 * C), 1)
    incl2 = jnp.where(col2 >= C, col2 - C, col2) <= lax.broadcasted_iota(jnp.int32, (C, 2 * C), 0)
    n_sq = max(int(math.log2(C)), 1)
    st = []
    for b, h in chains:
        rs = slice(b * C, (b + 1) * C)
        sl = slice(h * RW_HEAD, (h + 1) * RW_HEAD)
        cut = lambda t, rs=rs, sl=sl: t[rs, sl]
        kk_h = cut(seq["kkv"])
        nrm = jnp.sqrt(jnp.sum(kk_h * kk_h, axis=-1, keepdims=True))
        kap = jnp.where(seq["valid"][rs], kk_h / jnp.maximum(nrm, 1e-12), 0.0)
        k_h, v_h, r_h = cut(seq["kmod"]), cut(seq["v"]), cut(seq["r"])
        gi = cut(seq["gi"])
        b_t = kap * cut(seq["a"]) * gi
        k_t = k_h * gi
        st.append(dict(b=b, h=h, sl=sl, k_h=k_h, v_h=v_h, r_h=r_h,
                       a_t=_bf(-kap * cut(seq["gp"])), r_t=_bf(r_h * cut(seq["g"])),
                       bk=jnp.concatenate([b_t, k_t], axis=0),
                       ge=seq["g"][(b + 1) * C - 1:(b + 1) * C, sl], s0=s_s[b, h]))
    for c in st:
        bkb = _bf(c["bk"])
        mt = _dot_nt(bkb, c["a_t"])
        c["pt"] = jnp.where(upper, mt[:C], 0.0)
        c["akt"] = _bf(jnp.where(upper, mt[C:], 0.0))
        c["a_r"] = jnp.where(incl2, _dot_nt(c["r_t"], bkb), 0.0)
        c["s0b"] = _bf(c["s0"])
        c["vb"] = _bf(c["v_h"])
    for c in st:
        c["xt"] = _dot_nt(c["s0b"], c["a_t"]) + _dot_tn(c["vb"], c["akt"])
    for _ in range(n_sq - 1):
        for c in st:
            z = _dot(_bf(jnp.concatenate([c["pt"], c["xt"]], axis=0)), _bf(c["pt"]))
            c["pt"] = z[:C]
            c["xt"] = c["xt"] + z[C:]
    for c in st:
        c["ut"] = _bf(c["xt"] + _dot(_bf(c["xt"]), _bf(c["pt"])))
    for c in st:
        a_r = c["a_r"]
        c["y"] = (_dot_nt(c["r_t"], c["s0b"]) + _dot_nt(_bf(a_r[:, :C]), c["ut"])
                  + _dot(_bf(a_r[:, C:]), c["vb"]))
        bkg = c["bk"] * c["ge"]
        s_s[c["b"], c["h"]] = (c["s0"] * c["ge"] + _dot(c["ut"], _bf(bkg[:C]))
                               + _dot_tn(c["vb"], _bf(bkg[C:])))
    for c in st:
        y, sl = c["y"], c["sl"]
        mean = jnp.mean(y, axis=-1, keepdims=True)
        yc = y - mean
        var = jnp.mean(yc * yc, axis=-1, keepdims=True)
        yn = yc * lax.rsqrt(var + GN_EPS) * gnw_ref[:, sl] + gnb_ref[:, sl]
        bonus = jnp.sum(c["r_h"] * c["k_h"] * rk_ref[:, sl], axis=-1, keepdims=True) * c["v_h"]
        o_ref[c["b"], :, sl] = (yn + bonus).astype(o_ref.dtype)


def _rwkv_kernel(*refs, C, bt, l_valid, has_carry):
    (u_ref, sh_ref, s0_ref, mu_ref, w0_ref, w2_ref, a0_ref, a2_ref, kk_ref, ka_ref, rk_ref, gnw_ref,
     gnb_ref) = refs[:13]
    o_ref, sN_ref, prev_s, s_s = refs[13 + has_carry:]
    ci = pl.program_id(1)

    @pl.when(ci == 0)
    def _():
        prev_s[...] = sh_ref[...]
        s_s[...] = s0_ref[...]

    N = bt * C
    u = u_ref[...].reshape(N, RW_COLS)
    tok = lax.broadcasted_iota(jnp.int32, (N, 1), 0) % C
    valid = tok < l_valid
    prev = jnp.concatenate([jnp.broadcast_to(prev_s[b], (C, RW_COLS)) for b in range(bt)], axis=0)
    u_prev = jnp.where(tok == 0, prev, pltpu.roll(u, 1, 0))
    for b in range(bt):
        prev_s[b] = u[(b + 1) * C - 1:(b + 1) * C, :]
    us = u + (u_prev - u) * mu_ref[...]
    W = RW_WIDTH
    k = us[:, W:2 * W]
    wd = us[:, 3 * W:3 * W + RW_RANK]
    ad = us[:, 3 * W + RW_RANK:3 * W + 2 * RW_RANK]
    xw = w0_ref[...] + _dot(jnp.tanh(wd), w2_ref[...], HI)
    log_w = -_softplus(-xw) - 0.5
    ld = jnp.where(valid, -jnp.exp(log_w), 0.0)
    a = jax.nn.sigmoid(a0_ref[...] + _dot(ad, a2_ref[...], HI))
    cum = _dot(_seq_prefix_mask(N, C).astype(F32), ld, HI)
    seq = dict(valid=valid, r=us[:, 0:W], v=jnp.where(valid, us[:, 2 * W:3 * W], 0.0), a=a,
               g=jnp.exp(cum), gi=jnp.exp(-cum), gp=jnp.exp(cum - ld), kkv=k * kk_ref[...],
               kmod=jnp.where(valid, k * (1.0 + (a - 1.0) * ka_ref[...]), 0.0))
    chains = [(b, h) for b in range(bt) for h in range(RW_HEADS)]
    group = RW_CHAIN_GROUP * bt
    for i in range(0, len(chains), group):
        _rwkv_chain_group(chains[i:i + group], seq, s_s, o_ref, rk_ref, gnw_ref, gnb_ref, C=C)

    @pl.when(ci == pl.num_programs(1) - 1)
    def _():
        sN_ref[...] = s_s[...]


def rwkv7_mix(u_rw, shift_prev, s_prev, l, l_out, s_carry, mu, w0, w2, a0, a2, k_k, k_a, r_k, gn_w, gn_b, *, B, L,
              l_valid):
    C = min(L, CHUNK_ROWS)
    bt = CHUNK_ROWS // C
    u3 = u_rw.reshape(B, L, RW_COLS)
    vec = lambda n: pl.BlockSpec((1, n), lambda b, i: (0, 0))
    row = lambda t: t.reshape(1, -1)
    s_blk = (bt, RW_HEADS, RW_HEAD, RW_HEAD)
    at_b = lambda b, i: (b, 0, 0, 0)
    in_specs = [pl.BlockSpec((bt, C, RW_COLS), lambda b, i: (b, i, 0)),
                _layer_block(l, (bt, 1, RW_COLS), lambda b, i: (b, 0, 0)),
                _layer_block(l, s_blk, at_b),
                vec(RW_COLS), vec(RW_WIDTH),
                pl.BlockSpec((RW_RANK, RW_WIDTH), lambda b, i: (0, 0)),
                vec(RW_WIDTH),
                pl.BlockSpec((RW_RANK, RW_WIDTH), lambda b, i: (0, 0)),
                vec(RW_WIDTH), vec(RW_WIDTH), vec(RW_WIDTH), vec(RW_WIDTH), vec(RW_WIDTH)]
    args = [u3, shift_prev.reshape(shift_prev.shape[0], B, 1, RW_COLS), s_prev, row(mu), row(w0), w2, row(a0), a2,
            row(k_k), row(k_a), row(r_k), row(gn_w), row(gn_b)]
    aliases = {}
    if s_carry is not None:
        in_specs.append(pl.BlockSpec(memory_space=pl.ANY))
        args.append(s_carry)
        aliases = {len(args) - 1: 1}
    out, sN = pl.pallas_call(
        functools.partial(_rwkv_kernel, C=C, bt=bt, l_valid=l_valid, has_carry=s_carry is not None),
        grid=(B // bt, L // C),
        in_specs=in_specs,
        out_specs=[pl.BlockSpec((bt, C, RW_WIDTH), lambda b, i: (b, i, 0)), _layer_block(l_out, s_blk, at_b)],
        out_shape=[jax.ShapeDtypeStruct((B, L, RW_WIDTH), BF16),
                   jax.ShapeDtypeStruct((DEPTH, B, RW_HEADS, RW_HEAD, RW_HEAD), F32)],
        scratch_shapes=[pltpu.VMEM((bt, 1, RW_COLS), F32), pltpu.VMEM(s_blk, F32)],
        input_output_aliases=aliases,
        compiler_params=_cparams(2),
        name="rwkv7_mix",
    )(*args)
    return out.reshape(B * L, RW_WIDTH), sN


ATTN_SEQS_PER_STEP = 8


def _attn_kernel(q_ref, k_ref, v_ref, o_ref, *, bt, packed):
    pairs = []
    for b in range(bt):
        for h in range(XA_HEADS):
            hs = slice(h * XA_HEAD_DIM, (h + 1) * XA_HEAD_DIM)
            vs = slice(XA_WIDTH + h * XA_HEAD_DIM, XA_WIDTH + (h + 1) * XA_HEAD_DIM)
            pairs.append((b, hs, (b, slice(None), hs) if packed else (b, slice(None), h),
                          (b, slice(None), vs) if packed else (b, slice(None), h)))
    s = [_dot_nt(q_ref[b, :, hs], _bf(k_ref[ki])) * (XA_HEAD_DIM ** -0.5) for b, hs, ki, _ in pairs]
    e = [jnp.exp(t - jnp.max(t, axis=-1, keepdims=True)) for t in s]
    p = [t / jnp.sum(t, axis=-1, keepdims=True) for t in e]
    for (b, hs, _, vi), pr in zip(pairs, p):
        o_ref[b, :, hs] = _dot(_bf(pr), _bf(v_ref[vi])).astype(o_ref.dtype)


def cross_attention(q, mem_k, mem_v, l, *, B, L):
    q3 = q.reshape(B, L, XA_WIDTH)
    if l is None:
        bt, tq = 1, min(L, 512)
        k_spec = v_spec = pl.BlockSpec((bt, MEM_LEN, 2 * XA_WIDTH), lambda b, i: (b, 0, 0))
    else:
        bt, tq = ATTN_SEQS_PER_STEP, L
        k_spec = v_spec = pl.BlockSpec((None, bt, MEM_LEN, XA_HEADS, XA_HEAD_DIM), lambda b, i: (l, b, 0, 0, 0))
    qo_spec = pl.BlockSpec((bt, tq, XA_WIDTH), lambda b, i: (b, i, 0))
    out = pl.pallas_call(
        functools.partial(_attn_kernel, bt=bt, packed=l is None),
        grid=(B // bt, L // tq),
        in_specs=[qo_spec, k_spec, v_spec],
        out_specs=qo_spec,
        out_shape=jax.ShapeDtypeStruct((B, L, XA_WIDTH), BF16),
        compiler_params=_cparams(2),
        name="cross_attention",
    )(q3, mem_k, mem_v)
    return out.reshape(B * L, XA_WIDTH)


ROUTER_GROUP_LANE = MOE_EXPERTS


MOE_TB = 512
MOE_SUB = 256
MOE_TE_LONG = 1024
MOE_TE_SHORT = 256
ROUTE_GROUP_LANE = 0
ROUTE_RANK_LANE = 1


def _router_kernel(x_ref, g_ref, w_ref, b_ref, xn_ref, comb_ref, route_ref, cnt_s):
    @pl.when(pl.program_id(0) == 0)
    def _():
        cnt_s[...] = jnp.zeros_like(cnt_s)

    xn = _rms(x_ref[...], g_ref[...])
    xn_ref[...] = _bf(xn)
    z = _dot(xn, w_ref[...], HI) + b_ref[...]
    tm = z.shape[0]
    lane = lax.broadcasted_iota(jnp.int32, z.shape, 1).astype(F32)
    big = float(LANES)
    neg = -jnp.inf
    first = lambda mask: jnp.min(jnp.where(mask, lane, big), axis=-1, keepdims=True)
    is_g = jnp.logical_and(lane >= ROUTER_GROUP_LANE, lane < ROUTER_GROUP_LANE + MOE_GROUPS)
    zg = jnp.where(is_g, z, neg)
    mg = jnp.max(zg, axis=-1, keepdims=True)
    grp = first(zg == mg) - ROUTER_GROUP_LANE
    p_grp = 1.0 / jnp.sum(jnp.exp(zg - mg), axis=-1, keepdims=True)
    lo = grp * MOE_PER_GROUP
    ze = jnp.where(jnp.logical_and(lane >= lo, lane < lo + MOE_PER_GROUP), z, neg)
    t1 = jnp.max(ze, axis=-1, keepdims=True)
    i1 = first(ze == t1)
    ze2 = jnp.where(lane == i1, neg, ze)
    t2 = jnp.max(ze2, axis=-1, keepdims=True)
    i2 = first(ze2 == t2)
    e2 = jnp.exp(t2 - t1)
    g1 = p_grp / (1.0 + e2)
    comb_ref[...] = jnp.where(lane == i1, g1, 0.0) + jnp.where(lane == i2, g1 * e2, 0.0)
    onehot = jnp.where(lane == grp, 1.0, 0.0)
    r = lax.broadcasted_iota(jnp.int32, (tm, tm), 0)
    c = lax.broadcasted_iota(jnp.int32, (tm, tm), 1)
    before = _dot(_bf(jnp.where(c < r, 1.0, 0.0)), _bf(onehot)) + cnt_s[...]
    rank = jnp.sum(onehot * before, axis=-1, keepdims=True)
    cnt_s[...] = cnt_s[...] + jnp.sum(onehot, axis=0, keepdims=True)
    route_ref[...] = jnp.where(lane == ROUTE_GROUP_LANE, grp, 0.0) + jnp.where(lane == ROUTE_RANK_LANE, rank, 0.0)


def moe_router(x, g, w_r1, b_r1, w_r2, b_r2):
    T, D = x.shape
    tm = MOE_TB
    pad = LANES - MOE_EXPERTS - MOE_GROUPS
    w = jnp.concatenate([w_r2, w_r1, jnp.zeros((D, pad), F32)], axis=1)
    b = jnp.concatenate([b_r2, b_r1, jnp.zeros((pad,), F32)]).reshape(1, LANES)
    return pl.pallas_call(
        _router_kernel,
        grid=(T // tm,),
        in_specs=[pl.BlockSpec((tm, D), lambda i: (i, 0)),
                  pl.BlockSpec((1, D), lambda i: (0, 0)),
                  pl.BlockSpec((D, LANES), lambda i: (0, 0)),
                  pl.BlockSpec((1, LANES), lambda i: (0, 0))],
        out_specs=[pl.BlockSpec((tm, D), lambda i: (i, 0)),
                   pl.BlockSpec((tm, LANES), lambda i: (i, 0)),
                   pl.BlockSpec((tm, LANES), lambda i: (i, 0))],
        out_shape=[jax.ShapeDtypeStruct((T, D), BF16), jax.ShapeDtypeStruct((T, LANES), F32),
                   jax.ShapeDtypeStruct((T, LANES), F32)],
        scratch_shapes=[pltpu.VMEM((1, LANES), F32)],
        compiler_params=_cparams(1),
        name="moe_router",
    )(x, g.reshape(1, D), w, b)


def _moe_plan(route, T, te):
    i32 = jnp.int32
    grp = route[:, ROUTE_GROUP_LANE].astype(i32)
    rank = route[:, ROUTE_RANK_LANE].astype(i32)
    nb = T // MOE_TB
    rows = T + MOE_GROUPS * te
    n_sub = rows // MOE_SUB
    n_tiles = rows // te
    onehot = (grp[:, None] == jnp.arange(MOE_GROUPS, dtype=i32)[None]).astype(i32)
    blk_cnt = onehot.reshape(nb, MOE_TB, MOE_GROUPS).sum(axis=1)
    cum_blk = jnp.concatenate([jnp.zeros((1, MOE_GROUPS), i32), jnp.cumsum(blk_cnt, axis=0)])
    seg_rows = (cum_blk[-1] + te - 1) // te * te
    seg_end = jnp.cumsum(seg_rows)
    seg_start = seg_end - seg_rows
    dest = seg_start[grp] + rank
    group_of = lambda row: jnp.minimum(jnp.sum(row[:, None] >= seg_end[None], axis=1), MOE_GROUPS - 1).astype(i32)
    sub_row = jnp.arange(n_sub, dtype=i32) * MOE_SUB
    sub_g = group_of(sub_row)
    sub_valid = sub_row < seg_end[-1]
    r0 = sub_row - seg_start[sub_g]
    lo = cum_blk[:-1][:, sub_g].T
    hi = cum_blk[1:][:, sub_g].T
    overlap = (lo < (r0 + MOE_SUB)[:, None]) & (hi > r0[:, None]) & sub_valid[:, None]
    first_col = (jnp.arange(nb) == 0)[None]
    g_mask = overlap | (first_col & ~overlap.any(axis=1, keepdims=True))
    n_items = MOE_GROUPS * nb + 2 * n_sub

    def items(mask, ncol):
        flat = jnp.nonzero(mask.reshape(-1), size=n_items, fill_value=-1)[0].astype(i32)
        valid = flat >= 0
        flat = jnp.where(valid, flat, jnp.max(flat))
        major, minor = flat // ncol, flat % ncol
        prev = jnp.concatenate([jnp.full((1,), -1, i32), major[:-1]])
        nxt = jnp.concatenate([major[1:], jnp.full((1,), -1, i32)])
        nvalid = jnp.concatenate([valid[1:], jnp.zeros((1,), bool)])
        first = (major != prev) & valid
        last = ((major != nxt) | ~nvalid) & valid
        return major, minor, first.astype(i32), last.astype(i32), valid.astype(i32)

    g_sub, g_blk, g_first, _, g_valid = items(g_mask, nb)
    s_blk, s_sub, s_first, s_last, s_valid = items(overlap.T, n_sub)
    tile_row = jnp.arange(n_tiles, dtype=i32) * te
    return dict(dest=dest, rows=rows, gather=(g_sub, g_blk, g_first, g_valid),
                scatter=(s_sub, s_blk, s_first, s_last, s_valid),
                tile_group=group_of(tile_row), tile_valid=(tile_row < seg_end[-1]).astype(i32))


def _moe_gather_kernel(sub_ref, blk_ref, first_ref, valid_ref, dest_ref, xn_ref, comb_ref, xs_ref, cs_ref,
                       acc_x, acc_c):
    w = pl.program_id(0)

    @pl.when(valid_ref[w] == 1)
    def _():
        rows = sub_ref[w] * MOE_SUB + lax.broadcasted_iota(jnp.int32, (MOE_SUB, MOE_TB), 0)
        hit = dest_ref[0] == rows
        gx = _dot(_bf(jnp.where(hit, 1.0, 0.0)), xn_ref[...])
        gc = _dot(jnp.where(hit, 1.0, 0.0), comb_ref[...], HI)

        @pl.when(first_ref[w] == 1)
        def _():
            acc_x[...] = gx
            acc_c[...] = gc

        @pl.when(first_ref[w] == 0)
        def _():
            acc_x[...] += gx
            acc_c[...] += gc

        xs_ref[...] = _bf(acc_x[...])
        cs_ref[...] = acc_c[...]


def moe_gather(xn, comb, plan):
    T, D = xn.shape
    g_sub, g_blk, g_first, g_valid = plan["gather"]
    rows = plan["rows"]
    dest3 = plan["dest"].reshape(T // MOE_TB, 1, MOE_TB)
    grid_spec = pltpu.PrefetchScalarGridSpec(
        num_scalar_prefetch=4,
        grid=(g_sub.shape[0],),
        in_specs=[pl.BlockSpec((1, 1, MOE_TB), lambda w, s, b, f, v: (b[w], 0, 0)),
                  pl.BlockSpec((MOE_TB, D), lambda w, s, b, f, v: (b[w], 0)),
                  pl.BlockSpec((MOE_TB, LANES), lambda w, s, b, f, v: (b[w], 0))],
        out_specs=[pl.BlockSpec((MOE_SUB, D), lambda w, s, b, f, v: (s[w], 0)),
                   pl.BlockSpec((MOE_SUB, LANES), lambda w, s, b, f, v: (s[w], 0))],
        scratch_shapes=[pltpu.VMEM((MOE_SUB, D), F32), pltpu.VMEM((MOE_SUB, LANES), F32)])
    return pl.pallas_call(
        _moe_gather_kernel,
        grid_spec=grid_spec,
        out_shape=[jax.ShapeDtypeStruct((rows, D), BF16), jax.ShapeDtypeStruct((rows, LANES), F32)],
        compiler_params=_cparams(1),
        name="moe_gather",
    )(g_sub, g_blk, g_first, g_valid, dest3, xn, comb)


def _moe_group_experts_kernel(tg_ref, tv_ref, xs_ref, cs_ref, wg_ref, wu_ref, wd_ref, ys_ref):
    i = pl.program_id(0)
    e = pl.program_id(1)

    @pl.when(e == 0)
    def _():
        ys_ref[...] = jnp.zeros_like(ys_ref)

    @pl.when(tv_ref[i] == 1)
    def _():
        xs = xs_ref[...]
        cs = cs_ref[...]
        lane = lax.broadcasted_iota(jnp.int32, cs.shape, 1)
        ce = jnp.sum(jnp.where(lane == tg_ref[i] * MOE_PER_GROUP + e, cs, 0.0), axis=-1, keepdims=True)
        hid = jax.nn.silu(_dot(xs, _bf(wg_ref[...]))) * _dot(xs, _bf(wu_ref[...])) * ce
        ys_ref[...] += _dot(_bf(hid), _bf(wd_ref[...]))


def moe_group_experts(xs, cs, plan, wg, wu, wd, l, te):
    rows, D = xs.shape
    expert = lambda i, e, tg, tv: (l, tg[i] * MOE_PER_GROUP + e, 0, 0)
    w_in = pl.BlockSpec((None, None, D, MOE_HIDDEN), expert)
    grid_spec = pltpu.PrefetchScalarGridSpec(
        num_scalar_prefetch=2,
        grid=(rows // te, MOE_PER_GROUP),
        in_specs=[pl.BlockSpec((te, D), lambda i, e, tg, tv: (i, 0)),
                  pl.BlockSpec((te, LANES), lambda i, e, tg, tv: (i, 0)),
                  w_in, w_in,
                  pl.BlockSpec((None, None, MOE_HIDDEN, D), expert)],
        out_specs=pl.BlockSpec((te, D), lambda i, e, tg, tv: (i, 0)))
    return pl.pallas_call(
        _moe_group_experts_kernel,
        grid_spec=grid_spec,
        out_shape=jax.ShapeDtypeStruct((rows, D), F32),
        compiler_params=_cparams(2),
        name="moe_group_experts",
    )(plan["tile_group"], plan["tile_valid"], xs, cs, wg, wu, wd)


def _moe_scatter_kernel(sub_ref, blk_ref, first_ref, last_ref, valid_ref, dest_ref, ys_ref, x_ref, gf_ref, o_ref,
                        *, final_norm):
    w = pl.program_id(0)

    @pl.when(valid_ref[w] == 1)
    def _():
        cols = sub_ref[w] * MOE_SUB + lax.broadcasted_iota(jnp.int32, (MOE_TB, MOE_SUB), 1)
        pick = _bf(jnp.where(dest_ref[...] == cols, 1.0, 0.0))
        ys = ys_ref[...]
        hi = _bf(ys)
        lo = _bf(ys - hi.astype(F32))
        upd = _dot(pick, hi) + _dot(pick, lo)

        @pl.when(first_ref[w] == 1)
        def _():
            o_ref[...] = x_ref[...] + upd

        @pl.when(first_ref[w] == 0)
        def _():
            o_ref[...] += upd

        if final_norm:
            @pl.when(last_ref[w] == 1)
            def _():
                o_ref[...] = _rms(o_ref[...], gf_ref[...])


def moe_scatter(ys, x, plan, g_final, *, final_norm):
    T, D = x.shape
    s_sub, s_blk, s_first, s_last, s_valid = plan["scatter"]
    dest_col = plan["dest"].reshape(T, 1)
    im = lambda f: (lambda w, s, b, fi, la, v: f(w, s, b))
    grid_spec = pltpu.PrefetchScalarGridSpec(
        num_scalar_prefetch=5,
        grid=(s_sub.shape[0],),
        in_specs=[pl.BlockSpec((MOE_TB, 1), im(lambda w, s, b: (b[w], 0))),
                  pl.BlockSpec((MOE_SUB, D), im(lambda w, s, b: (s[w], 0))),
                  pl.BlockSpec((MOE_TB, D), im(lambda w, s, b: (b[w], 0))),
                  pl.BlockSpec((1, D), im(lambda w, s, b: (0, 0)))],
        out_specs=pl.BlockSpec((MOE_TB, D), im(lambda w, s, b: (b[w], 0))))
    return pl.pallas_call(
        functools.partial(_moe_scatter_kernel, final_norm=final_norm),
        grid_spec=grid_spec,
        out_shape=jax.ShapeDtypeStruct((T, D), F32),
        compiler_params=_cparams(1),
        name="moe_scatter",
    )(s_sub, s_blk, s_first, s_last, s_valid, dest_col, ys, x, g_final.reshape(1, D))


def hmoe_block(x, p, l, *, final_norm):
    T = x.shape[0]
    te = MOE_TE_LONG if T >= 4 * MOE_TE_LONG else MOE_TE_SHORT
    xn, comb, route = moe_router(x, p["g_moe"][l], p["moe_wr1"][l], p["moe_br1"][l], p["moe_wr2"][l],
                                 p["moe_br2"][l])
    plan = _moe_plan(route, T, te)
    xs, cs = moe_gather(xn, comb, plan)
    ys = moe_group_experts(xs, cs, plan, p["moe_wg"], p["moe_wu"], p["moe_wd"], l, te)
    return moe_scatter(ys, x, plan, p["g_final"], final_norm=final_norm)


def _layer_weights(l, p):
    w_in = p["w_in"][l]
    c0 = RW_COLS
    c1 = c0 + ML_MAIN
    c2 = c1 + 2 * ML_HEADS
    c3 = c2 + POOL_WIDTH
    w_pl = jnp.concatenate([w_in[:, c2:c3], w_in[:, c1:c2], jnp.zeros((D_MODEL, LANES - 2 * ML_HEADS), F32)], axis=1)
    return dict(
        w_rw=_bf(w_in[:, :c0]), w_ml=_bf(w_in[:, c0:c1]), w_pl=_bf(w_pl), w_gate=_bf(w_in[:, c3:]),
        w_up_rwkv=_bf(p["w_up_rwkv"][l]), w_up_mlstm=_bf(p["w_up_mlstm"][l]), w_up_pool=_bf(p["w_up_pool"][l]),
        w_out=_bf(p["w_out"][l]), pool_w=_bf(p["pool_w"][l]),
        xa_wq=_bf(p["xa_wq"][l]), xa_wo=_bf(p["xa_wo"][l]),
        xa_wkv=_bf(jnp.concatenate([p["xa_wk"][l], p["xa_wv"][l]], axis=1)),
    )


def _trunk_layer(x, mem_k, mem_v, mem_l, st, carry, *, l, sl, p, lw, B, L, l_valid, start, final_norm):
    g_mix = p["g_mix"][l]
    u_rw = norm_matmul(x, g_mix, lw["w_rw"], tn=TN_RWKV, name="in_rwkv")
    u_ml = norm_matmul(x, g_mix, lw["w_ml"], tn=TN_MLSTM, name="in_mlstm")
    u_pl = norm_matmul(x, g_mix, lw["w_pl"], tn=POOL_WIDTH + LANES, name="in_pool")
    gates = norm_matmul(x, g_mix, lw["w_gate"], tn=TN_GATES, out_dtype=BF16, act="sigmoid", name="in_gates")

    o_rw, rw_s = rwkv7_mix(u_rw, st["rw_shift"], st["rw_s"], sl, l, carry["rw_s"], p["rw_mu"][l], p["rw_w0"][l],
                           p["rw_w2"][l], p["rw_a0"][l], p["rw_a2"][l], p["rw_k_k"][l], p["rw_k_a"][l],
                           p["rw_r_k"][l], p["rw_gn_w"][l], p["rw_gn_b"][l], B=B, L=L, l_valid=l_valid)
    rw_shift = u_rw.reshape(B, L, RW_COLS)[:, l_valid - 1]
    o_ml, ml_c, ml_n, ml_m = mlstm_mix(u_ml, u_pl, st["ml_c"], st["ml_n"], st["ml_m"], sl, l, carry["ml_c"],
                                       p["ml_b_i"][l], p["ml_b_f"][l], p["ml_gn_w"][l], B=B, L=L, l_valid=l_valid)
    o_pl, pool_buf = pool_mix(u_pl, st["pool"], sl, lw["pool_w"], p["pool_scale"][l], B=B, L=L, l_valid=l_valid,
                              start=start)
    merged = merge_branches(o_rw, o_ml, o_pl, gates, lw["w_up_rwkv"], lw["w_up_mlstm"], lw["w_up_pool"])
    x = matmul_residual(merged, lw["w_out"], x)

    q = norm_matmul(x, p["g_xa"][l], lw["xa_wq"], tn=XA_WIDTH, out_dtype=BF16, name="xa_q")
    att = cross_attention(q, mem_k, mem_v, mem_l, B=B, L=L)
    x = matmul_residual(att, lw["xa_wo"], x)

    x = hmoe_block(x, p, l, final_norm=final_norm)
    return x, dict(rw_s=rw_s, ml_c=ml_c), (rw_shift, ml_n, ml_m, pool_buf)


def kernel(x_prompt, x_sample, cache_mem_k, cache_mem_v, state_rwkv_s, state_rwkv_shift, state_mlstm_c, state_mlstm_n, state_mlstm_m, state_pool, mem_prompt, g_mix, w_in, rw_mu, rw_w0, rw_w2, rw_a0, rw_a2, rw_k_k, rw_k_a, rw_r_k, rw_gn_w, rw_gn_b, ml_b_i, ml_b_f, ml_gn_w, pool_w, pool_scale, w_up_rwkv, w_up_mlstm, w_up_pool, w_out, g_xa, g_mem, xa_wq, xa_wk, xa_wv, xa_wo, g_moe, moe_wr1, moe_br1, moe_wr2, moe_br2, moe_wg, moe_wu, moe_wd, g_final):
    p = dict(g_mix=g_mix, w_in=w_in, rw_mu=rw_mu, rw_w0=rw_w0, rw_w2=rw_w2, rw_a0=rw_a0, rw_a2=rw_a2, rw_k_k=rw_k_k,
             rw_k_a=rw_k_a, rw_r_k=rw_r_k, rw_gn_w=rw_gn_w, rw_gn_b=rw_gn_b, ml_b_i=ml_b_i, ml_b_f=ml_b_f,
             ml_gn_w=ml_gn_w, pool_w=pool_w, pool_scale=pool_scale, w_up_rwkv=w_up_rwkv, w_up_mlstm=w_up_mlstm,
             w_up_pool=w_up_pool, w_out=w_out, g_xa=g_xa, g_mem=g_mem, xa_wq=xa_wq, xa_wk=xa_wk, xa_wv=xa_wv,
             xa_wo=xa_wo, g_moe=g_moe, moe_wr1=moe_wr1, moe_br1=moe_br1, moe_wr2=moe_wr2, moe_br2=moe_br2,
             moe_wg=moe_wg, moe_wu=moe_wu, moe_wd=moe_wd, g_final=g_final)
    Bp, Lp, D = x_prompt.shape
    Bs, Ls, _ = x_sample.shape
    Ls_pad = -(-Ls // SUBLANES) * SUBLANES
    yp = x_prompt.reshape(Bp * Lp, D)
    ys = jnp.pad(x_sample, ((0, 0), (0, Ls_pad - Ls), (0, 0))).reshape(Bs * Ls_pad, D)
    zeros = lambda *s: jnp.zeros((1,) + s, F32)
    st_p = dict(rw_shift=zeros(Bp, RW_COLS), rw_s=zeros(Bp, RW_HEADS, RW_HEAD, RW_HEAD),
                ml_c=zeros(Bp, ML_HEADS, ML_DQK, ML_DV), ml_n=zeros(Bp, ML_HEADS, ML_DQK), ml_m=zeros(Bp, ML_HEADS),
                pool=zeros(Bp, POOL_BUF, POOL_WIDTH))
    st_s = dict(rw_shift=state_rwkv_shift, rw_s=state_rwkv_s, ml_c=state_mlstm_c, ml_n=state_mlstm_n,
                ml_m=state_mlstm_m, pool=state_pool)
    carry_p = carry_s = dict(rw_s=None, ml_c=None)
    small_p = [[] for _ in range(6)]
    small_s = [[] for _ in range(4)]
    for l in range(DEPTH):
        lw = _layer_weights(l, p)
        final = l == DEPTH - 1
        kv = norm_matmul(mem_prompt.reshape(Bp * MEM_LEN, D), g_mem[l], lw["xa_wkv"], tn=2 * XA_WIDTH,
                         name="memory_kv")
        kv3 = kv.reshape(Bp, MEM_LEN, 2 * XA_WIDTH)
        yp, carry_p, small = _trunk_layer(yp, kv3, kv3, None, st_p, carry_p, l=l, sl=0, p=p, lw=lw, B=Bp, L=Lp,
                                          l_valid=Lp, start=0, final_norm=final)
        mk = kv[:, :XA_WIDTH].reshape(Bp, MEM_LEN, XA_HEADS, XA_HEAD_DIM)
        mv = kv[:, XA_WIDTH:].reshape(Bp, MEM_LEN, XA_HEADS, XA_HEAD_DIM)
        for acc, t in zip(small_p, small + (mk, mv)):
            acc.append(t)
        ys, carry_s, small = _trunk_layer(ys, cache_mem_k, cache_mem_v, l, st_s, carry_s, l=l, sl=l, p=p, lw=lw,
                                          B=Bs, L=Ls_pad, l_valid=Ls, start=PAST_LEN, final_norm=final)
        for acc, t in zip(small_s, small):
            acc.append(t)
    y_prompt = yp.reshape(Bp, Lp, D)
    y_sample = ys.reshape(Bs, Ls_pad, D)[:, :Ls]
    p_sh, p_n, p_m, p_pool, p_mk, p_mv = [jnp.stack(t) for t in small_p]
    s_sh, s_n, s_m, s_pool = [jnp.stack(t) for t in small_s]
    return (y_prompt, y_sample, carry_p["rw_s"], p_sh, carry_p["ml_c"], p_n, p_m, p_pool, p_mk, p_mv,
            carry_s["rw_s"], s_sh, carry_s["ml_c"], s_n, s_m, s_pool)
```

```python
import functools
import math

import jax
import jax.numpy as jnp
from jax import lax
from jax.experimental import pallas as pl
from jax.experimental.pallas import tpu as pltpu

F32 = jnp.float32
BF16 = jnp.bfloat16
HI = lax.Precision.HIGHEST

D_MODEL = 2048
DEPTH = 2
PAST_LEN = 16384
RW_HEAD = 64
RW_WIDTH = 1024
RW_HEADS = 16
RW_RANK = 64
RW_COLS = 3 * RW_WIDTH + 2 * RW_RANK
GN_EPS = 64e-5
ML_HEADS = 4
ML_DQK = 128
ML_DV = 256
ML_QK_WIDTH = 512
ML_V_WIDTH = 1024
ML_MAIN = 2 * ML_QK_WIDTH + 2 * ML_V_WIDTH
POOL_WIDTH = 1024
POOL_WINDOWS = (2, 4, 8, 16)
POOL_GW = 256
POOL_BUF = 15
POOL_HIST = 16
POOL_TAIL = 24
MEM_LEN = 256
XA_HEADS = 4
XA_HEAD_DIM = 128
XA_WIDTH = 512
MOE_GROUPS = 4
MOE_PER_GROUP = 8
MOE_EXPERTS = 32
MOE_HIDDEN = 256
RMS_EPS = 1e-6
LANES = 128
SUBLANES = 8
VMEM_LIMIT = 56 * 1024 * 1024
ROW_TILE = 1024
CHUNK_ROWS = 64
TN_RWKV = 640
TN_MLSTM = 1024
TN_GATES = 1024


def _cparams(n_axes):
    return pltpu.CompilerParams(dimension_semantics=("arbitrary",) * n_axes, vmem_limit_bytes=VMEM_LIMIT)


def _dot(a, b, precision=None):
    return jnp.dot(a, b, preferred_element_type=F32, precision=precision)


def _dot_nt(a, b, precision=None):
    return lax.dot_general(a, b, (((1,), (1,)), ((), ())), preferred_element_type=F32, precision=precision)


def _dot_tn(a, b, precision=None):
    return lax.dot_general(a, b, (((0,), (0,)), ((), ())), preferred_element_type=F32, precision=precision)


def _bf(x):
    return x.astype(BF16)


def _rms(x, g):
    return x * lax.rsqrt(jnp.mean(x * x, axis=-1, keepdims=True) + RMS_EPS) * g


def _log_sigmoid(x):
    return jnp.minimum(x, 0.0) - jnp.log1p(jnp.exp(-jnp.abs(x)))


def _softplus(x):
    return jnp.maximum(x, 0.0) + jnp.log1p(jnp.exp(-jnp.abs(x)))


def _seq_prefix_mask(n, c):
    r = lax.broadcasted_iota(jnp.int32, (n, n), 0)
    q = lax.broadcasted_iota(jnp.int32, (n, n), 1)
    return jnp.logical_and(q <= r, q // c == r // c)


def _layer_block(l, blk, idx):
    return pl.BlockSpec((None,) + tuple(blk), lambda b, i: (l,) + tuple(idx(b, i)))


def _norm_matmul_kernel(x_ref, g_ref, w_ref, o_ref, xn_ref, *, act):
    @pl.when(pl.program_id(1) == 0)
    def _():
        xn_ref[...] = _bf(_rms(x_ref[...], g_ref[...]))

    acc = _dot(xn_ref[...], w_ref[...])
    if act == "sigmoid":
        acc = jax.nn.sigmoid(acc)
    o_ref[...] = acc.astype(o_ref.dtype)


def norm_matmul(x, g, w, *, tn, name, out_dtype=F32, act=None):
    T, D = x.shape
    N = w.shape[1]
    tm = min(T, ROW_TILE)
    return pl.pallas_call(
        functools.partial(_norm_matmul_kernel, act=act),
        grid=(T // tm, N // tn),
        in_specs=[pl.BlockSpec((tm, D), lambda i, j: (i, 0)),
                  pl.BlockSpec((1, D), lambda i, j: (0, 0)),
                  pl.BlockSpec((D, tn), lambda i, j: (0, j))],
        out_specs=pl.BlockSpec((tm, tn), lambda i, j: (i, j)),
        out_shape=jax.ShapeDtypeStruct((T, N), out_dtype),
        scratch_shapes=[pltpu.VMEM((tm, D), BF16)],
        compiler_params=_cparams(2),
        name=name,
    )(x, g.reshape(1, D), w)


def _matmul_residual_kernel(a_ref, w_ref, r_ref, o_ref):
    o_ref[...] = r_ref[...] + _dot(a_ref[...], w_ref[...])


def matmul_residual(a, w, res, *, tn=1024):
    T, K = a.shape
    N = w.shape[1]
    tm = min(T, ROW_TILE)
    return pl.pallas_call(
        _matmul_residual_kernel,
        grid=(T // tm, N // tn),
        in_specs=[pl.BlockSpec((tm, K), lambda i, j: (i, 0)),
                  pl.BlockSpec((K, tn), lambda i, j: (0, j)),
                  pl.BlockSpec((tm, tn), lambda i, j: (i, j))],
        out_specs=pl.BlockSpec((tm, tn), lambda i, j: (i, j)),
        out_shape=jax.ShapeDtypeStruct((T, N), F32),
        compiler_params=_cparams(2),
        name="matmul_residual",
    )(a, w, res)


def _merge_kernel(orw_ref, oml_ref, opl_ref, g0_ref, g1_ref, g2_ref, wr_ref, wm_ref, wp_ref, o_ref):
    m = (g0_ref[...].astype(F32) * _dot(orw_ref[...], wr_ref[...])
         + g1_ref[...].astype(F32) * _dot(oml_ref[...], wm_ref[...])
         + g2_ref[...].astype(F32) * _dot(opl_ref[...], wp_ref[...]))
    o_ref[...] = _bf(m)


def merge_branches(o_rw, o_ml, o_pl, gates, w_rw, w_ml, w_pl, *, tn=1024):
    T, K = o_rw.shape
    tm = min(T, ROW_TILE)
    nj = D_MODEL // tn
    act = pl.BlockSpec((tm, K), lambda i, j: (i, 0))
    wsp = pl.BlockSpec((K, tn), lambda i, j: (0, j))
    gate = lambda b: pl.BlockSpec((tm, tn), lambda i, j, b=b: (i, b * nj + j))
    return pl.pallas_call(
        _merge_kernel,
        grid=(T // tm, nj),
        in_specs=[act, act, act, gate(0), gate(1), gate(2), wsp, wsp, wsp],
        out_specs=pl.BlockSpec((tm, tn), lambda i, j: (i, j)),
        out_shape=jax.ShapeDtypeStruct((T, D_MODEL), BF16),
        compiler_params=_cparams(2),
        name="merge_branches",
    )(o_rw, o_ml, o_pl, gates, gates, gates, w_rw, w_ml, w_pl)


def _pool_kernel(hist_ref, u_ref, w_ref, sc_ref, o_ref, tail_ref, e_ref, *, tl, bt, start):
    li = pl.program_id(1)

    @pl.when(li == 0)
    def _():
        e_ref[:, 0:POOL_HIST, :] = hist_ref[...]

    pos = start + li * tl + lax.broadcasted_iota(jnp.int32, (tl, 1), 0)
    for b in range(bt):
        u = u_ref[b]
        e_ref[b, POOL_HIST:, :] = u
        e = e_ref[b]
        s2 = e + pltpu.roll(e, 1, 0)
        s4 = s2[:, POOL_GW:] + pltpu.roll(s2[:, POOL_GW:], 2, 0)
        s8 = s4[:, POOL_GW:] + pltpu.roll(s4[:, POOL_GW:], 4, 0)
        s16 = s8[:, POOL_GW:] + pltpu.roll(s8[:, POOL_GW:], 8, 0)
        sums = (s2[:, :POOL_GW], s4[:, :POOL_GW], s8[:, :POOL_GW], s16)
        for g, win in enumerate(POOL_WINDOWS):
            cols = slice(g * POOL_GW, (g + 1) * POOL_GW)
            cnt = jnp.minimum(win, pos + 1).astype(F32)
            d = sums[g][POOL_HIST:, :] / cnt - u[:, cols]
            out = _dot(_bf(d), w_ref[g]) * sc_ref[:, cols]
            o_ref[b, :, cols] = out.astype(o_ref.dtype)
        tail_ref[b] = e[tl + POOL_HIST - POOL_TAIL:, :]
        e_ref[b, 0:POOL_HIST, :] = e[tl:, :]


def pool_mix(u_pl, buf, l, w_grp, scale, *, B, L, l_valid, start):
    tl = min(L, 256)
    bt = max(CHUNK_ROWS // L, 1)
    hist = jnp.pad(buf[l], ((0, 0), (POOL_HIST - POOL_BUF, 0), (0, 0)))
    u3 = u_pl.reshape(B, L, u_pl.shape[-1])
    out, tail = pl.pallas_call(
        functools.partial(_pool_kernel, tl=tl, bt=bt, start=start),
        grid=(B // bt, L // tl),
        in_specs=[pl.BlockSpec((bt, POOL_HIST, POOL_WIDTH), lambda b, i: (b, 0, 0)),
                  pl.BlockSpec((bt, tl, POOL_WIDTH), lambda b, i: (b, i, 0)),
                  pl.BlockSpec((4, POOL_GW, POOL_GW), lambda b, i: (0, 0, 0)),
                  pl.BlockSpec((1, POOL_WIDTH), lambda b, i: (0, 0))],
        out_specs=[pl.BlockSpec((bt, tl, POOL_WIDTH), lambda b, i: (b, i, 0)),
                   pl.BlockSpec((bt, POOL_TAIL, POOL_WIDTH), lambda b, i: (b, 0, 0))],
        out_shape=[jax.ShapeDtypeStruct((B, L, POOL_WIDTH), BF16),
                   jax.ShapeDtypeStruct((B, POOL_TAIL, POOL_WIDTH), F32)],
        scratch_shapes=[pltpu.VMEM((bt, tl + POOL_HIST, POOL_WIDTH), F32)],
        compiler_params=_cparams(2),
        name="pool_mix",
    )(hist, u3, w_grp, scale.reshape(1, POOL_WIDTH))
    pad = L - l_valid
    new_buf = tail[:, POOL_TAIL - pad - POOL_BUF:POOL_TAIL - pad, :]
    return out.reshape(B * L, POOL_WIDTH), new_buf


def _mlstm_kernel(*refs, c, bt, l_valid, has_carry):
    u_ref, gif_ref, c0_ref, n0_ref, m0_ref, bi_ref, bf_ref, gnw_ref = refs[:8]
    o_ref, cN_ref, nN_ref, mN_ref, c_s, n_s, m_s = refs[8 + has_carry:]
    ci = pl.program_id(1)

    @pl.when(ci == 0)
    def _():
        c_s[...] = c0_ref[...]
        n_s[...] = n0_ref[...]
        m_s[...] = m0_ref[...]

    N = bt * c
    gif = gif_ref[...].reshape(N, LANES)
    tok = lax.broadcasted_iota(jnp.int32, (N, 1), 0) % c
    valid = tok < l_valid
    ig_all = jnp.where(valid, gif + bi_ref[...], -jnp.inf)
    lf_all = jnp.where(valid, _log_sigmoid(gif + bf_ref[...]), 0.0)
    bcum_all = _dot(_seq_prefix_mask(N, c).astype(F32), lf_all, HI)
    ig_t = ig_all.T
    bcum_t = bcum_all.T
    ri = lax.broadcasted_iota(jnp.int32, (c, c), 0)
    causal = lax.broadcasted_iota(jnp.int32, (c, c), 1) <= ri
    st = []
    for b in range(bt):
        rs = slice(b * c, (b + 1) * c)
        for h in range(ML_HEADS):
            qs = slice(h * ML_DQK, (h + 1) * ML_DQK)
            ks = slice(ML_QK_WIDTH + h * ML_DQK, ML_QK_WIDTH + (h + 1) * ML_DQK)
            vs = slice(2 * ML_QK_WIDTH + h * ML_DV, 2 * ML_QK_WIDTH + (h + 1) * ML_DV)
            os_ = slice(2 * ML_QK_WIDTH + ML_V_WIDTH + h * ML_DV, 2 * ML_QK_WIDTH + ML_V_WIDTH + (h + 1) * ML_DV)
            i_c = ig_all[rs, h:h + 1]
            b_c = bcum_all[rs, ML_HEADS + h:ML_HEADS + h + 1]
            i_r = ig_t[h:h + 1, rs]
            b_r = bcum_t[ML_HEADS + h:ML_HEADS + h + 1, rs]
            m_prev = m_s[b, h]
            dlog = jnp.where(causal, b_c - b_r + i_r, -jnp.inf)
            inter = b_c + m_prev
            m_t = jnp.maximum(inter, jnp.max(dlog, axis=-1, keepdims=True))
            b_last = b_c[c - 1:c, :]
            s_log = b_last - b_c + i_c
            m_new = jnp.maximum(b_last + m_prev, jnp.max(s_log, axis=0, keepdims=True))
            q = u_ref[b, :, qs]
            k = u_ref[b, :, ks] * (ML_DQK ** -0.5)
            st.append(dict(b=b, h=h, hs=slice(h * ML_DV, (h + 1) * ML_DV), os=os_, q=q, k=k, qb=_bf(q), vb=_bf(u_ref[b, :, vs]),
                           dlog=dlog, m_t=m_t, w_prev=jnp.exp(inter - m_t), m_new=m_new,
                           kw=k * jnp.exp(s_log - m_new), wp=jnp.exp(b_last + m_prev - m_new)))
    for s in st:
        s["wts"] = jnp.exp(s["dlog"] - s["m_t"]) * _dot_nt(s["qb"], _bf(s["k"]))
    for s in st:
        b, h = s["b"], s["h"]
        c_prev = c_s[b, h]
        n_prev = n_s[b, h]
        num = s["w_prev"] * _dot(s["qb"], _bf(c_prev)) + _dot(_bf(s["wts"]), s["vb"])
        den = (s["w_prev"] * jnp.sum(s["q"] * n_prev, axis=-1, keepdims=True)
               + jnp.sum(s["wts"], axis=-1, keepdims=True))
        s["hh"] = num / jnp.maximum(jnp.abs(den), jnp.exp(-s["m_t"]))
        c_s[b, h] = s["wp"] * c_prev + _dot_tn(_bf(s["kw"]), s["vb"])
        n_s[b, h] = s["wp"] * n_prev + jnp.sum(s["kw"], axis=0, keepdims=True)
        m_s[b, h] = s["m_new"]
    for s in st:
        hh = s["hh"]
        hn = hh * lax.rsqrt(jnp.mean(hh * hh, axis=-1, keepdims=True) + RMS_EPS) * gnw_ref[:, s["hs"]]
        o_ref[s["b"], :, s["hs"]] =(hn * jax.nn.sigmoid(u_ref[s["b"], :, s["os"]])).astype(o_ref.dtype)

    @pl.when(ci == pl.num_programs(1) - 1)
    def _():
        cN_ref[...] = c_s[...]
        nN_ref[...] = n_s[...]
        mN_ref[...] = m_s[...]


def mlstm_mix(u_ml, u_pl, c0, n0, m0, l, l_out, c_carry, b_i, b_f, gn_w, *, B, L, l_valid):
    c = min(L, CHUNK_ROWS)
    bt = CHUNK_ROWS // c
    u3 = u_ml.reshape(B, L, ML_MAIN)
    g3 = u_pl.reshape(B, L, POOL_WIDTH + LANES)
    zeros = jnp.zeros((LANES - 2 * ML_HEADS,), F32)
    bi = jnp.concatenate([b_i, jnp.zeros((ML_HEADS,), F32), zeros]).reshape(1, LANES)
    bf = jnp.concatenate([jnp.zeros((ML_HEADS,), F32), b_f, zeros]).reshape(1, LANES)
    nl = n0.shape[0]
    c_blk, n_blk, m_blk = (bt, ML_HEADS, ML_DQK, ML_DV), (bt, ML_HEADS, 1, ML_DQK), (bt, ML_HEADS, 1, 1)
    at_b = lambda b, i: (b, 0, 0, 0)
    in_specs = [pl.BlockSpec((bt, c, ML_MAIN), lambda b, i: (b, i, 0)),
                pl.BlockSpec((bt, c, LANES), lambda b, i: (b, i, POOL_WIDTH // LANES)),
                _layer_block(l, c_blk, at_b), _layer_block(l, n_blk, at_b), _layer_block(l, m_blk, at_b),
                pl.BlockSpec((1, LANES), lambda b, i: (0, 0)),
                pl.BlockSpec((1, LANES), lambda b, i: (0, 0)),
                pl.BlockSpec((1, ML_V_WIDTH), lambda b, i: (0, 0))]
    args = [u3, g3, c0, n0.reshape(nl, B, ML_HEADS, 1, ML_DQK), m0.reshape(nl, B, ML_HEADS, 1, 1), bi, bf,
            gn_w.reshape(1, ML_V_WIDTH)]
    aliases = {}
    if c_carry is not None:
        in_specs.append(pl.BlockSpec(memory_space=pl.ANY))
        args.append(c_carry)
        aliases = {len(args) - 1: 1}
    out, cN, nN, mN = pl.pallas_call(
        functools.partial(_mlstm_kernel, c=c, bt=bt, l_valid=l_valid, has_carry=c_carry is not None),
        grid=(B // bt, L // c),
        in_specs=in_specs,
        out_specs=[pl.BlockSpec((bt, c, ML_V_WIDTH), lambda b, i: (b, i, 0)),
                   _layer_block(l_out, c_blk, at_b),
                   pl.BlockSpec(n_blk, at_b), pl.BlockSpec(m_blk, at_b)],
        out_shape=[jax.ShapeDtypeStruct((B, L, ML_V_WIDTH), BF16),
                   jax.ShapeDtypeStruct((DEPTH, B, ML_HEADS, ML_DQK, ML_DV), F32),
                   jax.ShapeDtypeStruct((B, ML_HEADS, 1, ML_DQK), F32),
                   jax.ShapeDtypeStruct((B, ML_HEADS, 1, 1), F32)],
        scratch_shapes=[pltpu.VMEM(c_blk, F32), pltpu.VMEM(n_blk, F32), pltpu.VMEM(m_blk, F32)],
        input_output_aliases=aliases,
        compiler_params=_cparams(2),
        name="mlstm_mix",
    )(*args)
    return (out.reshape(B * L, ML_V_WIDTH), cN, nN.reshape(B, ML_HEADS, ML_DQK), mN.reshape(B, ML_HEADS))


RW_CHAIN_GROUP = 16

def _rwkv_chain_group(chains, seq, s_s, o_ref, rk_ref, gnw_ref, gnb_ref, *, C):
    rowi = lax.broadcasted_iota(jnp.int32, (C, C), 0)
    coli = lax.broadcasted_iota(jnp.int32, (C, C), 1)
    upper = rowi < coli
    col2 = lax.broadcasted_iota(jnp.int32, (C, 2 * C), 1)
    incl2 = jnp.where(col2 >= C, col2 - C, col2) <= lax.broadcasted_iota(jnp.int32, (C, 2 * C), 0)
    n_sq = max(int(math.log2(C)), 1)
    st = []
    for b, h in chains:
        rs = slice(b * C, (b + 1) * C)
        sl = slice(h * RW_HEAD, (h + 1) * RW_HEAD)
        cut = lambda t, rs=rs, sl=sl: t[rs, sl]
        kk_h = cut(seq["kkv"])
        nrm = jnp.sqrt(jnp.sum(kk_h * kk_h, axis=-1, keepdims=True))
        kap = jnp.where(seq["valid"][rs], kk_h / jnp.maximum(nrm, 1e-12), 0.0)
        k_h, v_h, r_h = cut(seq["kmod"]), cut(seq["v"]), cut(seq["r"])
        gi = cut(seq["gi"])
        b_t = kap * cut(seq["a"]) * gi
        k_t = k_h * gi
        st.append(dict(b=b, h=h, sl=sl, k_h=k_h, v_h=v_h, r_h=r_h,
                       a_t=_bf(-kap * cut(seq["gp"])), r_t=_bf(r_h * cut(seq["g"])),
                       bk=jnp.concatenate([b_t, k_t], axis=0),
                       ge=seq["g"][(b + 1) * C - 1:(b + 1) * C, sl], s0=s_s[b, h]))
    for c in st:
        bkb = _bf(c["bk"])
        mt = _dot_nt(bkb, c["a_t"])
        c["pt"] = jnp.where(upper, mt[:C], 0.0)
        c["akt"] = _bf(jnp.where(upper, mt[C:], 0.0))
        c["a_r"] = jnp.where(incl2, _dot_nt(c["r_t"], bkb), 0.0)
        c["s0b"] = _bf(c["s0"])
        c["vb"] = _bf(c["v_h"])
    for c in st:
        c["xt"] = _dot_nt(c["s0b"], c["a_t"]) + _dot_tn(c["vb"], c["akt"])
    for _ in range(n_sq - 1):
        for c in st:
            z = _dot(_bf(jnp.concatenate([c["pt"], c["xt"]], axis=0)), _bf(c["pt"]))
            c["pt"] = z[:C]
            c["xt"] = c["xt"] + z[C:]
    for c in st:
        c["ut"] = _bf(c["xt"] + _dot(_bf(c["xt"]), _bf(c["pt"])))
    for c in st:
        a_r = c["a_r"]
        c["y"] = (_dot_nt(c["r_t"], c["s0b"]) + _dot_nt(_bf(a_r[:, :C]), c["ut"])
                  + _dot(_bf(a_r[:, C:]), c["vb"]))
        bkg = c["bk"] * c["ge"]
        s_s[c["b"], c["h"]] = (c["s0"] * c["ge"] + _dot(c["ut"], _bf(bkg[:C]))
                               + _dot_tn(c["vb"], _bf(bkg[C:])))
    for c in st:
        y, sl = c["y"], c["sl"]
        mean = jnp.mean(y, axis=-1, keepdims=True)
        yc = y - mean
        var = jnp.mean(yc * yc, axis=-1, keepdims=True)
        yn = yc * lax.rsqrt(var + GN_EPS) * gnw_ref[:, sl] + gnb_ref[:, sl]
        bonus = jnp.sum(c["r_h"] * c["k_h"] * rk_ref[:, sl], axis=-1, keepdims=True) * c["v_h"]
        o_ref[c["b"], :, sl] = (yn + bonus).astype(o_ref.dtype)


def _rwkv_kernel(*refs, C, bt, l_valid, has_carry):
    (u_ref, sh_ref, s0_ref, mu_ref, w0_ref, w2_ref, a0_ref, a2_ref, kk_ref, ka_ref, rk_ref, gnw_ref,
     gnb_ref) = refs[:13]
    o_ref, sN_ref, prev_s, s_s = refs[13 + has_carry:]
    ci = pl.program_id(1)

    @pl.when(ci == 0)
    def _():
        prev_s[...] = sh_ref[...]
        s_s[...] = s0_ref[...]

    N = bt * C
    u = u_ref[...].reshape(N, RW_COLS)
    tok = lax.broadcasted_iota(jnp.int32, (N, 1), 0) % C
    valid = tok < l_valid
    prev = jnp.concatenate([jnp.broadcast_to(prev_s[b], (C, RW_COLS)) for b in range(bt)], axis=0)
    u_prev = jnp.where(tok == 0, prev, pltpu.roll(u, 1, 0))
    for b in range(bt):
        prev_s[b] = u[(b + 1) * C - 1:(b + 1) * C, :]
    us = u + (u_prev - u) * mu_ref[...]
    W = RW_WIDTH
    k = us[:, W:2 * W]
    wd = us[:, 3 * W:3 * W + RW_RANK]
    ad = us[:, 3 * W + RW_RANK:3 * W + 2 * RW_RANK]
    xw = w0_ref[...] + _dot(jnp.tanh(wd), w2_ref[...], HI)
    log_w = -_softplus(-xw) - 0.5
    ld = jnp.where(valid, -jnp.exp(log_w), 0.0)
    a = jax.nn.sigmoid(a0_ref[...] + _dot(ad, a2_ref[...], HI))
    cum = _dot(_seq_prefix_mask(N, C).astype(F32), ld, HI)
    seq = dict(valid=valid, r=us[:, 0:W], v=jnp.where(valid, us[:, 2 * W:3 * W], 0.0), a=a,
               g=jnp.exp(cum), gi=jnp.exp(-cum), gp=jnp.exp(cum - ld), kkv=k * kk_ref[...],
               kmod=jnp.where(valid, k * (1.0 + (a - 1.0) * ka_ref[...]), 0.0))
    chains = [(b, h) for b in range(bt) for h in range(RW_HEADS)]
    group = RW_CHAIN_GROUP * bt
    for i in range(0, len(chains), group):
        _rwkv_chain_group(chains[i:i + group], seq, s_s, o_ref, rk_ref, gnw_ref, gnb_ref, C=C)

    @pl.when(ci == pl.num_programs(1) - 1)
    def _():
        sN_ref[...] = s_s[...]


def rwkv7_mix(u_rw, shift_prev, s_prev, l, l_out, s_carry, mu, w0, w2, a0, a2, k_k, k_a, r_k, gn_w, gn_b, *, B, L,
              l_valid):
    C = min(L, CHUNK_ROWS)
    bt = CHUNK_ROWS // C
    u3 = u_rw.reshape(B, L, RW_COLS)
    vec = lambda n: pl.BlockSpec((1, n), lambda b, i: (0, 0))
    row = lambda t: t.reshape(1, -1)
    s_blk = (bt, RW_HEADS, RW_HEAD, RW_HEAD)
    at_b = lambda b, i: (b, 0, 0, 0)
    in_specs = [pl.BlockSpec((bt, C, RW_COLS), lambda b, i: (b, i, 0)),
                _layer_block(l, (bt, 1, RW_COLS), lambda b, i: (b, 0, 0)),
                _layer_block(l, s_blk, at_b),
                vec(RW_COLS), vec(RW_WIDTH),
                pl.BlockSpec((RW_RANK, RW_WIDTH), lambda b, i: (0, 0)),
                vec(RW_WIDTH),
                pl.BlockSpec((RW_RANK, RW_WIDTH), lambda b, i: (0, 0)),
                vec(RW_WIDTH), vec(RW_WIDTH), vec(RW_WIDTH), vec(RW_WIDTH), vec(RW_WIDTH)]
    args = [u3, shift_prev.reshape(shift_prev.shape[0], B, 1, RW_COLS), s_prev, row(mu), row(w0), w2, row(a0), a2,
            row(k_k), row(k_a), row(r_k), row(gn_w), row(gn_b)]
    aliases = {}
    if s_carry is not None:
        in_specs.append(pl.BlockSpec(memory_space=pl.ANY))
        args.append(s_carry)
        aliases = {len(args) - 1: 1}
    out, sN = pl.pallas_call(
        functools.partial(_rwkv_kernel, C=C, bt=bt, l_valid=l_valid, has_carry=s_carry is not None),
        grid=(B // bt, L // C),
        in_specs=in_specs,
        out_specs=[pl.BlockSpec((bt, C, RW_WIDTH), lambda b, i: (b, i, 0)), _layer_block(l_out, s_blk, at_b)],
        out_shape=[jax.ShapeDtypeStruct((B, L, RW_WIDTH), BF16),
                   jax.ShapeDtypeStruct((DEPTH, B, RW_HEADS, RW_HEAD, RW_HEAD), F32)],
        scratch_shapes=[pltpu.VMEM((bt, 1, RW_COLS), F32), pltpu.VMEM(s_blk, F32)],
        input_output_aliases=aliases,
        compiler_params=_cparams(2),
        name="rwkv7_mix",
    )(*args)
    return out.reshape(B * L, RW_WIDTH), sN


ATTN_SEQS_PER_STEP = 8


def _attn_kernel(q_ref, kv_ref, o_ref):
    heads = [slice(h * XA_HEAD_DIM, (h + 1) * XA_HEAD_DIM) for h in range(XA_HEADS)]
    s = [_dot_nt(q_ref[0, :, hs], _bf(kv_ref[0, :, hs])) * (XA_HEAD_DIM ** -0.5) for hs in heads]
    e = [jnp.exp(t - jnp.max(t, axis=-1, keepdims=True)) for t in s]
    p = [t / jnp.sum(t, axis=-1, keepdims=True) for t in e]
    for h, hs in enumerate(heads):
        vs = slice(XA_WIDTH + h * XA_HEAD_DIM, XA_WIDTH + (h + 1) * XA_HEAD_DIM)
        o_ref[0, :, hs] = _dot(_bf(p[h]), _bf(kv_ref[0, :, vs])).astype(o_ref.dtype)


def _attn_short_kernel(q_ref, k_ref, v_ref, o_ref, *, bt):
    L = q_ref.shape[1]
    rows, cols = XA_HEADS * L, MEM_LEN * XA_HEADS
    own = (lax.broadcasted_iota(jnp.int32, (rows, cols), 0) // L
           == lax.broadcasted_iota(jnp.int32, (rows, cols), 1) % XA_HEADS)
    heads = [slice(h * XA_HEAD_DIM, (h + 1) * XA_HEAD_DIM) for h in range(XA_HEADS)]
    qs = [jnp.concatenate([q_ref[b, :, hs] for hs in heads], axis=0) for b in range(bt)]
    s = [_dot_nt(qs[b], _bf(k_ref[b].reshape(cols, XA_HEAD_DIM))) * (XA_HEAD_DIM ** -0.5) for b in range(bt)]
    s = [jnp.where(own, t, -jnp.inf) for t in s]
    e = [jnp.exp(t - jnp.max(t, axis=-1, keepdims=True)) for t in s]
    p = [t / jnp.sum(t, axis=-1, keepdims=True) for t in e]
    for b in range(bt):
        out = _dot(_bf(p[b]), _bf(v_ref[b].reshape(cols, XA_HEAD_DIM)))
        for h, hs in enumerate(heads):
            o_ref[b, :, hs] = out[h * L:(h + 1) * L].astype(o_ref.dtype)


def cross_attention(q, mem, l, *, B, L):
    q3 = q.reshape(B, L, XA_WIDTH)
    if l is None:
        bt, tq, body, mems = 1, min(L, 512), _attn_kernel, [mem]
        mem_specs = [pl.BlockSpec((bt, MEM_LEN, 2 * XA_WIDTH), lambda b, i: (b, 0, 0))]
    else:
        bt, tq, mems = ATTN_SEQS_PER_STEP, L, list(mem)
        body = functools.partial(_attn_short_kernel, bt=bt)
        mem_specs = [pl.BlockSpec((None, bt, MEM_LEN, XA_HEADS, XA_HEAD_DIM), lambda b, i: (l, b, 0, 0, 0))] * 2
    qo_spec = pl.BlockSpec((bt, tq, XA_WIDTH), lambda b, i: (b, i, 0))
    out = pl.pallas_call(
        body,
        grid=(B // bt, L // tq),
        in_specs=[qo_spec] + mem_specs,
        out_specs=qo_spec,
        out_shape=jax.ShapeDtypeStruct((B, L, XA_WIDTH), BF16),
        compiler_params=_cparams(2),
        name="cross_attention",
    )(q3, *mems)
    return out.reshape(B * L, XA_WIDTH)


ROUTER_GROUP_LANE = MOE_EXPERTS


MOE_TB = 512
MOE_TB_GATHER = 1024
MOE_SUB = 256
MOE_TE_LONG = 1024
MOE_TE_SHORT = 512
ROUTE_GROUP_LANE = 0
ROUTE_RANK_LANE = 1


def _router_kernel(x_ref, g_ref, w_ref, b_ref, xn_ref, comb_ref, route_ref, cnt_s):
    @pl.when(pl.program_id(0) == 0)
    def _():
        cnt_s[...] = jnp.zeros_like(cnt_s)

    xn = _rms(x_ref[...], g_ref[...])
    xn_ref[...] = _bf(xn)
    z = _dot(xn, w_ref[...], HI) + b_ref[...]
    tm = z.shape[0]
    lane = lax.broadcasted_iota(jnp.int32, z.shape, 1).astype(F32)
    big = float(LANES)
    neg = -jnp.inf
    first = lambda mask: jnp.min(jnp.where(mask, lane, big), axis=-1, keepdims=True)
    is_g = jnp.logical_and(lane >= ROUTER_GROUP_LANE, lane < ROUTER_GROUP_LANE + MOE_GROUPS)
    zg = jnp.where(is_g, z, neg)
    mg = jnp.max(zg, axis=-1, keepdims=True)
    grp = first(zg == mg) - ROUTER_GROUP_LANE
    p_grp = 1.0 / jnp.sum(jnp.exp(zg - mg), axis=-1, keepdims=True)
    lo = grp * MOE_PER_GROUP
    ze = jnp.where(jnp.logical_and(lane >= lo, lane < lo + MOE_PER_GROUP), z, neg)
    t1 = jnp.max(ze, axis=-1, keepdims=True)
    i1 = first(ze == t1)
    ze2 = jnp.where(lane == i1, neg, ze)
    t2 = jnp.max(ze2, axis=-1, keepdims=True)
    i2 = first(ze2 == t2)
    e2 = jnp.exp(t2 - t1)
    g1 = p_grp / (1.0 + e2)
    comb_ref[...] = jnp.where(lane == i1, g1, 0.0) + jnp.where(lane == i2, g1 * e2, 0.0)
    onehot = jnp.where(lane == grp, 1.0, 0.0)
    r = lax.broadcasted_iota(jnp.int32, (tm, tm), 0)
    c = lax.broadcasted_iota(jnp.int32, (tm, tm), 1)
    before = _dot(_bf(jnp.where(c < r, 1.0, 0.0)), _bf(onehot)) + cnt_s[...]
    rank = jnp.sum(onehot * before, axis=-1, keepdims=True)
    cnt_s[...] = cnt_s[...] + jnp.sum(onehot, axis=0, keepdims=True)
    route_ref[...] = jnp.where(lane == ROUTE_GROUP_LANE, grp, 0.0) + jnp.where(lane == ROUTE_RANK_LANE, rank, 0.0)


def moe_router(x, g, w_r1, b_r1, w_r2, b_r2):
    T, D = x.shape
    tm = MOE_TB
    pad = LANES - MOE_EXPERTS - MOE_GROUPS
    w = jnp.concatenate([w_r2, w_r1, jnp.zeros((D, pad), F32)], axis=1)
    b = jnp.concatenate([b_r2, b_r1, jnp.zeros((pad,), F32)]).reshape(1, LANES)
    return pl.pallas_call(
        _router_kernel,
        grid=(T // tm,),
        in_specs=[pl.BlockSpec((tm, D), lambda i: (i, 0)),
                  pl.BlockSpec((1, D), lambda i: (0, 0)),
                  pl.BlockSpec((D, LANES), lambda i: (0, 0)),
                  pl.BlockSpec((1, LANES), lambda i: (0, 0))],
        out_specs=[pl.BlockSpec((tm, D), lambda i: (i, 0)),
                   pl.BlockSpec((tm, LANES), lambda i: (i, 0)),
                   pl.BlockSpec((tm, LANES), lambda i: (i, 0))],
        out_shape=[jax.ShapeDtypeStruct((T, D), BF16), jax.ShapeDtypeStruct((T, LANES), F32),
                   jax.ShapeDtypeStruct((T, LANES), F32)],
        scratch_shapes=[pltpu.VMEM((1, LANES), F32)],
        compiler_params=_cparams(1),
        name="moe_router",
    )(x, g.reshape(1, D), w, b)


def _moe_plan(route, T, te):
    i32 = jnp.int32
    grp = route[:, ROUTE_GROUP_LANE].astype(i32)
    rank = route[:, ROUTE_RANK_LANE].astype(i32)
    rows = T + MOE_GROUPS * te
    n_sub = rows // MOE_SUB
    n_tiles = rows // te
    onehot = (grp[:, None] == jnp.arange(MOE_GROUPS, dtype=i32)[None]).astype(i32)
    seg_rows = (onehot.sum(axis=0) + te - 1) // te * te
    seg_end = jnp.cumsum(seg_rows)
    seg_start = seg_end - seg_rows
    dest = seg_start[grp] + rank
    group_of = lambda row: jnp.minimum(jnp.sum(row[:, None] >= seg_end[None], axis=1), MOE_GROUPS - 1).astype(i32)
    sub_row = jnp.arange(n_sub, dtype=i32) * MOE_SUB
    sub_g = group_of(sub_row)
    sub_valid = sub_row < seg_end[-1]
    r0 = sub_row - seg_start[sub_g]

    def overlap(tb):
        blk_cnt = onehot.reshape(T // tb, tb, MOE_GROUPS).sum(axis=1)
        cum_blk = jnp.concatenate([jnp.zeros((1, MOE_GROUPS), i32), jnp.cumsum(blk_cnt, axis=0)])
        lo = cum_blk[:-1][:, sub_g].T
        hi = cum_blk[1:][:, sub_g].T
        return (lo < (r0 + MOE_SUB)[:, None]) & (hi > r0[:, None]) & sub_valid[:, None]

    g_mask = overlap(MOE_TB_GATHER)
    first_col = (jnp.arange(T // MOE_TB_GATHER) == 0)[None]
    g_mask = g_mask | (first_col & ~g_mask.any(axis=1, keepdims=True))
    n_items = MOE_GROUPS * (T // MOE_TB) + 2 * n_sub

    def items(mask, ncol):
        flat = jnp.nonzero(mask.reshape(-1), size=n_items, fill_value=-1)[0].astype(i32)
        valid = flat >= 0
        flat = jnp.where(valid, flat, jnp.max(flat))
        major, minor = flat // ncol, flat % ncol
        prev = jnp.concatenate([jnp.full((1,), -1, i32), major[:-1]])
        nxt = jnp.concatenate([major[1:], jnp.full((1,), -1, i32)])
        nvalid = jnp.concatenate([valid[1:], jnp.zeros((1,), bool)])
        first = (major != prev) & valid
        last = ((major != nxt) | ~nvalid) & valid
        return major, minor, first.astype(i32), last.astype(i32), valid.astype(i32)

    g_sub, g_blk, g_first, _, g_valid = items(g_mask, T // MOE_TB_GATHER)
    s_blk, s_sub, s_first, s_last, s_valid = items(overlap(MOE_TB).T, n_sub)
    tile_row = jnp.arange(n_tiles, dtype=i32) * te
    return dict(dest=dest, rows=rows, gather=(g_sub, g_blk, g_first, g_valid),
                scatter=(s_sub, s_blk, s_first, s_last, s_valid),
                tile_group=group_of(tile_row), tile_valid=(tile_row < seg_end[-1]).astype(i32))


def _moe_gather_kernel(sub_ref, blk_ref, first_ref, valid_ref, dest_ref, xn_ref, comb_ref, xs_ref, cs_ref,
                       acc_x, acc_c):
    w = pl.program_id(0)

    @pl.when(valid_ref[w] == 1)
    def _():
        rows = sub_ref[w] * MOE_SUB + lax.broadcasted_iota(jnp.int32, (MOE_SUB, MOE_TB_GATHER), 0)
        hit = dest_ref[0] == rows
        gx = _dot(_bf(jnp.where(hit, 1.0, 0.0)), xn_ref[...])
        gc = _dot(jnp.where(hit, 1.0, 0.0), comb_ref[...], HI)

        @pl.when(first_ref[w] == 1)
        def _():
            acc_x[...] = gx
            acc_c[...] = gc

        @pl.when(first_ref[w] == 0)
        def _():
            acc_x[...] += gx
            acc_c[...] += gc

        xs_ref[...] = _bf(acc_x[...])
        cs_ref[...] = acc_c[...]


def moe_gather(xn, comb, plan):
    T, D = xn.shape
    g_sub, g_blk, g_first, g_valid = plan["gather"]
    rows = plan["rows"]
    tb = MOE_TB_GATHER
    dest3 = plan["dest"].reshape(T // tb, 1, tb)
    grid_spec = pltpu.PrefetchScalarGridSpec(
        num_scalar_prefetch=4,
        grid=(g_sub.shape[0],),
        in_specs=[pl.BlockSpec((1, 1, tb), lambda w, s, b, f, v: (b[w], 0, 0)),
                  pl.BlockSpec((tb, D), lambda w, s, b, f, v: (b[w], 0)),
                  pl.BlockSpec((tb, LANES), lambda w, s, b, f, v: (b[w], 0))],
        out_specs=[pl.BlockSpec((MOE_SUB, D), lambda w, s, b, f, v: (s[w], 0)),
                   pl.BlockSpec((MOE_SUB, LANES), lambda w, s, b, f, v: (s[w], 0))],
        scratch_shapes=[pltpu.VMEM((MOE_SUB, D), F32), pltpu.VMEM((MOE_SUB, LANES), F32)])
    return pl.pallas_call(
        _moe_gather_kernel,
        grid_spec=grid_spec,
        out_shape=[jax.ShapeDtypeStruct((rows, D), BF16), jax.ShapeDtypeStruct((rows, LANES), F32)],
        compiler_params=_cparams(1),
        name="moe_gather",
    )(g_sub, g_blk, g_first, g_valid, dest3, xn, comb)


def _moe_group_experts_kernel(tg_ref, tv_ref, xs_ref, cs_ref, wg_ref, wu_ref, wd_ref, ys_ref):
    i = pl.program_id(0)
    e = pl.program_id(1)

    @pl.when(e == 0)
    def _():
        ys_ref[...] = jnp.zeros_like(ys_ref)

    @pl.when(tv_ref[i] == 1)
    def _():
        xs = xs_ref[...]
        cs = cs_ref[...]
        lane = lax.broadcasted_iota(jnp.int32, cs.shape, 1)
        ce = jnp.sum(jnp.where(lane == tg_ref[i] * MOE_PER_GROUP + e, cs, 0.0), axis=-1, keepdims=True)
        hid = jax.nn.silu(_dot(xs, _bf(wg_ref[...]))) * _dot(xs, _bf(wu_ref[...])) * ce
        ys_ref[...] += _dot(_bf(hid), _bf(wd_ref[...]))


def moe_group_experts(xs, cs, plan, wg, wu, wd, l, te):
    rows, D = xs.shape
    expert = lambda i, e, tg, tv: (l, tg[i] * MOE_PER_GROUP + e, 0, 0)
    w_in = pl.BlockSpec((None, None, D, MOE_HIDDEN), expert)
    grid_spec = pltpu.PrefetchScalarGridSpec(
        num_scalar_prefetch=2,
        grid=(rows // te, MOE_PER_GROUP),
        in_specs=[pl.BlockSpec((te, D), lambda i, e, tg, tv: (i, 0)),
                  pl.BlockSpec((te, LANES), lambda i, e, tg, tv: (i, 0)),
                  w_in, w_in,
                  pl.BlockSpec((None, None, MOE_HIDDEN, D), expert)],
        out_specs=pl.BlockSpec((te, D), lambda i, e, tg, tv: (i, 0)))
    return pl.pallas_call(
        _moe_group_experts_kernel,
        grid_spec=grid_spec,
        out_shape=jax.ShapeDtypeStruct((rows, D), F32),
        compiler_params=_cparams(2),
        name="moe_group_experts",
    )(plan["tile_group"], plan["tile_valid"], xs, cs, wg, wu, wd)


def _moe_scatter_kernel(sub_ref, blk_ref, first_ref, last_ref, valid_ref, dest_ref, ys_ref, x_ref, gf_ref, o_ref,
                        *, final_norm):
    w = pl.program_id(0)

    @pl.when(valid_ref[w] == 1)
    def _():
        cols = sub_ref[w] * MOE_SUB + lax.broadcasted_iota(jnp.int32, (MOE_TB, MOE_SUB), 1)
        pick = _bf(jnp.where(dest_ref[...] == cols, 1.0, 0.0))
        ys = ys_ref[...]
        hi = _bf(ys)
        lo = _bf(ys - hi.astype(F32))
        upd = _dot(pick, hi) + _dot(pick, lo)

        @pl.when(first_ref[w] == 1)
        def _():
            o_ref[...] = x_ref[...] + upd

        @pl.when(first_ref[w] == 0)
        def _():
            o_ref[...] += upd

        if final_norm:
            @pl.when(last_ref[w] == 1)
            def _():
                o_ref[...] = _rms(o_ref[...], gf_ref[...])


def moe_scatter(ys, x, plan, g_final, *, final_norm):
    T, D = x.shape
    s_sub, s_blk, s_first, s_last, s_valid = plan["scatter"]
    dest_col = plan["dest"].reshape(T, 1)
    im = lambda f: (lambda w, s, b, fi, la, v: f(w, s, b))
    grid_spec = pltpu.PrefetchScalarGridSpec(
        num_scalar_prefetch=5,
        grid=(s_sub.shape[0],),
        in_specs=[pl.BlockSpec((MOE_TB, 1), im(lambda w, s, b: (b[w], 0))),
                  pl.BlockSpec((MOE_SUB, D), im(lambda w, s, b: (s[w], 0))),
                  pl.BlockSpec((MOE_TB, D), im(lambda w, s, b: (b[w], 0))),
                  pl.BlockSpec((1, D), im(lambda w, s, b: (0, 0)))],
        out_specs=pl.BlockSpec((MOE_TB, D), im(lambda w, s, b: (b[w], 0))))
    return pl.pallas_call(
        functools.partial(_moe_scatter_kernel, final_norm=final_norm),
        grid_spec=grid_spec,
        out_shape=jax.ShapeDtypeStruct((T, D), F32),
        compiler_params=_cparams(1),
        name="moe_scatter",
    )(s_sub, s_blk, s_first, s_last, s_valid, dest_col, ys, x, g_final.reshape(1, D))


def hmoe_block(x, p, l, *, final_norm):
    T = x.shape[0]
    te = MOE_TE_LONG if T >= 4 * MOE_TE_LONG else MOE_TE_SHORT
    xn, comb, route = moe_router(x, p["g_moe"][l], p["moe_wr1"][l], p["moe_br1"][l], p["moe_wr2"][l],
                                 p["moe_br2"][l])
    plan = _moe_plan(route, T, te)
    xs, cs = moe_gather(xn, comb, plan)
    ys = moe_group_experts(xs, cs, plan, p["moe_wg"], p["moe_wu"], p["moe_wd"], l, te)
    return moe_scatter(ys, x, plan, p["g_final"], final_norm=final_norm)


def _layer_weights(l, p):
    w_in = p["w_in"][l]
    c0 = RW_COLS
    c1 = c0 + ML_MAIN
    c2 = c1 + 2 * ML_HEADS
    c3 = c2 + POOL_WIDTH
    w_pl = jnp.concatenate([w_in[:, c2:c3], w_in[:, c1:c2], jnp.zeros((D_MODEL, LANES - 2 * ML_HEADS), F32)], axis=1)
    return dict(
        w_rw=_bf(w_in[:, :c0]), w_ml=_bf(w_in[:, c0:c1]), w_pl=_bf(w_pl), w_gate=_bf(w_in[:, c3:]),
        w_up_rwkv=_bf(p["w_up_rwkv"][l]), w_up_mlstm=_bf(p["w_up_mlstm"][l]), w_up_pool=_bf(p["w_up_pool"][l]),
        w_out=_bf(p["w_out"][l]), pool_w=_bf(p["pool_w"][l]),
        xa_wq=_bf(p["xa_wq"][l]), xa_wo=_bf(p["xa_wo"][l]),
        xa_wkv=_bf(jnp.concatenate([p["xa_wk"][l], p["xa_wv"][l]], axis=1)),
    )


def _trunk_layer(x, mem, mem_l, st, carry, *, l, sl, p, lw, B, L, l_valid, start, final_norm):
    g_mix = p["g_mix"][l]
    u_rw = norm_matmul(x, g_mix, lw["w_rw"], tn=TN_RWKV, name="in_rwkv")
    u_ml = norm_matmul(x, g_mix, lw["w_ml"], tn=TN_MLSTM, name="in_mlstm")
    u_pl = norm_matmul(x, g_mix, lw["w_pl"], tn=POOL_WIDTH + LANES, name="in_pool")
    gates = norm_matmul(x, g_mix, lw["w_gate"], tn=TN_GATES, out_dtype=BF16, act="sigmoid", name="in_gates")

    o_rw, rw_s = rwkv7_mix(u_rw, st["rw_shift"], st["rw_s"], sl, l, carry["rw_s"], p["rw_mu"][l], p["rw_w0"][l],
                           p["rw_w2"][l], p["rw_a0"][l], p["rw_a2"][l], p["rw_k_k"][l], p["rw_k_a"][l],
                           p["rw_r_k"][l], p["rw_gn_w"][l], p["rw_gn_b"][l], B=B, L=L, l_valid=l_valid)
    rw_shift = u_rw.reshape(B, L, RW_COLS)[:, l_valid - 1]
    o_ml, ml_c, ml_n, ml_m = mlstm_mix(u_ml, u_pl, st["ml_c"], st["ml_n"], st["ml_m"], sl, l, carry["ml_c"],
                                       p["ml_b_i"][l], p["ml_b_f"][l], p["ml_gn_w"][l], B=B, L=L, l_valid=l_valid)
    o_pl, pool_buf = pool_mix(u_pl, st["pool"], sl, lw["pool_w"], p["pool_scale"][l], B=B, L=L, l_valid=l_valid,
                              start=start)
    merged = merge_branches(o_rw, o_ml, o_pl, gates, lw["w_up_rwkv"], lw["w_up_mlstm"], lw["w_up_pool"])
    x = matmul_residual(merged, lw["w_out"], x)

    q = norm_matmul(x, p["g_xa"][l], lw["xa_wq"], tn=XA_WIDTH, out_dtype=BF16, name="xa_q")
    att = cross_attention(q, mem, mem_l, B=B, L=L)
    x = matmul_residual(att, lw["xa_wo"], x)

    x = hmoe_block(x, p, l, final_norm=final_norm)
    return x, dict(rw_s=rw_s, ml_c=ml_c), (rw_shift, ml_n, ml_m, pool_buf)


def kernel(x_prompt, x_sample, cache_mem_k, cache_mem_v, state_rwkv_s, state_rwkv_shift, state_mlstm_c, state_mlstm_n, state_mlstm_m, state_pool, mem_prompt, g_mix, w_in, rw_mu, rw_w0, rw_w2, rw_a0, rw_a2, rw_k_k, rw_k_a, rw_r_k, rw_gn_w, rw_gn_b, ml_b_i, ml_b_f, ml_gn_w, pool_w, pool_scale, w_up_rwkv, w_up_mlstm, w_up_pool, w_out, g_xa, g_mem, xa_wq, xa_wk, xa_wv, xa_wo, g_moe, moe_wr1, moe_br1, moe_wr2, moe_br2, moe_wg, moe_wu, moe_wd, g_final):
    p = dict(g_mix=g_mix, w_in=w_in, rw_mu=rw_mu, rw_w0=rw_w0, rw_w2=rw_w2, rw_a0=rw_a0, rw_a2=rw_a2, rw_k_k=rw_k_k,
             rw_k_a=rw_k_a, rw_r_k=rw_r_k, rw_gn_w=rw_gn_w, rw_gn_b=rw_gn_b, ml_b_i=ml_b_i, ml_b_f=ml_b_f,
             ml_gn_w=ml_gn_w, pool_w=pool_w, pool_scale=pool_scale, w_up_rwkv=w_up_rwkv, w_up_mlstm=w_up_mlstm,
             w_up_pool=w_up_pool, w_out=w_out, g_xa=g_xa, g_mem=g_mem, xa_wq=xa_wq, xa_wk=xa_wk, xa_wv=xa_wv,
             xa_wo=xa_wo, g_moe=g_moe, moe_wr1=moe_wr1, moe_br1=moe_br1, moe_wr2=moe_wr2, moe_br2=moe_br2,
             moe_wg=moe_wg, moe_wu=moe_wu, moe_wd=moe_wd, g_final=g_final)
    Bp, Lp, D = x_prompt.shape
    Bs, Ls, _ = x_sample.shape
    Ls_pad = -(-Ls // SUBLANES) * SUBLANES
    yp = x_prompt.reshape(Bp * Lp, D)
    ys = jnp.pad(x_sample, ((0, 0), (0, Ls_pad - Ls), (0, 0))).reshape(Bs * Ls_pad, D)
    zeros = lambda *s: jnp.zeros((1,) + s, F32)
    st_p = dict(rw_shift=zeros(Bp, RW_COLS), rw_s=zeros(Bp, RW_HEADS, RW_HEAD, RW_HEAD),
                ml_c=zeros(Bp, ML_HEADS, ML_DQK, ML_DV), ml_n=zeros(Bp, ML_HEADS, ML_DQK), ml_m=zeros(Bp, ML_HEADS),
                pool=zeros(Bp, POOL_BUF, POOL_WIDTH))
    st_s = dict(rw_shift=state_rwkv_shift, rw_s=state_rwkv_s, ml_c=state_mlstm_c, ml_n=state_mlstm_n,
                ml_m=state_mlstm_m, pool=state_pool)
    carry_p = carry_s = dict(rw_s=None, ml_c=None)
    small_p = [[] for _ in range(6)]
    small_s = [[] for _ in range(4)]
    for l in range(DEPTH):
        lw = _layer_weights(l, p)
        final = l == DEPTH - 1
        kv = norm_matmul(mem_prompt.reshape(Bp * MEM_LEN, D), g_mem[l], lw["xa_wkv"], tn=2 * XA_WIDTH,
                         name="memory_kv")
        kv3 = kv.reshape(Bp, MEM_LEN, 2 * XA_WIDTH)
        yp, carry_p, small = _trunk_layer(yp, kv3, None, st_p, carry_p, l=l, sl=0, p=p, lw=lw, B=Bp, L=Lp,
                                          l_valid=Lp, start=0, final_norm=final)
        mk = kv[:, :XA_WIDTH].reshape(Bp, MEM_LEN, XA_HEADS, XA_HEAD_DIM)
        mv = kv[:, XA_WIDTH:].reshape(Bp, MEM_LEN, XA_HEADS, XA_HEAD_DIM)
        for acc, t in zip(small_p, small + (mk, mv)):
            acc.append(t)
        ys, carry_s, small = _trunk_layer(ys, (cache_mem_k, cache_mem_v), l, st_s, carry_s, l=l, sl=l, p=p, lw=lw,
                                          B=Bs, L=Ls_pad, l_valid=Ls, start=PAST_LEN, final_norm=final)
        for acc, t in zip(small_s, small):
            acc.append(t)
    y_prompt = yp.reshape(Bp, Lp, D)
    y_sample = ys.reshape(Bs, Ls_pad, D)[:, :Ls]
    p_sh, p_n, p_m, p_pool, p_mk, p_mv = [jnp.stack(t) for t in small_p]
    s_sh, s_n, s_m, s_pool = [jnp.stack(t) for t in small_s]
    return (y_prompt, y_sample, carry_p["rw_s"], p_sh, carry_p["ml_c"], p_n, p_m, p_pool, p_mk, p_mv,
            carry_s["rw_s"], s_sh, carry_s["ml_c"], s_n, s_m, s_pool)
```

```python
import functools
import math

import jax
import jax.numpy as jnp
from jax import lax
from jax.experimental import pallas as pl
from jax.experimental.pallas import tpu as pltpu

F32 = jnp.float32
BF16 = jnp.bfloat16
HI = lax.Precision.HIGHEST

D_MODEL = 2048
DEPTH = 2
PAST_LEN = 16384
RW_HEAD = 64
RW_WIDTH = 1024
RW_HEADS = 16
RW_RANK = 64
RW_COLS = 3 * RW_WIDTH + 2 * RW_RANK
GN_EPS = 64e-5
ML_HEADS = 4
ML_DQK = 128
ML_DV = 256
ML_QK_WIDTH = 512
ML_V_WIDTH = 1024
ML_MAIN = 2 * ML_QK_WIDTH + 2 * ML_V_WIDTH
POOL_WIDTH = 1024
POOL_WINDOWS = (2, 4, 8, 16)
POOL_GW = 256
POOL_BUF = 15
POOL_HIST = 16
POOL_TAIL = 24
MEM_LEN = 256
XA_HEADS = 4
XA_HEAD_DIM = 128
XA_WIDTH = 512
MOE_GROUPS = 4
MOE_PER_GROUP = 8
MOE_EXPERTS = 32
MOE_HIDDEN = 256
RMS_EPS = 1e-6
LANES = 128
SUBLANES = 8
VMEM_LIMIT = 56 * 1024 * 1024
ROW_TILE = 1024
CHUNK_ROWS = 64
TN_RWKV = 640
TN_MLSTM = 1024
TN_GATES = 1024


def _cparams(n_axes):
    return pltpu.CompilerParams(dimension_semantics=("arbitrary",) * n_axes, vmem_limit_bytes=VMEM_LIMIT)


def _dot(a, b, precision=None):
    return jnp.dot(a, b, preferred_element_type=F32, precision=precision)


def _dot_nt(a, b, precision=None):
    return lax.dot_general(a, b, (((1,), (1,)), ((), ())), preferred_element_type=F32, precision=precision)


def _dot_tn(a, b, precision=None):
    return lax.dot_general(a, b, (((0,), (0,)), ((), ())), preferred_element_type=F32, precision=precision)


def _bf(x):
    return x.astype(BF16)


def _rms(x, g):
    return x * lax.rsqrt(jnp.mean(x * x, axis=-1, keepdims=True) + RMS_EPS) * g


def _log_sigmoid(x):
    return jnp.minimum(x, 0.0) - jnp.log1p(jnp.exp(-jnp.abs(x)))


def _softplus(x):
    return jnp.maximum(x, 0.0) + jnp.log1p(jnp.exp(-jnp.abs(x)))


def _seq_prefix_mask(n, c):
    r = lax.broadcasted_iota(jnp.int32, (n, n), 0)
    q = lax.broadcasted_iota(jnp.int32, (n, n), 1)
    return jnp.logical_and(q <= r, q // c == r // c)


def _layer_block(l, blk, idx):
    return pl.BlockSpec((None,) + tuple(blk), lambda b, i: (l,) + tuple(idx(b, i)))


def _norm_matmul_kernel(x_ref, g_ref, w_ref, o_ref, xn_ref, *, act):
    @pl.when(pl.program_id(1) == 0)
    def _():
        xn_ref[...] = _bf(_rms(x_ref[...], g_ref[...]))

    acc = _dot(xn_ref[...], w_ref[...])
    if act == "sigmoid":
        acc = jax.nn.sigmoid(acc)
    o_ref[...] = acc.astype(o_ref.dtype)


def norm_matmul(x, g, w, *, tn, name, out_dtype=F32, act=None):
    T, D = x.shape
    N = w.shape[1]
    tm = min(T, ROW_TILE)
    return pl.pallas_call(
        functools.partial(_norm_matmul_kernel, act=act),
        grid=(T // tm, N // tn),
        in_specs=[pl.BlockSpec((tm, D), lambda i, j: (i, 0)),
                  pl.BlockSpec((1, D), lambda i, j: (0, 0)),
                  pl.BlockSpec((D, tn), lambda i, j: (0, j))],
        out_specs=pl.BlockSpec((tm, tn), lambda i, j: (i, j)),
        out_shape=jax.ShapeDtypeStruct((T, N), out_dtype),
        scratch_shapes=[pltpu.VMEM((tm, D), BF16)],
        compiler_params=_cparams(2),
        name=name,
    )(x, g.reshape(1, D), w)


def _matmul_residual_kernel(a_ref, w_ref, r_ref, o_ref):
    o_ref[...] = r_ref[...] + _dot(a_ref[...], w_ref[...])


def matmul_residual(a, w, res, *, tn=1024):
    T, K = a.shape
    N = w.shape[1]
    tm = min(T, ROW_TILE)
    return pl.pallas_call(
        _matmul_residual_kernel,
        grid=(T // tm, N // tn),
        in_specs=[pl.BlockSpec((tm, K), lambda i, j: (i, 0)),
                  pl.BlockSpec((K, tn), lambda i, j: (0, j)),
                  pl.BlockSpec((tm, tn), lambda i, j: (i, j))],
        out_specs=pl.BlockSpec((tm, tn), lambda i, j: (i, j)),
        out_shape=jax.ShapeDtypeStruct((T, N), F32),
        compiler_params=_cparams(2),
        name="matmul_residual",
    )(a, w, res)


def _merge_kernel(orw_ref, oml_ref, opl_ref, g0_ref, g1_ref, g2_ref, wr_ref, wm_ref, wp_ref, o_ref):
    m = (g0_ref[...].astype(F32) * _dot(orw_ref[...], wr_ref[...])
         + g1_ref[...].astype(F32) * _dot(oml_ref[...], wm_ref[...])
         + g2_ref[...].astype(F32) * _dot(opl_ref[...], wp_ref[...]))
    o_ref[...] = _bf(m)


def merge_branches(o_rw, o_ml, o_pl, gates, w_rw, w_ml, w_pl, *, tn=1024):
    T, K = o_rw.shape
    tm = min(T, ROW_TILE)
    nj = D_MODEL // tn
    act = pl.BlockSpec((tm, K), lambda i, j: (i, 0))
    wsp = pl.BlockSpec((K, tn), lambda i, j: (0, j))
    gate = lambda b: pl.BlockSpec((tm, tn), lambda i, j, b=b: (i, b * nj + j))
    return pl.pallas_call(
        _merge_kernel,
        grid=(T // tm, nj),
        in_specs=[act, act, act, gate(0), gate(1), gate(2), wsp, wsp, wsp],
        out_specs=pl.BlockSpec((tm, tn), lambda i, j: (i, j)),
        out_shape=jax.ShapeDtypeStruct((T, D_MODEL), BF16),
        compiler_params=_cparams(2),
        name="merge_branches",
    )(o_rw, o_ml, o_pl, gates, gates, gates, w_rw, w_ml, w_pl)


def _pool_kernel(hist_ref, u_ref, w_ref, sc_ref, o_ref, tail_ref, e_ref, *, tl, bt, start):
    li = pl.program_id(1)

    @pl.when(li == 0)
    def _():
        e_ref[:, 0:POOL_HIST, :] = hist_ref[...]

    pos = start + li * tl + lax.broadcasted_iota(jnp.int32, (tl, 1), 0)
    for b in range(bt):
        u = u_ref[b]
        e_ref[b, POOL_HIST:, :] = u
        e = e_ref[b]
        s2 = e + pltpu.roll(e, 1, 0)
        s4 = s2[:, POOL_GW:] + pltpu.roll(s2[:, POOL_GW:], 2, 0)
        s8 = s4[:, POOL_GW:] + pltpu.roll(s4[:, POOL_GW:], 4, 0)
        s16 = s8[:, POOL_GW:] + pltpu.roll(s8[:, POOL_GW:], 8, 0)
        sums = (s2[:, :POOL_GW], s4[:, :POOL_GW], s8[:, :POOL_GW], s16)
        for g, win in enumerate(POOL_WINDOWS):
            cols = slice(g * POOL_GW, (g + 1) * POOL_GW)
            cnt = jnp.minimum(win, pos + 1).astype(F32)
            d = sums[g][POOL_HIST:, :] / cnt - u[:, cols]
            out = _dot(_bf(d), w_ref[g]) * sc_ref[:, cols]
            o_ref[b, :, cols] = out.astype(o_ref.dtype)
        tail_ref[b] = e[tl + POOL_HIST - POOL_TAIL:, :]
        e_ref[b, 0:POOL_HIST, :] = e[tl:, :]


def pool_mix(u_pl, buf, l, w_grp, scale, *, B, L, l_valid, start):
    tl = min(L, 256)
    bt = max(CHUNK_ROWS // L, 1)
    hist = jnp.pad(buf[l], ((0, 0), (POOL_HIST - POOL_BUF, 0), (0, 0)))
    u3 = u_pl.reshape(B, L, u_pl.shape[-1])
    out, tail = pl.pallas_call(
        functools.partial(_pool_kernel, tl=tl, bt=bt, start=start),
        grid=(B // bt, L // tl),
        in_specs=[pl.BlockSpec((bt, POOL_HIST, POOL_WIDTH), lambda b, i: (b, 0, 0)),
                  pl.BlockSpec((bt, tl, POOL_WIDTH), lambda b, i: (b, i, 0)),
                  pl.BlockSpec((4, POOL_GW, POOL_GW), lambda b, i: (0, 0, 0)),
                  pl.BlockSpec((1, POOL_WIDTH), lambda b, i: (0, 0))],
        out_specs=[pl.BlockSpec((bt, tl, POOL_WIDTH), lambda b, i: (b, i, 0)),
                   pl.BlockSpec((bt, POOL_TAIL, POOL_WIDTH), lambda b, i: (b, 0, 0))],
        out_shape=[jax.ShapeDtypeStruct((B, L, POOL_WIDTH), BF16),
                   jax.ShapeDtypeStruct((B, POOL_TAIL, POOL_WIDTH), F32)],
        scratch_shapes=[pltpu.VMEM((bt, tl + POOL_HIST, POOL_WIDTH), F32)],
        compiler_params=_cparams(2),
        name="pool_mix",
    )(hist, u3, w_grp, scale.reshape(1, POOL_WIDTH))
    pad = L - l_valid
    new_buf = tail[:, POOL_TAIL - pad - POOL_BUF:POOL_TAIL - pad, :]
    return out.reshape(B * L, POOL_WIDTH), new_buf


def _mlstm_kernel(*refs, c, bt, l_valid, has_carry):
    u_ref, gif_ref, c0_ref, n0_ref, m0_ref, bi_ref, bf_ref, gnw_ref = refs[:8]
    o_ref, cN_ref, nN_ref, mN_ref, c_s, n_s, m_s = refs[8 + has_carry:]
    ci = pl.program_id(1)

    @pl.when(ci == 0)
    def _():
        c_s[...] = c0_ref[...]
        n_s[...] = n0_ref[...]
        m_s[...] = m0_ref[...]

    N = bt * c
    gif = gif_ref[...].reshape(N, LANES)
    tok = lax.broadcasted_iota(jnp.int32, (N, 1), 0) % c
    valid = tok < l_valid
    ig_all = jnp.where(valid, gif + bi_ref[...], -jnp.inf)
    lf_all = jnp.where(valid, _log_sigmoid(gif + bf_ref[...]), 0.0)
    bcum_all = _dot(_seq_prefix_mask(N, c).astype(F32), lf_all, HI)
    ig_t = ig_all.T
    bcum_t = bcum_all.T
    ri = lax.broadcasted_iota(jnp.int32, (c, c), 0)
    causal = lax.broadcasted_iota(jnp.int32, (c, c), 1) <= ri
    st = []
    for b in range(bt):
        rs = slice(b * c, (b + 1) * c)
        for h in range(ML_HEADS):
            qs = slice(h * ML_DQK, (h + 1) * ML_DQK)
            ks = slice(ML_QK_WIDTH + h * ML_DQK, ML_QK_WIDTH + (h + 1) * ML_DQK)
            vs = slice(2 * ML_QK_WIDTH + h * ML_DV, 2 * ML_QK_WIDTH + (h + 1) * ML_DV)
            os_ = slice(2 * ML_QK_WIDTH + ML_V_WIDTH + h * ML_DV, 2 * ML_QK_WIDTH + ML_V_WIDTH + (h + 1) * ML_DV)
            i_c = ig_all[rs, h:h + 1]
            b_c = bcum_all[rs, ML_HEADS + h:ML_HEADS + h + 1]
            i_r = ig_t[h:h + 1, rs]
            b_r = bcum_t[ML_HEADS + h:ML_HEADS + h + 1, rs]
            m_prev = m_s[b, h]
            dlog = jnp.where(causal, b_c - b_r + i_r, -jnp.inf)
            inter = b_c + m_prev
            m_t = jnp.maximum(inter, jnp.max(dlog, axis=-1, keepdims=True))
            b_last = b_c[c - 1:c, :]
            s_log = b_last - b_c + i_c
            m_new = jnp.maximum(b_last + m_prev, jnp.max(s_log, axis=0, keepdims=True))
            q = u_ref[b, :, qs]
            k = u_ref[b, :, ks] * (ML_DQK ** -0.5)
            st.append(dict(b=b, h=h, hs=slice(h * ML_DV, (h + 1) * ML_DV), os=os_, q=q, k=k, qb=_bf(q), vb=_bf(u_ref[b, :, vs]),
                           dlog=dlog, m_t=m_t, w_prev=jnp.exp(inter - m_t), m_new=m_new,
                           kw=k * jnp.exp(s_log - m_new), wp=jnp.exp(b_last + m_prev - m_new)))
    for s in st:
        s["wts"] = jnp.exp(s["dlog"] - s["m_t"]) * _dot_nt(s["qb"], _bf(s["k"]))
    for s in st:
        b, h = s["b"], s["h"]
        c_prev = c_s[b, h]
        n_prev = n_s[b, h]
        num = s["w_prev"] * _dot(s["qb"], _bf(c_prev)) + _dot(_bf(s["wts"]), s["vb"])
        den = (s["w_prev"] * jnp.sum(s["q"] * n_prev, axis=-1, keepdims=True)
               + jnp.sum(s["wts"], axis=-1, keepdims=True))
        s["hh"] = num / jnp.maximum(jnp.abs(den), jnp.exp(-s["m_t"]))
        c_s[b, h] = s["wp"] * c_prev + _dot_tn(_bf(s["kw"]), s["vb"])
        n_s[b, h] = s["wp"] * n_prev + jnp.sum(s["kw"], axis=0, keepdims=True)
        m_s[b, h] = s["m_new"]
    for s in st:
        hh = s["hh"]
        hn = hh * lax.rsqrt(jnp.mean(hh * hh, axis=-1, keepdims=True) + RMS_EPS) * gnw_ref[:, s["hs"]]
        o_ref[s["b"], :, s["hs"]] =(hn * jax.nn.sigmoid(u_ref[s["b"], :, s["os"]])).astype(o_ref.dtype)

    @pl.when(ci == pl.num_programs(1) - 1)
    def _():
        cN_ref[...] = c_s[...]
        nN_ref[...] = n_s[...]
        mN_ref[...] = m_s[...]


def mlstm_mix(u_ml, u_pl, c0, n0, m0, l, l_out, c_carry, b_i, b_f, gn_w, *, B, L, l_valid):
    c = min(L, CHUNK_ROWS)
    bt = CHUNK_ROWS // c
    u3 = u_ml.reshape(B, L, ML_MAIN)
    g3 = u_pl.reshape(B, L, POOL_WIDTH + LANES)
    zeros = jnp.zeros((LANES - 2 * ML_HEADS,), F32)
    bi = jnp.concatenate([b_i, jnp.zeros((ML_HEADS,), F32), zeros]).reshape(1, LANES)
    bf = jnp.concatenate([jnp.zeros((ML_HEADS,), F32), b_f, zeros]).reshape(1, LANES)
    nl = n0.shape[0]
    c_blk, n_blk, m_blk = (bt, ML_HEADS, ML_DQK, ML_DV), (bt, ML_HEADS, 1, ML_DQK), (bt, ML_HEADS, 1, 1)
    at_b = lambda b, i: (b, 0, 0, 0)
    in_specs = [pl.BlockSpec((bt, c, ML_MAIN), lambda b, i: (b, i, 0)),
                pl.BlockSpec((bt, c, LANES), lambda b, i: (b, i, POOL_WIDTH // LANES)),
                _layer_block(l, c_blk, at_b), _layer_block(l, n_blk, at_b), _layer_block(l, m_blk, at_b),
                pl.BlockSpec((1, LANES), lambda b, i: (0, 0)),
                pl.BlockSpec((1, LANES), lambda b, i: (0, 0)),
                pl.BlockSpec((1, ML_V_WIDTH), lambda b, i: (0, 0))]
    args = [u3, g3, c0, n0.reshape(nl, B, ML_HEADS, 1, ML_DQK), m0.reshape(nl, B, ML_HEADS, 1, 1), bi, bf,
            gn_w.reshape(1, ML_V_WIDTH)]
    aliases = {}
    if c_carry is not None:
        in_specs.append(pl.BlockSpec(memory_space=pl.ANY))
        args.append(c_carry)
        aliases = {len(args) - 1: 1}
    out, cN, nN, mN = pl.pallas_call(
        functools.partial(_mlstm_kernel, c=c, bt=bt, l_valid=l_valid, has_carry=c_carry is not None),
        grid=(B // bt, L // c),
        in_specs=in_specs,
        out_specs=[pl.BlockSpec((bt, c, ML_V_WIDTH), lambda b, i: (b, i, 0)),
                   _layer_block(l_out, c_blk, at_b),
                   pl.BlockSpec(n_blk, at_b), pl.BlockSpec(m_blk, at_b)],
        out_shape=[jax.ShapeDtypeStruct((B, L, ML_V_WIDTH), BF16),
                   jax.ShapeDtypeStruct((DEPTH, B, ML_HEADS, ML_DQK, ML_DV), F32),
                   jax.ShapeDtypeStruct((B, ML_HEADS, 1, ML_DQK), F32),
                   jax.ShapeDtypeStruct((B, ML_HEADS, 1, 1), F32)],
        scratch_shapes=[pltpu.VMEM(c_blk, F32), pltpu.VMEM(n_blk, F32), pltpu.VMEM(m_blk, F32)],
        input_output_aliases=aliases,
        compiler_params=_cparams(2),
        name="mlstm_mix",
    )(*args)
    return (out.reshape(B * L, ML_V_WIDTH), cN, nN.reshape(B, ML_HEADS, ML_DQK), mN.reshape(B, ML_HEADS))


RW_CHAIN_GROUP = 16

def _rwkv_chain_group(chains, seq, s_s, o_ref, rk_ref, gnw_ref, gnb_ref, *, C):
    rowi = lax.broadcasted_iota(jnp.int32, (C, C), 0)
    coli = lax.broadcasted_iota(jnp.int32, (C, C), 1)
    upper = rowi < coli
    col2 = lax.broadcasted_iota(jnp.int32, (C, 2 * C), 1)
    incl2 = jnp.where(col2 >= C, col2 - C, col2) <= lax.broadcasted_iota(jnp.int32, (C, 2 * C), 0)
    n_sq = max(int(math.log2(C)), 1)
    st = []
    for b, h in chains:
        rs = slice(b * C, (b + 1) * C)
        sl = slice(h * RW_HEAD, (h + 1) * RW_HEAD)
        cut = lambda t, rs=rs, sl=sl: t[rs, sl]
        kk_h = cut(seq["kkv"])
        nrm = jnp.sqrt(jnp.sum(kk_h * kk_h, axis=-1, keepdims=True))
        kap = jnp.where(seq["valid"][rs], kk_h / jnp.maximum(nrm, 1e-12), 0.0)
        k_h, v_h, r_h = cut(seq["kmod"]), cut(seq["v"]), cut(seq["r"])
        gi = cut(seq["gi"])
        b_t = kap * cut(seq["a"]) * gi
        k_t = k_h * gi
        st.append(dict(b=b, h=h, sl=sl, k_h=k_h, v_h=v_h, r_h=r_h,
                       a_t=_bf(-kap * cut(seq["gp"])), r_t=_bf(r_h * cut(seq["g"])),
                       bk=jnp.concatenate([b_t, k_t], axis=0),
                       ge=seq["g"][(b + 1) * C - 1:(b + 1) * C, sl], s0=s_s[b, h]))
    for c in st:
        bkb = _bf(c["bk"])
        mt = _dot_nt(bkb, c["a_t"])
        c["pt"] = jnp.where(upper, mt[:C], 0.0)
        c["akt"] = _bf(jnp.where(upper, mt[C:], 0.0))
        c["a_r"] = jnp.where(incl2, _dot_nt(c["r_t"], bkb), 0.0)
        c["s0b"] = _bf(c["s0"])
        c["vb"] = _bf(c["v_h"])
    for c in st:
        c["xt"] = _dot_nt(c["s0b"], c["a_t"]) + _dot_tn(c["vb"], c["akt"])
    for _ in range(n_sq - 1):
        for c in st:
            z = _dot(_bf(jnp.concatenate([c["pt"], c["xt"]], axis=0)), _bf(c["pt"]))
            c["pt"] = z[:C]
            c["xt"] = c["xt"] + z[C:]
    for c in st:
        c["ut"] = _bf(c["xt"] + _dot(_bf(c["xt"]), _bf(c["pt"])))
    for c in st:
        a_r = c["a_r"]
        c["y"] = (_dot_nt(c["r_t"], c["s0b"]) + _dot_nt(_bf(a_r[:, :C]), c["ut"])
                  + _dot(_bf(a_r[:, C:]), c["vb"]))
        bkg = c["bk"] * c["ge"]
        s_s[c["b"], c["h"]] = (c["s0"] * c["ge"] + _dot(c["ut"], _bf(bkg[:C]))
                               + _dot_tn(c["vb"], _bf(bkg[C:])))
    for c in st:
        y, sl = c["y"], c["sl"]
        mean = jnp.mean(y, axis=-1, keepdims=True)
        yc = y - mean
        var = jnp.mean(yc * yc, axis=-1, keepdims=True)
        yn = yc * lax.rsqrt(var + GN_EPS) * gnw_ref[:, sl] + gnb_ref[:, sl]
        bonus = jnp.sum(c["r_h"] * c["k_h"] * rk_ref[:, sl], axis=-1, keepdims=True) * c["v_h"]
        o_ref[c["b"], :, sl] = (yn + bonus).astype(o_ref.dtype)


def _rwkv_kernel(*refs, C, bt, l_valid, has_carry):
    (u_ref, sh_ref, s0_ref, mu_ref, w0_ref, w2_ref, a0_ref, a2_ref, kk_ref, ka_ref, rk_ref, gnw_ref,
     gnb_ref) = refs[:13]
    o_ref, sN_ref, prev_s, s_s = refs[13 + has_carry:]
    ci = pl.program_id(1)

    @pl.when(ci == 0)
    def _():
        prev_s[...] = sh_ref[...]
        s_s[...] = s0_ref[...]

    N = bt * C
    u = u_ref[...].reshape(N, RW_COLS)
    tok = lax.broadcasted_iota(jnp.int32, (N, 1), 0) % C
    valid = tok < l_valid
    prev = jnp.concatenate([jnp.broadcast_to(prev_s[b], (C, RW_COLS)) for b in range(bt)], axis=0)
    u_prev = jnp.where(tok == 0, prev, pltpu.roll(u, 1, 0))
    for b in range(bt):
        prev_s[b] = u[(b + 1) * C - 1:(b + 1) * C, :]
    us = u + (u_prev - u) * mu_ref[...]
    W = RW_WIDTH
    k = us[:, W:2 * W]
    wd = us[:, 3 * W:3 * W + RW_RANK]
    ad = us[:, 3 * W + RW_RANK:3 * W + 2 * RW_RANK]
    xw = w0_ref[...] + _dot(jnp.tanh(wd), w2_ref[...], HI)
    log_w = -_softplus(-xw) - 0.5
    ld = jnp.where(valid, -jnp.exp(log_w), 0.0)
    a = jax.nn.sigmoid(a0_ref[...] + _dot(ad, a2_ref[...], HI))
    cum = _dot(_seq_prefix_mask(N, C).astype(F32), ld, HI)
    seq = dict(valid=valid, r=us[:, 0:W], v=jnp.where(valid, us[:, 2 * W:3 * W], 0.0), a=a,
               g=jnp.exp(cum), gi=jnp.exp(-cum), gp=jnp.exp(cum - ld), kkv=k * kk_ref[...],
               kmod=jnp.where(valid, k * (1.0 + (a - 1.0) * ka_ref[...]), 0.0))
    chains = [(b, h) for b in range(bt) for h in range(RW_HEADS)]
    group = RW_CHAIN_GROUP * bt
    for i in range(0, len(chains), group):
        _rwkv_chain_group(chains[i:i + group], seq, s_s, o_ref, rk_ref, gnw_ref, gnb_ref, C=C)

    @pl.when(ci == pl.num_programs(1) - 1)
    def _():
        sN_ref[...] = s_s[...]


def rwkv7_mix(u_rw, shift_prev, s_prev, l, l_out, s_carry, mu, w0, w2, a0, a2, k_k, k_a, r_k, gn_w, gn_b, *, B, L,
              l_valid):
    C = min(L, CHUNK_ROWS)
    bt = CHUNK_ROWS // C
    u3 = u_rw.reshape(B, L, RW_COLS)
    vec = lambda n: pl.BlockSpec((1, n), lambda b, i: (0, 0))
    row = lambda t: t.reshape(1, -1)
    s_blk = (bt, RW_HEADS, RW_HEAD, RW_HEAD)
    at_b = lambda b, i: (b, 0, 0, 0)
    in_specs = [pl.BlockSpec((bt, C, RW_COLS), lambda b, i: (b, i, 0)),
                _layer_block(l, (bt, 1, RW_COLS), lambda b, i: (b, 0, 0)),
                _layer_block(l, s_blk, at_b),
                vec(RW_COLS), vec(RW_WIDTH),
                pl.BlockSpec((RW_RANK, RW_WIDTH), lambda b, i: (0, 0)),
                vec(RW_WIDTH),
                pl.BlockSpec((RW_RANK, RW_WIDTH), lambda b, i: (0, 0)),
                vec(RW_WIDTH), vec(RW_WIDTH), vec(RW_WIDTH), vec(RW_WIDTH), vec(RW_WIDTH)]
    args = [u3, shift_prev.reshape(shift_prev.shape[0], B, 1, RW_COLS), s_prev, row(mu), row(w0), w2, row(a0), a2,
            row(k_k), row(k_a), row(r_k), row(gn_w), row(gn_b)]
    aliases = {}
    if s_carry is not None:
        in_specs.append(pl.BlockSpec(memory_space=pl.ANY))
        args.append(s_carry)
        aliases = {len(args) - 1: 1}
    out, sN = pl.pallas_call(
        functools.partial(_rwkv_kernel, C=C, bt=bt, l_valid=l_valid, has_carry=s_carry is not None),
        grid=(B // bt, L // C),
        in_specs=in_specs,
        out_specs=[pl.BlockSpec((bt, C, RW_WIDTH), lambda b, i: (b, i, 0)), _layer_block(l_out, s_blk, at_b)],
        out_shape=[jax.ShapeDtypeStruct((B, L, RW_WIDTH), BF16),
                   jax.ShapeDtypeStruct((DEPTH, B, RW_HEADS, RW_HEAD, RW_HEAD), F32)],
        scratch_shapes=[pltpu.VMEM((bt, 1, RW_COLS), F32), pltpu.VMEM(s_blk, F32)],
        input_output_aliases=aliases,
        compiler_params=_cparams(2),
        name="rwkv7_mix",
    )(*args)
    return out.reshape(B * L, RW_WIDTH), sN


RWS_VECS = 5


def _rwkv_short_kernel(*refs, l_valid, has_carry):
    (ur_ref, uk_ref, uv_ref, ul_ref, shr_ref, shk_ref, shv_ref, shl_ref, mur_ref, muk_ref, muv_ref, mul_ref,
     w0_ref, w2_ref, a0_ref, a2_ref, kk_ref, ka_ref, rk_ref, gnw_ref, gnb_ref, s_ref) = refs[:22]
    o_ref, sN_ref, vec_s, val_s, y_s = refs[22 + has_carry:]
    H2 = LANES // RW_HEAD
    o_ref[...] = jnp.zeros_like(o_ref)

    def shifted(u_ref, sh_ref, mu_ref, t):
        u = u_ref[:, t, :]
        prev = sh_ref[...] if t == 0 else u_ref[:, t - 1, :]
        return u + (prev - u) * mu_ref[...]

    rows = []
    for t in range(l_valid):
        r = shifted(ur_ref, shr_ref, mur_ref, t)
        k = shifted(uk_ref, shk_ref, muk_ref, t)
        v = shifted(uv_ref, shv_ref, muv_ref, t)
        lo = shifted(ul_ref, shl_ref, mul_ref, t)
        xw = w0_ref[...] + _dot(jnp.tanh(lo[:, :RW_RANK]), w2_ref[...], HI)
        w = jnp.exp(-jnp.exp(-_softplus(-xw) - 0.5))
        a = jax.nn.sigmoid(a0_ref[...] + _dot(lo[:, RW_RANK:], a2_ref[...], HI))
        kkv = k * kk_ref[...]
        kmod = k * (1.0 + (a - 1.0) * ka_ref[...])
        for h in range(H2):
            sl = slice(h * RW_HEAD, (h + 1) * RW_HEAD)
            kk_h = kkv[:, sl]
            nrm = jnp.sqrt(jnp.sum(kk_h * kk_h, axis=-1, keepdims=True))
            kap = kk_h / jnp.maximum(nrm, 1e-12)
            for j, x in enumerate((w[:, sl], kap, kap * a[:, sl], kmod[:, sl], r[:, sl])):
                vec_s[t, j, h] = x.T
            val_s[t, h] = v[:, sl].T
        rows.append((r, kmod, v))

    for h in range(H2):
        def body(vi, carry, h=h):
            s = s_ref[h, vi]
            for t in range(l_valid):
                w, kap, bb, kk, rr = (vec_s[t, j, h] for j in range(RWS_VECS))
                sa = jnp.sum(s * kap, axis=0, keepdims=True)
                s = s * w - sa * bb + val_s[t, h, pl.ds(vi, 1), :] * kk
                y_s[t, h, pl.ds(vi, 1), :] = jnp.sum(s * rr, axis=0, keepdims=True)
            sN_ref[h, vi] = s
            return carry
        lax.fori_loop(0, RW_HEAD, body, 0)

    for t in range(l_valid):
        r, kmod, v = rows[t]
        outs = []
        for h in range(H2):
            sl = slice(h * RW_HEAD, (h + 1) * RW_HEAD)
            y = y_s[t, h].T
            mean = jnp.mean(y, axis=-1, keepdims=True)
            yc = y - mean
            var = jnp.mean(yc * yc, axis=-1, keepdims=True)
            yn = yc * lax.rsqrt(var + GN_EPS) * gnw_ref[:, sl] + gnb_ref[:, sl]
            bonus = jnp.sum(r[:, sl] * kmod[:, sl] * rk_ref[:, sl], axis=-1, keepdims=True) * v[:, sl]
            outs.append(yn + bonus)
        o_ref[:, t, :] = jnp.concatenate(outs, axis=1).astype(o_ref.dtype)


def rwkv7_short(u_rw, shift_prev, s_prev_t, l, l_out, s_carry, mu, w0, w2, a0, a2, k_k, k_a, r_k, gn_w, gn_b, *, B, L,
                l_valid):
    assert B == LANES, "the batch must fill the lane dimension"
    H2 = LANES // RW_HEAD
    nblk = RW_WIDTH // LANES
    u3 = u_rw.reshape(B, L, RW_COLS)
    seg = lambda off: pl.BlockSpec((B, L, LANES), lambda hp, off=off: (0, 0, off + hp))
    lora = pl.BlockSpec((B, L, LANES), lambda hp: (0, 0, 3 * nblk))
    sh = lambda off: pl.BlockSpec((None, B, LANES), lambda hp, off=off: (l, 0, off + hp))
    sh_lora = pl.BlockSpec((None, B, LANES), lambda hp: (l, 0, 3 * nblk))
    mus = lambda off: pl.BlockSpec((1, LANES), lambda hp, off=off: (0, off + hp))
    mu_lora = pl.BlockSpec((1, LANES), lambda hp: (0, 3 * nblk))
    vec = pl.BlockSpec((1, LANES), lambda hp: (0, hp))
    mat = pl.BlockSpec((RW_RANK, LANES), lambda hp: (0, hp))
    s_blk = (H2, RW_HEAD, RW_HEAD, B)
    in_specs = [seg(0), seg(nblk), seg(2 * nblk), lora, sh(0), sh(nblk), sh(2 * nblk), sh_lora,
                mus(0), mus(nblk), mus(2 * nblk), mu_lora,
                vec, mat, vec, mat, vec, vec, vec, vec, vec,
                pl.BlockSpec((None,) + s_blk, lambda hp: (l, hp, 0, 0, 0))]
    row = lambda t: t.reshape(1, -1)
    args = [u3, u3, u3, u3, shift_prev, shift_prev, shift_prev, shift_prev, row(mu), row(mu), row(mu), row(mu),
            row(w0), w2, row(a0), a2, row(k_k), row(k_a), row(r_k), row(gn_w), row(gn_b), s_prev_t]
    aliases = {}
    if s_carry is not None:
        in_specs.append(pl.BlockSpec(memory_space=pl.ANY))
        args.append(s_carry)
        aliases = {len(args) - 1: 1}
    out, sN = pl.pallas_call(
        functools.partial(_rwkv_short_kernel, l_valid=l_valid, has_carry=s_carry is not None),
        grid=(RW_HEADS // H2,),
        in_specs=in_specs,
        out_specs=[pl.BlockSpec((B, L, LANES), lambda hp: (0, 0, hp)),
                   pl.BlockSpec((None,) + s_blk, lambda hp: (l_out, hp, 0, 0, 0))],
        out_shape=[jax.ShapeDtypeStruct((B, L, RW_WIDTH), BF16),
                   jax.ShapeDtypeStruct((DEPTH, RW_HEADS, RW_HEAD, RW_HEAD, B), F32)],
        scratch_shapes=[pltpu.VMEM((l_valid, RWS_VECS, H2, RW_HEAD, B), F32),
                        pltpu.VMEM((l_valid, H2, RW_HEAD, B), F32),
                        pltpu.VMEM((l_valid, H2, RW_HEAD, B), F32)],
        input_output_aliases=aliases,
        compiler_params=_cparams(1),
        name="rwkv7_short",
    )(*args)
    return out.reshape(B * L, RW_WIDTH), sN


ATTN_SEQS_PER_STEP = 8


def _attn_kernel(q_ref, kv_ref, o_ref):
    heads = [slice(h * XA_HEAD_DIM, (h + 1) * XA_HEAD_DIM) for h in range(XA_HEADS)]
    s = [_dot_nt(q_ref[0, :, hs], _bf(kv_ref[0, :, hs])) * (XA_HEAD_DIM ** -0.5) for hs in heads]
    e = [jnp.exp(t - jnp.max(t, axis=-1, keepdims=True)) for t in s]
    p = [t / jnp.sum(t, axis=-1, keepdims=True) for t in e]
    for h, hs in enumerate(heads):
        vs = slice(XA_WIDTH + h * XA_HEAD_DIM, XA_WIDTH + (h + 1) * XA_HEAD_DIM)
        o_ref[0, :, hs] = _dot(_bf(p[h]), _bf(kv_ref[0, :, vs])).astype(o_ref.dtype)


def _attn_short_kernel(q_ref, k_ref, v_ref, o_ref, *, bt):
    L = q_ref.shape[1]
    rows, cols = XA_HEADS * L, MEM_LEN * XA_HEADS
    own = (lax.broadcasted_iota(jnp.int32, (rows, cols), 0) // L
           == lax.broadcasted_iota(jnp.int32, (rows, cols), 1) % XA_HEADS)
    heads = [slice(h * XA_HEAD_DIM, (h + 1) * XA_HEAD_DIM) for h in range(XA_HEADS)]
    qs = [jnp.concatenate([q_ref[b, :, hs] for hs in heads], axis=0) for b in range(bt)]
    s = [_dot_nt(qs[b], _bf(k_ref[b].reshape(cols, XA_HEAD_DIM))) * (XA_HEAD_DIM ** -0.5) for b in range(bt)]
    s = [jnp.where(own, t, -jnp.inf) for t in s]
    e = [jnp.exp(t - jnp.max(t, axis=-1, keepdims=True)) for t in s]
    p = [t / jnp.sum(t, axis=-1, keepdims=True) for t in e]
    for b in range(bt):
        out = _dot(_bf(p[b]), _bf(v_ref[b].reshape(cols, XA_HEAD_DIM)))
        for h, hs in enumerate(heads):
            o_ref[b, :, hs] = out[h * L:(h + 1) * L].astype(o_ref.dtype)


def cross_attention(q, mem, l, *, B, L):
    q3 = q.reshape(B, L, XA_WIDTH)
    if l is None:
        bt, tq, body, mems = 1, min(L, 512), _attn_kernel, [mem]
        mem_specs = [pl.BlockSpec((bt, MEM_LEN, 2 * XA_WIDTH), lambda b, i: (b, 0, 0))]
    else:
        bt, tq, mems = ATTN_SEQS_PER_STEP, L, list(mem)
        body = functools.partial(_attn_short_kernel, bt=bt)
        mem_specs = [pl.BlockSpec((None, bt, MEM_LEN, XA_HEADS, XA_HEAD_DIM), lambda b, i: (l, b, 0, 0, 0))] * 2
    qo_spec = pl.BlockSpec((bt, tq, XA_WIDTH), lambda b, i: (b, i, 0))
    out = pl.pallas_call(
        body,
        grid=(B // bt, L // tq),
        in_specs=[qo_spec] + mem_specs,
        out_specs=qo_spec,
        out_shape=jax.ShapeDtypeStruct((B, L, XA_WIDTH), BF16),
        compiler_params=_cparams(2),
        name="cross_attention",
    )(q3, *mems)
    return out.reshape(B * L, XA_WIDTH)


ROUTER_GROUP_LANE = MOE_EXPERTS


MOE_TB = 512
MOE_TB_GATHER = 1024
MOE_SUB = 256
MOE_TE_LONG = 1024
MOE_TE_SHORT = 512
ROUTE_GROUP_LANE = 0
ROUTE_RANK_LANE = 1
COMB_PIECES = 3


def _router_kernel(x_ref, g_ref, w_ref, b_ref, xn_ref, comb_ref, route_ref, cnt_s):
    @pl.when(pl.program_id(0) == 0)
    def _():
        cnt_s[...] = jnp.zeros_like(cnt_s)

    xn = _rms(x_ref[...], g_ref[...])
    xn_ref[...] = _bf(xn)
    z = _dot(xn, w_ref[...], HI) + b_ref[...]
    tm = z.shape[0]
    lane = lax.broadcasted_iota(jnp.int32, z.shape, 1).astype(F32)
    big = float(LANES)
    neg = -jnp.inf
    first = lambda mask: jnp.min(jnp.where(mask, lane, big), axis=-1, keepdims=True)
    is_g = jnp.logical_and(lane >= ROUTER_GROUP_LANE, lane < ROUTER_GROUP_LANE + MOE_GROUPS)
    zg = jnp.where(is_g, z, neg)
    mg = jnp.max(zg, axis=-1, keepdims=True)
    grp = first(zg == mg) - ROUTER_GROUP_LANE
    p_grp = 1.0 / jnp.sum(jnp.exp(zg - mg), axis=-1, keepdims=True)
    lo = grp * MOE_PER_GROUP
    ze = jnp.where(jnp.logical_and(lane >= lo, lane < lo + MOE_PER_GROUP), z, neg)
    t1 = jnp.max(ze, axis=-1, keepdims=True)
    i1 = first(ze == t1)
    ze2 = jnp.where(lane == i1, neg, ze)
    t2 = jnp.max(ze2, axis=-1, keepdims=True)
    i2 = first(ze2 == t2)
    e2 = jnp.exp(t2 - t1)
    g1 = p_grp / (1.0 + e2)
    comb = jnp.where(lane == i1, g1, 0.0) + jnp.where(lane == i2, g1 * e2, 0.0)
    c_hi = _bf(comb)
    rest = comb - c_hi.astype(F32)
    c_mid = _bf(rest)
    comb_ref[...] = jnp.concatenate([c_hi, c_mid, _bf(rest - c_mid.astype(F32))], axis=1)
    onehot = jnp.where(lane == grp, 1.0, 0.0)
    r = lax.broadcasted_iota(jnp.int32, (tm, tm), 0)
    c = lax.broadcasted_iota(jnp.int32, (tm, tm), 1)
    before = _dot(_bf(jnp.where(c < r, 1.0, 0.0)), _bf(onehot)) + cnt_s[...]
    rank = jnp.sum(onehot * before, axis=-1, keepdims=True)
    cnt_s[...] = cnt_s[...] + jnp.sum(onehot, axis=0, keepdims=True)
    route_ref[...] = jnp.where(lane == ROUTE_GROUP_LANE, grp, 0.0) + jnp.where(lane == ROUTE_RANK_LANE, rank, 0.0)


def moe_router(x, g, w_r1, b_r1, w_r2, b_r2):
    T, D = x.shape
    tm = MOE_TB
    pad = LANES - MOE_EXPERTS - MOE_GROUPS
    w = jnp.concatenate([w_r2, w_r1, jnp.zeros((D, pad), F32)], axis=1)
    b = jnp.concatenate([b_r2, b_r1, jnp.zeros((pad,), F32)]).reshape(1, LANES)
    return pl.pallas_call(
        _router_kernel,
        grid=(T // tm,),
        in_specs=[pl.BlockSpec((tm, D), lambda i: (i, 0)),
                  pl.BlockSpec((1, D), lambda i: (0, 0)),
                  pl.BlockSpec((D, LANES), lambda i: (0, 0)),
                  pl.BlockSpec((1, LANES), lambda i: (0, 0))],
        out_specs=[pl.BlockSpec((tm, D), lambda i: (i, 0)),
                   pl.BlockSpec((tm, COMB_PIECES * LANES), lambda i: (i, 0)),
                   pl.BlockSpec((tm, LANES), lambda i: (i, 0))],
        out_shape=[jax.ShapeDtypeStruct((T, D), BF16), jax.ShapeDtypeStruct((T, COMB_PIECES * LANES), BF16),
                   jax.ShapeDtypeStruct((T, LANES), F32)],
        scratch_shapes=[pltpu.VMEM((1, LANES), F32)],
        compiler_params=_cparams(1),
        name="moe_router",
    )(x, g.reshape(1, D), w, b)


def _moe_plan(route, T, te):
    i32 = jnp.int32
    grp = route[:, ROUTE_GROUP_LANE].astype(i32)
    rank = route[:, ROUTE_RANK_LANE].astype(i32)
    rows = T + MOE_GROUPS * te
    n_sub = rows // MOE_SUB
    n_tiles = rows // te
    onehot = (grp[:, None] == jnp.arange(MOE_GROUPS, dtype=i32)[None]).astype(i32)
    seg_rows = (onehot.sum(axis=0) + te - 1) // te * te
    seg_end = jnp.cumsum(seg_rows)
    seg_start = seg_end - seg_rows
    dest = seg_start[grp] + rank
    group_of = lambda row: jnp.minimum(jnp.sum(row[:, None] >= seg_end[None], axis=1), MOE_GROUPS - 1).astype(i32)
    sub_row = jnp.arange(n_sub, dtype=i32) * MOE_SUB
    sub_g = group_of(sub_row)
    sub_valid = sub_row < seg_end[-1]
    r0 = sub_row - seg_start[sub_g]

    def overlap(tb):
        blk_cnt = onehot.reshape(T // tb, tb, MOE_GROUPS).sum(axis=1)
        cum_blk = jnp.concatenate([jnp.zeros((1, MOE_GROUPS), i32), jnp.cumsum(blk_cnt, axis=0)])
        lo = cum_blk[:-1][:, sub_g].T
        hi = cum_blk[1:][:, sub_g].T
        return (lo < (r0 + MOE_SUB)[:, None]) & (hi > r0[:, None]) & sub_valid[:, None]

    g_mask = overlap(MOE_TB_GATHER)
    first_col = (jnp.arange(T // MOE_TB_GATHER) == 0)[None]
    g_mask = g_mask | (first_col & ~g_mask.any(axis=1, keepdims=True))
    n_items = MOE_GROUPS * (T // MOE_TB) + 2 * n_sub

    def items(mask, ncol):
        flat = jnp.nonzero(mask.reshape(-1), size=n_items, fill_value=-1)[0].astype(i32)
        valid = flat >= 0
        flat = jnp.where(valid, flat, jnp.max(flat))
        major, minor = flat // ncol, flat % ncol
        prev = jnp.concatenate([jnp.full((1,), -1, i32), major[:-1]])
        nxt = jnp.concatenate([major[1:], jnp.full((1,), -1, i32)])
        nvalid = jnp.concatenate([valid[1:], jnp.zeros((1,), bool)])
        first = (major != prev) & valid
        last = ((major != nxt) | ~nvalid) & valid
        return major, minor, first.astype(i32), last.astype(i32), valid.astype(i32)

    g_sub, g_blk, g_first, _, g_valid = items(g_mask, T // MOE_TB_GATHER)
    s_blk, s_sub, s_first, s_last, s_valid = items(overlap(MOE_TB).T, n_sub)
    tile_row = jnp.arange(n_tiles, dtype=i32) * te
    return dict(dest=dest, rows=rows, gather=(g_sub, g_blk, g_first, g_valid),
                scatter=(s_sub, s_blk, s_first, s_last, s_valid),
                tile_group=group_of(tile_row), tile_valid=(tile_row < seg_end[-1]).astype(i32))


def _moe_gather_kernel(sub_ref, blk_ref, first_ref, valid_ref, dest_ref, xn_ref, comb_ref, xs_ref, cs_ref):
    w = pl.program_id(0)

    @pl.when(valid_ref[w] == 1)
    def _():
        rows = sub_ref[w] * MOE_SUB + lax.broadcasted_iota(jnp.int32, (MOE_SUB, MOE_TB_GATHER), 0)
        hit = _bf(jnp.where(dest_ref[0] == rows, 1.0, 0.0))
        gx = _bf(_dot(hit, xn_ref[...]))
        g3 = _dot(hit, comb_ref[...])
        gc = g3[:, :LANES] + g3[:, LANES:2 * LANES] + g3[:, 2 * LANES:]

        @pl.when(first_ref[w] == 1)
        def _():
            xs_ref[...] = gx
            cs_ref[...] = gc

        @pl.when(first_ref[w] == 0)
        def _():
            xs_ref[...] = xs_ref[...] + gx
            cs_ref[...] = cs_ref[...] + gc


def moe_gather(xn, comb, plan):
    T, D = xn.shape
    g_sub, g_blk, g_first, g_valid = plan["gather"]
    rows = plan["rows"]
    tb = MOE_TB_GATHER
    dest3 = plan["dest"].reshape(T // tb, 1, tb)
    grid_spec = pltpu.PrefetchScalarGridSpec(
        num_scalar_prefetch=4,
        grid=(g_sub.shape[0],),
        in_specs=[pl.BlockSpec((1, 1, tb), lambda w, s, b, f, v: (b[w], 0, 0)),
                  pl.BlockSpec((tb, D), lambda w, s, b, f, v: (b[w], 0)),
                  pl.BlockSpec((tb, COMB_PIECES * LANES), lambda w, s, b, f, v: (b[w], 0))],
        out_specs=[pl.BlockSpec((MOE_SUB, D), lambda w, s, b, f, v: (s[w], 0)),
                   pl.BlockSpec((MOE_SUB, LANES), lambda w, s, b, f, v: (s[w], 0))])
    return pl.pallas_call(
        _moe_gather_kernel,
        grid_spec=grid_spec,
        out_shape=[jax.ShapeDtypeStruct((rows, D), BF16), jax.ShapeDtypeStruct((rows, LANES), F32)],
        compiler_params=_cparams(1),
        name="moe_gather",
    )(g_sub, g_blk, g_first, g_valid, dest3, xn, comb)


def _moe_group_experts_kernel(tg_ref, tv_ref, xs_ref, cs_ref, wg_ref, wu_ref, wd_ref, yh_ref, yl_ref, acc):
    i = pl.program_id(0)
    e = pl.program_id(1)

    @pl.when(e == 0)
    def _():
        acc[...] = jnp.zeros_like(acc)

    @pl.when(tv_ref[i] == 1)
    def _():
        xs = xs_ref[...]
        cs = cs_ref[...]
        lane = lax.broadcasted_iota(jnp.int32, cs.shape, 1)
        ce = jnp.sum(jnp.where(lane == tg_ref[i] * MOE_PER_GROUP + e, cs, 0.0), axis=-1, keepdims=True)
        hid = jax.nn.silu(_dot(xs, _bf(wg_ref[...]))) * _dot(xs, _bf(wu_ref[...])) * ce
        acc[...] += _dot(_bf(hid), _bf(wd_ref[...]))

    @pl.when(e == pl.num_programs(1) - 1)
    def _():
        y = acc[...]
        hi = _bf(y)
        yh_ref[...] = hi
        yl_ref[...] = _bf(y - hi.astype(F32))


def moe_group_experts(xs, cs, plan, wg, wu, wd, l, te):
    rows, D = xs.shape
    expert = lambda i, e, tg, tv: (l, tg[i] * MOE_PER_GROUP + e, 0, 0)
    w_in = pl.BlockSpec((None, None, D, MOE_HIDDEN), expert)
    grid_spec = pltpu.PrefetchScalarGridSpec(
        num_scalar_prefetch=2,
        grid=(rows // te, MOE_PER_GROUP),
        in_specs=[pl.BlockSpec((te, D), lambda i, e, tg, tv: (i, 0)),
                  pl.BlockSpec((te, LANES), lambda i, e, tg, tv: (i, 0)),
                  w_in, w_in,
                  pl.BlockSpec((None, None, MOE_HIDDEN, D), expert)],
        out_specs=[pl.BlockSpec((te, D), lambda i, e, tg, tv: (i, 0))] * 2,
        scratch_shapes=[pltpu.VMEM((te, D), F32)])
    return pl.pallas_call(
        _moe_group_experts_kernel,
        grid_spec=grid_spec,
        out_shape=[jax.ShapeDtypeStruct((rows, D), BF16)] * 2,
        compiler_params=_cparams(2),
        name="moe_group_experts",
    )(plan["tile_group"], plan["tile_valid"], xs, cs, wg, wu, wd)


def _moe_scatter_kernel(sub_ref, blk_ref, first_ref, last_ref, valid_ref, dest_ref, yh_ref, yl_ref, x_ref, gf_ref,
                        o_ref, *, final_norm):
    w = pl.program_id(0)

    @pl.when(valid_ref[w] == 1)
    def _():
        cols = sub_ref[w] * MOE_SUB + lax.broadcasted_iota(jnp.int32, (MOE_TB, MOE_SUB), 1)
        pick = _bf(jnp.where(dest_ref[...] == cols, 1.0, 0.0))
        upd = _dot(pick, yh_ref[...]) + _dot(pick, yl_ref[...])

        @pl.when(first_ref[w] == 1)
        def _():
            o_ref[...] = x_ref[...] + upd

        @pl.when(first_ref[w] == 0)
        def _():
            o_ref[...] += upd

        if final_norm:
            @pl.when(last_ref[w] == 1)
            def _():
                o_ref[...] = _rms(o_ref[...], gf_ref[...])


def moe_scatter(yh, yl, x, plan, g_final, *, final_norm):
    T, D = x.shape
    s_sub, s_blk, s_first, s_last, s_valid = plan["scatter"]
    dest_col = plan["dest"].reshape(T, 1)
    im = lambda f: (lambda w, s, b, fi, la, v: f(w, s, b))
    grid_spec = pltpu.PrefetchScalarGridSpec(
        num_scalar_prefetch=5,
        grid=(s_sub.shape[0],),
        in_specs=[pl.BlockSpec((MOE_TB, 1), im(lambda w, s, b: (b[w], 0))),
                  pl.BlockSpec((MOE_SUB, D), im(lambda w, s, b: (s[w], 0))),
                  pl.BlockSpec((MOE_SUB, D), im(lambda w, s, b: (s[w], 0))),
                  pl.BlockSpec((MOE_TB, D), im(lambda w, s, b: (b[w], 0))),
                  pl.BlockSpec((1, D), im(lambda w, s, b: (0, 0)))],
        out_specs=pl.BlockSpec((MOE_TB, D), im(lambda w, s, b: (b[w], 0))))
    return pl.pallas_call(
        functools.partial(_moe_scatter_kernel, final_norm=final_norm),
        grid_spec=grid_spec,
        out_shape=jax.ShapeDtypeStruct((T, D), F32),
        compiler_params=_cparams(1),
        name="moe_scatter",
    )(s_sub, s_blk, s_first, s_last, s_valid, dest_col, yh, yl, x, g_final.reshape(1, D))


def hmoe_block(x, p, l, *, final_norm):
    T = x.shape[0]
    te = MOE_TE_LONG if T >= 4 * MOE_TE_LONG else MOE_TE_SHORT
    xn, comb, route = moe_router(x, p["g_moe"][l], p["moe_wr1"][l], p["moe_br1"][l], p["moe_wr2"][l],
                                 p["moe_br2"][l])
    plan = _moe_plan(route, T, te)
    xs, cs = moe_gather(xn, comb, plan)
    yh, yl = moe_group_experts(xs, cs, plan, p["moe_wg"], p["moe_wu"], p["moe_wd"], l, te)
    return moe_scatter(yh, yl, x, plan, p["g_final"], final_norm=final_norm)


def _layer_weights(l, p):
    w_in = p["w_in"][l]
    c0 = RW_COLS
    c1 = c0 + ML_MAIN
    c2 = c1 + 2 * ML_HEADS
    c3 = c2 + POOL_WIDTH
    w_pl = jnp.concatenate([w_in[:, c2:c3], w_in[:, c1:c2], jnp.zeros((D_MODEL, LANES - 2 * ML_HEADS), F32)], axis=1)
    return dict(
        w_rw=_bf(w_in[:, :c0]), w_ml=_bf(w_in[:, c0:c1]), w_pl=_bf(w_pl), w_gate=_bf(w_in[:, c3:]),
        w_up_rwkv=_bf(p["w_up_rwkv"][l]), w_up_mlstm=_bf(p["w_up_mlstm"][l]), w_up_pool=_bf(p["w_up_pool"][l]),
        w_out=_bf(p["w_out"][l]), pool_w=_bf(p["pool_w"][l]),
        xa_wq=_bf(p["xa_wq"][l]), xa_wo=_bf(p["xa_wo"][l]),
        xa_wkv=_bf(jnp.concatenate([p["xa_wk"][l], p["xa_wv"][l]], axis=1)),
    )


def _trunk_layer(x, mem, mem_l, st, carry, *, l, sl, p, lw, B, L, l_valid, start, final_norm):
    g_mix = p["g_mix"][l]
    u_rw = norm_matmul(x, g_mix, lw["w_rw"], tn=TN_RWKV, name="in_rwkv")
    u_ml = norm_matmul(x, g_mix, lw["w_ml"], tn=TN_MLSTM, name="in_mlstm")
    u_pl = norm_matmul(x, g_mix, lw["w_pl"], tn=POOL_WIDTH + LANES, name="in_pool")
    gates = norm_matmul(x, g_mix, lw["w_gate"], tn=TN_GATES, out_dtype=BF16, act="sigmoid", name="in_gates")

    rwkv = rwkv7_short if L < CHUNK_ROWS else rwkv7_mix
    o_rw, rw_s = rwkv(u_rw, st["rw_shift"], st["rw_s"], sl, l, carry["rw_s"], p["rw_mu"][l], p["rw_w0"][l],
                      p["rw_w2"][l], p["rw_a0"][l], p["rw_a2"][l], p["rw_k_k"][l], p["rw_k_a"][l],
                      p["rw_r_k"][l], p["rw_gn_w"][l], p["rw_gn_b"][l], B=B, L=L, l_valid=l_valid)
    rw_shift = u_rw.reshape(B, L, RW_COLS)[:, l_valid - 1]
    o_ml, ml_c, ml_n, ml_m = mlstm_mix(u_ml, u_pl, st["ml_c"], st["ml_n"], st["ml_m"], sl, l, carry["ml_c"],
                                       p["ml_b_i"][l], p["ml_b_f"][l], p["ml_gn_w"][l], B=B, L=L, l_valid=l_valid)
    o_pl, pool_buf = pool_mix(u_pl, st["pool"], sl, lw["pool_w"], p["pool_scale"][l], B=B, L=L, l_valid=l_valid,
                              start=start)
    merged = merge_branches(o_rw, o_ml, o_pl, gates, lw["w_up_rwkv"], lw["w_up_mlstm"], lw["w_up_pool"])
    x = matmul_residual(merged, lw["w_out"], x)

    q = norm_matmul(x, p["g_xa"][l], lw["xa_wq"], tn=XA_WIDTH, out_dtype=BF16, name="xa_q")
    att = cross_attention(q, mem, mem_l, B=B, L=L)
    x = matmul_residual(att, lw["xa_wo"], x)

    x = hmoe_block(x, p, l, final_norm=final_norm)
    return x, dict(rw_s=rw_s, ml_c=ml_c), (rw_shift, ml_n, ml_m, pool_buf)


def kernel(x_prompt, x_sample, cache_mem_k, cache_mem_v, state_rwkv_s, state_rwkv_shift, state_mlstm_c, state_mlstm_n, state_mlstm_m, state_pool, mem_prompt, g_mix, w_in, rw_mu, rw_w0, rw_w2, rw_a0, rw_a2, rw_k_k, rw_k_a, rw_r_k, rw_gn_w, rw_gn_b, ml_b_i, ml_b_f, ml_gn_w, pool_w, pool_scale, w_up_rwkv, w_up_mlstm, w_up_pool, w_out, g_xa, g_mem, xa_wq, xa_wk, xa_wv, xa_wo, g_moe, moe_wr1, moe_br1, moe_wr2, moe_br2, moe_wg, moe_wu, moe_wd, g_final):
    p = dict(g_mix=g_mix, w_in=w_in, rw_mu=rw_mu, rw_w0=rw_w0, rw_w2=rw_w2, rw_a0=rw_a0, rw_a2=rw_a2, rw_k_k=rw_k_k,
             rw_k_a=rw_k_a, rw_r_k=rw_r_k, rw_gn_w=rw_gn_w, rw_gn_b=rw_gn_b, ml_b_i=ml_b_i, ml_b_f=ml_b_f,
             ml_gn_w=ml_gn_w, pool_w=pool_w, pool_scale=pool_scale, w_up_rwkv=w_up_rwkv, w_up_mlstm=w_up_mlstm,
             w_up_pool=w_up_pool, w_out=w_out, g_xa=g_xa, g_mem=g_mem, xa_wq=xa_wq, xa_wk=xa_wk, xa_wv=xa_wv,
             xa_wo=xa_wo, g_moe=g_moe, moe_wr1=moe_wr1, moe_br1=moe_br1, moe_wr2=moe_wr2, moe_br2=moe_br2,
             moe_wg=moe_wg, moe_wu=moe_wu, moe_wd=moe_wd, g_final=g_final)
    Bp, Lp, D = x_prompt.shape
    Bs, Ls, _ = x_sample.shape
    Ls_pad = -(-Ls // SUBLANES) * SUBLANES
    yp = x_prompt.reshape(Bp * Lp, D)
    ys = jnp.pad(x_sample, ((0, 0), (0, Ls_pad - Ls), (0, 0))).reshape(Bs * Ls_pad, D)
    zeros = lambda *s: jnp.zeros((1,) + s, F32)
    st_p = dict(rw_shift=zeros(Bp, RW_COLS), rw_s=zeros(Bp, RW_HEADS, RW_HEAD, RW_HEAD),
                ml_c=zeros(Bp, ML_HEADS, ML_DQK, ML_DV), ml_n=zeros(Bp, ML_HEADS, ML_DQK), ml_m=zeros(Bp, ML_HEADS),
                pool=zeros(Bp, POOL_BUF, POOL_WIDTH))
    batch_minor = (0, 2, 3, 4, 1)
    st_s = dict(rw_shift=state_rwkv_shift, rw_s=jnp.transpose(state_rwkv_s, batch_minor), ml_c=state_mlstm_c,
                ml_n=state_mlstm_n, ml_m=state_mlstm_m, pool=state_pool)
    carry_p = carry_s = dict(rw_s=None, ml_c=None)
    small_p = [[] for _ in range(6)]
    small_s = [[] for _ in range(4)]
    for l in range(DEPTH):
        lw = _layer_weights(l, p)
        final = l == DEPTH - 1
        kv = norm_matmul(mem_prompt.reshape(Bp * MEM_LEN, D), g_mem[l], lw["xa_wkv"], tn=2 * XA_WIDTH,
                         name="memory_kv")
        kv3 = kv.reshape(Bp, MEM_LEN, 2 * XA_WIDTH)
        yp, carry_p, small = _trunk_layer(yp, kv3, None, st_p, carry_p, l=l, sl=0, p=p, lw=lw, B=Bp, L=Lp,
                                          l_valid=Lp, start=0, final_norm=final)
        mk = kv[:, :XA_WIDTH].reshape(Bp, MEM_LEN, XA_HEADS, XA_HEAD_DIM)
        mv = kv[:, XA_WIDTH:].reshape(Bp, MEM_LEN, XA_HEADS, XA_HEAD_DIM)
        for acc, t in zip(small_p, small + (mk, mv)):
            acc.append(t)
        ys, carry_s, small = _trunk_layer(ys, (cache_mem_k, cache_mem_v), l, st_s, carry_s, l=l, sl=l, p=p, lw=lw,
                                          B=Bs, L=Ls_pad, l_valid=Ls, start=PAST_LEN, final_norm=final)
        for acc, t in zip(small_s, small):
            acc.append(t)
    y_prompt = yp.reshape(Bp, Lp, D)
    y_sample = ys.reshape(Bs, Ls_pad, D)[:, :Ls]
    p_sh, p_n, p_m, p_pool, p_mk, p_mv = [jnp.stack(t) for t in small_p]
    s_sh, s_n, s_m, s_pool = [jnp.stack(t) for t in small_s]
    return (y_prompt, y_sample, carry_p["rw_s"], p_sh, carry_p["ml_c"], p_n, p_m, p_pool, p_mk, p_mv,
            jnp.transpose(carry_s["rw_s"], (0, 4, 1, 2, 3)), s_sh, carry_s["ml_c"], s_n, s_m, s_pool)
```

```python
import functools
import math

import jax
import jax.numpy as jnp
from jax import lax
from jax.experimental import pallas as pl
from jax.experimental.pallas import tpu as pltpu

F32 = jnp.float32
BF16 = jnp.bfloat16
HI = lax.Precision.HIGHEST

D_MODEL = 2048
DEPTH = 2
PAST_LEN = 16384
RW_HEAD = 64
RW_WIDTH = 1024
RW_HEADS = 16
RW_RANK = 64
RW_COLS = 3 * RW_WIDTH + 2 * RW_RANK
GN_EPS = 64e-5
ML_HEADS = 4
ML_DQK = 128
ML_DV = 256
ML_QK_WIDTH = 512
ML_V_WIDTH = 1024
ML_MAIN = 2 * ML_QK_WIDTH + 2 * ML_V_WIDTH
POOL_WIDTH = 1024
POOL_WINDOWS = (2, 4, 8, 16)
POOL_GW = 256
POOL_BUF = 15
POOL_HIST = 16
POOL_TAIL = 24
MEM_LEN = 256
XA_HEADS = 4
XA_HEAD_DIM = 128
XA_WIDTH = 512
MOE_GROUPS = 4
MOE_PER_GROUP = 8
MOE_EXPERTS = 32
MOE_HIDDEN = 256
RMS_EPS = 1e-6
LANES = 128
SUBLANES = 8
VMEM_LIMIT = 56 * 1024 * 1024
ROW_TILE = 1024
CHUNK_ROWS = 64
TN_RWKV = 640
TN_MLSTM = 1024
TN_GATES = 1024


def _cparams(n_axes):
    return pltpu.CompilerParams(dimension_semantics=("arbitrary",) * n_axes, vmem_limit_bytes=VMEM_LIMIT)


def _dot(a, b, precision=None):
    return jnp.dot(a, b, preferred_element_type=F32, precision=precision)


def _dot_nt(a, b, precision=None):
    return lax.dot_general(a, b, (((1,), (1,)), ((), ())), preferred_element_type=F32, precision=precision)


def _dot_tn(a, b, precision=None):
    return lax.dot_general(a, b, (((0,), (0,)), ((), ())), preferred_element_type=F32, precision=precision)


def _bf(x):
    return x.astype(BF16)


def _rms(x, g):
    return x * lax.rsqrt(jnp.mean(x * x, axis=-1, keepdims=True) + RMS_EPS) * g


def _log_sigmoid(x):
    return jnp.minimum(x, 0.0) - jnp.log1p(jnp.exp(-jnp.abs(x)))


def _softplus(x):
    return jnp.maximum(x, 0.0) + jnp.log1p(jnp.exp(-jnp.abs(x)))


def _seq_prefix_mask(n, c):
    r = lax.broadcasted_iota(jnp.int32, (n, n), 0)
    q = lax.broadcasted_iota(jnp.int32, (n, n), 1)
    return jnp.logical_and(q <= r, q // c == r // c)


def _layer_block(l, blk, idx):
    return pl.BlockSpec((None,) + tuple(blk), lambda b, i: (l,) + tuple(idx(b, i)))


def _norm_matmul_kernel(x_ref, g_ref, w_ref, o_ref, xn_ref, *, act):
    @pl.when(pl.program_id(1) == 0)
    def _():
        xn_ref[...] = _bf(_rms(x_ref[...], g_ref[...]))

    acc = _dot(xn_ref[...], w_ref[...])
    if act == "sigmoid":
        acc = jax.nn.sigmoid(acc)
    o_ref[...] = acc.astype(o_ref.dtype)


def norm_matmul(x, g, w, *, tn, name, out_dtype=F32, act=None):
    T, D = x.shape
    N = w.shape[1]
    tm = min(T, ROW_TILE)
    return pl.pallas_call(
        functools.partial(_norm_matmul_kernel, act=act),
        grid=(T // tm, N // tn),
        in_specs=[pl.BlockSpec((tm, D), lambda i, j: (i, 0)),
                  pl.BlockSpec((1, D), lambda i, j: (0, 0)),
                  pl.BlockSpec((D, tn), lambda i, j: (0, j))],
        out_specs=pl.BlockSpec((tm, tn), lambda i, j: (i, j)),
        out_shape=jax.ShapeDtypeStruct((T, N), out_dtype),
        scratch_shapes=[pltpu.VMEM((tm, D), BF16)],
        compiler_params=_cparams(2),
        name=name,
    )(x, g.reshape(1, D), w)


def _matmul_residual_kernel(a_ref, w_ref, r_ref, o_ref):
    o_ref[...] = r_ref[...] + _dot(a_ref[...], w_ref[...])


def matmul_residual(a, w, res, *, tn=1024):
    T, K = a.shape
    N = w.shape[1]
    tm = min(T, ROW_TILE)
    return pl.pallas_call(
        _matmul_residual_kernel,
        grid=(T // tm, N // tn),
        in_specs=[pl.BlockSpec((tm, K), lambda i, j: (i, 0)),
                  pl.BlockSpec((K, tn), lambda i, j: (0, j)),
                  pl.BlockSpec((tm, tn), lambda i, j: (i, j))],
        out_specs=pl.BlockSpec((tm, tn), lambda i, j: (i, j)),
        out_shape=jax.ShapeDtypeStruct((T, N), F32),
        compiler_params=_cparams(2),
        name="matmul_residual",
    )(a, w, res)


def _merge_kernel(orw_ref, oml_ref, opl_ref, g0_ref, g1_ref, g2_ref, wr_ref, wm_ref, wp_ref, o_ref):
    m = (g0_ref[...].astype(F32) * _dot(orw_ref[...], wr_ref[...])
         + g1_ref[...].astype(F32) * _dot(oml_ref[...], wm_ref[...])
         + g2_ref[...].astype(F32) * _dot(opl_ref[...], wp_ref[...]))
    o_ref[...] = _bf(m)


def merge_branches(o_rw, o_ml, o_pl, gates, w_rw, w_ml, w_pl, *, tn=1024):
    T, K = o_rw.shape
    tm = min(T, ROW_TILE)
    nj = D_MODEL // tn
    act = pl.BlockSpec((tm, K), lambda i, j: (i, 0))
    wsp = pl.BlockSpec((K, tn), lambda i, j: (0, j))
    gate = lambda b: pl.BlockSpec((tm, tn), lambda i, j, b=b: (i, b * nj + j))
    return pl.pallas_call(
        _merge_kernel,
        grid=(T // tm, nj),
        in_specs=[act, act, act, gate(0), gate(1), gate(2), wsp, wsp, wsp],
        out_specs=pl.BlockSpec((tm, tn), lambda i, j: (i, j)),
        out_shape=jax.ShapeDtypeStruct((T, D_MODEL), BF16),
        compiler_params=_cparams(2),
        name="merge_branches",
    )(o_rw, o_ml, o_pl, gates, gates, gates, w_rw, w_ml, w_pl)


def _pool_kernel(hist_ref, u_ref, w_ref, sc_ref, o_ref, tail_ref, e_ref, *, tl, bt, start):
    li = pl.program_id(1)

    @pl.when(li == 0)
    def _():
        e_ref[:, 0:POOL_HIST, :] = hist_ref[...]

    pos = start + li * tl + lax.broadcasted_iota(jnp.int32, (tl, 1), 0)
    for b in range(bt):
        u = u_ref[b]
        e_ref[b, POOL_HIST:, :] = u
        e = e_ref[b]
        s2 = e + pltpu.roll(e, 1, 0)
        s4 = s2[:, POOL_GW:] + pltpu.roll(s2[:, POOL_GW:], 2, 0)
        s8 = s4[:, POOL_GW:] + pltpu.roll(s4[:, POOL_GW:], 4, 0)
        s16 = s8[:, POOL_GW:] + pltpu.roll(s8[:, POOL_GW:], 8, 0)
        sums = (s2[:, :POOL_GW], s4[:, :POOL_GW], s8[:, :POOL_GW], s16)
        for g, win in enumerate(POOL_WINDOWS):
            cols = slice(g * POOL_GW, (g + 1) * POOL_GW)
            cnt = jnp.minimum(win, pos + 1).astype(F32)
            d = sums[g][POOL_HIST:, :] / cnt - u[:, cols]
            out = _dot(_bf(d), w_ref[g]) * sc_ref[:, cols]
            o_ref[b, :, cols] = out.astype(o_ref.dtype)
        tail_ref[b] = e[tl + POOL_HIST - POOL_TAIL:, :]
        e_ref[b, 0:POOL_HIST, :] = e[tl:, :]


def pool_mix(u_pl, buf, l, w_grp, scale, *, B, L, l_valid, start):
    tl = min(L, 256)
    bt = max(CHUNK_ROWS // L, 1)
    hist = jnp.pad(buf[l], ((0, 0), (POOL_HIST - POOL_BUF, 0), (0, 0)))
    u3 = u_pl.reshape(B, L, u_pl.shape[-1])
    out, tail = pl.pallas_call(
        functools.partial(_pool_kernel, tl=tl, bt=bt, start=start),
        grid=(B // bt, L // tl),
        in_specs=[pl.BlockSpec((bt, POOL_HIST, POOL_WIDTH), lambda b, i: (b, 0, 0)),
                  pl.BlockSpec((bt, tl, POOL_WIDTH), lambda b, i: (b, i, 0)),
                  pl.BlockSpec((4, POOL_GW, POOL_GW), lambda b, i: (0, 0, 0)),
                  pl.BlockSpec((1, POOL_WIDTH), lambda b, i: (0, 0))],
        out_specs=[pl.BlockSpec((bt, tl, POOL_WIDTH), lambda b, i: (b, i, 0)),
                   pl.BlockSpec((bt, POOL_TAIL, POOL_WIDTH), lambda b, i: (b, 0, 0))],
        out_shape=[jax.ShapeDtypeStruct((B, L, POOL_WIDTH), BF16),
                   jax.ShapeDtypeStruct((B, POOL_TAIL, POOL_WIDTH), F32)],
        scratch_shapes=[pltpu.VMEM((bt, tl + POOL_HIST, POOL_WIDTH), F32)],
        compiler_params=_cparams(2),
        name="pool_mix",
    )(hist, u3, w_grp, scale.reshape(1, POOL_WIDTH))
    pad = L - l_valid
    new_buf = tail[:, POOL_TAIL - pad - POOL_BUF:POOL_TAIL - pad, :]
    return out.reshape(B * L, POOL_WIDTH), new_buf


def _mlstm_kernel(*refs, c, bt, l_valid, has_carry):
    u_ref, gif_ref, c0_ref, n0_ref, m0_ref, bi_ref, bf_ref, gnw_ref = refs[:8]
    o_ref, cN_ref, nN_ref, mN_ref, c_s, n_s, m_s = refs[8 + has_carry:]
    ci = pl.program_id(1)

    @pl.when(ci == 0)
    def _():
        c_s[...] = c0_ref[...]
        n_s[...] = n0_ref[...]
        m_s[...] = m0_ref[...]

    N = bt * c
    gif = gif_ref[...].reshape(N, LANES)
    tok = lax.broadcasted_iota(jnp.int32, (N, 1), 0) % c
    valid = tok < l_valid
    ig_all = jnp.where(valid, gif + bi_ref[...], -jnp.inf)
    lf_all = jnp.where(valid, _log_sigmoid(gif + bf_ref[...]), 0.0)
    bcum_all = _dot(_seq_prefix_mask(N, c).astype(F32), lf_all, HI)
    ig_t = ig_all.T
    bcum_t = bcum_all.T
    ri = lax.broadcasted_iota(jnp.int32, (c, c), 0)
    causal = lax.broadcasted_iota(jnp.int32, (c, c), 1) <= ri
    st = []
    for b in range(bt):
        rs = slice(b * c, (b + 1) * c)
        for h in range(ML_HEADS):
            qs = slice(h * ML_DQK, (h + 1) * ML_DQK)
            ks = slice(ML_QK_WIDTH + h * ML_DQK, ML_QK_WIDTH + (h + 1) * ML_DQK)
            vs = slice(2 * ML_QK_WIDTH + h * ML_DV, 2 * ML_QK_WIDTH + (h + 1) * ML_DV)
            os_ = slice(2 * ML_QK_WIDTH + ML_V_WIDTH + h * ML_DV, 2 * ML_QK_WIDTH + ML_V_WIDTH + (h + 1) * ML_DV)
            i_c = ig_all[rs, h:h + 1]
            b_c = bcum_all[rs, ML_HEADS + h:ML_HEADS + h + 1]
            i_r = ig_t[h:h + 1, rs]
            b_r = bcum_t[ML_HEADS + h:ML_HEADS + h + 1, rs]
            m_prev = m_s[b, h]
            dlog = jnp.where(causal, b_c - b_r + i_r, -jnp.inf)
            inter = b_c + m_prev
            m_t = jnp.maximum(inter, jnp.max(dlog, axis=-1, keepdims=True))
            b_last = b_c[c - 1:c, :]
            s_log = b_last - b_c + i_c
            m_new = jnp.maximum(b_last + m_prev, jnp.max(s_log, axis=0, keepdims=True))
            q = u_ref[b, :, qs]
            k = u_ref[b, :, ks] * (ML_DQK ** -0.5)
            st.append(dict(b=b, h=h, hs=slice(h * ML_DV, (h + 1) * ML_DV), os=os_, q=q, k=k, qb=_bf(q), vb=_bf(u_ref[b, :, vs]),
                           dlog=dlog, m_t=m_t, w_prev=jnp.exp(inter - m_t), m_new=m_new,
                           kw=k * jnp.exp(s_log - m_new), wp=jnp.exp(b_last + m_prev - m_new)))
    for s in st:
        s["wts"] = jnp.exp(s["dlog"] - s["m_t"]) * _dot_nt(s["qb"], _bf(s["k"]))
    for s in st:
        b, h = s["b"], s["h"]
        c_prev = c_s[b, h]
        n_prev = n_s[b, h]
        num = s["w_prev"] * _dot(s["qb"], _bf(c_prev)) + _dot(_bf(s["wts"]), s["vb"])
        den = (s["w_prev"] * jnp.sum(s["q"] * n_prev, axis=-1, keepdims=True)
               + jnp.sum(s["wts"], axis=-1, keepdims=True))
        s["hh"] = num / jnp.maximum(jnp.abs(den), jnp.exp(-s["m_t"]))
        c_s[b, h] = s["wp"] * c_prev + _dot_tn(_bf(s["kw"]), s["vb"])
        n_s[b, h] = s["wp"] * n_prev + jnp.sum(s["kw"], axis=0, keepdims=True)
        m_s[b, h] = s["m_new"]
    for s in st:
        hh = s["hh"]
        hn = hh * lax.rsqrt(jnp.mean(hh * hh, axis=-1, keepdims=True) + RMS_EPS) * gnw_ref[:, s["hs"]]
        o_ref[s["b"], :, s["hs"]] =(hn * jax.nn.sigmoid(u_ref[s["b"], :, s["os"]])).astype(o_ref.dtype)

    @pl.when(ci == pl.num_programs(1) - 1)
    def _():
        cN_ref[...] = c_s[...]
        nN_ref[...] = n_s[...]
        mN_ref[...] = m_s[...]


def mlstm_mix(u_ml, u_pl, c0, n0, m0, l, l_out, c_carry, b_i, b_f, gn_w, *, B, L, l_valid):
    c = min(L, CHUNK_ROWS)
    bt = CHUNK_ROWS // c
    u3 = u_ml.reshape(B, L, ML_MAIN)
    g3 = u_pl.reshape(B, L, POOL_WIDTH + LANES)
    zeros = jnp.zeros((LANES - 2 * ML_HEADS,), F32)
    bi = jnp.concatenate([b_i, jnp.zeros((ML_HEADS,), F32), zeros]).reshape(1, LANES)
    bf = jnp.concatenate([jnp.zeros((ML_HEADS,), F32), b_f, zeros]).reshape(1, LANES)
    nl = n0.shape[0]
    c_blk, n_blk, m_blk = (bt, ML_HEADS, ML_DQK, ML_DV), (bt, ML_HEADS, 1, ML_DQK), (bt, ML_HEADS, 1, 1)
    at_b = lambda b, i: (b, 0, 0, 0)
    in_specs = [pl.BlockSpec((bt, c, ML_MAIN), lambda b, i: (b, i, 0)),
                pl.BlockSpec((bt, c, LANES), lambda b, i: (b, i, POOL_WIDTH // LANES)),
                _layer_block(l, c_blk, at_b), _layer_block(l, n_blk, at_b), _layer_block(l, m_blk, at_b),
                pl.BlockSpec((1, LANES), lambda b, i: (0, 0)),
                pl.BlockSpec((1, LANES), lambda b, i: (0, 0)),
                pl.BlockSpec((1, ML_V_WIDTH), lambda b, i: (0, 0))]
    args = [u3, g3, c0, n0.reshape(nl, B, ML_HEADS, 1, ML_DQK), m0.reshape(nl, B, ML_HEADS, 1, 1), bi, bf,
            gn_w.reshape(1, ML_V_WIDTH)]
    aliases = {}
    if c_carry is not None:
        in_specs.append(pl.BlockSpec(memory_space=pl.ANY))
        args.append(c_carry)
        aliases = {len(args) - 1: 1}
    out, cN, nN, mN = pl.pallas_call(
        functools.partial(_mlstm_kernel, c=c, bt=bt, l_valid=l_valid, has_carry=c_carry is not None),
        grid=(B // bt, L // c),
        in_specs=in_specs,
        out_specs=[pl.BlockSpec((bt, c, ML_V_WIDTH), lambda b, i: (b, i, 0)),
                   _layer_block(l_out, c_blk, at_b),
                   pl.BlockSpec(n_blk, at_b), pl.BlockSpec(m_blk, at_b)],
        out_shape=[jax.ShapeDtypeStruct((B, L, ML_V_WIDTH), BF16),
                   jax.ShapeDtypeStruct((DEPTH, B, ML_HEADS, ML_DQK, ML_DV), F32),
                   jax.ShapeDtypeStruct((B, ML_HEADS, 1, ML_DQK), F32),
                   jax.ShapeDtypeStruct((B, ML_HEADS, 1, 1), F32)],
        scratch_shapes=[pltpu.VMEM(c_blk, F32), pltpu.VMEM(n_blk, F32), pltpu.VMEM(m_blk, F32)],
        input_output_aliases=aliases,
        compiler_params=_cparams(2),
        name="mlstm_mix",
    )(*args)
    return (out.reshape(B * L, ML_V_WIDTH), cN, nN.reshape(B, ML_HEADS, ML_DQK), mN.reshape(B, ML_HEADS))


RW_CHAIN_GROUP = 16

def _rwkv_chain_group(chains, seq, s_s, o_ref, rk_ref, gnw_ref, gnb_ref, *, C):
    rowi = lax.broadcasted_iota(jnp.int32, (C, C), 0)
    coli = lax.broadcasted_iota(jnp.int32, (C, C), 1)
    upper = rowi < coli
    col2 = lax.broadcasted_iota(jnp.int32, (C, 2 * C), 1)
    incl2 = jnp.where(col2 >= C, col2 - C, col2) <= lax.broadcasted_iota(jnp.int32, (C, 2 * C), 0)
    n_sq = max(int(math.log2(C)), 1)
    st = []
    for b, h in chains:
        rs = slice(b * C, (b + 1) * C)
        sl = slice(h * RW_HEAD, (h + 1) * RW_HEAD)
        cut = lambda t, rs=rs, sl=sl: t[rs, sl]
        kk_h = cut(seq["kkv"])
        nrm = jnp.sqrt(jnp.sum(kk_h * kk_h, axis=-1, keepdims=True))
        kap = jnp.where(seq["valid"][rs], kk_h / jnp.maximum(nrm, 1e-12), 0.0)
        k_h, v_h, r_h = cut(seq["kmod"]), cut(seq["v"]), cut(seq["r"])
        gi = cut(seq["gi"])
        b_t = kap * cut(seq["a"]) * gi
        k_t = k_h * gi
        st.append(dict(b=b, h=h, sl=sl, k_h=k_h, v_h=v_h, r_h=r_h,
                       a_t=_bf(-kap * cut(seq["gp"])), r_t=_bf(r_h * cut(seq["g"])),
                       bk=jnp.concatenate([b_t, k_t], axis=0),
                       ge=seq["g"][(b + 1) * C - 1:(b + 1) * C, sl], s0=s_s[b, h]))
    for c in st:
        bkb = _bf(c["bk"])
        mt = _dot_nt(bkb, c["a_t"])
        c["pt"] = jnp.where(upper, mt[:C], 0.0)
        c["akt"] = _bf(jnp.where(upper, mt[C:], 0.0))
        c["a_r"] = jnp.where(incl2, _dot_nt(c["r_t"], bkb), 0.0)
        c["s0b"] = _bf(c["s0"])
        c["vb"] = _bf(c["v_h"])
    for c in st:
        c["xt"] = _dot_nt(c["s0b"], c["a_t"]) + _dot_tn(c["vb"], c["akt"])
    for _ in range(n_sq - 1):
        for c in st:
            z = _dot(_bf(jnp.concatenate([c["pt"], c["xt"]], axis=0)), _bf(c["pt"]))
            c["pt"] = z[:C]
            c["xt"] = c["xt"] + z[C:]
    for c in st:
        c["ut"] = _bf(c["xt"] + _dot(_bf(c["xt"]), _bf(c["pt"])))
    for c in st:
        a_r = c["a_r"]
        c["y"] = (_dot_nt(c["r_t"], c["s0b"]) + _dot_nt(_bf(a_r[:, :C]), c["ut"])
                  + _dot(_bf(a_r[:, C:]), c["vb"]))
        bkg = c["bk"] * c["ge"]
        s_s[c["b"], c["h"]] = (c["s0"] * c["ge"] + _dot(c["ut"], _bf(bkg[:C]))
                               + _dot_tn(c["vb"], _bf(bkg[C:])))
    for c in st:
        y, sl = c["y"], c["sl"]
        mean = jnp.mean(y, axis=-1, keepdims=True)
        yc = y - mean
        var = jnp.mean(yc * yc, axis=-1, keepdims=True)
        yn = yc * lax.rsqrt(var + GN_EPS) * gnw_ref[:, sl] + gnb_ref[:, sl]
        bonus = jnp.sum(c["r_h"] * c["k_h"] * rk_ref[:, sl], axis=-1, keepdims=True) * c["v_h"]
        o_ref[c["b"], :, sl] = (yn + bonus).astype(o_ref.dtype)


def _rwkv_kernel(*refs, C, bt, l_valid, has_carry):
    (u_ref, sh_ref, s0_ref, mu_ref, w0_ref, w2_ref, a0_ref, a2_ref, kk_ref, ka_ref, rk_ref, gnw_ref,
     gnb_ref) = refs[:13]
    o_ref, sN_ref, prev_s, s_s = refs[13 + has_carry:]
    ci = pl.program_id(1)

    @pl.when(ci == 0)
    def _():
        prev_s[...] = sh_ref[...]
        s_s[...] = s0_ref[...]

    N = bt * C
    u = u_ref[...].reshape(N, RW_COLS)
    tok = lax.broadcasted_iota(jnp.int32, (N, 1), 0) % C
    valid = tok < l_valid
    prev = jnp.concatenate([jnp.broadcast_to(prev_s[b], (C, RW_COLS)) for b in range(bt)], axis=0)
    u_prev = jnp.where(tok == 0, prev, pltpu.roll(u, 1, 0))
    for b in range(bt):
        prev_s[b] = u[(b + 1) * C - 1:(b + 1) * C, :]
    us = u + (u_prev - u) * mu_ref[...]
    W = RW_WIDTH
    k = us[:, W:2 * W]
    wd = us[:, 3 * W:3 * W + RW_RANK]
    ad = us[:, 3 * W + RW_RANK:3 * W + 2 * RW_RANK]
    xw = w0_ref[...] + _dot(_bf(jnp.tanh(wd)), _bf(w2_ref[...]))
    log_w = -_softplus(-xw) - 0.5
    ld = jnp.where(valid, -jnp.exp(log_w), 0.0)
    a = jax.nn.sigmoid(a0_ref[...] + _dot(_bf(ad), _bf(a2_ref[...])))
    cum = _dot(_seq_prefix_mask(N, C).astype(F32), ld, HI)
    seq = dict(valid=valid, r=us[:, 0:W], v=jnp.where(valid, us[:, 2 * W:3 * W], 0.0), a=a,
               g=jnp.exp(cum), gi=jnp.exp(-cum), gp=jnp.exp(cum - ld), kkv=k * kk_ref[...],
               kmod=jnp.where(valid, k * (1.0 + (a - 1.0) * ka_ref[...]), 0.0))
    chains = [(b, h) for b in range(bt) for h in range(RW_HEADS)]
    group = RW_CHAIN_GROUP * bt
    for i in range(0, len(chains), group):
        _rwkv_chain_group(chains[i:i + group], seq, s_s, o_ref, rk_ref, gnw_ref, gnb_ref, C=C)

    @pl.when(ci == pl.num_programs(1) - 1)
    def _():
        sN_ref[...] = s_s[...]


def rwkv7_mix(u_rw, shift_prev, s_prev, l, l_out, s_carry, mu, w0, w2, a0, a2, k_k, k_a, r_k, gn_w, gn_b, *, B, L,
              l_valid):
    C = min(L, CHUNK_ROWS)
    bt = CHUNK_ROWS // C
    u3 = u_rw.reshape(B, L, RW_COLS)
    vec = lambda n: pl.BlockSpec((1, n), lambda b, i: (0, 0))
    row = lambda t: t.reshape(1, -1)
    s_blk = (bt, RW_HEADS, RW_HEAD, RW_HEAD)
    at_b = lambda b, i: (b, 0, 0, 0)
    in_specs = [pl.BlockSpec((bt, C, RW_COLS), lambda b, i: (b, i, 0)),
                _layer_block(l, (bt, 1, RW_COLS), lambda b, i: (b, 0, 0)),
                _layer_block(l, s_blk, at_b),
                vec(RW_COLS), vec(RW_WIDTH),
                pl.BlockSpec((RW_RANK, RW_WIDTH), lambda b, i: (0, 0)),
                vec(RW_WIDTH),
                pl.BlockSpec((RW_RANK, RW_WIDTH), lambda b, i: (0, 0)),
                vec(RW_WIDTH), vec(RW_WIDTH), vec(RW_WIDTH), vec(RW_WIDTH), vec(RW_WIDTH)]
    args = [u3, shift_prev.reshape(shift_prev.shape[0], B, 1, RW_COLS), s_prev, row(mu), row(w0), w2, row(a0), a2,
            row(k_k), row(k_a), row(r_k), row(gn_w), row(gn_b)]
    aliases = {}
    if s_carry is not None:
        in_specs.append(pl.BlockSpec(memory_space=pl.ANY))
        args.append(s_carry)
        aliases = {len(args) - 1: 1}
    out, sN = pl.pallas_call(
        functools.partial(_rwkv_kernel, C=C, bt=bt, l_valid=l_valid, has_carry=s_carry is not None),
        grid=(B // bt, L // C),
        in_specs=in_specs,
        out_specs=[pl.BlockSpec((bt, C, RW_WIDTH), lambda b, i: (b, i, 0)), _layer_block(l_out, s_blk, at_b)],
        out_shape=[jax.ShapeDtypeStruct((B, L, RW_WIDTH), BF16),
                   jax.ShapeDtypeStruct((DEPTH, B, RW_HEADS, RW_HEAD, RW_HEAD), F32)],
        scratch_shapes=[pltpu.VMEM((bt, 1, RW_COLS), F32), pltpu.VMEM(s_blk, F32)],
        input_output_aliases=aliases,
        compiler_params=_cparams(2),
        name="rwkv7_mix",
    )(*args)
    return out.reshape(B * L, RW_WIDTH), sN


RWS_VECS = 5


def _rwkv_short_kernel(*refs, l_valid, has_carry):
    (ur_ref, uk_ref, uv_ref, ul_ref, shr_ref, shk_ref, shv_ref, shl_ref, mur_ref, muk_ref, muv_ref, mul_ref,
     w0_ref, w2_ref, a0_ref, a2_ref, kk_ref, ka_ref, rk_ref, gnw_ref, gnb_ref, s_ref) = refs[:22]
    o_ref, sN_ref, vec_s, val_s, y_s = refs[22 + has_carry:]
    H2 = LANES // RW_HEAD
    heads = [slice(h * RW_HEAD, (h + 1) * RW_HEAD) for h in range(H2)]
    o_ref[...] = jnp.zeros_like(o_ref)

    def shifted(u_ref, sh_ref, mu_ref, t):
        u = u_ref[:, t, :]
        prev = sh_ref[...] if t == 0 else u_ref[:, t - 1, :]
        return u + (prev - u) * mu_ref[...]

    rows = []
    for t in range(l_valid):
        r = shifted(ur_ref, shr_ref, mur_ref, t)
        k = shifted(uk_ref, shk_ref, muk_ref, t)
        v = shifted(uv_ref, shv_ref, muv_ref, t)
        lo = shifted(ul_ref, shl_ref, mul_ref, t)
        xw = w0_ref[...] + _dot(jnp.tanh(lo[:, :RW_RANK]), w2_ref[...], HI)
        w = jnp.exp(-jnp.exp(-_softplus(-xw) - 0.5))
        a = jax.nn.sigmoid(a0_ref[...] + _dot(lo[:, RW_RANK:], a2_ref[...], HI))
        kkv = k * kk_ref[...]
        kmod = k * (1.0 + (a - 1.0) * ka_ref[...])
        kap = jnp.concatenate(
            [kkv[:, sl] / jnp.maximum(jnp.sqrt(jnp.sum(kkv[:, sl] * kkv[:, sl], axis=-1, keepdims=True)), 1e-12)
             for sl in heads], axis=1)
        for j, x in enumerate((w, kap, kap * a, kmod, r)):
            vec_s[t, j] = x.T.reshape(H2, RW_HEAD, x.shape[0])
        val_s[t] = v.T.reshape(H2, RW_HEAD, v.shape[0])
        rows.append((r, kmod, v))

    for h in range(H2):
        def body(vi, carry, h=h):
            s = s_ref[h, vi]
            for t in range(l_valid):
                w, kap, bb, kk, rr = (vec_s[t, j, h] for j in range(RWS_VECS))
                sa = jnp.sum(s * kap, axis=0, keepdims=True)
                s = s * w - sa * bb + val_s[t, h, pl.ds(vi, 1), :] * kk
                y_s[t, h, pl.ds(vi, 1), :] = jnp.sum(s * rr, axis=0, keepdims=True)
            sN_ref[h, vi] = s
            return carry
        lax.fori_loop(0, RW_HEAD, body, 0)

    for t in range(l_valid):
        r, kmod, v = rows[t]
        y_all = y_s[t].reshape(LANES, y_s.shape[-1]).T
        outs = []
        for sl in heads:
            y = y_all[:, sl]
            mean = jnp.mean(y, axis=-1, keepdims=True)
            yc = y - mean
            var = jnp.mean(yc * yc, axis=-1, keepdims=True)
            yn = yc * lax.rsqrt(var + GN_EPS) * gnw_ref[:, sl] + gnb_ref[:, sl]
            bonus = jnp.sum(r[:, sl] * kmod[:, sl] * rk_ref[:, sl], axis=-1, keepdims=True) * v[:, sl]
            outs.append(yn + bonus)
        o_ref[:, t, :] = jnp.concatenate(outs, axis=1).astype(o_ref.dtype)


def rwkv7_short(u_rw, shift_prev, s_prev_t, l, l_out, s_carry, mu, w0, w2, a0, a2, k_k, k_a, r_k, gn_w, gn_b, *, B, L,
                l_valid):
    assert B == LANES, "the batch must fill the lane dimension"
    H2 = LANES // RW_HEAD
    nblk = RW_WIDTH // LANES
    u3 = u_rw.reshape(B, L, RW_COLS)
    seg = lambda off: pl.BlockSpec((B, L, LANES), lambda hp, off=off: (0, 0, off + hp))
    lora = pl.BlockSpec((B, L, LANES), lambda hp: (0, 0, 3 * nblk))
    sh = lambda off: pl.BlockSpec((None, B, LANES), lambda hp, off=off: (l, 0, off + hp))
    sh_lora = pl.BlockSpec((None, B, LANES), lambda hp: (l, 0, 3 * nblk))
    mus = lambda off: pl.BlockSpec((1, LANES), lambda hp, off=off: (0, off + hp))
    mu_lora = pl.BlockSpec((1, LANES), lambda hp: (0, 3 * nblk))
    vec = pl.BlockSpec((1, LANES), lambda hp: (0, hp))
    mat = pl.BlockSpec((RW_RANK, LANES), lambda hp: (0, hp))
    s_blk = (H2, RW_HEAD, RW_HEAD, B)
    in_specs = [seg(0), seg(nblk), seg(2 * nblk), lora, sh(0), sh(nblk), sh(2 * nblk), sh_lora,
                mus(0), mus(nblk), mus(2 * nblk), mu_lora,
                vec, mat, vec, mat, vec, vec, vec, vec, vec,
                pl.BlockSpec((None,) + s_blk, lambda hp: (l, hp, 0, 0, 0))]
    row = lambda t: t.reshape(1, -1)
    args = [u3, u3, u3, u3, shift_prev, shift_prev, shift_prev, shift_prev, row(mu), row(mu), row(mu), row(mu),
            row(w0), w2, row(a0), a2, row(k_k), row(k_a), row(r_k), row(gn_w), row(gn_b), s_prev_t]
    aliases = {}
    if s_carry is not None:
        in_specs.append(pl.BlockSpec(memory_space=pl.ANY))
        args.append(s_carry)
        aliases = {len(args) - 1: 1}
    out, sN = pl.pallas_call(
        functools.partial(_rwkv_short_kernel, l_valid=l_valid, has_carry=s_carry is not None),
        grid=(RW_HEADS // H2,),
        in_specs=in_specs,
        out_specs=[pl.BlockSpec((B, L, LANES), lambda hp: (0, 0, hp)),
                   pl.BlockSpec((None,) + s_blk, lambda hp: (l_out, hp, 0, 0, 0))],
        out_shape=[jax.ShapeDtypeStruct((B, L, RW_WIDTH), BF16),
                   jax.ShapeDtypeStruct((DEPTH, RW_HEADS, RW_HEAD, RW_HEAD, B), F32)],
        scratch_shapes=[pltpu.VMEM((l_valid, RWS_VECS, H2, RW_HEAD, B), F32),
                        pltpu.VMEM((l_valid, H2, RW_HEAD, B), F32),
                        pltpu.VMEM((l_valid, H2, RW_HEAD, B), F32)],
        input_output_aliases=aliases,
        compiler_params=_cparams(1),
        name="rwkv7_short",
    )(*args)
    return out.reshape(B * L, RW_WIDTH), sN


ATTN_SEQS_PER_STEP = 8


def _attn_kernel(q_ref, kv_ref, o_ref):
    heads = [slice(h * XA_HEAD_DIM, (h + 1) * XA_HEAD_DIM) for h in range(XA_HEADS)]
    s = [_dot_nt(q_ref[0, :, hs], _bf(kv_ref[0, :, hs])) * (XA_HEAD_DIM ** -0.5) for hs in heads]
    e = [jnp.exp(t - jnp.max(t, axis=-1, keepdims=True)) for t in s]
    p = [t / jnp.sum(t, axis=-1, keepdims=True) for t in e]
    for h, hs in enumerate(heads):
        vs = slice(XA_WIDTH + h * XA_HEAD_DIM, XA_WIDTH + (h + 1) * XA_HEAD_DIM)
        o_ref[0, :, hs] = _dot(_bf(p[h]), _bf(kv_ref[0, :, vs])).astype(o_ref.dtype)


def _attn_short_kernel(q_ref, k_ref, v_ref, o_ref, *, bt):
    L = q_ref.shape[1]
    rows, cols = XA_HEADS * L, MEM_LEN * XA_HEADS
    own = (lax.broadcasted_iota(jnp.int32, (rows, cols), 0) // L
           == lax.broadcasted_iota(jnp.int32, (rows, cols), 1) % XA_HEADS)
    heads = [slice(h * XA_HEAD_DIM, (h + 1) * XA_HEAD_DIM) for h in range(XA_HEADS)]
    qs = [jnp.concatenate([q_ref[b, :, hs] for hs in heads], axis=0) for b in range(bt)]
    s = [_dot_nt(qs[b], _bf(k_ref[b].reshape(cols, XA_HEAD_DIM))) * (XA_HEAD_DIM ** -0.5) for b in range(bt)]
    s = [jnp.where(own, t, -jnp.inf) for t in s]
    e = [jnp.exp(t - jnp.max(t, axis=-1, keepdims=True)) for t in s]
    p = [t / jnp.sum(t, axis=-1, keepdims=True) for t in e]
    for b in range(bt):
        out = _dot(_bf(p[b]), _bf(v_ref[b].reshape(cols, XA_HEAD_DIM)))
        for h, hs in enumerate(heads):
            o_ref[b, :, hs] = out[h * L:(h + 1) * L].astype(o_ref.dtype)


def cross_attention(q, mem, l, *, B, L):
    q3 = q.reshape(B, L, XA_WIDTH)
    if l is None:
        bt, tq, body, mems = 1, min(L, 512), _attn_kernel, [mem]
        mem_specs = [pl.BlockSpec((bt, MEM_LEN, 2 * XA_WIDTH), lambda b, i: (b, 0, 0))]
    else:
        bt, tq, mems = ATTN_SEQS_PER_STEP, L, list(mem)
        body = functools.partial(_attn_short_kernel, bt=bt)
        mem_specs = [pl.BlockSpec((None, bt, MEM_LEN, XA_HEADS, XA_HEAD_DIM), lambda b, i: (l, b, 0, 0, 0))] * 2
    qo_spec = pl.BlockSpec((bt, tq, XA_WIDTH), lambda b, i: (b, i, 0))
    out = pl.pallas_call(
        body,
        grid=(B // bt, L // tq),
        in_specs=[qo_spec] + mem_specs,
        out_specs=qo_spec,
        out_shape=jax.ShapeDtypeStruct((B, L, XA_WIDTH), BF16),
        compiler_params=_cparams(2),
        name="cross_attention",
    )(q3, *mems)
    return out.reshape(B * L, XA_WIDTH)


ROUTER_GROUP_LANE = MOE_EXPERTS


MOE_TB = 512
MOE_TB_GATHER = 1024
MOE_SUB = 256
MOE_TE_LONG = 1024
MOE_TE_SHORT = 512
ROUTE_GROUP_LANE = 0
ROUTE_RANK_LANE = 1
COMB_PIECES = 3


def _router_kernel(x_ref, g_ref, w_ref, b_ref, xn_ref, comb_ref, route_ref, cnt_s):
    @pl.when(pl.program_id(0) == 0)
    def _():
        cnt_s[...] = jnp.zeros_like(cnt_s)

    xn = _rms(x_ref[...], g_ref[...])
    xn_ref[...] = _bf(xn)
    z = _dot(xn, w_ref[...], HI) + b_ref[...]
    tm = z.shape[0]
    lane = lax.broadcasted_iota(jnp.int32, z.shape, 1).astype(F32)
    big = float(LANES)
    neg = -jnp.inf
    first = lambda mask: jnp.min(jnp.where(mask, lane, big), axis=-1, keepdims=True)
    is_g = jnp.logical_and(lane >= ROUTER_GROUP_LANE, lane < ROUTER_GROUP_LANE + MOE_GROUPS)
    zg = jnp.where(is_g, z, neg)
    mg = jnp.max(zg, axis=-1, keepdims=True)
    grp = first(zg == mg) - ROUTER_GROUP_LANE
    p_grp = 1.0 / jnp.sum(jnp.exp(zg - mg), axis=-1, keepdims=True)
    lo = grp * MOE_PER_GROUP
    ze = jnp.where(jnp.logical_and(lane >= lo, lane < lo + MOE_PER_GROUP), z, neg)
    t1 = jnp.max(ze, axis=-1, keepdims=True)
    i1 = first(ze == t1)
    ze2 = jnp.where(lane == i1, neg, ze)
    t2 = jnp.max(ze2, axis=-1, keepdims=True)
    i2 = first(ze2 == t2)
    e2 = jnp.exp(t2 - t1)
    g1 = p_grp / (1.0 + e2)
    comb = jnp.where(lane == i1, g1, 0.0) + jnp.where(lane == i2, g1 * e2, 0.0)
    c_hi = _bf(comb)
    rest = comb - c_hi.astype(F32)
    c_mid = _bf(rest)
    comb_ref[...] = jnp.concatenate([c_hi, c_mid, _bf(rest - c_mid.astype(F32))], axis=1)
    onehot = jnp.where(lane == grp, 1.0, 0.0)
    r = lax.broadcasted_iota(jnp.int32, (tm, tm), 0)
    c = lax.broadcasted_iota(jnp.int32, (tm, tm), 1)
    before = _dot(_bf(jnp.where(c < r, 1.0, 0.0)), _bf(onehot)) + cnt_s[...]
    rank = jnp.sum(onehot * before, axis=-1, keepdims=True)
    cnt_s[...] = cnt_s[...] + jnp.sum(onehot, axis=0, keepdims=True)
    route_ref[...] = jnp.where(lane == ROUTE_GROUP_LANE, grp, 0.0) + jnp.where(lane == ROUTE_RANK_LANE, rank, 0.0)


def moe_router(x, g, w_r1, b_r1, w_r2, b_r2):
    T, D = x.shape
    tm = MOE_TB
    pad = LANES - MOE_EXPERTS - MOE_GROUPS
    w = jnp.concatenate([w_r2, w_r1, jnp.zeros((D, pad), F32)], axis=1)
    b = jnp.concatenate([b_r2, b_r1, jnp.zeros((pad,), F32)]).reshape(1, LANES)
    return pl.pallas_call(
        _router_kernel,
        grid=(T // tm,),
        in_specs=[pl.BlockSpec((tm, D), lambda i: (i, 0)),
                  pl.BlockSpec((1, D), lambda i: (0, 0)),
                  pl.BlockSpec((D, LANES), lambda i: (0, 0)),
                  pl.BlockSpec((1, LANES), lambda i: (0, 0))],
        out_specs=[pl.BlockSpec((tm, D), lambda i: (i, 0)),
                   pl.BlockSpec((tm, COMB_PIECES * LANES), lambda i: (i, 0)),
                   pl.BlockSpec((tm, LANES), lambda i: (i, 0))],
        out_shape=[jax.ShapeDtypeStruct((T, D), BF16), jax.ShapeDtypeStruct((T, COMB_PIECES * LANES), BF16),
                   jax.ShapeDtypeStruct((T, LANES), F32)],
        scratch_shapes=[pltpu.VMEM((1, LANES), F32)],
        compiler_params=_cparams(1),
        name="moe_router",
    )(x, g.reshape(1, D), w, b)


def _moe_plan(route, T, te):
    i32 = jnp.int32
    grp = route[:, ROUTE_GROUP_LANE].astype(i32)
    rank = route[:, ROUTE_RANK_LANE].astype(i32)
    rows = T + MOE_GROUPS * te
    n_sub = rows // MOE_SUB
    n_tiles = rows // te
    onehot = (grp[:, None] == jnp.arange(MOE_GROUPS, dtype=i32)[None]).astype(i32)
    seg_rows = (onehot.sum(axis=0) + te - 1) // te * te
    seg_end = jnp.cumsum(seg_rows)
    seg_start = seg_end - seg_rows
    dest = seg_start[grp] + rank
    group_of = lambda row: jnp.minimum(jnp.sum(row[:, None] >= seg_end[None], axis=1), MOE_GROUPS - 1).astype(i32)
    sub_row = jnp.arange(n_sub, dtype=i32) * MOE_SUB
    sub_g = group_of(sub_row)
    sub_valid = sub_row < seg_end[-1]
    r0 = sub_row - seg_start[sub_g]

    def overlap(tb):
        blk_cnt = onehot.reshape(T // tb, tb, MOE_GROUPS).sum(axis=1)
        cum_blk = jnp.concatenate([jnp.zeros((1, MOE_GROUPS), i32), jnp.cumsum(blk_cnt, axis=0)])
        lo = cum_blk[:-1][:, sub_g].T
        hi = cum_blk[1:][:, sub_g].T
        return (lo < (r0 + MOE_SUB)[:, None]) & (hi > r0[:, None]) & sub_valid[:, None]

    g_mask = overlap(MOE_TB_GATHER)
    first_col = (jnp.arange(T // MOE_TB_GATHER) == 0)[None]
    g_mask = g_mask | (first_col & ~g_mask.any(axis=1, keepdims=True))

    def items(mask, ncol, nblk):
        n_items = n_sub + MOE_GROUPS * nblk
        flat = jnp.nonzero(mask.reshape(-1), size=n_items, fill_value=-1)[0].astype(i32)
        valid = flat >= 0
        flat = jnp.where(valid, flat, jnp.max(flat))
        major, minor = flat // ncol, flat % ncol
        prev = jnp.concatenate([jnp.full((1,), -1, i32), major[:-1]])
        nxt = jnp.concatenate([major[1:], jnp.full((1,), -1, i32)])
        nvalid = jnp.concatenate([valid[1:], jnp.zeros((1,), bool)])
        first = (major != prev) & valid
        last = ((major != nxt) | ~nvalid) & valid
        return major, minor, first.astype(i32), last.astype(i32), valid.astype(i32)

    g_sub, g_blk, g_first, _, g_valid = items(g_mask, T // MOE_TB_GATHER, T // MOE_TB_GATHER)
    s_blk, s_sub, s_first, s_last, s_valid = items(overlap(MOE_TB).T, n_sub, T // MOE_TB)
    tile_row = jnp.arange(n_tiles, dtype=i32) * te
    return dict(dest=dest, rows=rows, gather=(g_sub, g_blk, g_first, g_valid),
                scatter=(s_sub, s_blk, s_first, s_last, s_valid),
                tile_group=group_of(tile_row), tile_valid=(tile_row < seg_end[-1]).astype(i32))


def _moe_gather_kernel(sub_ref, blk_ref, first_ref, valid_ref, dest_ref, xn_ref, comb_ref, xs_ref, cs_ref):
    w = pl.program_id(0)

    @pl.when(valid_ref[w] == 1)
    def _():
        rows = sub_ref[w] * MOE_SUB + lax.broadcasted_iota(jnp.int32, (MOE_SUB, MOE_TB_GATHER), 0)
        hit = _bf(jnp.where(dest_ref[0] == rows, 1.0, 0.0))
        gx = _bf(_dot(hit, xn_ref[...]))
        g3 = _dot(hit, comb_ref[...])
        gc = g3[:, :LANES] + g3[:, LANES:2 * LANES] + g3[:, 2 * LANES:]

        @pl.when(first_ref[w] == 1)
        def _():
            xs_ref[...] = gx
            cs_ref[...] = gc

        @pl.when(first_ref[w] == 0)
        def _():
            xs_ref[...] = xs_ref[...] + gx
            cs_ref[...] = cs_ref[...] + gc


def moe_gather(xn, comb, plan):
    T, D = xn.shape
    g_sub, g_blk, g_first, g_valid = plan["gather"]
    rows = plan["rows"]
    tb = MOE_TB_GATHER
    dest3 = plan["dest"].reshape(T // tb, 1, tb)
    grid_spec = pltpu.PrefetchScalarGridSpec(
        num_scalar_prefetch=4,
        grid=(g_sub.shape[0],),
        in_specs=[pl.BlockSpec((1, 1, tb), lambda w, s, b, f, v: (b[w], 0, 0)),
                  pl.BlockSpec((tb, D), lambda w, s, b, f, v: (b[w], 0)),
                  pl.BlockSpec((tb, COMB_PIECES * LANES), lambda w, s, b, f, v: (b[w], 0))],
        out_specs=[pl.BlockSpec((MOE_SUB, D), lambda w, s, b, f, v: (s[w], 0)),
                   pl.BlockSpec((MOE_SUB, LANES), lambda w, s, b, f, v: (s[w], 0))])
    return pl.pallas_call(
        _moe_gather_kernel,
        grid_spec=grid_spec,
        out_shape=[jax.ShapeDtypeStruct((rows, D), BF16), jax.ShapeDtypeStruct((rows, LANES), F32)],
        compiler_params=_cparams(1),
        name="moe_gather",
    )(g_sub, g_blk, g_first, g_valid, dest3, xn, comb)


def _moe_group_experts_kernel(tg_ref, tv_ref, xs_ref, cs_ref, wg_ref, wu_ref, wd_ref, yh_ref, yl_ref, acc):
    i = pl.program_id(0)
    e = pl.program_id(1)

    @pl.when(e == 0)
    def _():
        acc[...] = jnp.zeros_like(acc)

    @pl.when(tv_ref[i] == 1)
    def _():
        xs = xs_ref[...]
        cs = cs_ref[...]
        lane = lax.broadcasted_iota(jnp.int32, cs.shape, 1)
        ce = jnp.sum(jnp.where(lane == tg_ref[i] * MOE_PER_GROUP + e, cs, 0.0), axis=-1, keepdims=True)
        hid = jax.nn.silu(_dot(xs, _bf(wg_ref[...]))) * _dot(xs, _bf(wu_ref[...])) * ce
        acc[...] += _dot(_bf(hid), _bf(wd_ref[...]))

    @pl.when(e == pl.num_programs(1) - 1)
    def _():
        y = acc[...]
        hi = _bf(y)
        yh_ref[...] = hi
        yl_ref[...] = _bf(y - hi.astype(F32))


def moe_group_experts(xs, cs, plan, wg, wu, wd, l, te):
    rows, D = xs.shape
    expert = lambda i, e, tg, tv: (l, tg[i] * MOE_PER_GROUP + e, 0, 0)
    w_in = pl.BlockSpec((None, None, D, MOE_HIDDEN), expert)
    grid_spec = pltpu.PrefetchScalarGridSpec(
        num_scalar_prefetch=2,
        grid=(rows // te, MOE_PER_GROUP),
        in_specs=[pl.BlockSpec((te, D), lambda i, e, tg, tv: (i, 0)),
                  pl.BlockSpec((te, LANES), lambda i, e, tg, tv: (i, 0)),
                  w_in, w_in,
                  pl.BlockSpec((None, None, MOE_HIDDEN, D), expert)],
        out_specs=[pl.BlockSpec((te, D), lambda i, e, tg, tv: (i, 0))] * 2,
        scratch_shapes=[pltpu.VMEM((te, D), F32)])
    return pl.pallas_call(
        _moe_group_experts_kernel,
        grid_spec=grid_spec,
        out_shape=[jax.ShapeDtypeStruct((rows, D), BF16)] * 2,
        compiler_params=_cparams(2),
        name="moe_group_experts",
    )(plan["tile_group"], plan["tile_valid"], xs, cs, wg, wu, wd)


def _moe_scatter_kernel(sub_ref, blk_ref, first_ref, last_ref, valid_ref, dest_ref, yh_ref, yl_ref, x_ref, gf_ref,
                        o_ref, *, final_norm):
    w = pl.program_id(0)

    @pl.when(valid_ref[w] == 1)
    def _():
        cols = sub_ref[w] * MOE_SUB + lax.broadcasted_iota(jnp.int32, (MOE_TB, MOE_SUB), 1)
        pick = _bf(jnp.where(dest_ref[...] == cols, 1.0, 0.0))
        upd = _dot(pick, yh_ref[...]) + _dot(pick, yl_ref[...])

        @pl.when(first_ref[w] == 1)
        def _():
            o_ref[...] = x_ref[...] + upd

        @pl.when(first_ref[w] == 0)
        def _():
            o_ref[...] += upd

        if final_norm:
            @pl.when(last_ref[w] == 1)
            def _():
                o_ref[...] = _rms(o_ref[...], gf_ref[...])


def moe_scatter(yh, yl, x, plan, g_final, *, final_norm):
    T, D = x.shape
    s_sub, s_blk, s_first, s_last, s_valid = plan["scatter"]
    dest_col = plan["dest"].reshape(T, 1)
    im = lambda f: (lambda w, s, b, fi, la, v: f(w, s, b))
    grid_spec = pltpu.PrefetchScalarGridSpec(
        num_scalar_prefetch=5,
        grid=(s_sub.shape[0],),
        in_specs=[pl.BlockSpec((MOE_TB, 1), im(lambda w, s, b: (b[w], 0))),
                  pl.BlockSpec((MOE_SUB, D), im(lambda w, s, b: (s[w], 0))),
                  pl.BlockSpec((MOE_SUB, D), im(lambda w, s, b: (s[w], 0))),
                  pl.BlockSpec((MOE_TB, D), im(lambda w, s, b: (b[w], 0))),
                  pl.BlockSpec((1, D), im(lambda w, s, b: (0, 0)))],
        out_specs=pl.BlockSpec((MOE_TB, D), im(lambda w, s, b: (b[w], 0))))
    return pl.pallas_call(
        functools.partial(_moe_scatter_kernel, final_norm=final_norm),
        grid_spec=grid_spec,
        out_shape=jax.ShapeDtypeStruct((T, D), F32),
        compiler_params=_cparams(1),
        name="moe_scatter",
    )(s_sub, s_blk, s_first, s_last, s_valid, dest_col, yh, yl, x, g_final.reshape(1, D))


def hmoe_block(x, p, l, *, final_norm):
    T = x.shape[0]
    te = MOE_TE_LONG if T >= 4 * MOE_TE_LONG else MOE_TE_SHORT
    xn, comb, route = moe_router(x, p["g_moe"][l], p["moe_wr1"][l], p["moe_br1"][l], p["moe_wr2"][l],
                                 p["moe_br2"][l])
    plan = _moe_plan(route, T, te)
    xs, cs = moe_gather(xn, comb, plan)
    yh, yl = moe_group_experts(xs, cs, plan, p["moe_wg"], p["moe_wu"], p["moe_wd"], l, te)
    return moe_scatter(yh, yl, x, plan, p["g_final"], final_norm=final_norm)


def _layer_weights(l, p):
    w_in = p["w_in"][l]
    c0 = RW_COLS
    c1 = c0 + ML_MAIN
    c2 = c1 + 2 * ML_HEADS
    c3 = c2 + POOL_WIDTH
    w_pl = jnp.concatenate([w_in[:, c2:c3], w_in[:, c1:c2], jnp.zeros((D_MODEL, LANES - 2 * ML_HEADS), F32)], axis=1)
    return dict(
        w_rw=_bf(w_in[:, :c0]), w_ml=_bf(w_in[:, c0:c1]), w_pl=_bf(w_pl), w_gate=_bf(w_in[:, c3:]),
        w_up_rwkv=_bf(p["w_up_rwkv"][l]), w_up_mlstm=_bf(p["w_up_mlstm"][l]), w_up_pool=_bf(p["w_up_pool"][l]),
        w_out=_bf(p["w_out"][l]), pool_w=_bf(p["pool_w"][l]),
        xa_wq=_bf(p["xa_wq"][l]), xa_wo=_bf(p["xa_wo"][l]),
        xa_wkv=_bf(jnp.concatenate([p["xa_wk"][l], p["xa_wv"][l]], axis=1)),
    )


def _trunk_layer(x, mem, mem_l, st, carry, *, l, sl, p, lw, B, L, l_valid, start, final_norm):
    g_mix = p["g_mix"][l]
    u_rw = norm_matmul(x, g_mix, lw["w_rw"], tn=TN_RWKV, name="in_rwkv")
    u_ml = norm_matmul(x, g_mix, lw["w_ml"], tn=TN_MLSTM, name="in_mlstm")
    u_pl = norm_matmul(x, g_mix, lw["w_pl"], tn=POOL_WIDTH + LANES, name="in_pool")
    gates = norm_matmul(x, g_mix, lw["w_gate"], tn=TN_GATES, out_dtype=BF16, act="sigmoid", name="in_gates")

    rwkv = rwkv7_short if L < CHUNK_ROWS else rwkv7_mix
    o_rw, rw_s = rwkv(u_rw, st["rw_shift"], st["rw_s"], sl, l, carry["rw_s"], p["rw_mu"][l], p["rw_w0"][l],
                      p["rw_w2"][l], p["rw_a0"][l], p["rw_a2"][l], p["rw_k_k"][l], p["rw_k_a"][l],
                      p["rw_r_k"][l], p["rw_gn_w"][l], p["rw_gn_b"][l], B=B, L=L, l_valid=l_valid)
    rw_shift = u_rw.reshape(B, L, RW_COLS)[:, l_valid - 1]
    o_ml, ml_c, ml_n, ml_m = mlstm_mix(u_ml, u_pl, st["ml_c"], st["ml_n"], st["ml_m"], sl, l, carry["ml_c"],
                                       p["ml_b_i"][l], p["ml_b_f"][l], p["ml_gn_w"][l], B=B, L=L, l_valid=l_valid)
    o_pl, pool_buf = pool_mix(u_pl, st["pool"], sl, lw["pool_w"], p["pool_scale"][l], B=B, L=L, l_valid=l_valid,
                              start=start)
    merged = merge_branches(o_rw, o_ml, o_pl, gates, lw["w_up_rwkv"], lw["w_up_mlstm"], lw["w_up_pool"])
    x = matmul_residual(merged, lw["w_out"], x)

    q = norm_matmul(x, p["g_xa"][l], lw["xa_wq"], tn=XA_WIDTH, out_dtype=BF16, name="xa_q")
    att = cross_attention(q, mem, mem_l, B=B, L=L)
    x = matmul_residual(att, lw["xa_wo"], x)

    x = hmoe_block(x, p, l, final_norm=final_norm)
    return x, dict(rw_s=rw_s, ml_c=ml_c), (rw_shift, ml_n, ml_m, pool_buf)


def kernel(x_prompt, x_sample, cache_mem_k, cache_mem_v, state_rwkv_s, state_rwkv_shift, state_mlstm_c, state_mlstm_n, state_mlstm_m, state_pool, mem_prompt, g_mix, w_in, rw_mu, rw_w0, rw_w2, rw_a0, rw_a2, rw_k_k, rw_k_a, rw_r_k, rw_gn_w, rw_gn_b, ml_b_i, ml_b_f, ml_gn_w, pool_w, pool_scale, w_up_rwkv, w_up_mlstm, w_up_pool, w_out, g_xa, g_mem, xa_wq, xa_wk, xa_wv, xa_wo, g_moe, moe_wr1, moe_br1, moe_wr2, moe_br2, moe_wg, moe_wu, moe_wd, g_final):
    p = dict(g_mix=g_mix, w_in=w_in, rw_mu=rw_mu, rw_w0=rw_w0, rw_w2=rw_w2, rw_a0=rw_a0, rw_a2=rw_a2, rw_k_k=rw_k_k,
             rw_k_a=rw_k_a, rw_r_k=rw_r_k, rw_gn_w=rw_gn_w, rw_gn_b=rw_gn_b, ml_b_i=ml_b_i, ml_b_f=ml_b_f,
             ml_gn_w=ml_gn_w, pool_w=pool_w, pool_scale=pool_scale, w_up_rwkv=w_up_rwkv, w_up_mlstm=w_up_mlstm,
             w_up_pool=w_up_pool, w_out=w_out, g_xa=g_xa, g_mem=g_mem, xa_wq=xa_wq, xa_wk=xa_wk, xa_wv=xa_wv,
             xa_wo=xa_wo, g_moe=g_moe, moe_wr1=moe_wr1, moe_br1=moe_br1, moe_wr2=moe_wr2, moe_br2=moe_br2,
             moe_wg=moe_wg, moe_wu=moe_wu, moe_wd=moe_wd, g_final=g_final)
    Bp, Lp, D = x_prompt.shape
    Bs, Ls, _ = x_sample.shape
    Ls_pad = -(-Ls // SUBLANES) * SUBLANES
    yp = x_prompt.reshape(Bp * Lp, D)
    ys = jnp.pad(x_sample, ((0, 0), (0, Ls_pad - Ls), (0, 0))).reshape(Bs * Ls_pad, D)
    zeros = lambda *s: jnp.zeros((1,) + s, F32)
    st_p = dict(rw_shift=zeros(Bp, RW_COLS), rw_s=zeros(Bp, RW_HEADS, RW_HEAD, RW_HEAD),
                ml_c=zeros(Bp, ML_HEADS, ML_DQK, ML_DV), ml_n=zeros(Bp, ML_HEADS, ML_DQK), ml_m=zeros(Bp, ML_HEADS),
                pool=zeros(Bp, POOL_BUF, POOL_WIDTH))
    batch_minor = (0, 2, 3, 4, 1)
    st_s = dict(rw_shift=state_rwkv_shift, rw_s=jnp.transpose(state_rwkv_s, batch_minor), ml_c=state_mlstm_c,
                ml_n=state_mlstm_n, ml_m=state_mlstm_m, pool=state_pool)
    carry_p = carry_s = dict(rw_s=None, ml_c=None)
    small_p = [[] for _ in range(6)]
    small_s = [[] for _ in range(4)]
    for l in range(DEPTH):
        lw = _layer_weights(l, p)
        final = l == DEPTH - 1
        kv = norm_matmul(mem_prompt.reshape(Bp * MEM_LEN, D), g_mem[l], lw["xa_wkv"], tn=2 * XA_WIDTH,
                         name="memory_kv")
        kv3 = kv.reshape(Bp, MEM_LEN, 2 * XA_WIDTH)
        yp, carry_p, small = _trunk_layer(yp, kv3, None, st_p, carry_p, l=l, sl=0, p=p, lw=lw, B=Bp, L=Lp,
                                          l_valid=Lp, start=0, final_norm=final)
        mk = kv[:, :XA_WIDTH].reshape(Bp, MEM_LEN, XA_HEADS, XA_HEAD_DIM)
        mv = kv[:, XA_WIDTH:].reshape(Bp, MEM_LEN, XA_HEADS, XA_HEAD_DIM)
        for acc, t in zip(small_p, small + (mk, mv)):
            acc.append(t)
        ys, carry_s, small = _trunk_layer(ys, (cache_mem_k, cache_mem_v), l, st_s, carry_s, l=l, sl=l, p=p, lw=lw,
                                          B=Bs, L=Ls_pad, l_valid=Ls, start=PAST_LEN, final_norm=final)
        for acc, t in zip(small_s, small):
            acc.append(t)
    y_prompt = yp.reshape(Bp, Lp, D)
    y_sample = ys.reshape(Bs, Ls_pad, D)[:, :Ls]
    p_sh, p_n, p_m, p_pool, p_mk, p_mv = [jnp.stack(t) for t in small_p]
    s_sh, s_n, s_m, s_pool = [jnp.stack(t) for t in small_s]
    return (y_prompt, y_sample, carry_p["rw_s"], p_sh, carry_p["ml_c"], p_n, p_m, p_pool, p_mk, p_mv,
            jnp.transpose(carry_s["rw_s"], (0, 4, 1, 2, 3)), s_sh, carry_s["ml_c"], s_n, s_m, s_pool)
```

```python
import functools
import math

import jax
import jax.numpy as jnp
from jax import lax
from jax.experimental import pallas as pl
from jax.experimental.pallas import tpu as pltpu

F32 = jnp.float32
BF16 = jnp.bfloat16
HI = lax.Precision.HIGHEST

D_MODEL = 2048
DEPTH = 2
PAST_LEN = 16384
RW_HEAD = 64
RW_WIDTH = 1024
RW_HEADS = 16
RW_RANK = 64
RW_COLS = 3 * RW_WIDTH + 2 * RW_RANK
GN_EPS = 64e-5
ML_HEADS = 4
ML_DQK = 128
ML_DV = 256
ML_QK_WIDTH = 512
ML_V_WIDTH = 1024
ML_MAIN = 2 * ML_QK_WIDTH + 2 * ML_V_WIDTH
POOL_WIDTH = 1024
POOL_WINDOWS = (2, 4, 8, 16)
POOL_GW = 256
POOL_BUF = 15
POOL_HIST = 16
POOL_TAIL = 24
MEM_LEN = 256
XA_HEADS = 4
XA_HEAD_DIM = 128
XA_WIDTH = 512
MOE_GROUPS = 4
MOE_PER_GROUP = 8
MOE_EXPERTS = 32
MOE_HIDDEN = 256
RMS_EPS = 1e-6
LANES = 128
SUBLANES = 8
VMEM_LIMIT = 56 * 1024 * 1024
ROW_TILE = 1024
CHUNK_ROWS = 64
TN_RWKV = 640
TN_MLSTM = 1024
TN_GATES = 1024


def _cparams(n_axes):
    return pltpu.CompilerParams(dimension_semantics=("arbitrary",) * n_axes, vmem_limit_bytes=VMEM_LIMIT)


def _dot(a, b, precision=None):
    return jnp.dot(a, b, preferred_element_type=F32, precision=precision)


def _dot_nt(a, b, precision=None):
    return lax.dot_general(a, b, (((1,), (1,)), ((), ())), preferred_element_type=F32, precision=precision)


def _dot_tn(a, b, precision=None):
    return lax.dot_general(a, b, (((0,), (0,)), ((), ())), preferred_element_type=F32, precision=precision)


def _bf(x):
    return x.astype(BF16)


def _rms(x, g):
    return x * lax.rsqrt(jnp.mean(x * x, axis=-1, keepdims=True) + RMS_EPS) * g


def _log_sigmoid(x):
    return jnp.minimum(x, 0.0) - jnp.log1p(jnp.exp(-jnp.abs(x)))


def _softplus(x):
    return jnp.maximum(x, 0.0) + jnp.log1p(jnp.exp(-jnp.abs(x)))


def _seq_prefix_mask(n, c):
    r = lax.broadcasted_iota(jnp.int32, (n, n), 0)
    q = lax.broadcasted_iota(jnp.int32, (n, n), 1)
    return jnp.logical_and(q <= r, q // c == r // c)


def _layer_block(l, blk, idx):
    return pl.BlockSpec((None,) + tuple(blk), lambda b, i: (l,) + tuple(idx(b, i)))


def _norm_matmul_kernel(x_ref, g_ref, w_ref, o_ref, xn_ref, *, act):
    @pl.when(pl.program_id(1) == 0)
    def _():
        xn_ref[...] = _bf(_rms(x_ref[...], g_ref[...]))

    acc = _dot(xn_ref[...], w_ref[...])
    if act == "sigmoid":
        acc = jax.nn.sigmoid(acc)
    o_ref[...] = acc.astype(o_ref.dtype)


def norm_matmul(x, g, w, *, tn, name, out_dtype=F32, act=None):
    T, D = x.shape
    N = w.shape[1]
    tm = min(T, ROW_TILE)
    return pl.pallas_call(
        functools.partial(_norm_matmul_kernel, act=act),
        grid=(T // tm, N // tn),
        in_specs=[pl.BlockSpec((tm, D), lambda i, j: (i, 0)),
                  pl.BlockSpec((1, D), lambda i, j: (0, 0)),
                  pl.BlockSpec((D, tn), lambda i, j: (0, j))],
        out_specs=pl.BlockSpec((tm, tn), lambda i, j: (i, j)),
        out_shape=jax.ShapeDtypeStruct((T, N), out_dtype),
        scratch_shapes=[pltpu.VMEM((tm, D), BF16)],
        compiler_params=_cparams(2),
        name=name,
    )(x, g.reshape(1, D), w)


def _matmul_residual_kernel(a_ref, w_ref, r_ref, o_ref):
    o_ref[...] = r_ref[...] + _dot(a_ref[...], w_ref[...])


def matmul_residual(a, w, res, *, tn=1024):
    T, K = a.shape
    N = w.shape[1]
    tm = min(T, ROW_TILE)
    return pl.pallas_call(
        _matmul_residual_kernel,
        grid=(T // tm, N // tn),
        in_specs=[pl.BlockSpec((tm, K), lambda i, j: (i, 0)),
                  pl.BlockSpec((K, tn), lambda i, j: (0, j)),
                  pl.BlockSpec((tm, tn), lambda i, j: (i, j))],
        out_specs=pl.BlockSpec((tm, tn), lambda i, j: (i, j)),
        out_shape=jax.ShapeDtypeStruct((T, N), F32),
        compiler_params=_cparams(2),
        name="matmul_residual",
    )(a, w, res)


def _merge_kernel(orw_ref, oml_ref, opl_ref, g0_ref, g1_ref, g2_ref, wr_ref, wm_ref, wp_ref, o_ref):
    m = (g0_ref[...].astype(F32) * _dot(orw_ref[...], wr_ref[...])
         + g1_ref[...].astype(F32) * _dot(oml_ref[...], wm_ref[...])
         + g2_ref[...].astype(F32) * _dot(opl_ref[...], wp_ref[...]))
    o_ref[...] = _bf(m)


def merge_branches(o_rw, o_ml, o_pl, gates, w_rw, w_ml, w_pl, *, tn=1024):
    T, K = o_rw.shape
    tm = min(T, ROW_TILE)
    nj = D_MODEL // tn
    act = pl.BlockSpec((tm, K), lambda i, j: (i, 0))
    wsp = pl.BlockSpec((K, tn), lambda i, j: (0, j))
    gate = lambda b: pl.BlockSpec((tm, tn), lambda i, j, b=b: (i, b * nj + j))
    return pl.pallas_call(
        _merge_kernel,
        grid=(T // tm, nj),
        in_specs=[act, act, act, gate(0), gate(1), gate(2), wsp, wsp, wsp],
        out_specs=pl.BlockSpec((tm, tn), lambda i, j: (i, j)),
        out_shape=jax.ShapeDtypeStruct((T, D_MODEL), BF16),
        compiler_params=_cparams(2),
        name="merge_branches",
    )(o_rw, o_ml, o_pl, gates, gates, gates, w_rw, w_ml, w_pl)


def _pool_kernel(hist_ref, u_ref, w_ref, sc_ref, o_ref, tail_ref, e_ref, *, tl, bt, start):
    li = pl.program_id(1)

    @pl.when(li == 0)
    def _():
        e_ref[:, 0:POOL_HIST, :] = hist_ref[...]

    pos = start + li * tl + lax.broadcasted_iota(jnp.int32, (tl, 1), 0)
    for b in range(bt):
        u = u_ref[b]
        e_ref[b, POOL_HIST:, :] = u
        e = e_ref[b]
        s2 = e + pltpu.roll(e, 1, 0)
        s4 = s2[:, POOL_GW:] + pltpu.roll(s2[:, POOL_GW:], 2, 0)
        s8 = s4[:, POOL_GW:] + pltpu.roll(s4[:, POOL_GW:], 4, 0)
        s16 = s8[:, POOL_GW:] + pltpu.roll(s8[:, POOL_GW:], 8, 0)
        sums = (s2[:, :POOL_GW], s4[:, :POOL_GW], s8[:, :POOL_GW], s16)
        for g, win in enumerate(POOL_WINDOWS):
            cols = slice(g * POOL_GW, (g + 1) * POOL_GW)
            cnt = jnp.minimum(win, pos + 1).astype(F32)
            d = sums[g][POOL_HIST:, :] / cnt - u[:, cols]
            out = _dot(_bf(d), w_ref[g]) * sc_ref[:, cols]
            o_ref[b, :, cols] = out.astype(o_ref.dtype)
        tail_ref[b] = e[tl + POOL_HIST - POOL_TAIL:, :]
        e_ref[b, 0:POOL_HIST, :] = e[tl:, :]


def pool_mix(u_pl, buf, l, w_grp, scale, *, B, L, l_valid, start):
    tl = min(L, 256)
    bt = max(CHUNK_ROWS // L, 1)
    hist = jnp.pad(buf[l], ((0, 0), (POOL_HIST - POOL_BUF, 0), (0, 0)))
    u3 = u_pl.reshape(B, L, u_pl.shape[-1])
    out, tail = pl.pallas_call(
        functools.partial(_pool_kernel, tl=tl, bt=bt, start=start),
        grid=(B // bt, L // tl),
        in_specs=[pl.BlockSpec((bt, POOL_HIST, POOL_WIDTH), lambda b, i: (b, 0, 0)),
                  pl.BlockSpec((bt, tl, POOL_WIDTH), lambda b, i: (b, i, 0)),
                  pl.BlockSpec((4, POOL_GW, POOL_GW), lambda b, i: (0, 0, 0)),
                  pl.BlockSpec((1, POOL_WIDTH), lambda b, i: (0, 0))],
        out_specs=[pl.BlockSpec((bt, tl, POOL_WIDTH), lambda b, i: (b, i, 0)),
                   pl.BlockSpec((bt, POOL_TAIL, POOL_WIDTH), lambda b, i: (b, 0, 0))],
        out_shape=[jax.ShapeDtypeStruct((B, L, POOL_WIDTH), BF16),
                   jax.ShapeDtypeStruct((B, POOL_TAIL, POOL_WIDTH), F32)],
        scratch_shapes=[pltpu.VMEM((bt, tl + POOL_HIST, POOL_WIDTH), F32)],
        compiler_params=_cparams(2),
        name="pool_mix",
    )(hist, u3, w_grp, scale.reshape(1, POOL_WIDTH))
    pad = L - l_valid
    new_buf = tail[:, POOL_TAIL - pad - POOL_BUF:POOL_TAIL - pad, :]
    return out.reshape(B * L, POOL_WIDTH), new_buf


def _mlstm_kernel(*refs, c, bt, l_valid, has_carry):
    u_ref, gif_ref, c0_ref, n0_ref, m0_ref, bi_ref, bf_ref, gnw_ref = refs[:8]
    o_ref, cN_ref, nN_ref, mN_ref, c_s, n_s, m_s = refs[8 + has_carry:]
    ci = pl.program_id(1)

    @pl.when(ci == 0)
    def _():
        c_s[...] = c0_ref[...]
        n_s[...] = n0_ref[...]
        m_s[...] = m0_ref[...]

    N = bt * c
    gif = gif_ref[...].reshape(N, LANES)
    tok = lax.broadcasted_iota(jnp.int32, (N, 1), 0) % c
    valid = tok < l_valid
    ig_all = jnp.where(valid, gif + bi_ref[...], -jnp.inf)
    lf_all = jnp.where(valid, _log_sigmoid(gif + bf_ref[...]), 0.0)
    bcum_all = _dot(_seq_prefix_mask(N, c).astype(F32), lf_all, HI)
    ig_t = ig_all.T
    bcum_t = bcum_all.T
    ri = lax.broadcasted_iota(jnp.int32, (c, c), 0)
    causal = lax.broadcasted_iota(jnp.int32, (c, c), 1) <= ri
    st = []
    for b in range(bt):
        rs = slice(b * c, (b + 1) * c)
        for h in range(ML_HEADS):
            qs = slice(h * ML_DQK, (h + 1) * ML_DQK)
            ks = slice(ML_QK_WIDTH + h * ML_DQK, ML_QK_WIDTH + (h + 1) * ML_DQK)
            vs = slice(2 * ML_QK_WIDTH + h * ML_DV, 2 * ML_QK_WIDTH + (h + 1) * ML_DV)
            os_ = slice(2 * ML_QK_WIDTH + ML_V_WIDTH + h * ML_DV, 2 * ML_QK_WIDTH + ML_V_WIDTH + (h + 1) * ML_DV)
            i_c = ig_all[rs, h:h + 1]
            b_c = bcum_all[rs, ML_HEADS + h:ML_HEADS + h + 1]
            i_r = ig_t[h:h + 1, rs]
            b_r = bcum_t[ML_HEADS + h:ML_HEADS + h + 1, rs]
            m_prev = m_s[b, h]
            dlog = jnp.where(causal, b_c - b_r + i_r, -jnp.inf)
            inter = b_c + m_prev
            m_t = jnp.maximum(inter, jnp.max(dlog, axis=-1, keepdims=True))
            b_last = b_c[c - 1:c, :]
            s_log = b_last - b_c + i_c
            m_new = jnp.maximum(b_last + m_prev, jnp.max(s_log, axis=0, keepdims=True))
            q = u_ref[b, :, qs]
            k = u_ref[b, :, ks] * (ML_DQK ** -0.5)
            st.append(dict(b=b, h=h, hs=slice(h * ML_DV, (h + 1) * ML_DV), os=os_, q=q, k=k, qb=_bf(q), vb=_bf(u_ref[b, :, vs]),
                           dlog=dlog, m_t=m_t, w_prev=jnp.exp(inter - m_t), m_new=m_new,
                           kw=k * jnp.exp(s_log - m_new), wp=jnp.exp(b_last + m_prev - m_new)))
    for s in st:
        s["wts"] = jnp.exp(s["dlog"] - s["m_t"]) * _dot_nt(s["qb"], _bf(s["k"]))
    for s in st:
        b, h = s["b"], s["h"]
        c_prev = c_s[b, h]
        n_prev = n_s[b, h]
        num = s["w_prev"] * _dot(s["qb"], _bf(c_prev)) + _dot(_bf(s["wts"]), s["vb"])
        den = (s["w_prev"] * jnp.sum(s["q"] * n_prev, axis=-1, keepdims=True)
               + jnp.sum(s["wts"], axis=-1, keepdims=True))
        s["hh"] = num / jnp.maximum(jnp.abs(den), jnp.exp(-s["m_t"]))
        c_s[b, h] = s["wp"] * c_prev + _dot_tn(_bf(s["kw"]), s["vb"])
        n_s[b, h] = s["wp"] * n_prev + jnp.sum(s["kw"], axis=0, keepdims=True)
        m_s[b, h] = s["m_new"]
    for s in st:
        hh = s["hh"]
        hn = hh * lax.rsqrt(jnp.mean(hh * hh, axis=-1, keepdims=True) + RMS_EPS) * gnw_ref[:, s["hs"]]
        o_ref[s["b"], :, s["hs"]] =(hn * jax.nn.sigmoid(u_ref[s["b"], :, s["os"]])).astype(o_ref.dtype)

    @pl.when(ci == pl.num_programs(1) - 1)
    def _():
        cN_ref[...] = c_s[...]
        nN_ref[...] = n_s[...]
        mN_ref[...] = m_s[...]


def mlstm_mix(u_ml, u_pl, c0, n0, m0, l, l_out, c_carry, b_i, b_f, gn_w, *, B, L, l_valid):
    c = min(L, CHUNK_ROWS)
    bt = CHUNK_ROWS // c
    u3 = u_ml.reshape(B, L, ML_MAIN)
    g3 = u_pl.reshape(B, L, POOL_WIDTH + LANES)
    zeros = jnp.zeros((LANES - 2 * ML_HEADS,), F32)
    bi = jnp.concatenate([b_i, jnp.zeros((ML_HEADS,), F32), zeros]).reshape(1, LANES)
    bf = jnp.concatenate([jnp.zeros((ML_HEADS,), F32), b_f, zeros]).reshape(1, LANES)
    nl = n0.shape[0]
    c_blk, n_blk, m_blk = (bt, ML_HEADS, ML_DQK, ML_DV), (bt, ML_HEADS, 1, ML_DQK), (bt, ML_HEADS, 1, 1)
    at_b = lambda b, i: (b, 0, 0, 0)
    in_specs = [pl.BlockSpec((bt, c, ML_MAIN), lambda b, i: (b, i, 0)),
                pl.BlockSpec((bt, c, LANES), lambda b, i: (b, i, POOL_WIDTH // LANES)),
                _layer_block(l, c_blk, at_b), _layer_block(l, n_blk, at_b), _layer_block(l, m_blk, at_b),
                pl.BlockSpec((1, LANES), lambda b, i: (0, 0)),
                pl.BlockSpec((1, LANES), lambda b, i: (0, 0)),
                pl.BlockSpec((1, ML_V_WIDTH), lambda b, i: (0, 0))]
    args = [u3, g3, c0, n0.reshape(nl, B, ML_HEADS, 1, ML_DQK), m0.reshape(nl, B, ML_HEADS, 1, 1), bi, bf,
            gn_w.reshape(1, ML_V_WIDTH)]
    aliases = {}
    if c_carry is not None:
        in_specs.append(pl.BlockSpec(memory_space=pl.ANY))
        args.append(c_carry)
        aliases = {len(args) - 1: 1}
    out, cN, nN, mN = pl.pallas_call(
        functools.partial(_mlstm_kernel, c=c, bt=bt, l_valid=l_valid, has_carry=c_carry is not None),
        grid=(B // bt, L // c),
        in_specs=in_specs,
        out_specs=[pl.BlockSpec((bt, c, ML_V_WIDTH), lambda b, i: (b, i, 0)),
                   _layer_block(l_out, c_blk, at_b),
                   pl.BlockSpec(n_blk, at_b), pl.BlockSpec(m_blk, at_b)],
        out_shape=[jax.ShapeDtypeStruct((B, L, ML_V_WIDTH), BF16),
                   jax.ShapeDtypeStruct((DEPTH, B, ML_HEADS, ML_DQK, ML_DV), F32),
                   jax.ShapeDtypeStruct((B, ML_HEADS, 1, ML_DQK), F32),
                   jax.ShapeDtypeStruct((B, ML_HEADS, 1, 1), F32)],
        scratch_shapes=[pltpu.VMEM(c_blk, F32), pltpu.VMEM(n_blk, F32), pltpu.VMEM(m_blk, F32)],
        input_output_aliases=aliases,
        compiler_params=_cparams(2),
        name="mlstm_mix",
    )(*args)
    return (out.reshape(B * L, ML_V_WIDTH), cN, nN.reshape(B, ML_HEADS, ML_DQK), mN.reshape(B, ML_HEADS))


RW_CHAIN_GROUP = 16

def _rwkv_chain_group(chains, seq, s_s, o_ref, rk_ref, gnw_ref, gnb_ref, *, C):
    rowi = lax.broadcasted_iota(jnp.int32, (C, C), 0)
    coli = lax.broadcasted_iota(jnp.int32, (C, C), 1)
    upper = rowi < coli
    col2 = lax.broadcasted_iota(jnp.int32, (C, 2 * C), 1)
    incl2 = jnp.where(col2 >= C, col2 - C, col2) <= lax.broadcasted_iota(jnp.int32, (C, 2 * C), 0)
    n_sq = max(int(math.log2(C)), 1)
    st = []
    for b, h in chains:
        rs = slice(b * C, (b + 1) * C)
        sl = slice(h * RW_HEAD, (h + 1) * RW_HEAD)
        cut = lambda t, rs=rs, sl=sl: t[rs, sl]
        kk_h = cut(seq["kkv"])
        nrm = jnp.sqrt(jnp.sum(kk_h * kk_h, axis=-1, keepdims=True))
        kap = jnp.where(seq["valid"][rs], kk_h / jnp.maximum(nrm, 1e-12), 0.0)
        k_h, v_h, r_h = cut(seq["kmod"]), cut(seq["v"]), cut(seq["r"])
        gi = cut(seq["gi"])
        b_t = kap * cut(seq["a"]) * gi
        k_t = k_h * gi
        st.append(dict(b=b, h=h, sl=sl, k_h=k_h, v_h=v_h, r_h=r_h,
                       a_t=_bf(-kap * cut(seq["gp"])), r_t=_bf(r_h * cut(seq["g"])),
                       bk=jnp.concatenate([b_t, k_t], axis=0),
                       ge=seq["g"][(b + 1) * C - 1:(b + 1) * C, sl], s0=s_s[b, h]))
    for c in st:
        bkb = _bf(c["bk"])
        mt = _dot_nt(bkb, c["a_t"])
        c["pt"] = jnp.where(upper, mt[:C], 0.0)
        c["akt"] = _bf(jnp.where(upper, mt[C:], 0.0))
        c["a_r"] = jnp.where(incl2, _dot_nt(c["r_t"], bkb), 0.0)
        c["s0b"] = _bf(c["s0"])
        c["vb"] = _bf(c["v_h"])
    for c in st:
        c["xt"] = _dot_nt(c["s0b"], c["a_t"]) + _dot_tn(c["vb"], c["akt"])
    for _ in range(n_sq - 1):
        for c in st:
            z = _dot(_bf(jnp.concatenate([c["pt"], c["xt"]], axis=0)), _bf(c["pt"]))
            c["pt"] = z[:C]
            c["xt"] = c["xt"] + z[C:]
    for c in st:
        c["ut"] = _bf(c["xt"] + _dot(_bf(c["xt"]), _bf(c["pt"])))
    for c in st:
        a_r = c["a_r"]
        c["y"] = (_dot_nt(c["r_t"], c["s0b"]) + _dot_nt(_bf(a_r[:, :C]), c["ut"])
                  + _dot(_bf(a_r[:, C:]), c["vb"]))
        bkg = c["bk"] * c["ge"]
        s_s[c["b"], c["h"]] = (c["s0"] * c["ge"] + _dot(c["ut"], _bf(bkg[:C]))
                               + _dot_tn(c["vb"], _bf(bkg[C:])))
    for c in st:
        y, sl = c["y"], c["sl"]
        mean = jnp.mean(y, axis=-1, keepdims=True)
        yc = y - mean
        var = jnp.mean(yc * yc, axis=-1, keepdims=True)
        yn = yc * lax.rsqrt(var + GN_EPS) * gnw_ref[:, sl] + gnb_ref[:, sl]
        bonus = jnp.sum(c["r_h"] * c["k_h"] * rk_ref[:, sl], axis=-1, keepdims=True) * c["v_h"]
        o_ref[c["b"], :, sl] = (yn + bonus).astype(o_ref.dtype)


def _rwkv_kernel(*refs, C, bt, l_valid, has_carry):
    (u_ref, sh_ref, s0_ref, mu_ref, w0_ref, w2_ref, a0_ref, a2_ref, kk_ref, ka_ref, rk_ref, gnw_ref,
     gnb_ref) = refs[:13]
    o_ref, sN_ref, prev_s, s_s = refs[13 + has_carry:]
    ci = pl.program_id(1)

    @pl.when(ci == 0)
    def _():
        prev_s[...] = sh_ref[...]
        s_s[...] = s0_ref[...]

    N = bt * C
    u = u_ref[...].reshape(N, RW_COLS)
    tok = lax.broadcasted_iota(jnp.int32, (N, 1), 0) % C
    valid = tok < l_valid
    prev = jnp.concatenate([jnp.broadcast_to(prev_s[b], (C, RW_COLS)) for b in range(bt)], axis=0)
    u_prev = jnp.where(tok == 0, prev, pltpu.roll(u, 1, 0))
    for b in range(bt):
        prev_s[b] = u[(b + 1) * C - 1:(b + 1) * C, :]
    us = u + (u_prev - u) * mu_ref[...]
    W = RW_WIDTH
    k = us[:, W:2 * W]
    wd = us[:, 3 * W:3 * W + RW_RANK]
    ad = us[:, 3 * W + RW_RANK:3 * W + 2 * RW_RANK]
    xw = w0_ref[...] + _dot(_bf(jnp.tanh(wd)), _bf(w2_ref[...]))
    log_w = -_softplus(-xw) - 0.5
    ld = jnp.where(valid, -jnp.exp(log_w), 0.0)
    a = jax.nn.sigmoid(a0_ref[...] + _dot(_bf(ad), _bf(a2_ref[...])))
    cum = _dot(_seq_prefix_mask(N, C).astype(F32), ld, HI)
    seq = dict(valid=valid, r=us[:, 0:W], v=jnp.where(valid, us[:, 2 * W:3 * W], 0.0), a=a,
               g=jnp.exp(cum), gi=jnp.exp(-cum), gp=jnp.exp(cum - ld), kkv=k * kk_ref[...],
               kmod=jnp.where(valid, k * (1.0 + (a - 1.0) * ka_ref[...]), 0.0))
    chains = [(b, h) for b in range(bt) for h in range(RW_HEADS)]
    group = RW_CHAIN_GROUP * bt
    for i in range(0, len(chains), group):
        _rwkv_chain_group(chains[i:i + group], seq, s_s, o_ref, rk_ref, gnw_ref, gnb_ref, C=C)

    @pl.when(ci == pl.num_programs(1) - 1)
    def _():
        sN_ref[...] = s_s[...]


def rwkv7_mix(u_rw, shift_prev, s_prev, l, l_out, s_carry, mu, w0, w2, a0, a2, k_k, k_a, r_k, gn_w, gn_b, *, B, L,
              l_valid):
    C = min(L, CHUNK_ROWS)
    bt = CHUNK_ROWS // C
    u3 = u_rw.reshape(B, L, RW_COLS)
    vec = lambda n: pl.BlockSpec((1, n), lambda b, i: (0, 0))
    row = lambda t: t.reshape(1, -1)
    s_blk = (bt, RW_HEADS, RW_HEAD, RW_HEAD)
    at_b = lambda b, i: (b, 0, 0, 0)
    in_specs = [pl.BlockSpec((bt, C, RW_COLS), lambda b, i: (b, i, 0)),
                _layer_block(l, (bt, 1, RW_COLS), lambda b, i: (b, 0, 0)),
                _layer_block(l, s_blk, at_b),
                vec(RW_COLS), vec(RW_WIDTH),
                pl.BlockSpec((RW_RANK, RW_WIDTH), lambda b, i: (0, 0)),
                vec(RW_WIDTH),
                pl.BlockSpec((RW_RANK, RW_WIDTH), lambda b, i: (0, 0)),
                vec(RW_WIDTH), vec(RW_WIDTH), vec(RW_WIDTH), vec(RW_WIDTH), vec(RW_WIDTH)]
    args = [u3, shift_prev.reshape(shift_prev.shape[0], B, 1, RW_COLS), s_prev, row(mu), row(w0), w2, row(a0), a2,
            row(k_k), row(k_a), row(r_k), row(gn_w), row(gn_b)]
    aliases = {}
    if s_carry is not None:
        in_specs.append(pl.BlockSpec(memory_space=pl.ANY))
        args.append(s_carry)
        aliases = {len(args) - 1: 1}
    out, sN = pl.pallas_call(
        functools.partial(_rwkv_kernel, C=C, bt=bt, l_valid=l_valid, has_carry=s_carry is not None),
        grid=(B // bt, L // C),
        in_specs=in_specs,
        out_specs=[pl.BlockSpec((bt, C, RW_WIDTH), lambda b, i: (b, i, 0)), _layer_block(l_out, s_blk, at_b)],
        out_shape=[jax.ShapeDtypeStruct((B, L, RW_WIDTH), BF16),
                   jax.ShapeDtypeStruct((DEPTH, B, RW_HEADS, RW_HEAD, RW_HEAD), F32)],
        scratch_shapes=[pltpu.VMEM((bt, 1, RW_COLS), F32), pltpu.VMEM(s_blk, F32)],
        input_output_aliases=aliases,
        compiler_params=_cparams(2),
        name="rwkv7_mix",
    )(*args)
    return out.reshape(B * L, RW_WIDTH), sN


RWS_VECS = 5


def _rwkv_short_kernel(*refs, l_valid, has_carry):
    (ur_ref, uk_ref, uv_ref, ul_ref, shr_ref, shk_ref, shv_ref, shl_ref, mur_ref, muk_ref, muv_ref, mul_ref,
     w0_ref, w2_ref, a0_ref, a2_ref, kk_ref, ka_ref, rk_ref, gnw_ref, gnb_ref, s_ref) = refs[:22]
    o_ref, sN_ref, vec_s, val_s, y_s = refs[22 + has_carry:]
    H2 = LANES // RW_HEAD
    heads = [slice(h * RW_HEAD, (h + 1) * RW_HEAD) for h in range(H2)]
    o_ref[...] = jnp.zeros_like(o_ref)

    def shifted(u_ref, sh_ref, mu_ref, t):
        u = u_ref[:, t, :]
        prev = sh_ref[...] if t == 0 else u_ref[:, t - 1, :]
        return u + (prev - u) * mu_ref[...]

    rows = []
    for t in range(l_valid):
        r = shifted(ur_ref, shr_ref, mur_ref, t)
        k = shifted(uk_ref, shk_ref, muk_ref, t)
        v = shifted(uv_ref, shv_ref, muv_ref, t)
        lo = shifted(ul_ref, shl_ref, mul_ref, t)
        xw = w0_ref[...] + _dot(jnp.tanh(lo[:, :RW_RANK]), w2_ref[...], HI)
        w = jnp.exp(-jnp.exp(-_softplus(-xw) - 0.5))
        a = jax.nn.sigmoid(a0_ref[...] + _dot(lo[:, RW_RANK:], a2_ref[...], HI))
        kkv = k * kk_ref[...]
        kmod = k * (1.0 + (a - 1.0) * ka_ref[...])
        kap = jnp.concatenate(
            [kkv[:, sl] / jnp.maximum(jnp.sqrt(jnp.sum(kkv[:, sl] * kkv[:, sl], axis=-1, keepdims=True)), 1e-12)
             for sl in heads], axis=1)
        for j, x in enumerate((w, kap, kap * a, kmod, r)):
            vec_s[t, j] = x.T.reshape(H2, RW_HEAD, x.shape[0])
        val_s[t] = v.T.reshape(H2, RW_HEAD, v.shape[0])
        rows.append((r, kmod, v))

    for h in range(H2):
        def body(vi, carry, h=h):
            s = s_ref[h, vi]
            for t in range(l_valid):
                w, kap, bb, kk, rr = (vec_s[t, j, h] for j in range(RWS_VECS))
                sa = jnp.sum(s * kap, axis=0, keepdims=True)
                s = s * w - sa * bb + val_s[t, h, pl.ds(vi, 1), :] * kk
                y_s[t, h, pl.ds(vi, 1), :] = jnp.sum(s * rr, axis=0, keepdims=True)
            sN_ref[h, vi] = s
            return carry
        lax.fori_loop(0, RW_HEAD, body, 0)

    for t in range(l_valid):
        r, kmod, v = rows[t]
        y_all = y_s[t].reshape(LANES, y_s.shape[-1]).T
        outs = []
        for sl in heads:
            y = y_all[:, sl]
            mean = jnp.mean(y, axis=-1, keepdims=True)
            yc = y - mean
            var = jnp.mean(yc * yc, axis=-1, keepdims=True)
            yn = yc * lax.rsqrt(var + GN_EPS) * gnw_ref[:, sl] + gnb_ref[:, sl]
            bonus = jnp.sum(r[:, sl] * kmod[:, sl] * rk_ref[:, sl], axis=-1, keepdims=True) * v[:, sl]
            outs.append(yn + bonus)
        o_ref[:, t, :] = jnp.concatenate(outs, axis=1).astype(o_ref.dtype)


def rwkv7_short(u_rw, shift_prev, s_prev_t, l, l_out, s_carry, mu, w0, w2, a0, a2, k_k, k_a, r_k, gn_w, gn_b, *, B, L,
                l_valid):
    assert B == LANES, "the batch must fill the lane dimension"
    H2 = LANES // RW_HEAD
    nblk = RW_WIDTH // LANES
    u3 = u_rw.reshape(B, L, RW_COLS)
    seg = lambda off: pl.BlockSpec((B, L, LANES), lambda hp, off=off: (0, 0, off + hp))
    lora = pl.BlockSpec((B, L, LANES), lambda hp: (0, 0, 3 * nblk))
    sh = lambda off: pl.BlockSpec((None, B, LANES), lambda hp, off=off: (l, 0, off + hp))
    sh_lora = pl.BlockSpec((None, B, LANES), lambda hp: (l, 0, 3 * nblk))
    mus = lambda off: pl.BlockSpec((1, LANES), lambda hp, off=off: (0, off + hp))
    mu_lora = pl.BlockSpec((1, LANES), lambda hp: (0, 3 * nblk))
    vec = pl.BlockSpec((1, LANES), lambda hp: (0, hp))
    mat = pl.BlockSpec((RW_RANK, LANES), lambda hp: (0, hp))
    s_blk = (H2, RW_HEAD, RW_HEAD, B)
    in_specs = [seg(0), seg(nblk), seg(2 * nblk), lora, sh(0), sh(nblk), sh(2 * nblk), sh_lora,
                mus(0), mus(nblk), mus(2 * nblk), mu_lora,
                vec, mat, vec, mat, vec, vec, vec, vec, vec,
                pl.BlockSpec((None,) + s_blk, lambda hp: (l, hp, 0, 0, 0))]
    row = lambda t: t.reshape(1, -1)
    args = [u3, u3, u3, u3, shift_prev, shift_prev, shift_prev, shift_prev, row(mu), row(mu), row(mu), row(mu),
            row(w0), w2, row(a0), a2, row(k_k), row(k_a), row(r_k), row(gn_w), row(gn_b), s_prev_t]
    aliases = {}
    if s_carry is not None:
        in_specs.append(pl.BlockSpec(memory_space=pl.ANY))
        args.append(s_carry)
        aliases = {len(args) - 1: 1}
    out, sN = pl.pallas_call(
        functools.partial(_rwkv_short_kernel, l_valid=l_valid, has_carry=s_carry is not None),
        grid=(RW_HEADS // H2,),
        in_specs=in_specs,
        out_specs=[pl.BlockSpec((B, L, LANES), lambda hp: (0, 0, hp)),
                   pl.BlockSpec((None,) + s_blk, lambda hp: (l_out, hp, 0, 0, 0))],
        out_shape=[jax.ShapeDtypeStruct((B, L, RW_WIDTH), BF16),
                   jax.ShapeDtypeStruct((DEPTH, RW_HEADS, RW_HEAD, RW_HEAD, B), F32)],
        scratch_shapes=[pltpu.VMEM((l_valid, RWS_VECS, H2, RW_HEAD, B), F32),
                        pltpu.VMEM((l_valid, H2, RW_HEAD, B), F32),
                        pltpu.VMEM((l_valid, H2, RW_HEAD, B), F32)],
        input_output_aliases=aliases,
        compiler_params=_cparams(1),
        name="rwkv7_short",
    )(*args)
    return out.reshape(B * L, RW_WIDTH), sN


ATTN_SEQS_PER_STEP = 8


def _attn_kernel(q_ref, kv_ref, o_ref):
    heads = [slice(h * XA_HEAD_DIM, (h + 1) * XA_HEAD_DIM) for h in range(XA_HEADS)]
    s = [_dot_nt(q_ref[0, :, hs], _bf(kv_ref[0, :, hs])) * (XA_HEAD_DIM ** -0.5) for hs in heads]
    e = [jnp.exp(t - jnp.max(t, axis=-1, keepdims=True)) for t in s]
    p = [t / jnp.sum(t, axis=-1, keepdims=True) for t in e]
    for h, hs in enumerate(heads):
        vs = slice(XA_WIDTH + h * XA_HEAD_DIM, XA_WIDTH + (h + 1) * XA_HEAD_DIM)
        o_ref[0, :, hs] = _dot(_bf(p[h]), _bf(kv_ref[0, :, vs])).astype(o_ref.dtype)


def _attn_short_kernel(q_ref, k_ref, v_ref, o_ref, *, bt):
    L = q_ref.shape[1]
    rows, cols = XA_HEADS * L, MEM_LEN * XA_HEADS
    own = (lax.broadcasted_iota(jnp.int32, (rows, cols), 0) // L
           == lax.broadcasted_iota(jnp.int32, (rows, cols), 1) % XA_HEADS)
    heads = [slice(h * XA_HEAD_DIM, (h + 1) * XA_HEAD_DIM) for h in range(XA_HEADS)]
    qs = [jnp.concatenate([q_ref[b, :, hs] for hs in heads], axis=0) for b in range(bt)]
    s = [_dot_nt(qs[b], _bf(k_ref[b].reshape(cols, XA_HEAD_DIM))) * (XA_HEAD_DIM ** -0.5) for b in range(bt)]
    s = [jnp.where(own, t, -jnp.inf) for t in s]
    e = [jnp.exp(t - jnp.max(t, axis=-1, keepdims=True)) for t in s]
    p = [t / jnp.sum(t, axis=-1, keepdims=True) for t in e]
    for b in range(bt):
        out = _dot(_bf(p[b]), _bf(v_ref[b].reshape(cols, XA_HEAD_DIM)))
        for h, hs in enumerate(heads):
            o_ref[b, :, hs] = out[h * L:(h + 1) * L].astype(o_ref.dtype)


def cross_attention(q, mem, l, *, B, L):
    q3 = q.reshape(B, L, XA_WIDTH)
    if l is None:
        bt, tq, body, mems = 1, min(L, 512), _attn_kernel, [mem]
        mem_specs = [pl.BlockSpec((bt, MEM_LEN, 2 * XA_WIDTH), lambda b, i: (b, 0, 0))]
    else:
        bt, tq, mems = ATTN_SEQS_PER_STEP, L, list(mem)
        body = functools.partial(_attn_short_kernel, bt=bt)
        mem_specs = [pl.BlockSpec((None, bt, MEM_LEN, XA_HEADS, XA_HEAD_DIM), lambda b, i: (l, b, 0, 0, 0))] * 2
    qo_spec = pl.BlockSpec((bt, tq, XA_WIDTH), lambda b, i: (b, i, 0))
    out = pl.pallas_call(
        body,
        grid=(B // bt, L // tq),
        in_specs=[qo_spec] + mem_specs,
        out_specs=qo_spec,
        out_shape=jax.ShapeDtypeStruct((B, L, XA_WIDTH), BF16),
        compiler_params=_cparams(2),
        name="cross_attention",
    )(q3, *mems)
    return out.reshape(B * L, XA_WIDTH)


ROUTER_GROUP_LANE = MOE_EXPERTS


MOE_TB = 512
MOE_TB_GATHER = 1024
MOE_SUB = 256
MOE_TE_LONG = 1024
MOE_TE_SHORT = 512
ROUTE_GROUP_LANE = 0
ROUTE_RANK_LANE = 1
COMB_PIECES = 3


def _router_kernel(x_ref, g_ref, w_ref, b_ref, xn_ref, comb_ref, route_ref, cnt_s):
    @pl.when(pl.program_id(0) == 0)
    def _():
        cnt_s[...] = jnp.zeros_like(cnt_s)

    xn = _rms(x_ref[...], g_ref[...])
    xn_ref[...] = _bf(xn)
    z = _dot(xn, w_ref[...], HI) + b_ref[...]
    tm = z.shape[0]
    lane = lax.broadcasted_iota(jnp.int32, z.shape, 1).astype(F32)
    big = float(LANES)
    neg = -jnp.inf
    first = lambda mask: jnp.min(jnp.where(mask, lane, big), axis=-1, keepdims=True)
    is_g = jnp.logical_and(lane >= ROUTER_GROUP_LANE, lane < ROUTER_GROUP_LANE + MOE_GROUPS)
    zg = jnp.where(is_g, z, neg)
    mg = jnp.max(zg, axis=-1, keepdims=True)
    grp = first(zg == mg) - ROUTER_GROUP_LANE
    p_grp = 1.0 / jnp.sum(jnp.exp(zg - mg), axis=-1, keepdims=True)
    lo = grp * MOE_PER_GROUP
    ze = jnp.where(jnp.logical_and(lane >= lo, lane < lo + MOE_PER_GROUP), z, neg)
    t1 = jnp.max(ze, axis=-1, keepdims=True)
    i1 = first(ze == t1)
    ze2 = jnp.where(lane == i1, neg, ze)
    t2 = jnp.max(ze2, axis=-1, keepdims=True)
    i2 = first(ze2 == t2)
    e2 = jnp.exp(t2 - t1)
    g1 = p_grp / (1.0 + e2)
    comb = jnp.where(lane == i1, g1, 0.0) + jnp.where(lane == i2, g1 * e2, 0.0)
    c_hi = _bf(comb)
    rest = comb - c_hi.astype(F32)
    c_mid = _bf(rest)
    comb_ref[...] = jnp.concatenate([c_hi, c_mid, _bf(rest - c_mid.astype(F32))], axis=1)
    onehot = jnp.where(lane == grp, 1.0, 0.0)
    r = lax.broadcasted_iota(jnp.int32, (tm, tm), 0)
    c = lax.broadcasted_iota(jnp.int32, (tm, tm), 1)
    before = _dot(_bf(jnp.where(c < r, 1.0, 0.0)), _bf(onehot)) + cnt_s[...]
    rank = jnp.sum(onehot * before, axis=-1, keepdims=True)
    cnt_s[...] = cnt_s[...] + jnp.sum(onehot, axis=0, keepdims=True)
    route_ref[...] = jnp.where(lane == ROUTE_GROUP_LANE, grp, 0.0) + jnp.where(lane == ROUTE_RANK_LANE, rank, 0.0)


def moe_router(x, g, w_r1, b_r1, w_r2, b_r2):
    T, D = x.shape
    tm = MOE_TB
    pad = LANES - MOE_EXPERTS - MOE_GROUPS
    w = jnp.concatenate([w_r2, w_r1, jnp.zeros((D, pad), F32)], axis=1)
    b = jnp.concatenate([b_r2, b_r1, jnp.zeros((pad,), F32)]).reshape(1, LANES)
    return pl.pallas_call(
        _router_kernel,
        grid=(T // tm,),
        in_specs=[pl.BlockSpec((tm, D), lambda i: (i, 0)),
                  pl.BlockSpec((1, D), lambda i: (0, 0)),
                  pl.BlockSpec((D, LANES), lambda i: (0, 0)),
                  pl.BlockSpec((1, LANES), lambda i: (0, 0))],
        out_specs=[pl.BlockSpec((tm, D), lambda i: (i, 0)),
                   pl.BlockSpec((tm, COMB_PIECES * LANES), lambda i: (i, 0)),
                   pl.BlockSpec((tm, LANES), lambda i: (i, 0))],
        out_shape=[jax.ShapeDtypeStruct((T, D), BF16), jax.ShapeDtypeStruct((T, COMB_PIECES * LANES), BF16),
                   jax.ShapeDtypeStruct((T, LANES), F32)],
        scratch_shapes=[pltpu.VMEM((1, LANES), F32)],
        compiler_params=_cparams(1),
        name="moe_router",
    )(x, g.reshape(1, D), w, b)


def _moe_plan(route, T, te):
    i32 = jnp.int32
    grp = route[:, ROUTE_GROUP_LANE].astype(i32)
    rank = route[:, ROUTE_RANK_LANE].astype(i32)
    rows = T + MOE_GROUPS * te
    n_sub = rows // MOE_SUB
    n_tiles = rows // te
    onehot = (grp[:, None] == jnp.arange(MOE_GROUPS, dtype=i32)[None]).astype(i32)
    seg_rows = (onehot.sum(axis=0) + te - 1) // te * te
    seg_end = jnp.cumsum(seg_rows)
    seg_start = seg_end - seg_rows
    dest = seg_start[grp] + rank
    group_of = lambda row: jnp.minimum(jnp.sum(row[:, None] >= seg_end[None], axis=1), MOE_GROUPS - 1).astype(i32)
    sub_row = jnp.arange(n_sub, dtype=i32) * MOE_SUB
    sub_g = group_of(sub_row)
    sub_valid = sub_row < seg_end[-1]
    r0 = sub_row - seg_start[sub_g]

    def overlap(tb):
        blk_cnt = onehot.reshape(T // tb, tb, MOE_GROUPS).sum(axis=1)
        cum_blk = jnp.concatenate([jnp.zeros((1, MOE_GROUPS), i32), jnp.cumsum(blk_cnt, axis=0)])
        lo = cum_blk[:-1][:, sub_g].T
        hi = cum_blk[1:][:, sub_g].T
        return (lo < (r0 + MOE_SUB)[:, None]) & (hi > r0[:, None]) & sub_valid[:, None]

    g_mask = overlap(MOE_TB_GATHER)
    first_col = (jnp.arange(T // MOE_TB_GATHER) == 0)[None]
    g_mask = g_mask | (first_col & ~g_mask.any(axis=1, keepdims=True))

    def items(mask, ncol, nblk):
        n_items = n_sub + MOE_GROUPS * nblk
        flat = jnp.nonzero(mask.reshape(-1), size=n_items, fill_value=-1)[0].astype(i32)
        valid = flat >= 0
        flat = jnp.where(valid, flat, jnp.max(flat))
        major, minor = flat // ncol, flat % ncol
        prev = jnp.concatenate([jnp.full((1,), -1, i32), major[:-1]])
        nxt = jnp.concatenate([major[1:], jnp.full((1,), -1, i32)])
        nvalid = jnp.concatenate([valid[1:], jnp.zeros((1,), bool)])
        first = (major != prev) & valid
        last = ((major != nxt) | ~nvalid) & valid
        return major, minor, first.astype(i32), last.astype(i32), valid.astype(i32)

    g_sub, g_blk, g_first, _, g_valid = items(g_mask, T // MOE_TB_GATHER, T // MOE_TB_GATHER)
    s_blk, s_sub, s_first, s_last, s_valid = items(overlap(MOE_TB).T, n_sub, T // MOE_TB)
    tile_row = jnp.arange(n_tiles, dtype=i32) * te
    tile_valid = tile_row < seg_end[-1]
    tile_group = group_of(jnp.where(tile_valid, tile_row, seg_end[-1] - 1))
    return dict(dest=dest, rows=rows, gather=(g_sub, g_blk, g_first, g_valid),
                scatter=(s_sub, s_blk, s_first, s_last, s_valid),
                tile_group=tile_group, tile_valid=tile_valid.astype(i32))


def _moe_gather_kernel(sub_ref, blk_ref, first_ref, valid_ref, dest_ref, xn_ref, comb_ref, xs_ref, cs_ref):
    w = pl.program_id(0)

    @pl.when(valid_ref[w] == 1)
    def _():
        rows = sub_ref[w] * MOE_SUB + lax.broadcasted_iota(jnp.int32, (MOE_SUB, MOE_TB_GATHER), 0)
        hit = _bf(jnp.where(dest_ref[0] == rows, 1.0, 0.0))
        gx = _bf(_dot(hit, xn_ref[...]))
        g3 = _dot(hit, comb_ref[...])
        gc = g3[:, :LANES] + g3[:, LANES:2 * LANES] + g3[:, 2 * LANES:]

        @pl.when(first_ref[w] == 1)
        def _():
            xs_ref[...] = gx
            cs_ref[...] = gc

        @pl.when(first_ref[w] == 0)
        def _():
            xs_ref[...] = xs_ref[...] + gx
            cs_ref[...] = cs_ref[...] + gc


def moe_gather(xn, comb, plan):
    T, D = xn.shape
    g_sub, g_blk, g_first, g_valid = plan["gather"]
    rows = plan["rows"]
    tb = MOE_TB_GATHER
    dest3 = plan["dest"].reshape(T // tb, 1, tb)
    grid_spec = pltpu.PrefetchScalarGridSpec(
        num_scalar_prefetch=4,
        grid=(g_sub.shape[0],),
        in_specs=[pl.BlockSpec((1, 1, tb), lambda w, s, b, f, v: (b[w], 0, 0)),
                  pl.BlockSpec((tb, D), lambda w, s, b, f, v: (b[w], 0)),
                  pl.BlockSpec((tb, COMB_PIECES * LANES), lambda w, s, b, f, v: (b[w], 0))],
        out_specs=[pl.BlockSpec((MOE_SUB, D), lambda w, s, b, f, v: (s[w], 0)),
                   pl.BlockSpec((MOE_SUB, LANES), lambda w, s, b, f, v: (s[w], 0))])
    return pl.pallas_call(
        _moe_gather_kernel,
        grid_spec=grid_spec,
        out_shape=[jax.ShapeDtypeStruct((rows, D), BF16), jax.ShapeDtypeStruct((rows, LANES), F32)],
        compiler_params=_cparams(1),
        name="moe_gather",
    )(g_sub, g_blk, g_first, g_valid, dest3, xn, comb)


MOE_EXPERTS_PER_STEP = 2


def _moe_group_experts_kernel(tg_ref, tv_ref, xs_ref, cs_ref, wg_ref, wu_ref, wd_ref, y_ref, acc):
    i = pl.program_id(0)
    e = pl.program_id(1)

    @pl.when(e == 0)
    def _():
        acc[...] = jnp.zeros_like(acc)

    @pl.when(tv_ref[i] == 1)
    def _():
        xs = xs_ref[...]
        cs = cs_ref[...]
        lane = lax.broadcasted_iota(jnp.int32, cs.shape, 1)
        first = tg_ref[i] * MOE_PER_GROUP + e * MOE_EXPERTS_PER_STEP
        hids = []
        for j in range(MOE_EXPERTS_PER_STEP):
            ce = jnp.sum(jnp.where(lane == first + j, cs, 0.0), axis=-1, keepdims=True)
            hids.append(_bf(jax.nn.silu(_dot(xs, _bf(wg_ref[j]))) * _dot(xs, _bf(wu_ref[j])) * ce))
        hid = jnp.concatenate(hids, axis=1)
        acc[...] += _dot(hid, _bf(wd_ref[...].reshape(MOE_EXPERTS_PER_STEP * MOE_HIDDEN, wd_ref.shape[-1])))

    @pl.when(e == pl.num_programs(1) - 1)
    def _():
        y_ref[...] = _bf(acc[...])


def moe_group_experts(xs, cs, plan, wg, wu, wd, l, te):
    rows, D = xs.shape
    eps = MOE_EXPERTS_PER_STEP
    steps = MOE_PER_GROUP // eps

    def expert(i, e, tg, tv):
        return (l, tg[i] * steps + jnp.where(tv[i] == 1, e, steps - 1), 0, 0)

    w_in = pl.BlockSpec((None, eps, D, MOE_HIDDEN), expert)
    grid_spec = pltpu.PrefetchScalarGridSpec(
        num_scalar_prefetch=2,
        grid=(rows // te, steps),
        in_specs=[pl.BlockSpec((te, D), lambda i, e, tg, tv: (i, 0)),
                  pl.BlockSpec((te, LANES), lambda i, e, tg, tv: (i, 0)),
                  w_in, w_in,
                  pl.BlockSpec((None, eps, MOE_HIDDEN, D), expert)],
        out_specs=pl.BlockSpec((te, D), lambda i, e, tg, tv: (i, 0)),
        scratch_shapes=[pltpu.VMEM((te, D), F32)])
    return pl.pallas_call(
        _moe_group_experts_kernel,
        grid_spec=grid_spec,
        out_shape=jax.ShapeDtypeStruct((rows, D), BF16),
        compiler_params=_cparams(2),
        name="moe_group_experts",
    )(plan["tile_group"], plan["tile_valid"], xs, cs, wg, wu, wd)


def _moe_scatter_kernel(sub_ref, blk_ref, first_ref, last_ref, valid_ref, dest_ref, y_ref, x_ref, gf_ref, o_ref,
                        *, final_norm):
    w = pl.program_id(0)

    @pl.when(valid_ref[w] == 1)
    def _():
        cols = sub_ref[w] * MOE_SUB + lax.broadcasted_iota(jnp.int32, (MOE_TB, MOE_SUB), 1)
        pick = _bf(jnp.where(dest_ref[...] == cols, 1.0, 0.0))
        upd = _dot(pick, y_ref[...])

        @pl.when(first_ref[w] == 1)
        def _():
            o_ref[...] = x_ref[...] + upd

        @pl.when(first_ref[w] == 0)
        def _():
            o_ref[...] += upd

        if final_norm:
            @pl.when(last_ref[w] == 1)
            def _():
                o_ref[...] = _rms(o_ref[...], gf_ref[...])


def moe_scatter(ys, x, plan, g_final, *, final_norm):
    T, D = x.shape
    s_sub, s_blk, s_first, s_last, s_valid = plan["scatter"]
    dest_col = plan["dest"].reshape(T, 1)
    im = lambda f: (lambda w, s, b, fi, la, v: f(w, s, b))
    grid_spec = pltpu.PrefetchScalarGridSpec(
        num_scalar_prefetch=5,
        grid=(s_sub.shape[0],),
        in_specs=[pl.BlockSpec((MOE_TB, 1), im(lambda w, s, b: (b[w], 0))),
                  pl.BlockSpec((MOE_SUB, D), im(lambda w, s, b: (s[w], 0))),
                  pl.BlockSpec((MOE_TB, D), im(lambda w, s, b: (b[w], 0))),
                  pl.BlockSpec((1, D), im(lambda w, s, b: (0, 0)))],
        out_specs=pl.BlockSpec((MOE_TB, D), im(lambda w, s, b: (b[w], 0))))
    return pl.pallas_call(
        functools.partial(_moe_scatter_kernel, final_norm=final_norm),
        grid_spec=grid_spec,
        out_shape=jax.ShapeDtypeStruct((T, D), F32),
        compiler_params=_cparams(1),
        name="moe_scatter",
    )(s_sub, s_blk, s_first, s_last, s_valid, dest_col, ys, x, g_final.reshape(1, D))


def hmoe_block(x, p, l, *, final_norm):
    T = x.shape[0]
    te = MOE_TE_LONG if T >= 4 * MOE_TE_LONG else MOE_TE_SHORT
    xn, comb, route = moe_router(x, p["g_moe"][l], p["moe_wr1"][l], p["moe_br1"][l], p["moe_wr2"][l],
                                 p["moe_br2"][l])
    plan = _moe_plan(route, T, te)
    xs, cs = moe_gather(xn, comb, plan)
    ys = moe_group_experts(xs, cs, plan, p["moe_wg"], p["moe_wu"], p["moe_wd"], l, te)
    return moe_scatter(ys, x, plan, p["g_final"], final_norm=final_norm)


def _layer_weights(l, p):
    w_in = p["w_in"][l]
    c0 = RW_COLS
    c1 = c0 + ML_MAIN
    c2 = c1 + 2 * ML_HEADS
    c3 = c2 + POOL_WIDTH
    w_pl = jnp.concatenate([w_in[:, c2:c3], w_in[:, c1:c2], jnp.zeros((D_MODEL, LANES - 2 * ML_HEADS), F32)], axis=1)
    return dict(
        w_rw=_bf(w_in[:, :c0]), w_ml=_bf(w_in[:, c0:c1]), w_pl=_bf(w_pl), w_gate=_bf(w_in[:, c3:]),
        w_up_rwkv=_bf(p["w_up_rwkv"][l]), w_up_mlstm=_bf(p["w_up_mlstm"][l]), w_up_pool=_bf(p["w_up_pool"][l]),
        w_out=_bf(p["w_out"][l]), pool_w=_bf(p["pool_w"][l]),
        xa_wq=_bf(p["xa_wq"][l]), xa_wo=_bf(p["xa_wo"][l]),
        xa_wkv=_bf(jnp.concatenate([p["xa_wk"][l], p["xa_wv"][l]], axis=1)),
    )


def _trunk_layer(x, mem, mem_l, st, carry, *, l, sl, p, lw, B, L, l_valid, start, final_norm):
    g_mix = p["g_mix"][l]
    u_rw = norm_matmul(x, g_mix, lw["w_rw"], tn=TN_RWKV, name="in_rwkv")
    u_ml = norm_matmul(x, g_mix, lw["w_ml"], tn=TN_MLSTM, name="in_mlstm")
    u_pl = norm_matmul(x, g_mix, lw["w_pl"], tn=POOL_WIDTH + LANES, name="in_pool")
    gates = norm_matmul(x, g_mix, lw["w_gate"], tn=TN_GATES, out_dtype=BF16, act="sigmoid", name="in_gates")

    rwkv = rwkv7_short if L < CHUNK_ROWS else rwkv7_mix
    o_rw, rw_s = rwkv(u_rw, st["rw_shift"], st["rw_s"], sl, l, carry["rw_s"], p["rw_mu"][l], p["rw_w0"][l],
                      p["rw_w2"][l], p["rw_a0"][l], p["rw_a2"][l], p["rw_k_k"][l], p["rw_k_a"][l],
                      p["rw_r_k"][l], p["rw_gn_w"][l], p["rw_gn_b"][l], B=B, L=L, l_valid=l_valid)
    rw_shift = u_rw.reshape(B, L, RW_COLS)[:, l_valid - 1]
    o_ml, ml_c, ml_n, ml_m = mlstm_mix(u_ml, u_pl, st["ml_c"], st["ml_n"], st["ml_m"], sl, l, carry["ml_c"],
                                       p["ml_b_i"][l], p["ml_b_f"][l], p["ml_gn_w"][l], B=B, L=L, l_valid=l_valid)
    o_pl, pool_buf = pool_mix(u_pl, st["pool"], sl, lw["pool_w"], p["pool_scale"][l], B=B, L=L, l_valid=l_valid,
                              start=start)
    merged = merge_branches(o_rw, o_ml, o_pl, gates, lw["w_up_rwkv"], lw["w_up_mlstm"], lw["w_up_pool"])
    x = matmul_residual(merged, lw["w_out"], x)

    q = norm_matmul(x, p["g_xa"][l], lw["xa_wq"], tn=XA_WIDTH, out_dtype=BF16, name="xa_q")
    att = cross_attention(q, mem, mem_l, B=B, L=L)
    x = matmul_residual(att, lw["xa_wo"], x)

    x = hmoe_block(x, p, l, final_norm=final_norm)
    return x, dict(rw_s=rw_s, ml_c=ml_c), (rw_shift, ml_n, ml_m, pool_buf)


def kernel(x_prompt, x_sample, cache_mem_k, cache_mem_v, state_rwkv_s, state_rwkv_shift, state_mlstm_c, state_mlstm_n, state_mlstm_m, state_pool, mem_prompt, g_mix, w_in, rw_mu, rw_w0, rw_w2, rw_a0, rw_a2, rw_k_k, rw_k_a, rw_r_k, rw_gn_w, rw_gn_b, ml_b_i, ml_b_f, ml_gn_w, pool_w, pool_scale, w_up_rwkv, w_up_mlstm, w_up_pool, w_out, g_xa, g_mem, xa_wq, xa_wk, xa_wv, xa_wo, g_moe, moe_wr1, moe_br1, moe_wr2, moe_br2, moe_wg, moe_wu, moe_wd, g_final):
    p = dict(g_mix=g_mix, w_in=w_in, rw_mu=rw_mu, rw_w0=rw_w0, rw_w2=rw_w2, rw_a0=rw_a0, rw_a2=rw_a2, rw_k_k=rw_k_k,
             rw_k_a=rw_k_a, rw_r_k=rw_r_k, rw_gn_w=rw_gn_w, rw_gn_b=rw_gn_b, ml_b_i=ml_b_i, ml_b_f=ml_b_f,
             ml_gn_w=ml_gn_w, pool_w=pool_w, pool_scale=pool_scale, w_up_rwkv=w_up_rwkv, w_up_mlstm=w_up_mlstm,
             w_up_pool=w_up_pool, w_out=w_out, g_xa=g_xa, g_mem=g_mem, xa_wq=xa_wq, xa_wk=xa_wk, xa_wv=xa_wv,
             xa_wo=xa_wo, g_moe=g_moe, moe_wr1=moe_wr1, moe_br1=moe_br1, moe_wr2=moe_wr2, moe_br2=moe_br2,
             moe_wg=moe_wg, moe_wu=moe_wu, moe_wd=moe_wd, g_final=g_final)
    Bp, Lp, D = x_prompt.shape
    Bs, Ls, _ = x_sample.shape
    Ls_pad = -(-Ls // SUBLANES) * SUBLANES
    yp = x_prompt.reshape(Bp * Lp, D)
    ys = jnp.pad(x_sample, ((0, 0), (0, Ls_pad - Ls), (0, 0))).reshape(Bs * Ls_pad, D)
    zeros = lambda *s: jnp.zeros((1,) + s, F32)
    st_p = dict(rw_shift=zeros(Bp, RW_COLS), rw_s=zeros(Bp, RW_HEADS, RW_HEAD, RW_HEAD),
                ml_c=zeros(Bp, ML_HEADS, ML_DQK, ML_DV), ml_n=zeros(Bp, ML_HEADS, ML_DQK), ml_m=zeros(Bp, ML_HEADS),
                pool=zeros(Bp, POOL_BUF, POOL_WIDTH))
    batch_minor = (0, 2, 3, 4, 1)
    st_s = dict(rw_shift=state_rwkv_shift, rw_s=jnp.transpose(state_rwkv_s, batch_minor), ml_c=state_mlstm_c,
                ml_n=state_mlstm_n, ml_m=state_mlstm_m, pool=state_pool)
    carry_p = carry_s = dict(rw_s=None, ml_c=None)
    small_p = [[] for _ in range(6)]
    small_s = [[] for _ in range(4)]
    for l in range(DEPTH):
        lw = _layer_weights(l, p)
        final = l == DEPTH - 1
        kv = norm_matmul(mem_prompt.reshape(Bp * MEM_LEN, D), g_mem[l], lw["xa_wkv"], tn=2 * XA_WIDTH,
                         name="memory_kv")
        kv3 = kv.reshape(Bp, MEM_LEN, 2 * XA_WIDTH)
        yp, carry_p, small = _trunk_layer(yp, kv3, None, st_p, carry_p, l=l, sl=0, p=p, lw=lw, B=Bp, L=Lp,
                                          l_valid=Lp, start=0, final_norm=final)
        mk = kv[:, :XA_WIDTH].reshape(Bp, MEM_LEN, XA_HEADS, XA_HEAD_DIM)
        mv = kv[:, XA_WIDTH:].reshape(Bp, MEM_LEN, XA_HEADS, XA_HEAD_DIM)
        for acc, t in zip(small_p, small + (mk, mv)):
            acc.append(t)
        ys, carry_s, small = _trunk_layer(ys, (cache_mem_k, cache_mem_v), l, st_s, carry_s, l=l, sl=l, p=p, lw=lw,
                                          B=Bs, L=Ls_pad, l_valid=Ls, start=PAST_LEN, final_norm=final)
        for acc, t in zip(small_s, small):
            acc.append(t)
    y_prompt = yp.reshape(Bp, Lp, D)
    y_sample = ys.reshape(Bs, Ls_pad, D)[:, :Ls]
    p_sh, p_n, p_m, p_pool, p_mk, p_mv = [jnp.stack(t) for t in small_p]
    s_sh, s_n, s_m, s_pool = [jnp.stack(t) for t in small_s]
    return (y_prompt, y_sample, carry_p["rw_s"], p_sh, carry_p["ml_c"], p_n, p_m, p_pool, p_mk, p_mv,
            jnp.transpose(carry_s["rw_s"], (0, 4, 1, 2, 3)), s_sh, carry_s["ml_c"], s_n, s_m, s_pool)
```

```python
import functools
import math

import jax
import jax.numpy as jnp
from jax import lax
from jax.experimental import pallas as pl
from jax.experimental.pallas import tpu as pltpu

F32 = jnp.float32
BF16 = jnp.bfloat16
HI = lax.Precision.HIGHEST

D_MODEL = 2048
DEPTH = 2
PAST_LEN = 16384
RW_HEAD = 64
RW_WIDTH = 1024
RW_HEADS = 16
RW_RANK = 64
RW_COLS = 3 * RW_WIDTH + 2 * RW_RANK
GN_EPS = 64e-5
ML_HEADS = 4
ML_DQK = 128
ML_DV = 256
ML_QK_WIDTH = 512
ML_V_WIDTH = 1024
ML_MAIN = 2 * ML_QK_WIDTH + 2 * ML_V_WIDTH
POOL_WIDTH = 1024
POOL_WINDOWS = (2, 4, 8, 16)
POOL_GW = 256
POOL_BUF = 15
POOL_HIST = 16
POOL_TAIL = 24
MEM_LEN = 256
XA_HEADS = 4
XA_HEAD_DIM = 128
XA_WIDTH = 512
MOE_GROUPS = 4
MOE_PER_GROUP = 8
MOE_EXPERTS = 32
MOE_HIDDEN = 256
RMS_EPS = 1e-6
LANES = 128
SUBLANES = 8
VMEM_LIMIT = 56 * 1024 * 1024
ROW_TILE = 1024
CHUNK_ROWS = 64
TN_RWKV = RW_COLS
TN_MLSTM = ML_MAIN
TM_WHOLE_SEGMENT = ROW_TILE // 2
TN_GATES = 1024


def _cparams(n_axes):
    return pltpu.CompilerParams(dimension_semantics=("arbitrary",) * n_axes, vmem_limit_bytes=VMEM_LIMIT)


def _dot(a, b, precision=None):
    return jnp.dot(a, b, preferred_element_type=F32, precision=precision)


def _dot_nt(a, b, precision=None):
    return lax.dot_general(a, b, (((1,), (1,)), ((), ())), preferred_element_type=F32, precision=precision)


def _dot_tn(a, b, precision=None):
    return lax.dot_general(a, b, (((0,), (0,)), ((), ())), preferred_element_type=F32, precision=precision)


def _bf(x):
    return x.astype(BF16)


def _split3(x):
    hi = _bf(x)
    rest = x - hi.astype(F32)
    mid = _bf(rest)
    return jnp.concatenate([hi, mid, _bf(rest - mid.astype(F32))], axis=1)


def _rms(x, g):
    return x * lax.rsqrt(jnp.mean(x * x, axis=-1, keepdims=True) + RMS_EPS) * g


def _log_sigmoid(x):
    return jnp.minimum(x, 0.0) - jnp.log1p(jnp.exp(-jnp.abs(x)))


def _softplus(x):
    return jnp.maximum(x, 0.0) + jnp.log1p(jnp.exp(-jnp.abs(x)))


def _seq_prefix_mask(n, c):
    r = lax.broadcasted_iota(jnp.int32, (n, n), 0)
    q = lax.broadcasted_iota(jnp.int32, (n, n), 1)
    return jnp.logical_and(q <= r, q // c == r // c)


def _layer_block(l, blk, idx):
    return pl.BlockSpec((None,) + tuple(blk), lambda b, i: (l,) + tuple(idx(b, i)))


def _norm_matmul_kernel(x_ref, g_ref, w_ref, o_ref, xn_ref, *, act):
    @pl.when(pl.program_id(1) == 0)
    def _():
        xn_ref[...] = _bf(_rms(x_ref[...], g_ref[...]))

    acc = _dot(xn_ref[...], w_ref[...])
    if act == "sigmoid":
        acc = jax.nn.sigmoid(acc)
    o_ref[...] = acc.astype(o_ref.dtype)


def norm_matmul(x, g, w, *, tn, name, tm=ROW_TILE, out_dtype=F32, act=None):
    T, D = x.shape
    N = w.shape[1]
    tm = min(T, tm)
    return pl.pallas_call(
        functools.partial(_norm_matmul_kernel, act=act),
        grid=(T // tm, N // tn),
        in_specs=[pl.BlockSpec((tm, D), lambda i, j: (i, 0)),
                  pl.BlockSpec((1, D), lambda i, j: (0, 0)),
                  pl.BlockSpec((D, tn), lambda i, j: (0, j))],
        out_specs=pl.BlockSpec((tm, tn), lambda i, j: (i, j)),
        out_shape=jax.ShapeDtypeStruct((T, N), out_dtype),
        scratch_shapes=[pltpu.VMEM((tm, D), BF16)],
        compiler_params=_cparams(2),
        name=name,
    )(x, g.reshape(1, D), w)


def _matmul_residual_kernel(a_ref, w_ref, r_ref, o_ref):
    o_ref[...] = r_ref[...] + _dot(a_ref[...], w_ref[...])


def matmul_residual(a, w, res, *, tn=1024):
    T, K = a.shape
    N = w.shape[1]
    tm = min(T, ROW_TILE)
    return pl.pallas_call(
        _matmul_residual_kernel,
        grid=(T // tm, N // tn),
        in_specs=[pl.BlockSpec((tm, K), lambda i, j: (i, 0)),
                  pl.BlockSpec((K, tn), lambda i, j: (0, j)),
                  pl.BlockSpec((tm, tn), lambda i, j: (i, j))],
        out_specs=pl.BlockSpec((tm, tn), lambda i, j: (i, j)),
        out_shape=jax.ShapeDtypeStruct((T, N), F32),
        compiler_params=_cparams(2),
        name="matmul_residual",
    )(a, w, res)


def _merge_kernel(orw_ref, oml_ref, opl_ref, g0_ref, g1_ref, g2_ref, wr_ref, wm_ref, wp_ref, o_ref):
    m = (g0_ref[...].astype(F32) * _dot(orw_ref[...], wr_ref[...])
         + g1_ref[...].astype(F32) * _dot(oml_ref[...], wm_ref[...])
         + g2_ref[...].astype(F32) * _dot(opl_ref[...], wp_ref[...]))
    o_ref[...] = _bf(m)


def merge_branches(o_rw, o_ml, o_pl, gates, w_rw, w_ml, w_pl, *, tn=1024):
    T, K = o_rw.shape
    tm = min(T, ROW_TILE)
    nj = D_MODEL // tn
    act = pl.BlockSpec((tm, K), lambda i, j: (i, 0))
    wsp = pl.BlockSpec((K, tn), lambda i, j: (0, j))
    gate = lambda b: pl.BlockSpec((tm, tn), lambda i, j, b=b: (i, b * nj + j))
    return pl.pallas_call(
        _merge_kernel,
        grid=(T // tm, nj),
        in_specs=[act, act, act, gate(0), gate(1), gate(2), wsp, wsp, wsp],
        out_specs=pl.BlockSpec((tm, tn), lambda i, j: (i, j)),
        out_shape=jax.ShapeDtypeStruct((T, D_MODEL), BF16),
        compiler_params=_cparams(2),
        name="merge_branches",
    )(o_rw, o_ml, o_pl, gates, gates, gates, w_rw, w_ml, w_pl)


def _pool_kernel(hist_ref, u_ref, w_ref, sc_ref, o_ref, tail_ref, e_ref, *, tl, bt, start):
    li = pl.program_id(1)

    @pl.when(li == 0)
    def _():
        e_ref[:, 0:POOL_HIST, :] = hist_ref[...]

    pos = start + li * tl + lax.broadcasted_iota(jnp.int32, (tl, 1), 0)
    for b in range(bt):
        u = u_ref[b]
        e_ref[b, POOL_HIST:, :] = u
        e = e_ref[b]
        s2 = e + pltpu.roll(e, 1, 0)
        s4 = s2[:, POOL_GW:] + pltpu.roll(s2[:, POOL_GW:], 2, 0)
        s8 = s4[:, POOL_GW:] + pltpu.roll(s4[:, POOL_GW:], 4, 0)
        s16 = s8[:, POOL_GW:] + pltpu.roll(s8[:, POOL_GW:], 8, 0)
        sums = (s2[:, :POOL_GW], s4[:, :POOL_GW], s8[:, :POOL_GW], s16)
        for g, win in enumerate(POOL_WINDOWS):
            cols = slice(g * POOL_GW, (g + 1) * POOL_GW)
            cnt = jnp.minimum(win, pos + 1).astype(F32)
            d = sums[g][POOL_HIST:, :] / cnt - u[:, cols]
            out = _dot(_bf(d), w_ref[g]) * sc_ref[:, cols]
            o_ref[b, :, cols] = out.astype(o_ref.dtype)
        tail_ref[b] = e[tl + POOL_HIST - POOL_TAIL:, :]
        e_ref[b, 0:POOL_HIST, :] = e[tl:, :]


def pool_mix(u_pl, buf, l, w_grp, scale, *, B, L, l_valid, start):
    tl = min(L, 256)
    bt = max(CHUNK_ROWS // L, 1)
    hist = jnp.pad(buf[l], ((0, 0), (POOL_HIST - POOL_BUF, 0), (0, 0)))
    u3 = u_pl.reshape(B, L, u_pl.shape[-1])
    out, tail = pl.pallas_call(
        functools.partial(_pool_kernel, tl=tl, bt=bt, start=start),
        grid=(B // bt, L // tl),
        in_specs=[pl.BlockSpec((bt, POOL_HIST, POOL_WIDTH), lambda b, i: (b, 0, 0)),
                  pl.BlockSpec((bt, tl, POOL_WIDTH), lambda b, i: (b, i, 0)),
                  pl.BlockSpec((4, POOL_GW, POOL_GW), lambda b, i: (0, 0, 0)),
                  pl.BlockSpec((1, POOL_WIDTH), lambda b, i: (0, 0))],
        out_specs=[pl.BlockSpec((bt, tl, POOL_WIDTH), lambda b, i: (b, i, 0)),
                   pl.BlockSpec((bt, POOL_TAIL, POOL_WIDTH), lambda b, i: (b, 0, 0))],
        out_shape=[jax.ShapeDtypeStruct((B, L, POOL_WIDTH), BF16),
                   jax.ShapeDtypeStruct((B, POOL_TAIL, POOL_WIDTH), F32)],
        scratch_shapes=[pltpu.VMEM((bt, tl + POOL_HIST, POOL_WIDTH), F32)],
        compiler_params=_cparams(2),
        name="pool_mix",
    )(hist, u3, w_grp, scale.reshape(1, POOL_WIDTH))
    pad = L - l_valid
    new_buf = tail[:, POOL_TAIL - pad - POOL_BUF:POOL_TAIL - pad, :]
    return out.reshape(B * L, POOL_WIDTH), new_buf


def _mlstm_kernel(*refs, c, bt, l_valid, has_carry):
    u_ref, gif_ref, c0_ref, n0_ref, m0_ref, bi_ref, bf_ref, gnw_ref = refs[:8]
    o_ref, cN_ref, nN_ref, mN_ref, c_s, n_s, m_s = refs[8 + has_carry:]
    ci = pl.program_id(1)

    @pl.when(ci == 0)
    def _():
        c_s[...] = c0_ref[...]
        n_s[...] = n0_ref[...]
        m_s[...] = m0_ref[...]

    N = bt * c
    gif = gif_ref[...].reshape(N, LANES)
    tok = lax.broadcasted_iota(jnp.int32, (N, 1), 0) % c
    valid = tok < l_valid
    ig_all = jnp.where(valid, gif + bi_ref[...], -jnp.inf)
    lf_all = jnp.where(valid, _log_sigmoid(gif + bf_ref[...]), 0.0)
    bcum_all = _dot(_seq_prefix_mask(N, c).astype(F32), lf_all, HI)
    ig_t = ig_all.T
    bcum_t = bcum_all.T
    ri = lax.broadcasted_iota(jnp.int32, (c, c), 0)
    causal = lax.broadcasted_iota(jnp.int32, (c, c), 1) <= ri
    st = []
    for b in range(bt):
        rs = slice(b * c, (b + 1) * c)
        for h in range(ML_HEADS):
            qs = slice(h * ML_DQK, (h + 1) * ML_DQK)
            ks = slice(ML_QK_WIDTH + h * ML_DQK, ML_QK_WIDTH + (h + 1) * ML_DQK)
            vs = slice(2 * ML_QK_WIDTH + h * ML_DV, 2 * ML_QK_WIDTH + (h + 1) * ML_DV)
            os_ = slice(2 * ML_QK_WIDTH + ML_V_WIDTH + h * ML_DV, 2 * ML_QK_WIDTH + ML_V_WIDTH + (h + 1) * ML_DV)
            i_c = ig_all[rs, h:h + 1]
            b_c = bcum_all[rs, ML_HEADS + h:ML_HEADS + h + 1]
            i_r = ig_t[h:h + 1, rs]
            b_r = bcum_t[ML_HEADS + h:ML_HEADS + h + 1, rs]
            m_prev = m_s[b, h]
            dlog = jnp.where(causal, b_c - b_r + i_r, -jnp.inf)
            inter = b_c + m_prev
            m_t = jnp.maximum(inter, jnp.max(dlog, axis=-1, keepdims=True))
            b_last = b_c[c - 1:c, :]
            s_log = b_last - b_c + i_c
            m_new = jnp.maximum(b_last + m_prev, jnp.max(s_log, axis=0, keepdims=True))
            q = u_ref[b, :, qs]
            k = u_ref[b, :, ks] * (ML_DQK ** -0.5)
            st.append(dict(b=b, h=h, hs=slice(h * ML_DV, (h + 1) * ML_DV), os=os_, q=q, k=k, qb=_bf(q), vb=_bf(u_ref[b, :, vs]),
                           dlog=dlog, m_t=m_t, w_prev=jnp.exp(inter - m_t), m_new=m_new,
                           kw=k * jnp.exp(s_log - m_new), wp=jnp.exp(b_last + m_prev - m_new)))
    for s in st:
        s["wts"] = jnp.exp(s["dlog"] - s["m_t"]) * _dot_nt(s["qb"], _bf(s["k"]))
    for s in st:
        b, h = s["b"], s["h"]
        c_prev = c_s[b, h]
        n_prev = n_s[b, h]
        num = s["w_prev"] * _dot(s["qb"], _bf(c_prev)) + _dot(_bf(s["wts"]), s["vb"])
        den = (s["w_prev"] * jnp.sum(s["q"] * n_prev, axis=-1, keepdims=True)
               + jnp.sum(s["wts"], axis=-1, keepdims=True))
        s["hh"] = num / jnp.maximum(jnp.abs(den), jnp.exp(-s["m_t"]))
        c_s[b, h] = s["wp"] * c_prev + _dot_tn(_bf(s["kw"]), s["vb"])
        n_s[b, h] = s["wp"] * n_prev + jnp.sum(s["kw"], axis=0, keepdims=True)
        m_s[b, h] = s["m_new"]
    for s in st:
        hh = s["hh"]
        hn = hh * lax.rsqrt(jnp.mean(hh * hh, axis=-1, keepdims=True) + RMS_EPS) * gnw_ref[:, s["hs"]]
        o_ref[s["b"], :, s["hs"]] =(hn * jax.nn.sigmoid(u_ref[s["b"], :, s["os"]])).astype(o_ref.dtype)

    @pl.when(ci == pl.num_programs(1) - 1)
    def _():
        cN_ref[...] = c_s[...]
        nN_ref[...] = n_s[...]
        mN_ref[...] = m_s[...]


def mlstm_mix(u_ml, u_pl, c0, n0, m0, l, l_out, c_carry, b_i, b_f, gn_w, *, B, L, l_valid):
    c = min(L, CHUNK_ROWS)
    bt = CHUNK_ROWS // c
    u3 = u_ml.reshape(B, L, ML_MAIN)
    g3 = u_pl.reshape(B, L, POOL_WIDTH + LANES)
    zeros = jnp.zeros((LANES - 2 * ML_HEADS,), F32)
    bi = jnp.concatenate([b_i, jnp.zeros((ML_HEADS,), F32), zeros]).reshape(1, LANES)
    bf = jnp.concatenate([jnp.zeros((ML_HEADS,), F32), b_f, zeros]).reshape(1, LANES)
    nl = n0.shape[0]
    c_blk, n_blk, m_blk = (bt, ML_HEADS, ML_DQK, ML_DV), (bt, ML_HEADS, 1, ML_DQK), (bt, ML_HEADS, 1, 1)
    at_b = lambda b, i: (b, 0, 0, 0)
    in_specs = [pl.BlockSpec((bt, c, ML_MAIN), lambda b, i: (b, i, 0)),
                pl.BlockSpec((bt, c, LANES), lambda b, i: (b, i, POOL_WIDTH // LANES)),
                _layer_block(l, c_blk, at_b), _layer_block(l, n_blk, at_b), _layer_block(l, m_blk, at_b),
                pl.BlockSpec((1, LANES), lambda b, i: (0, 0)),
                pl.BlockSpec((1, LANES), lambda b, i: (0, 0)),
                pl.BlockSpec((1, ML_V_WIDTH), lambda b, i: (0, 0))]
    args = [u3, g3, c0, n0.reshape(nl, B, ML_HEADS, 1, ML_DQK), m0.reshape(nl, B, ML_HEADS, 1, 1), bi, bf,
            gn_w.reshape(1, ML_V_WIDTH)]
    aliases = {}
    if c_carry is not None:
        in_specs.append(pl.BlockSpec(memory_space=pl.ANY))
        args.append(c_carry)
        aliases = {len(args) - 1: 1}
    out, cN, nN, mN = pl.pallas_call(
        functools.partial(_mlstm_kernel, c=c, bt=bt, l_valid=l_valid, has_carry=c_carry is not None),
        grid=(B // bt, L // c),
        in_specs=in_specs,
        out_specs=[pl.BlockSpec((bt, c, ML_V_WIDTH), lambda b, i: (b, i, 0)),
                   _layer_block(l_out, c_blk, at_b),
                   pl.BlockSpec(n_blk, at_b), pl.BlockSpec(m_blk, at_b)],
        out_shape=[jax.ShapeDtypeStruct((B, L, ML_V_WIDTH), BF16),
                   jax.ShapeDtypeStruct((DEPTH, B, ML_HEADS, ML_DQK, ML_DV), F32),
                   jax.ShapeDtypeStruct((B, ML_HEADS, 1, ML_DQK), F32),
                   jax.ShapeDtypeStruct((B, ML_HEADS, 1, 1), F32)],
        scratch_shapes=[pltpu.VMEM(c_blk, F32), pltpu.VMEM(n_blk, F32), pltpu.VMEM(m_blk, F32)],
        input_output_aliases=aliases,
        compiler_params=_cparams(2),
        name="mlstm_mix",
    )(*args)
    return (out.reshape(B * L, ML_V_WIDTH), cN, nN.reshape(B, ML_HEADS, ML_DQK), mN.reshape(B, ML_HEADS))


RW_CHAIN_GROUP = 16

def _rwkv_chain_group(chains, seq, s_s, o_ref, rk_ref, gnw_ref, gnb_ref, *, C):
    rowi = lax.broadcasted_iota(jnp.int32, (C, C), 0)
    coli = lax.broadcasted_iota(jnp.int32, (C, C), 1)
    upper = rowi < coli
    col2 = lax.broadcasted_iota(jnp.int32, (C, 2 * C), 1)
    incl2 = jnp.where(col2 >= C, col2 - C, col2) <= lax.broadcasted_iota(jnp.int32, (C, 2 * C), 0)
    n_sq = max(int(math.log2(C)), 1)
    st = []
    for b, h in chains:
        rs = slice(b * C, (b + 1) * C)
        sl = slice(h * RW_HEAD, (h + 1) * RW_HEAD)
        cut = lambda t, rs=rs, sl=sl: t[rs, sl]
        kk_h = cut(seq["kkv"])
        nrm = jnp.sqrt(jnp.sum(kk_h * kk_h, axis=-1, keepdims=True))
        kap = jnp.where(seq["valid"][rs], kk_h / jnp.maximum(nrm, 1e-12), 0.0)
        k_h, v_h, r_h = cut(seq["kmod"]), cut(seq["v"]), cut(seq["r"])
        gi = cut(seq["gi"])
        b_t = kap * cut(seq["a"]) * gi
        k_t = k_h * gi
        st.append(dict(b=b, h=h, sl=sl, k_h=k_h, v_h=v_h, r_h=r_h,
                       a_t=_bf(-kap * cut(seq["gp"])), r_t=_bf(r_h * cut(seq["g"])),
                       bk=jnp.concatenate([b_t, k_t], axis=0),
                       ge=seq["g"][(b + 1) * C - 1:(b + 1) * C, sl], s0=s_s[b, h]))
    for c in st:
        bkb = _bf(c["bk"])
        mt = _dot_nt(bkb, c["a_t"])
        c["pt"] = jnp.where(upper, mt[:C], 0.0)
        c["akt"] = _bf(jnp.where(upper, mt[C:], 0.0))
        c["a_r"] = jnp.where(incl2, _dot_nt(c["r_t"], bkb), 0.0)
        c["s0b"] = _bf(c["s0"])
        c["vb"] = _bf(c["v_h"])
    for c in st:
        c["xt"] = _dot_nt(c["s0b"], c["a_t"]) + _dot_tn(c["vb"], c["akt"])
    for _ in range(n_sq - 1):
        for c in st:
            z = _dot(_bf(jnp.concatenate([c["pt"], c["xt"]], axis=0)), _bf(c["pt"]))
            c["pt"] = z[:C]
            c["xt"] = c["xt"] + z[C:]
    for c in st:
        c["ut"] = _bf(c["xt"] + _dot(_bf(c["xt"]), _bf(c["pt"])))
    for c in st:
        a_r = c["a_r"]
        c["y"] = (_dot_nt(c["r_t"], c["s0b"]) + _dot_nt(_bf(a_r[:, :C]), c["ut"])
                  + _dot(_bf(a_r[:, C:]), c["vb"]))
        bkg = c["bk"] * c["ge"]
        s_s[c["b"], c["h"]] = (c["s0"] * c["ge"] + _dot(c["ut"], _bf(bkg[:C]))
                               + _dot_tn(c["vb"], _bf(bkg[C:])))
    for c in st:
        y, sl = c["y"], c["sl"]
        mean = jnp.mean(y, axis=-1, keepdims=True)
        yc = y - mean
        var = jnp.mean(yc * yc, axis=-1, keepdims=True)
        yn = yc * lax.rsqrt(var + GN_EPS) * gnw_ref[:, sl] + gnb_ref[:, sl]
        bonus = jnp.sum(c["r_h"] * c["k_h"] * rk_ref[:, sl], axis=-1, keepdims=True) * c["v_h"]
        o_ref[c["b"], :, sl] = (yn + bonus).astype(o_ref.dtype)


def _rwkv_kernel(*refs, C, bt, l_valid, has_carry):
    (u_ref, sh_ref, s0_ref, mu_ref, w0_ref, w2_ref, a0_ref, a2_ref, kk_ref, ka_ref, rk_ref, gnw_ref,
     gnb_ref) = refs[:13]
    o_ref, sN_ref, prev_s, s_s = refs[13 + has_carry:]
    ci = pl.program_id(1)

    @pl.when(ci == 0)
    def _():
        prev_s[...] = sh_ref[...]
        s_s[...] = s0_ref[...]

    N = bt * C
    u = u_ref[...].reshape(N, RW_COLS)
    tok = lax.broadcasted_iota(jnp.int32, (N, 1), 0) % C
    valid = tok < l_valid
    prev = jnp.concatenate([jnp.broadcast_to(prev_s[b], (C, RW_COLS)) for b in range(bt)], axis=0)
    u_prev = jnp.where(tok == 0, prev, pltpu.roll(u, 1, 0))
    for b in range(bt):
        prev_s[b] = u[(b + 1) * C - 1:(b + 1) * C, :]
    us = u + (u_prev - u) * mu_ref[...]
    W = RW_WIDTH
    k = us[:, W:2 * W]
    wd = us[:, 3 * W:3 * W + RW_RANK]
    ad = us[:, 3 * W + RW_RANK:3 * W + 2 * RW_RANK]
    xw = w0_ref[...] + _dot(_bf(jnp.tanh(wd)), _bf(w2_ref[...]))
    log_w = -_softplus(-xw) - 0.5
    ld = jnp.where(valid, -jnp.exp(log_w), 0.0)
    a = jax.nn.sigmoid(a0_ref[...] + _dot(_bf(ad), _bf(a2_ref[...])))
    cum = _dot(_seq_prefix_mask(N, C).astype(F32), ld, HI)
    seq = dict(valid=valid, r=us[:, 0:W], v=jnp.where(valid, us[:, 2 * W:3 * W], 0.0), a=a,
               g=jnp.exp(cum), gi=jnp.exp(-cum), gp=jnp.exp(cum - ld), kkv=k * kk_ref[...],
               kmod=jnp.where(valid, k * (1.0 + (a - 1.0) * ka_ref[...]), 0.0))
    chains = [(b, h) for b in range(bt) for h in range(RW_HEADS)]
    group = RW_CHAIN_GROUP * bt
    for i in range(0, len(chains), group):
        _rwkv_chain_group(chains[i:i + group], seq, s_s, o_ref, rk_ref, gnw_ref, gnb_ref, C=C)

    @pl.when(ci == pl.num_programs(1) - 1)
    def _():
        sN_ref[...] = s_s[...]


def rwkv7_mix(u_rw, shift_prev, s_prev, l, l_out, s_carry, mu, w0, w2, a0, a2, k_k, k_a, r_k, gn_w, gn_b, *, B, L,
              l_valid):
    C = min(L, CHUNK_ROWS)
    bt = CHUNK_ROWS // C
    u3 = u_rw.reshape(B, L, RW_COLS)
    vec = lambda n: pl.BlockSpec((1, n), lambda b, i: (0, 0))
    row = lambda t: t.reshape(1, -1)
    s_blk = (bt, RW_HEADS, RW_HEAD, RW_HEAD)
    at_b = lambda b, i: (b, 0, 0, 0)
    in_specs = [pl.BlockSpec((bt, C, RW_COLS), lambda b, i: (b, i, 0)),
                _layer_block(l, (bt, 1, RW_COLS), lambda b, i: (b, 0, 0)),
                _layer_block(l, s_blk, at_b),
                vec(RW_COLS), vec(RW_WIDTH),
                pl.BlockSpec((RW_RANK, RW_WIDTH), lambda b, i: (0, 0)),
                vec(RW_WIDTH),
                pl.BlockSpec((RW_RANK, RW_WIDTH), lambda b, i: (0, 0)),
                vec(RW_WIDTH), vec(RW_WIDTH), vec(RW_WIDTH), vec(RW_WIDTH), vec(RW_WIDTH)]
    args = [u3, shift_prev.reshape(shift_prev.shape[0], B, 1, RW_COLS), s_prev, row(mu), row(w0), w2, row(a0), a2,
            row(k_k), row(k_a), row(r_k), row(gn_w), row(gn_b)]
    aliases = {}
    if s_carry is not None:
        in_specs.append(pl.BlockSpec(memory_space=pl.ANY))
        args.append(s_carry)
        aliases = {len(args) - 1: 1}
    out, sN = pl.pallas_call(
        functools.partial(_rwkv_kernel, C=C, bt=bt, l_valid=l_valid, has_carry=s_carry is not None),
        grid=(B // bt, L // C),
        in_specs=in_specs,
        out_specs=[pl.BlockSpec((bt, C, RW_WIDTH), lambda b, i: (b, i, 0)), _layer_block(l_out, s_blk, at_b)],
        out_shape=[jax.ShapeDtypeStruct((B, L, RW_WIDTH), BF16),
                   jax.ShapeDtypeStruct((DEPTH, B, RW_HEADS, RW_HEAD, RW_HEAD), F32)],
        scratch_shapes=[pltpu.VMEM((bt, 1, RW_COLS), F32), pltpu.VMEM(s_blk, F32)],
        input_output_aliases=aliases,
        compiler_params=_cparams(2),
        name="rwkv7_mix",
    )(*args)
    return out.reshape(B * L, RW_WIDTH), sN


RWS_VECS = 5


def _rwkv_short_kernel(*refs, l_valid, has_carry):
    (ur_ref, uk_ref, uv_ref, ul_ref, shr_ref, shk_ref, shv_ref, shl_ref, mur_ref, muk_ref, muv_ref, mul_ref,
     w0_ref, w2_ref, a0_ref, a2_ref, kk_ref, ka_ref, rk_ref, gnw_ref, gnb_ref, s_ref) = refs[:22]
    o_ref, sN_ref, vec_s, val_s, y_s = refs[22 + has_carry:]
    H2 = LANES // RW_HEAD
    heads = [slice(h * RW_HEAD, (h + 1) * RW_HEAD) for h in range(H2)]
    o_ref[...] = jnp.zeros_like(o_ref)

    def shifted(u_ref, sh_ref, mu_ref, t):
        u = u_ref[:, t, :]
        prev = sh_ref[...] if t == 0 else u_ref[:, t - 1, :]
        return u + (prev - u) * mu_ref[...]

    rows = []
    for t in range(l_valid):
        r = shifted(ur_ref, shr_ref, mur_ref, t)
        k = shifted(uk_ref, shk_ref, muk_ref, t)
        v = shifted(uv_ref, shv_ref, muv_ref, t)
        lo = shifted(ul_ref, shl_ref, mul_ref, t)
        xw = w0_ref[...] + _dot(jnp.tanh(lo[:, :RW_RANK]), w2_ref[...], HI)
        w = jnp.exp(-jnp.exp(-_softplus(-xw) - 0.5))
        a = jax.nn.sigmoid(a0_ref[...] + _dot(lo[:, RW_RANK:], a2_ref[...], HI))
        kkv = k * kk_ref[...]
        kmod = k * (1.0 + (a - 1.0) * ka_ref[...])
        kap = jnp.concatenate(
            [kkv[:, sl] / jnp.maximum(jnp.sqrt(jnp.sum(kkv[:, sl] * kkv[:, sl], axis=-1, keepdims=True)), 1e-12)
             for sl in heads], axis=1)
        for j, x in enumerate((w, kap, kap * a, kmod, r)):
            vec_s[t, j] = x.T.reshape(H2, RW_HEAD, x.shape[0])
        val_s[t] = v.T.reshape(H2, RW_HEAD, v.shape[0])
        rows.append((r, kmod, v))

    for h in range(H2):
        def body(vi, carry, h=h):
            s = s_ref[h, vi]
            for t in range(l_valid):
                w, kap, bb, kk, rr = (vec_s[t, j, h] for j in range(RWS_VECS))
                sa = jnp.sum(s * kap, axis=0, keepdims=True)
                s = s * w - sa * bb + val_s[t, h, pl.ds(vi, 1), :] * kk
                y_s[t, h, pl.ds(vi, 1), :] = jnp.sum(s * rr, axis=0, keepdims=True)
            sN_ref[h, vi] = s
            return carry
        lax.fori_loop(0, RW_HEAD, body, 0)

    for t in range(l_valid):
        r, kmod, v = rows[t]
        y_all = y_s[t].reshape(LANES, y_s.shape[-1]).T
        outs = []
        for sl in heads:
            y = y_all[:, sl]
            mean = jnp.mean(y, axis=-1, keepdims=True)
            yc = y - mean
            var = jnp.mean(yc * yc, axis=-1, keepdims=True)
            yn = yc * lax.rsqrt(var + GN_EPS) * gnw_ref[:, sl] + gnb_ref[:, sl]
            bonus = jnp.sum(r[:, sl] * kmod[:, sl] * rk_ref[:, sl], axis=-1, keepdims=True) * v[:, sl]
            outs.append(yn + bonus)
        o_ref[:, t, :] = jnp.concatenate(outs, axis=1).astype(o_ref.dtype)


def rwkv7_short(u_rw, shift_prev, s_prev_t, l, l_out, s_carry, mu, w0, w2, a0, a2, k_k, k_a, r_k, gn_w, gn_b, *, B, L,
                l_valid):
    assert B == LANES, "the batch must fill the lane dimension"
    H2 = LANES // RW_HEAD
    nblk = RW_WIDTH // LANES
    u3 = u_rw.reshape(B, L, RW_COLS)
    seg = lambda off: pl.BlockSpec((B, L, LANES), lambda hp, off=off: (0, 0, off + hp))
    lora = pl.BlockSpec((B, L, LANES), lambda hp: (0, 0, 3 * nblk))
    sh = lambda off: pl.BlockSpec((None, B, LANES), lambda hp, off=off: (l, 0, off + hp))
    sh_lora = pl.BlockSpec((None, B, LANES), lambda hp: (l, 0, 3 * nblk))
    mus = lambda off: pl.BlockSpec((1, LANES), lambda hp, off=off: (0, off + hp))
    mu_lora = pl.BlockSpec((1, LANES), lambda hp: (0, 3 * nblk))
    vec = pl.BlockSpec((1, LANES), lambda hp: (0, hp))
    mat = pl.BlockSpec((RW_RANK, LANES), lambda hp: (0, hp))
    s_blk = (H2, RW_HEAD, RW_HEAD, B)
    in_specs = [seg(0), seg(nblk), seg(2 * nblk), lora, sh(0), sh(nblk), sh(2 * nblk), sh_lora,
                mus(0), mus(nblk), mus(2 * nblk), mu_lora,
                vec, mat, vec, mat, vec, vec, vec, vec, vec,
                pl.BlockSpec((None,) + s_blk, lambda hp: (l, hp, 0, 0, 0))]
    row = lambda t: t.reshape(1, -1)
    args = [u3, u3, u3, u3, shift_prev, shift_prev, shift_prev, shift_prev, row(mu), row(mu), row(mu), row(mu),
            row(w0), w2, row(a0), a2, row(k_k), row(k_a), row(r_k), row(gn_w), row(gn_b), s_prev_t]
    aliases = {}
    if s_carry is not None:
        in_specs.append(pl.BlockSpec(memory_space=pl.ANY))
        args.append(s_carry)
        aliases = {len(args) - 1: 1}
    out, sN = pl.pallas_call(
        functools.partial(_rwkv_short_kernel, l_valid=l_valid, has_carry=s_carry is not None),
        grid=(RW_HEADS // H2,),
        in_specs=in_specs,
        out_specs=[pl.BlockSpec((B, L, LANES), lambda hp: (0, 0, hp)),
                   pl.BlockSpec((None,) + s_blk, lambda hp: (l_out, hp, 0, 0, 0))],
        out_shape=[jax.ShapeDtypeStruct((B, L, RW_WIDTH), BF16),
                   jax.ShapeDtypeStruct((DEPTH, RW_HEADS, RW_HEAD, RW_HEAD, B), F32)],
        scratch_shapes=[pltpu.VMEM((l_valid, RWS_VECS, H2, RW_HEAD, B), F32),
                        pltpu.VMEM((l_valid, H2, RW_HEAD, B), F32),
                        pltpu.VMEM((l_valid, H2, RW_HEAD, B), F32)],
        input_output_aliases=aliases,
        compiler_params=_cparams(1),
        name="rwkv7_short",
    )(*args)
    return out.reshape(B * L, RW_WIDTH), sN


ATTN_SEQS_PER_STEP = 8


def _attn_kernel(q_ref, kv_ref, o_ref):
    heads = [slice(h * XA_HEAD_DIM, (h + 1) * XA_HEAD_DIM) for h in range(XA_HEADS)]
    s = [_dot_nt(q_ref[0, :, hs], _bf(kv_ref[0, :, hs])) * (XA_HEAD_DIM ** -0.5) for hs in heads]
    e = [jnp.exp(t - jnp.max(t, axis=-1, keepdims=True)) for t in s]
    p = [t / jnp.sum(t, axis=-1, keepdims=True) for t in e]
    for h, hs in enumerate(heads):
        vs = slice(XA_WIDTH + h * XA_HEAD_DIM, XA_WIDTH + (h + 1) * XA_HEAD_DIM)
        o_ref[0, :, hs] = _dot(_bf(p[h]), _bf(kv_ref[0, :, vs])).astype(o_ref.dtype)


def _attn_short_kernel(q_ref, k_ref, v_ref, o_ref, *, bt):
    L = q_ref.shape[1]
    rows, cols = XA_HEADS * L, MEM_LEN * XA_HEADS
    own = (lax.broadcasted_iota(jnp.int32, (rows, cols), 0) // L
           == lax.broadcasted_iota(jnp.int32, (rows, cols), 1) % XA_HEADS)
    heads = [slice(h * XA_HEAD_DIM, (h + 1) * XA_HEAD_DIM) for h in range(XA_HEADS)]
    qs = [jnp.concatenate([q_ref[b, :, hs] for hs in heads], axis=0) for b in range(bt)]
    s = [_dot_nt(qs[b], _bf(k_ref[b].reshape(cols, XA_HEAD_DIM))) * (XA_HEAD_DIM ** -0.5) for b in range(bt)]
    s = [jnp.where(own, t, -jnp.inf) for t in s]
    e = [jnp.exp(t - jnp.max(t, axis=-1, keepdims=True)) for t in s]
    p = [t / jnp.sum(t, axis=-1, keepdims=True) for t in e]
    for b in range(bt):
        out = _dot(_bf(p[b]), _bf(v_ref[b].reshape(cols, XA_HEAD_DIM)))
        for h, hs in enumerate(heads):
            o_ref[b, :, hs] = out[h * L:(h + 1) * L].astype(o_ref.dtype)


def cross_attention(q, mem, l, *, B, L):
    q3 = q.reshape(B, L, XA_WIDTH)
    if l is None:
        bt, tq, body, mems = 1, min(L, 512), _attn_kernel, [mem]
        mem_specs = [pl.BlockSpec((bt, MEM_LEN, 2 * XA_WIDTH), lambda b, i: (b, 0, 0))]
    else:
        bt, tq, mems = ATTN_SEQS_PER_STEP, L, list(mem)
        body = functools.partial(_attn_short_kernel, bt=bt)
        mem_specs = [pl.BlockSpec((None, bt, MEM_LEN, XA_HEADS, XA_HEAD_DIM), lambda b, i: (l, b, 0, 0, 0))] * 2
    qo_spec = pl.BlockSpec((bt, tq, XA_WIDTH), lambda b, i: (b, i, 0))
    out = pl.pallas_call(
        body,
        grid=(B // bt, L // tq),
        in_specs=[qo_spec] + mem_specs,
        out_specs=qo_spec,
        out_shape=jax.ShapeDtypeStruct((B, L, XA_WIDTH), BF16),
        compiler_params=_cparams(2),
        name="cross_attention",
    )(q3, *mems)
    return out.reshape(B * L, XA_WIDTH)


ROUTER_GROUP_LANE = MOE_EXPERTS


MOE_TB = 512
MOE_TB_GATHER = 1024
MOE_SUB = 256
MOE_TE_LONG = 1024
MOE_TE_SHORT = 512
ROUTE_GROUP_LANE = 0
ROUTE_RANK_LANE = 1
COMB_PIECES = 3


def _router_kernel(x_ref, g_ref, w_ref, b_ref, xn_ref, comb_ref, route_ref, cnt_s):
    @pl.when(pl.program_id(0) == 0)
    def _():
        cnt_s[...] = jnp.zeros_like(cnt_s)

    xn = _rms(x_ref[...], g_ref[...])
    xn_ref[...] = _bf(xn)
    z = _dot(xn, w_ref[...], HI) + b_ref[...]
    tm = z.shape[0]
    lane = lax.broadcasted_iota(jnp.int32, z.shape, 1).astype(F32)
    big = float(LANES)
    neg = -jnp.inf
    first = lambda mask: jnp.min(jnp.where(mask, lane, big), axis=-1, keepdims=True)
    is_g = jnp.logical_and(lane >= ROUTER_GROUP_LANE, lane < ROUTER_GROUP_LANE + MOE_GROUPS)
    zg = jnp.where(is_g, z, neg)
    mg = jnp.max(zg, axis=-1, keepdims=True)
    grp = first(zg == mg) - ROUTER_GROUP_LANE
    p_grp = 1.0 / jnp.sum(jnp.exp(zg - mg), axis=-1, keepdims=True)
    lo = grp * MOE_PER_GROUP
    ze = jnp.where(jnp.logical_and(lane >= lo, lane < lo + MOE_PER_GROUP), z, neg)
    t1 = jnp.max(ze, axis=-1, keepdims=True)
    i1 = first(ze == t1)
    ze2 = jnp.where(lane == i1, neg, ze)
    t2 = jnp.max(ze2, axis=-1, keepdims=True)
    i2 = first(ze2 == t2)
    e2 = jnp.exp(t2 - t1)
    g1 = p_grp / (1.0 + e2)
    comb = jnp.where(lane == i1, g1, 0.0) + jnp.where(lane == i2, g1 * e2, 0.0)
    comb_ref[...] = _split3(comb)
    onehot = jnp.where(lane == grp, 1.0, 0.0)
    r = lax.broadcasted_iota(jnp.int32, (tm, tm), 0)
    c = lax.broadcasted_iota(jnp.int32, (tm, tm), 1)
    before = _dot(_bf(jnp.where(c < r, 1.0, 0.0)), _bf(onehot)) + cnt_s[...]
    rank = jnp.sum(onehot * before, axis=-1, keepdims=True)
    cnt_s[...] = cnt_s[...] + jnp.sum(onehot, axis=0, keepdims=True)
    route_ref[...] = jnp.where(lane == ROUTE_GROUP_LANE, grp, 0.0) + jnp.where(lane == ROUTE_RANK_LANE, rank, 0.0)


def moe_router(x, g, w_r1, b_r1, w_r2, b_r2):
    T, D = x.shape
    tm = MOE_TB
    pad = LANES - MOE_EXPERTS - MOE_GROUPS
    w = jnp.concatenate([w_r2, w_r1, jnp.zeros((D, pad), F32)], axis=1)
    b = jnp.concatenate([b_r2, b_r1, jnp.zeros((pad,), F32)]).reshape(1, LANES)
    return pl.pallas_call(
        _router_kernel,
        grid=(T // tm,),
        in_specs=[pl.BlockSpec((tm, D), lambda i: (i, 0)),
                  pl.BlockSpec((1, D), lambda i: (0, 0)),
                  pl.BlockSpec((D, LANES), lambda i: (0, 0)),
                  pl.BlockSpec((1, LANES), lambda i: (0, 0))],
        out_specs=[pl.BlockSpec((tm, D), lambda i: (i, 0)),
                   pl.BlockSpec((tm, COMB_PIECES * LANES), lambda i: (i, 0)),
                   pl.BlockSpec((tm, LANES), lambda i: (i, 0))],
        out_shape=[jax.ShapeDtypeStruct((T, D), BF16), jax.ShapeDtypeStruct((T, COMB_PIECES * LANES), BF16),
                   jax.ShapeDtypeStruct((T, LANES), F32)],
        scratch_shapes=[pltpu.VMEM((1, LANES), F32)],
        compiler_params=_cparams(1),
        name="moe_router",
    )(x, g.reshape(1, D), w, b)


def _moe_plan(route, T, te):
    i32 = jnp.int32
    grp = route[:, ROUTE_GROUP_LANE].astype(i32)
    rank = route[:, ROUTE_RANK_LANE].astype(i32)
    rows = T + MOE_GROUPS * te
    n_sub = rows // MOE_SUB
    n_tiles = rows // te
    onehot = (grp[:, None] == jnp.arange(MOE_GROUPS, dtype=i32)[None]).astype(i32)
    seg_rows = (onehot.sum(axis=0) + te - 1) // te * te
    seg_end = jnp.cumsum(seg_rows)
    seg_start = seg_end - seg_rows
    dest = seg_start[grp] + rank
    group_of = lambda row: jnp.minimum(jnp.sum(row[:, None] >= seg_end[None], axis=1), MOE_GROUPS - 1).astype(i32)
    sub_row = jnp.arange(n_sub, dtype=i32) * MOE_SUB
    sub_g = group_of(sub_row)
    sub_valid = sub_row < seg_end[-1]
    r0 = sub_row - seg_start[sub_g]

    def overlap(tb):
        blk_cnt = onehot.reshape(T // tb, tb, MOE_GROUPS).sum(axis=1)
        cum_blk = jnp.concatenate([jnp.zeros((1, MOE_GROUPS), i32), jnp.cumsum(blk_cnt, axis=0)])
        lo = cum_blk[:-1][:, sub_g].T
        hi = cum_blk[1:][:, sub_g].T
        return (lo < (r0 + MOE_SUB)[:, None]) & (hi > r0[:, None]) & sub_valid[:, None]

    g_mask = overlap(MOE_TB_GATHER)
    first_col = (jnp.arange(T // MOE_TB_GATHER) == 0)[None]
    g_mask = g_mask | (first_col & ~g_mask.any(axis=1, keepdims=True))

    def items(mask, ncol, nblk):
        n_items = n_sub + MOE_GROUPS * nblk
        flat = jnp.nonzero(mask.reshape(-1), size=n_items, fill_value=-1)[0].astype(i32)
        valid = flat >= 0
        flat = jnp.where(valid, flat, jnp.max(flat))
        major, minor = flat // ncol, flat % ncol
        prev = jnp.concatenate([jnp.full((1,), -1, i32), major[:-1]])
        nxt = jnp.concatenate([major[1:], jnp.full((1,), -1, i32)])
        nvalid = jnp.concatenate([valid[1:], jnp.zeros((1,), bool)])
        first = (major != prev) & valid
        last = ((major != nxt) | ~nvalid) & valid
        return major, minor, first.astype(i32), last.astype(i32), valid.astype(i32)

    g_sub, g_blk, g_first, _, g_valid = items(g_mask, T // MOE_TB_GATHER, T // MOE_TB_GATHER)
    s_blk, s_sub, s_first, s_last, s_valid = items(overlap(MOE_TB).T, n_sub, T // MOE_TB)
    tile_row = jnp.arange(n_tiles, dtype=i32) * te
    tile_valid = tile_row < seg_end[-1]
    tile_group = group_of(jnp.where(tile_valid, tile_row, seg_end[-1] - 1))
    return dict(dest=dest, rows=rows, gather=(g_sub, g_blk, g_first, g_valid),
                scatter=(s_sub, s_blk, s_first, s_last, s_valid),
                tile_group=tile_group, tile_valid=tile_valid.astype(i32))


def _moe_gather_kernel(sub_ref, blk_ref, first_ref, valid_ref, dest_ref, xn_ref, comb_ref, xs_ref, cs_ref):
    w = pl.program_id(0)

    @pl.when(valid_ref[w] == 1)
    def _():
        rows = sub_ref[w] * MOE_SUB + lax.broadcasted_iota(jnp.int32, (MOE_SUB, MOE_TB_GATHER), 0)
        hit = _bf(jnp.where(dest_ref[0] == rows, 1.0, 0.0))
        gx = _bf(_dot(hit, xn_ref[...]))
        g3 = _dot(hit, comb_ref[...])
        gc = g3[:, :LANES] + g3[:, LANES:2 * LANES] + g3[:, 2 * LANES:]

        @pl.when(first_ref[w] == 1)
        def _():
            xs_ref[...] = gx
            cs_ref[...] = gc

        @pl.when(first_ref[w] == 0)
        def _():
            xs_ref[...] = xs_ref[...] + gx
            cs_ref[...] = cs_ref[...] + gc


def moe_gather(xn, comb, plan):
    T, D = xn.shape
    g_sub, g_blk, g_first, g_valid = plan["gather"]
    rows = plan["rows"]
    tb = MOE_TB_GATHER
    dest3 = plan["dest"].reshape(T // tb, 1, tb)
    grid_spec = pltpu.PrefetchScalarGridSpec(
        num_scalar_prefetch=4,
        grid=(g_sub.shape[0],),
        in_specs=[pl.BlockSpec((1, 1, tb), lambda w, s, b, f, v: (b[w], 0, 0)),
                  pl.BlockSpec((tb, D), lambda w, s, b, f, v: (b[w], 0)),
                  pl.BlockSpec((tb, COMB_PIECES * LANES), lambda w, s, b, f, v: (b[w], 0))],
        out_specs=[pl.BlockSpec((MOE_SUB, D), lambda w, s, b, f, v: (s[w], 0)),
                   pl.BlockSpec((MOE_SUB, LANES), lambda w, s, b, f, v: (s[w], 0))])
    return pl.pallas_call(
        _moe_gather_kernel,
        grid_spec=grid_spec,
        out_shape=[jax.ShapeDtypeStruct((rows, D), BF16), jax.ShapeDtypeStruct((rows, LANES), F32)],
        compiler_params=_cparams(1),
        name="moe_gather",
    )(g_sub, g_blk, g_first, g_valid, dest3, xn, comb)


MOE_EXPERTS_PER_STEP = 2


def _moe_group_experts_kernel(tg_ref, tv_ref, xs_ref, cs_ref, wg_ref, wu_ref, wd_ref, y_ref, acc):
    i = pl.program_id(0)
    e = pl.program_id(1)

    @pl.when(e == 0)
    def _():
        acc[...] = jnp.zeros_like(acc)

    @pl.when(tv_ref[i] == 1)
    def _():
        xs = xs_ref[...]
        cs = cs_ref[...]
        lane = lax.broadcasted_iota(jnp.int32, cs.shape, 1)
        first = tg_ref[i] * MOE_PER_GROUP + e * MOE_EXPERTS_PER_STEP
        hids = []
        for j in range(MOE_EXPERTS_PER_STEP):
            ce = jnp.sum(jnp.where(lane == first + j, cs, 0.0), axis=-1, keepdims=True)
            hids.append(_bf(jax.nn.silu(_dot(xs, _bf(wg_ref[j]))) * _dot(xs, _bf(wu_ref[j])) * ce))
        hid = jnp.concatenate(hids, axis=1)
        acc[...] += _dot(hid, _bf(wd_ref[...].reshape(MOE_EXPERTS_PER_STEP * MOE_HIDDEN, wd_ref.shape[-1])))

    @pl.when(e == pl.num_programs(1) - 1)
    def _():
        y_ref[...] = _bf(acc[...])


def moe_group_experts(xs, cs, plan, wg, wu, wd, l, te):
    rows, D = xs.shape
    eps = MOE_EXPERTS_PER_STEP
    steps = MOE_PER_GROUP // eps

    def expert(i, e, tg, tv):
        return (l, tg[i] * steps + jnp.where(tv[i] == 1, e, steps - 1), 0, 0)

    w_in = pl.BlockSpec((None, eps, D, MOE_HIDDEN), expert)
    grid_spec = pltpu.PrefetchScalarGridSpec(
        num_scalar_prefetch=2,
        grid=(rows // te, steps),
        in_specs=[pl.BlockSpec((te, D), lambda i, e, tg, tv: (i, 0)),
                  pl.BlockSpec((te, LANES), lambda i, e, tg, tv: (i, 0)),
                  w_in, w_in,
                  pl.BlockSpec((None, eps, MOE_HIDDEN, D), expert)],
        out_specs=pl.BlockSpec((te, D), lambda i, e, tg, tv: (i, 0)),
        scratch_shapes=[pltpu.VMEM((te, D), F32)])
    return pl.pallas_call(
        _moe_group_experts_kernel,
        grid_spec=grid_spec,
        out_shape=jax.ShapeDtypeStruct((rows, D), BF16),
        compiler_params=_cparams(2),
        name="moe_group_experts",
    )(plan["tile_group"], plan["tile_valid"], xs, cs, wg, wu, wd)


def _moe_scatter_kernel(sub_ref, blk_ref, first_ref, last_ref, valid_ref, dest_ref, y_ref, x_ref, gf_ref, o_ref,
                        *, final_norm):
    w = pl.program_id(0)

    @pl.when(valid_ref[w] == 1)
    def _():
        cols = sub_ref[w] * MOE_SUB + lax.broadcasted_iota(jnp.int32, (MOE_TB, MOE_SUB), 1)
        pick = _bf(jnp.where(dest_ref[...] == cols, 1.0, 0.0))
        upd = _dot(pick, y_ref[...])

        @pl.when(first_ref[w] == 1)
        def _():
            o_ref[...] = x_ref[...] + upd

        @pl.when(first_ref[w] == 0)
        def _():
            o_ref[...] += upd

        if final_norm:
            @pl.when(last_ref[w] == 1)
            def _():
                o_ref[...] = _rms(o_ref[...], gf_ref[...])


def moe_scatter(ys, x, plan, g_final, *, final_norm):
    T, D = x.shape
    s_sub, s_blk, s_first, s_last, s_valid = plan["scatter"]
    dest_col = plan["dest"].reshape(T, 1)
    im = lambda f: (lambda w, s, b, fi, la, v: f(w, s, b))
    grid_spec = pltpu.PrefetchScalarGridSpec(
        num_scalar_prefetch=5,
        grid=(s_sub.shape[0],),
        in_specs=[pl.BlockSpec((MOE_TB, 1), im(lambda w, s, b: (b[w], 0))),
                  pl.BlockSpec((MOE_SUB, D), im(lambda w, s, b: (s[w], 0))),
                  pl.BlockSpec((MOE_TB, D), im(lambda w, s, b: (b[w], 0))),
                  pl.BlockSpec((1, D), im(lambda w, s, b: (0, 0)))],
        out_specs=pl.BlockSpec((MOE_TB, D), im(lambda w, s, b: (b[w], 0))))
    return pl.pallas_call(
        functools.partial(_moe_scatter_kernel, final_norm=final_norm),
        grid_spec=grid_spec,
        out_shape=jax.ShapeDtypeStruct((T, D), F32),
        compiler_params=_cparams(1),
        name="moe_scatter",
    )(s_sub, s_blk, s_first, s_last, s_valid, dest_col, ys, x, g_final.reshape(1, D))


def hmoe_block(x, p, l, *, final_norm):
    T = x.shape[0]
    te = MOE_TE_LONG if T >= 4 * MOE_TE_LONG else MOE_TE_SHORT
    xn, comb, route = moe_router(x, p["g_moe"][l], p["moe_wr1"][l], p["moe_br1"][l], p["moe_wr2"][l],
                                 p["moe_br2"][l])
    plan = _moe_plan(route, T, te)
    xs, cs = moe_gather(xn, comb, plan)
    ys = moe_group_experts(xs, cs, plan, p["moe_wg"], p["moe_wu"], p["moe_wd"], l, te)
    return moe_scatter(ys, x, plan, p["g_final"], final_norm=final_norm)


def _layer_weights(l, p):
    w_in = p["w_in"][l]
    c0 = RW_COLS
    c1 = c0 + ML_MAIN
    c2 = c1 + 2 * ML_HEADS
    c3 = c2 + POOL_WIDTH
    w_pl = jnp.concatenate([w_in[:, c2:c3], w_in[:, c1:c2], jnp.zeros((D_MODEL, LANES - 2 * ML_HEADS), F32)], axis=1)
    return dict(
        w_rw=_bf(w_in[:, :c0]), w_ml=_bf(w_in[:, c0:c1]), w_pl=_bf(w_pl), w_gate=_bf(w_in[:, c3:]),
        w_up_rwkv=_bf(p["w_up_rwkv"][l]), w_up_mlstm=_bf(p["w_up_mlstm"][l]), w_up_pool=_bf(p["w_up_pool"][l]),
        w_out=_bf(p["w_out"][l]), pool_w=_bf(p["pool_w"][l]),
        xa_wq=_bf(p["xa_wq"][l]), xa_wo=_bf(p["xa_wo"][l]),
        xa_wkv=_bf(jnp.concatenate([p["xa_wk"][l], p["xa_wv"][l]], axis=1)),
    )


def _trunk_layer(x, mem, mem_l, st, carry, *, l, sl, p, lw, B, L, l_valid, start, final_norm):
    g_mix = p["g_mix"][l]
    u_rw = norm_matmul(x, g_mix, lw["w_rw"], tn=TN_RWKV, tm=TM_WHOLE_SEGMENT, name="in_rwkv")
    u_ml = norm_matmul(x, g_mix, lw["w_ml"], tn=TN_MLSTM, tm=TM_WHOLE_SEGMENT, name="in_mlstm")
    u_pl = norm_matmul(x, g_mix, lw["w_pl"], tn=POOL_WIDTH + LANES, name="in_pool")
    gates = norm_matmul(x, g_mix, lw["w_gate"], tn=TN_GATES, out_dtype=BF16, act="sigmoid", name="in_gates")

    rwkv = rwkv7_short if L < CHUNK_ROWS else rwkv7_mix
    o_rw, rw_s = rwkv(u_rw, st["rw_shift"], st["rw_s"], sl, l, carry["rw_s"], p["rw_mu"][l], p["rw_w0"][l],
                      p["rw_w2"][l], p["rw_a0"][l], p["rw_a2"][l], p["rw_k_k"][l], p["rw_k_a"][l],
                      p["rw_r_k"][l], p["rw_gn_w"][l], p["rw_gn_b"][l], B=B, L=L, l_valid=l_valid)
    rw_shift = u_rw.reshape(B, L, RW_COLS)[:, l_valid - 1]
    o_ml, ml_c, ml_n, ml_m = mlstm_mix(u_ml, u_pl, st["ml_c"], st["ml_n"], st["ml_m"], sl, l, carry["ml_c"],
                                       p["ml_b_i"][l], p["ml_b_f"][l], p["ml_gn_w"][l], B=B, L=L, l_valid=l_valid)
    o_pl, pool_buf = pool_mix(u_pl, st["pool"], sl, lw["pool_w"], p["pool_scale"][l], B=B, L=L, l_valid=l_valid,
                              start=start)
    merged = merge_branches(o_rw, o_ml, o_pl, gates, lw["w_up_rwkv"], lw["w_up_mlstm"], lw["w_up_pool"])
    x = matmul_residual(merged, lw["w_out"], x)

    q = norm_matmul(x, p["g_xa"][l], lw["xa_wq"], tn=XA_WIDTH, out_dtype=BF16, name="xa_q")
    att = cross_attention(q, mem, mem_l, B=B, L=L)
    x = matmul_residual(att, lw["xa_wo"], x)

    x = hmoe_block(x, p, l, final_norm=final_norm)
    return x, dict(rw_s=rw_s, ml_c=ml_c), (rw_shift, ml_n, ml_m, pool_buf)


def kernel(x_prompt, x_sample, cache_mem_k, cache_mem_v, state_rwkv_s, state_rwkv_shift, state_mlstm_c, state_mlstm_n, state_mlstm_m, state_pool, mem_prompt, g_mix, w_in, rw_mu, rw_w0, rw_w2, rw_a0, rw_a2, rw_k_k, rw_k_a, rw_r_k, rw_gn_w, rw_gn_b, ml_b_i, ml_b_f, ml_gn_w, pool_w, pool_scale, w_up_rwkv, w_up_mlstm, w_up_pool, w_out, g_xa, g_mem, xa_wq, xa_wk, xa_wv, xa_wo, g_moe, moe_wr1, moe_br1, moe_wr2, moe_br2, moe_wg, moe_wu, moe_wd, g_final):
    p = dict(g_mix=g_mix, w_in=w_in, rw_mu=rw_mu, rw_w0=rw_w0, rw_w2=rw_w2, rw_a0=rw_a0, rw_a2=rw_a2, rw_k_k=rw_k_k,
             rw_k_a=rw_k_a, rw_r_k=rw_r_k, rw_gn_w=rw_gn_w, rw_gn_b=rw_gn_b, ml_b_i=ml_b_i, ml_b_f=ml_b_f,
             ml_gn_w=ml_gn_w, pool_w=pool_w, pool_scale=pool_scale, w_up_rwkv=w_up_rwkv, w_up_mlstm=w_up_mlstm,
             w_up_pool=w_up_pool, w_out=w_out, g_xa=g_xa, g_mem=g_mem, xa_wq=xa_wq, xa_wk=xa_wk, xa_wv=xa_wv,
             xa_wo=xa_wo, g_moe=g_moe, moe_wr1=moe_wr1, moe_br1=moe_br1, moe_wr2=moe_wr2, moe_br2=moe_br2,
             moe_wg=moe_wg, moe_wu=moe_wu, moe_wd=moe_wd, g_final=g_final)
    Bp, Lp, D = x_prompt.shape
    Bs, Ls, _ = x_sample.shape
    Ls_pad = -(-Ls // SUBLANES) * SUBLANES
    yp = x_prompt.reshape(Bp * Lp, D)
    ys = jnp.pad(x_sample, ((0, 0), (0, Ls_pad - Ls), (0, 0))).reshape(Bs * Ls_pad, D)
    zeros = lambda *s: jnp.zeros((1,) + s, F32)
    st_p = dict(rw_shift=zeros(Bp, RW_COLS), rw_s=zeros(Bp, RW_HEADS, RW_HEAD, RW_HEAD),
                ml_c=zeros(Bp, ML_HEADS, ML_DQK, ML_DV), ml_n=zeros(Bp, ML_HEADS, ML_DQK), ml_m=zeros(Bp, ML_HEADS),
                pool=zeros(Bp, POOL_BUF, POOL_WIDTH))
    batch_minor = (0, 2, 3, 4, 1)
    st_s = dict(rw_shift=state_rwkv_shift, rw_s=jnp.transpose(state_rwkv_s, batch_minor), ml_c=state_mlstm_c,
                ml_n=state_mlstm_n, ml_m=state_mlstm_m, pool=state_pool)
    carry_p = carry_s = dict(rw_s=None, ml_c=None)
    small_p = [[] for _ in range(6)]
    small_s = [[] for _ in range(4)]
    for l in range(DEPTH):
        lw = _layer_weights(l, p)
        final = l == DEPTH - 1
        kv = norm_matmul(mem_prompt.reshape(Bp * MEM_LEN, D), g_mem[l], lw["xa_wkv"], tn=2 * XA_WIDTH,
                         name="memory_kv")
        kv3 = kv.reshape(Bp, MEM_LEN, 2 * XA_WIDTH)
        yp, carry_p, small = _trunk_layer(yp, kv3, None, st_p, carry_p, l=l, sl=0, p=p, lw=lw, B=Bp, L=Lp,
                                          l_valid=Lp, start=0, final_norm=final)
        mk = kv[:, :XA_WIDTH].reshape(Bp, MEM_LEN, XA_HEADS, XA_HEAD_DIM)
        mv = kv[:, XA_WIDTH:].reshape(Bp, MEM_LEN, XA_HEADS, XA_HEAD_DIM)
        for acc, t in zip(small_p, small + (mk, mv)):
            acc.append(t)
        ys, carry_s, small = _trunk_layer(ys, (cache_mem_k, cache_mem_v), l, st_s, carry_s, l=l, sl=l, p=p, lw=lw,
                                          B=Bs, L=Ls_pad, l_valid=Ls, start=PAST_LEN, final_norm=final)
        for acc, t in zip(small_s, small):
            acc.append(t)
    y_prompt = yp.reshape(Bp, Lp, D)
    y_sample = ys.reshape(Bs, Ls_pad, D)[:, :Ls]
    p_sh, p_n, p_m, p_pool, p_mk, p_mv = [jnp.stack(t) for t in small_p]
    s_sh, s_n, s_m, s_pool = [jnp.stack(t) for t in small_s]
    return (y_prompt, y_sample, carry_p["rw_s"], p_sh, carry_p["ml_c"], p_n, p_m, p_pool, p_mk, p_mv,
            jnp.transpose(carry_s["rw_s"], (0, 4, 1, 2, 3)), s_sh, carry_s["ml_c"], s_n, s_m, s_pool)
```

```python
import functools
import math

import jax
import jax.numpy as jnp
from jax import lax
from jax.experimental import pallas as pl
from jax.experimental.pallas import tpu as pltpu

F32 = jnp.float32
BF16 = jnp.bfloat16
HI = lax.Precision.HIGHEST

D_MODEL = 2048
DEPTH = 2
PAST_LEN = 16384
RW_HEAD = 64
RW_WIDTH = 1024
RW_HEADS = 16
RW_RANK = 64
RW_COLS = 3 * RW_WIDTH + 2 * RW_RANK
GN_EPS = 64e-5
ML_HEADS = 4
ML_DQK = 128
ML_DV = 256
ML_QK_WIDTH = 512
ML_V_WIDTH = 1024
ML_MAIN = 2 * ML_QK_WIDTH + 2 * ML_V_WIDTH
POOL_WIDTH = 1024
POOL_WINDOWS = (2, 4, 8, 16)
POOL_GW = 256
POOL_BUF = 15
POOL_HIST = 16
POOL_TAIL = 24
MEM_LEN = 256
XA_HEADS = 4
XA_HEAD_DIM = 128
XA_WIDTH = 512
MOE_GROUPS = 4
MOE_PER_GROUP = 8
MOE_EXPERTS = 32
MOE_HIDDEN = 256
RMS_EPS = 1e-6
LANES = 128
SUBLANES = 8
VMEM_LIMIT = 56 * 1024 * 1024
ROW_TILE = 1024
CHUNK_ROWS = 64
TN_RWKV = RW_COLS
TN_MLSTM = ML_MAIN
TN_GATES = 3 * D_MODEL // 2
TM_WHOLE_SEGMENT = ROW_TILE // 2


def _cparams(n_axes):
    return pltpu.CompilerParams(dimension_semantics=("arbitrary",) * n_axes, vmem_limit_bytes=VMEM_LIMIT)


def _dot(a, b, precision=None):
    return jnp.dot(a, b, preferred_element_type=F32, precision=precision)


def _dot_nt(a, b, precision=None):
    return lax.dot_general(a, b, (((1,), (1,)), ((), ())), preferred_element_type=F32, precision=precision)


def _dot_tn(a, b, precision=None):
    return lax.dot_general(a, b, (((0,), (0,)), ((), ())), preferred_element_type=F32, precision=precision)


def _bf(x):
    return x.astype(BF16)


def _split3(x):
    hi = _bf(x)
    rest = x - hi.astype(F32)
    mid = _bf(rest)
    return jnp.concatenate([hi, mid, _bf(rest - mid.astype(F32))], axis=1)


def _rms(x, g):
    return x * lax.rsqrt(jnp.mean(x * x, axis=-1, keepdims=True) + RMS_EPS) * g


def _log_sigmoid(x):
    return jnp.minimum(x, 0.0) - jnp.log1p(jnp.exp(-jnp.abs(x)))


def _softplus(x):
    return jnp.maximum(x, 0.0) + jnp.log1p(jnp.exp(-jnp.abs(x)))


def _seq_prefix_mask(n, c):
    r = lax.broadcasted_iota(jnp.int32, (n, n), 0)
    q = lax.broadcasted_iota(jnp.int32, (n, n), 1)
    return jnp.logical_and(q <= r, q // c == r // c)


def _layer_block(l, blk, idx):
    return pl.BlockSpec((None,) + tuple(blk), lambda b, i: (l,) + tuple(idx(b, i)))


def _norm_matmul_kernel(x_ref, g_ref, w_ref, o_ref, xn_ref, *, act):
    @pl.when(pl.program_id(1) == 0)
    def _():
        xn_ref[...] = _bf(_rms(x_ref[...], g_ref[...]))

    acc = _dot(xn_ref[...], w_ref[...])
    if act == "sigmoid":
        acc = jax.nn.sigmoid(acc)
    o_ref[...] = acc.astype(o_ref.dtype)


def norm_matmul(x, g, w, *, tn, name, tm=ROW_TILE, out_dtype=F32, act=None):
    T, D = x.shape
    N = w.shape[1]
    tm = min(T, tm)
    return pl.pallas_call(
        functools.partial(_norm_matmul_kernel, act=act),
        grid=(T // tm, N // tn),
        in_specs=[pl.BlockSpec((tm, D), lambda i, j: (i, 0)),
                  pl.BlockSpec((1, D), lambda i, j: (0, 0)),
                  pl.BlockSpec((D, tn), lambda i, j: (0, j))],
        out_specs=pl.BlockSpec((tm, tn), lambda i, j: (i, j)),
        out_shape=jax.ShapeDtypeStruct((T, N), out_dtype),
        scratch_shapes=[pltpu.VMEM((tm, D), BF16)],
        compiler_params=_cparams(2),
        name=name,
    )(x, g.reshape(1, D), w)


def _matmul_residual_kernel(a_ref, w_ref, r_ref, o_ref):
    o_ref[...] = r_ref[...] + _dot(a_ref[...], w_ref[...])


def matmul_residual(a, w, res):
    T, K = a.shape
    tn = w.shape[1]
    tm = min(T, TM_WHOLE_SEGMENT)
    return pl.pallas_call(
        _matmul_residual_kernel,
        grid=(T // tm, 1),
        in_specs=[pl.BlockSpec((tm, K), lambda i, j: (i, 0)),
                  pl.BlockSpec((K, tn), lambda i, j: (0, j)),
                  pl.BlockSpec((tm, tn), lambda i, j: (i, j))],
        out_specs=pl.BlockSpec((tm, tn), lambda i, j: (i, j)),
        out_shape=jax.ShapeDtypeStruct((T, tn), F32),
        compiler_params=_cparams(2),
        name="matmul_residual",
    )(a, w, res)


def _merge_kernel(orw_ref, oml_ref, opl_ref, g0_ref, g1_ref, g2_ref, wr_ref, wm_ref, wp_ref, o_ref):
    m = (g0_ref[...].astype(F32) * _dot(orw_ref[...], wr_ref[...])
         + g1_ref[...].astype(F32) * _dot(oml_ref[...], wm_ref[...])
         + g2_ref[...].astype(F32) * _dot(opl_ref[...], wp_ref[...]))
    o_ref[...] = _bf(m)


def merge_branches(o_rw, o_ml, o_pl, gates, w_rw, w_ml, w_pl, *, tn=1024):
    T, K = o_rw.shape
    tm = min(T, ROW_TILE)
    nj = D_MODEL // tn
    act = pl.BlockSpec((tm, K), lambda i, j: (i, 0))
    wsp = pl.BlockSpec((K, tn), lambda i, j: (0, j))
    gate = lambda b: pl.BlockSpec((tm, tn), lambda i, j, b=b: (i, b * nj + j))
    return pl.pallas_call(
        _merge_kernel,
        grid=(T // tm, nj),
        in_specs=[act, act, act, gate(0), gate(1), gate(2), wsp, wsp, wsp],
        out_specs=pl.BlockSpec((tm, tn), lambda i, j: (i, j)),
        out_shape=jax.ShapeDtypeStruct((T, D_MODEL), BF16),
        compiler_params=_cparams(2),
        name="merge_branches",
    )(o_rw, o_ml, o_pl, gates, gates, gates, w_rw, w_ml, w_pl)


def _pool_kernel(hist_ref, u_ref, w_ref, sc_ref, o_ref, tail_ref, e_ref, *, tl, bt, start):
    li = pl.program_id(1)

    @pl.when(li == 0)
    def _():
        e_ref[:, 0:POOL_HIST, :] = hist_ref[...]

    pos = start + li * tl + lax.broadcasted_iota(jnp.int32, (tl, 1), 0)
    for b in range(bt):
        u = u_ref[b]
        e_ref[b, POOL_HIST:, :] = u
        e = e_ref[b]
        s2 = e + pltpu.roll(e, 1, 0)
        s4 = s2[:, POOL_GW:] + pltpu.roll(s2[:, POOL_GW:], 2, 0)
        s8 = s4[:, POOL_GW:] + pltpu.roll(s4[:, POOL_GW:], 4, 0)
        s16 = s8[:, POOL_GW:] + pltpu.roll(s8[:, POOL_GW:], 8, 0)
        sums = (s2[:, :POOL_GW], s4[:, :POOL_GW], s8[:, :POOL_GW], s16)
        for g, win in enumerate(POOL_WINDOWS):
            cols = slice(g * POOL_GW, (g + 1) * POOL_GW)
            cnt = jnp.minimum(win, pos + 1).astype(F32)
            d = sums[g][POOL_HIST:, :] / cnt - u[:, cols]
            out = _dot(_bf(d), w_ref[g]) * sc_ref[:, cols]
            o_ref[b, :, cols] = out.astype(o_ref.dtype)
        tail_ref[b] = e[tl + POOL_HIST - POOL_TAIL:, :]
        e_ref[b, 0:POOL_HIST, :] = e[tl:, :]


def pool_mix(u_pl, buf, l, w_grp, scale, *, B, L, l_valid, start):
    tl = min(L, 256)
    bt = max(CHUNK_ROWS // L, 1)
    hist = jnp.pad(buf[l], ((0, 0), (POOL_HIST - POOL_BUF, 0), (0, 0)))
    u3 = u_pl.reshape(B, L, u_pl.shape[-1])
    out, tail = pl.pallas_call(
        functools.partial(_pool_kernel, tl=tl, bt=bt, start=start),
        grid=(B // bt, L // tl),
        in_specs=[pl.BlockSpec((bt, POOL_HIST, POOL_WIDTH), lambda b, i: (b, 0, 0)),
                  pl.BlockSpec((bt, tl, POOL_WIDTH), lambda b, i: (b, i, 0)),
                  pl.BlockSpec((4, POOL_GW, POOL_GW), lambda b, i: (0, 0, 0)),
                  pl.BlockSpec((1, POOL_WIDTH), lambda b, i: (0, 0))],
        out_specs=[pl.BlockSpec((bt, tl, POOL_WIDTH), lambda b, i: (b, i, 0)),
                   pl.BlockSpec((bt, POOL_TAIL, POOL_WIDTH), lambda b, i: (b, 0, 0))],
        out_shape=[jax.ShapeDtypeStruct((B, L, POOL_WIDTH), BF16),
                   jax.ShapeDtypeStruct((B, POOL_TAIL, POOL_WIDTH), F32)],
        scratch_shapes=[pltpu.VMEM((bt, tl + POOL_HIST, POOL_WIDTH), F32)],
        compiler_params=_cparams(2),
        name="pool_mix",
    )(hist, u3, w_grp, scale.reshape(1, POOL_WIDTH))
    pad = L - l_valid
    new_buf = tail[:, POOL_TAIL - pad - POOL_BUF:POOL_TAIL - pad, :]
    return out.reshape(B * L, POOL_WIDTH), new_buf


def _mlstm_kernel(*refs, c, bt, l_valid, has_carry):
    u_ref, gif_ref, c0_ref, n0_ref, m0_ref, bi_ref, bf_ref, gnw_ref = refs[:8]
    o_ref, cN_ref, nN_ref, mN_ref, c_s, n_s, m_s = refs[8 + has_carry:]
    ci = pl.program_id(1)

    @pl.when(ci == 0)
    def _():
        c_s[...] = c0_ref[...]
        n_s[...] = n0_ref[...]
        m_s[...] = m0_ref[...]

    N = bt * c
    gif = gif_ref[...].reshape(N, LANES)
    tok = lax.broadcasted_iota(jnp.int32, (N, 1), 0) % c
    valid = tok < l_valid
    ig_all = jnp.where(valid, gif + bi_ref[...], -jnp.inf)
    lf_all = jnp.where(valid, _log_sigmoid(gif + bf_ref[...]), 0.0)
    bcum_all = _dot(_seq_prefix_mask(N, c).astype(F32), lf_all, HI)
    ig_t = ig_all.T
    bcum_t = bcum_all.T
    ri = lax.broadcasted_iota(jnp.int32, (c, c), 0)
    causal = lax.broadcasted_iota(jnp.int32, (c, c), 1) <= ri
    st = []
    for b in range(bt):
        rs = slice(b * c, (b + 1) * c)
        for h in range(ML_HEADS):
            qs = slice(h * ML_DQK, (h + 1) * ML_DQK)
            ks = slice(ML_QK_WIDTH + h * ML_DQK, ML_QK_WIDTH + (h + 1) * ML_DQK)
            vs = slice(2 * ML_QK_WIDTH + h * ML_DV, 2 * ML_QK_WIDTH + (h + 1) * ML_DV)
            os_ = slice(2 * ML_QK_WIDTH + ML_V_WIDTH + h * ML_DV, 2 * ML_QK_WIDTH + ML_V_WIDTH + (h + 1) * ML_DV)
            i_c = ig_all[rs, h:h + 1]
            b_c = bcum_all[rs, ML_HEADS + h:ML_HEADS + h + 1]
            i_r = ig_t[h:h + 1, rs]
            b_r = bcum_t[ML_HEADS + h:ML_HEADS + h + 1, rs]
            m_prev = m_s[b, h]
            dlog = jnp.where(causal, b_c - b_r + i_r, -jnp.inf)
            inter = b_c + m_prev
            m_t = jnp.maximum(inter, jnp.max(dlog, axis=-1, keepdims=True))
            b_last = b_c[c - 1:c, :]
            s_log = b_last - b_c + i_c
            m_new = jnp.maximum(b_last + m_prev, jnp.max(s_log, axis=0, keepdims=True))
            q = u_ref[b, :, qs]
            k = u_ref[b, :, ks] * (ML_DQK ** -0.5)
            st.append(dict(b=b, h=h, hs=slice(h * ML_DV, (h + 1) * ML_DV), os=os_, q=q, k=k, qb=_bf(q), vb=_bf(u_ref[b, :, vs]),
                           dlog=dlog, m_t=m_t, w_prev=jnp.exp(inter - m_t), m_new=m_new,
                           kw=k * jnp.exp(s_log - m_new), wp=jnp.exp(b_last + m_prev - m_new)))
    for s in st:
        s["wts"] = jnp.exp(s["dlog"] - s["m_t"]) * _dot_nt(s["qb"], _bf(s["k"]))
    for s in st:
        b, h = s["b"], s["h"]
        c_prev = c_s[b, h]
        n_prev = n_s[b, h]
        num = s["w_prev"] * _dot(s["qb"], _bf(c_prev)) + _dot(_bf(s["wts"]), s["vb"])
        den = (s["w_prev"] * jnp.sum(s["q"] * n_prev, axis=-1, keepdims=True)
               + jnp.sum(s["wts"], axis=-1, keepdims=True))
        s["hh"] = num / jnp.maximum(jnp.abs(den), jnp.exp(-s["m_t"]))
        c_s[b, h] = s["wp"] * c_prev + _dot_tn(_bf(s["kw"]), s["vb"])
        n_s[b, h] = s["wp"] * n_prev + jnp.sum(s["kw"], axis=0, keepdims=True)
        m_s[b, h] = s["m_new"]
    for s in st:
        hh = s["hh"]
        hn = hh * lax.rsqrt(jnp.mean(hh * hh, axis=-1, keepdims=True) + RMS_EPS) * gnw_ref[:, s["hs"]]
        o_ref[s["b"], :, s["hs"]] =(hn * jax.nn.sigmoid(u_ref[s["b"], :, s["os"]])).astype(o_ref.dtype)

    @pl.when(ci == pl.num_programs(1) - 1)
    def _():
        cN_ref[...] = c_s[...]
        nN_ref[...] = n_s[...]
        mN_ref[...] = m_s[...]


def mlstm_mix(u_ml, u_pl, c0, n0, m0, l, l_out, c_carry, b_i, b_f, gn_w, *, B, L, l_valid):
    c = min(L, CHUNK_ROWS)
    bt = CHUNK_ROWS // c
    u3 = u_ml.reshape(B, L, ML_MAIN)
    g3 = u_pl.reshape(B, L, POOL_WIDTH + LANES)
    zeros = jnp.zeros((LANES - 2 * ML_HEADS,), F32)
    bi = jnp.concatenate([b_i, jnp.zeros((ML_HEADS,), F32), zeros]).reshape(1, LANES)
    bf = jnp.concatenate([jnp.zeros((ML_HEADS,), F32), b_f, zeros]).reshape(1, LANES)
    nl = n0.shape[0]
    c_blk, n_blk, m_blk = (bt, ML_HEADS, ML_DQK, ML_DV), (bt, ML_HEADS, 1, ML_DQK), (bt, ML_HEADS, 1, 1)
    at_b = lambda b, i: (b, 0, 0, 0)
    in_specs = [pl.BlockSpec((bt, c, ML_MAIN), lambda b, i: (b, i, 0)),
                pl.BlockSpec((bt, c, LANES), lambda b, i: (b, i, POOL_WIDTH // LANES)),
                _layer_block(l, c_blk, at_b), _layer_block(l, n_blk, at_b), _layer_block(l, m_blk, at_b),
                pl.BlockSpec((1, LANES), lambda b, i: (0, 0)),
                pl.BlockSpec((1, LANES), lambda b, i: (0, 0)),
                pl.BlockSpec((1, ML_V_WIDTH), lambda b, i: (0, 0))]
    args = [u3, g3, c0, n0.reshape(nl, B, ML_HEADS, 1, ML_DQK), m0.reshape(nl, B, ML_HEADS, 1, 1), bi, bf,
            gn_w.reshape(1, ML_V_WIDTH)]
    aliases = {}
    if c_carry is not None:
        in_specs.append(pl.BlockSpec(memory_space=pl.ANY))
        args.append(c_carry)
        aliases = {len(args) - 1: 1}
    out, cN, nN, mN = pl.pallas_call(
        functools.partial(_mlstm_kernel, c=c, bt=bt, l_valid=l_valid, has_carry=c_carry is not None),
        grid=(B // bt, L // c),
        in_specs=in_specs,
        out_specs=[pl.BlockSpec((bt, c, ML_V_WIDTH), lambda b, i: (b, i, 0)),
                   _layer_block(l_out, c_blk, at_b),
                   pl.BlockSpec(n_blk, at_b), pl.BlockSpec(m_blk, at_b)],
        out_shape=[jax.ShapeDtypeStruct((B, L, ML_V_WIDTH), BF16),
                   jax.ShapeDtypeStruct((DEPTH, B, ML_HEADS, ML_DQK, ML_DV), F32),
                   jax.ShapeDtypeStruct((B, ML_HEADS, 1, ML_DQK), F32),
                   jax.ShapeDtypeStruct((B, ML_HEADS, 1, 1), F32)],
        scratch_shapes=[pltpu.VMEM(c_blk, F32), pltpu.VMEM(n_blk, F32), pltpu.VMEM(m_blk, F32)],
        input_output_aliases=aliases,
        compiler_params=_cparams(2),
        name="mlstm_mix",
    )(*args)
    return (out.reshape(B * L, ML_V_WIDTH), cN, nN.reshape(B, ML_HEADS, ML_DQK), mN.reshape(B, ML_HEADS))


RW_CHAIN_GROUP = 16

def _rwkv_chain_group(chains, seq, s_s, o_ref, rk_ref, gnw_ref, gnb_ref, *, C):
    rowi = lax.broadcasted_iota(jnp.int32, (C, C), 0)
    coli = lax.broadcasted_iota(jnp.int32, (C, C), 1)
    upper = rowi < coli
    col2 = lax.broadcasted_iota(jnp.int32, (C, 2 * C), 1)
    incl2 = jnp.where(col2 >= C, col2 - C, col2) <= lax.broadcasted_iota(jnp.int32, (C, 2 * C), 0)
    n_sq = max(int(math.log2(C)), 1)
    st = []
    for b, h in chains:
        rs = slice(b * C, (b + 1) * C)
        sl = slice(h * RW_HEAD, (h + 1) * RW_HEAD)
        cut = lambda t, rs=rs, sl=sl: t[rs, sl]
        kk_h = cut(seq["kkv"])
        nrm = jnp.sqrt(jnp.sum(kk_h * kk_h, axis=-1, keepdims=True))
        kap = jnp.where(seq["valid"][rs], kk_h / jnp.maximum(nrm, 1e-12), 0.0)
        k_h, v_h, r_h = cut(seq["kmod"]), cut(seq["v"]), cut(seq["r"])
        gi = cut(seq["gi"])
        b_t = kap * cut(seq["a"]) * gi
        k_t = k_h * gi
        st.append(dict(b=b, h=h, sl=sl, k_h=k_h, v_h=v_h, r_h=r_h,
                       a_t=_bf(-kap * cut(seq["gp"])), r_t=_bf(r_h * cut(seq["g"])),
                       bk=jnp.concatenate([b_t, k_t], axis=0),
                       ge=seq["g"][(b + 1) * C - 1:(b + 1) * C, sl], s0=s_s[b, h]))
    for c in st:
        bkb = _bf(c["bk"])
        mt = _dot_nt(bkb, c["a_t"])
        c["pt"] = jnp.where(upper, mt[:C], 0.0)
        c["akt"] = _bf(jnp.where(upper, mt[C:], 0.0))
        c["a_r"] = jnp.where(incl2, _dot_nt(c["r_t"], bkb), 0.0)
        c["s0b"] = _bf(c["s0"])
        c["vb"] = _bf(c["v_h"])
    for c in st:
        c["xt"] = _dot_nt(c["s0b"], c["a_t"]) + _dot_tn(c["vb"], c["akt"])
    for _ in range(n_sq - 1):
        for c in st:
            z = _dot(_bf(jnp.concatenate([c["pt"], c["xt"]], axis=0)), _bf(c["pt"]))
            c["pt"] = z[:C]
            c["xt"] = c["xt"] + z[C:]
    for c in st:
        c["ut"] = _bf(c["xt"] + _dot(_bf(c["xt"]), _bf(c["pt"])))
    for c in st:
        a_r = c["a_r"]
        c["y"] = (_dot_nt(c["r_t"], c["s0b"]) + _dot_nt(_bf(a_r[:, :C]), c["ut"])
                  + _dot(_bf(a_r[:, C:]), c["vb"]))
        bkg = c["bk"] * c["ge"]
        s_s[c["b"], c["h"]] = (c["s0"] * c["ge"] + _dot(c["ut"], _bf(bkg[:C]))
                               + _dot_tn(c["vb"], _bf(bkg[C:])))
    for c in st:
        y, sl = c["y"], c["sl"]
        mean = jnp.mean(y, axis=-1, keepdims=True)
        yc = y - mean
        var = jnp.mean(yc * yc, axis=-1, keepdims=True)
        yn = yc * lax.rsqrt(var + GN_EPS) * gnw_ref[:, sl] + gnb_ref[:, sl]
        bonus = jnp.sum(c["r_h"] * c["k_h"] * rk_ref[:, sl], axis=-1, keepdims=True) * c["v_h"]
        o_ref[c["b"], :, sl] = (yn + bonus).astype(o_ref.dtype)


def _rwkv_kernel(*refs, C, bt, l_valid, has_carry):
    (u_ref, sh_ref, s0_ref, mu_ref, w0_ref, w2_ref, a0_ref, a2_ref, kk_ref, ka_ref, rk_ref, gnw_ref,
     gnb_ref) = refs[:13]
    o_ref, sN_ref, prev_s, s_s = refs[13 + has_carry:]
    ci = pl.program_id(1)

    @pl.when(ci == 0)
    def _():
        prev_s[...] = sh_ref[...]
        s_s[...] = s0_ref[...]

    N = bt * C
    u = u_ref[...].reshape(N, RW_COLS)
    tok = lax.broadcasted_iota(jnp.int32, (N, 1), 0) % C
    valid = tok < l_valid
    prev = jnp.concatenate([jnp.broadcast_to(prev_s[b], (C, RW_COLS)) for b in range(bt)], axis=0)
    u_prev = jnp.where(tok == 0, prev, pltpu.roll(u, 1, 0))
    for b in range(bt):
        prev_s[b] = u[(b + 1) * C - 1:(b + 1) * C, :]
    us = u + (u_prev - u) * mu_ref[...]
    W = RW_WIDTH
    k = us[:, W:2 * W]
    wd = us[:, 3 * W:3 * W + RW_RANK]
    ad = us[:, 3 * W + RW_RANK:3 * W + 2 * RW_RANK]
    xw = w0_ref[...] + _dot(_bf(jnp.tanh(wd)), _bf(w2_ref[...]))
    log_w = -_softplus(-xw) - 0.5
    ld = jnp.where(valid, -jnp.exp(log_w), 0.0)
    a = jax.nn.sigmoid(a0_ref[...] + _dot(_bf(ad), _bf(a2_ref[...])))
    cum = _dot(_seq_prefix_mask(N, C).astype(F32), ld, HI)
    seq = dict(valid=valid, r=us[:, 0:W], v=jnp.where(valid, us[:, 2 * W:3 * W], 0.0), a=a,
               g=jnp.exp(cum), gi=jnp.exp(-cum), gp=jnp.exp(cum - ld), kkv=k * kk_ref[...],
               kmod=jnp.where(valid, k * (1.0 + (a - 1.0) * ka_ref[...]), 0.0))
    chains = [(b, h) for b in range(bt) for h in range(RW_HEADS)]
    group = RW_CHAIN_GROUP * bt
    for i in range(0, len(chains), group):
        _rwkv_chain_group(chains[i:i + group], seq, s_s, o_ref, rk_ref, gnw_ref, gnb_ref, C=C)

    @pl.when(ci == pl.num_programs(1) - 1)
    def _():
        sN_ref[...] = s_s[...]


def rwkv7_mix(u_rw, shift_prev, s_prev, l, l_out, s_carry, mu, w0, w2, a0, a2, k_k, k_a, r_k, gn_w, gn_b, *, B, L,
              l_valid):
    C = min(L, CHUNK_ROWS)
    bt = CHUNK_ROWS // C
    u3 = u_rw.reshape(B, L, RW_COLS)
    vec = lambda n: pl.BlockSpec((1, n), lambda b, i: (0, 0))
    row = lambda t: t.reshape(1, -1)
    s_blk = (bt, RW_HEADS, RW_HEAD, RW_HEAD)
    at_b = lambda b, i: (b, 0, 0, 0)
    in_specs = [pl.BlockSpec((bt, C, RW_COLS), lambda b, i: (b, i, 0)),
                _layer_block(l, (bt, 1, RW_COLS), lambda b, i: (b, 0, 0)),
                _layer_block(l, s_blk, at_b),
                vec(RW_COLS), vec(RW_WIDTH),
                pl.BlockSpec((RW_RANK, RW_WIDTH), lambda b, i: (0, 0)),
                vec(RW_WIDTH),
                pl.BlockSpec((RW_RANK, RW_WIDTH), lambda b, i: (0, 0)),
                vec(RW_WIDTH), vec(RW_WIDTH), vec(RW_WIDTH), vec(RW_WIDTH), vec(RW_WIDTH)]
    args = [u3, shift_prev.reshape(shift_prev.shape[0], B, 1, RW_COLS), s_prev, row(mu), row(w0), w2, row(a0), a2,
            row(k_k), row(k_a), row(r_k), row(gn_w), row(gn_b)]
    aliases = {}
    if s_carry is not None:
        in_specs.append(pl.BlockSpec(memory_space=pl.ANY))
        args.append(s_carry)
        aliases = {len(args) - 1: 1}
    out, sN = pl.pallas_call(
        functools.partial(_rwkv_kernel, C=C, bt=bt, l_valid=l_valid, has_carry=s_carry is not None),
        grid=(B // bt, L // C),
        in_specs=in_specs,
        out_specs=[pl.BlockSpec((bt, C, RW_WIDTH), lambda b, i: (b, i, 0)), _layer_block(l_out, s_blk, at_b)],
        out_shape=[jax.ShapeDtypeStruct((B, L, RW_WIDTH), BF16),
                   jax.ShapeDtypeStruct((DEPTH, B, RW_HEADS, RW_HEAD, RW_HEAD), F32)],
        scratch_shapes=[pltpu.VMEM((bt, 1, RW_COLS), F32), pltpu.VMEM(s_blk, F32)],
        input_output_aliases=aliases,
        compiler_params=_cparams(2),
        name="rwkv7_mix",
    )(*args)
    return out.reshape(B * L, RW_WIDTH), sN


RWS_VECS = 5


def _rwkv_short_kernel(*refs, l_valid, has_carry):
    (ur_ref, uk_ref, uv_ref, ul_ref, shr_ref, shk_ref, shv_ref, shl_ref, mur_ref, muk_ref, muv_ref, mul_ref,
     w0_ref, w2_ref, a0_ref, a2_ref, kk_ref, ka_ref, rk_ref, gnw_ref, gnb_ref, s_ref) = refs[:22]
    o_ref, sN_ref, vec_s, val_s, y_s = refs[22 + has_carry:]
    H2 = LANES // RW_HEAD
    heads = [slice(h * RW_HEAD, (h + 1) * RW_HEAD) for h in range(H2)]
    o_ref[...] = jnp.zeros_like(o_ref)

    def shifted(u_ref, sh_ref, mu_ref, t):
        u = u_ref[:, t, :]
        prev = sh_ref[...] if t == 0 else u_ref[:, t - 1, :]
        return u + (prev - u) * mu_ref[...]

    rows = []
    for t in range(l_valid):
        r = shifted(ur_ref, shr_ref, mur_ref, t)
        k = shifted(uk_ref, shk_ref, muk_ref, t)
        v = shifted(uv_ref, shv_ref, muv_ref, t)
        lo = shifted(ul_ref, shl_ref, mul_ref, t)
        xw = w0_ref[...] + _dot(jnp.tanh(lo[:, :RW_RANK]), w2_ref[...], HI)
        w = jnp.exp(-jnp.exp(-_softplus(-xw) - 0.5))
        a = jax.nn.sigmoid(a0_ref[...] + _dot(lo[:, RW_RANK:], a2_ref[...], HI))
        kkv = k * kk_ref[...]
        kmod = k * (1.0 + (a - 1.0) * ka_ref[...])
        kap = jnp.concatenate(
            [kkv[:, sl] / jnp.maximum(jnp.sqrt(jnp.sum(kkv[:, sl] * kkv[:, sl], axis=-1, keepdims=True)), 1e-12)
             for sl in heads], axis=1)
        for j, x in enumerate((w, kap, kap * a, kmod, r)):
            vec_s[t, j] = x.T.reshape(H2, RW_HEAD, x.shape[0])
        val_s[t] = v.T.reshape(H2, RW_HEAD, v.shape[0])
        rows.append((r, kmod, v))

    for h in range(H2):
        def body(vi, carry, h=h):
            s = s_ref[h, vi]
            for t in range(l_valid):
                w, kap, bb, kk, rr = (vec_s[t, j, h] for j in range(RWS_VECS))
                sa = jnp.sum(s * kap, axis=0, keepdims=True)
                s = s * w - sa * bb + val_s[t, h, pl.ds(vi, 1), :] * kk
                y_s[t, h, pl.ds(vi, 1), :] = jnp.sum(s * rr, axis=0, keepdims=True)
            sN_ref[h, vi] = s
            return carry
        lax.fori_loop(0, RW_HEAD, body, 0)

    for t in range(l_valid):
        r, kmod, v = rows[t]
        y_all = y_s[t].reshape(LANES, y_s.shape[-1]).T
        outs = []
        for sl in heads:
            y = y_all[:, sl]
            mean = jnp.mean(y, axis=-1, keepdims=True)
            yc = y - mean
            var = jnp.mean(yc * yc, axis=-1, keepdims=True)
            yn = yc * lax.rsqrt(var + GN_EPS) * gnw_ref[:, sl] + gnb_ref[:, sl]
            bonus = jnp.sum(r[:, sl] * kmod[:, sl] * rk_ref[:, sl], axis=-1, keepdims=True) * v[:, sl]
            outs.append(yn + bonus)
        o_ref[:, t, :] = jnp.concatenate(outs, axis=1).astype(o_ref.dtype)


def rwkv7_short(u_rw, shift_prev, s_prev_t, l, l_out, s_carry, mu, w0, w2, a0, a2, k_k, k_a, r_k, gn_w, gn_b, *, B, L,
                l_valid):
    assert B == LANES, "the batch must fill the lane dimension"
    H2 = LANES // RW_HEAD
    nblk = RW_WIDTH // LANES
    u3 = u_rw.reshape(B, L, RW_COLS)
    seg = lambda off: pl.BlockSpec((B, L, LANES), lambda hp, off=off: (0, 0, off + hp))
    lora = pl.BlockSpec((B, L, LANES), lambda hp: (0, 0, 3 * nblk))
    sh = lambda off: pl.BlockSpec((None, B, LANES), lambda hp, off=off: (l, 0, off + hp))
    sh_lora = pl.BlockSpec((None, B, LANES), lambda hp: (l, 0, 3 * nblk))
    mus = lambda off: pl.BlockSpec((1, LANES), lambda hp, off=off: (0, off + hp))
    mu_lora = pl.BlockSpec((1, LANES), lambda hp: (0, 3 * nblk))
    vec = pl.BlockSpec((1, LANES), lambda hp: (0, hp))
    mat = pl.BlockSpec((RW_RANK, LANES), lambda hp: (0, hp))
    s_blk = (H2, RW_HEAD, RW_HEAD, B)
    in_specs = [seg(0), seg(nblk), seg(2 * nblk), lora, sh(0), sh(nblk), sh(2 * nblk), sh_lora,
                mus(0), mus(nblk), mus(2 * nblk), mu_lora,
                vec, mat, vec, mat, vec, vec, vec, vec, vec,
                pl.BlockSpec((None,) + s_blk, lambda hp: (l, hp, 0, 0, 0))]
    row = lambda t: t.reshape(1, -1)
    args = [u3, u3, u3, u3, shift_prev, shift_prev, shift_prev, shift_prev, row(mu), row(mu), row(mu), row(mu),
            row(w0), w2, row(a0), a2, row(k_k), row(k_a), row(r_k), row(gn_w), row(gn_b), s_prev_t]
    aliases = {}
    if s_carry is not None:
        in_specs.append(pl.BlockSpec(memory_space=pl.ANY))
        args.append(s_carry)
        aliases = {len(args) - 1: 1}
    out, sN = pl.pallas_call(
        functools.partial(_rwkv_short_kernel, l_valid=l_valid, has_carry=s_carry is not None),
        grid=(RW_HEADS // H2,),
        in_specs=in_specs,
        out_specs=[pl.BlockSpec((B, L, LANES), lambda hp: (0, 0, hp)),
                   pl.BlockSpec((None,) + s_blk, lambda hp: (l_out, hp, 0, 0, 0))],
        out_shape=[jax.ShapeDtypeStruct((B, L, RW_WIDTH), BF16),
                   jax.ShapeDtypeStruct((DEPTH, RW_HEADS, RW_HEAD, RW_HEAD, B), F32)],
        scratch_shapes=[pltpu.VMEM((l_valid, RWS_VECS, H2, RW_HEAD, B), F32),
                        pltpu.VMEM((l_valid, H2, RW_HEAD, B), F32),
                        pltpu.VMEM((l_valid, H2, RW_HEAD, B), F32)],
        input_output_aliases=aliases,
        compiler_params=_cparams(1),
        name="rwkv7_short",
    )(*args)
    return out.reshape(B * L, RW_WIDTH), sN


ATTN_SEQS_PER_STEP = 8


def _attn_kernel(q_ref, kv_ref, o_ref):
    heads = [slice(h * XA_HEAD_DIM, (h + 1) * XA_HEAD_DIM) for h in range(XA_HEADS)]
    s = [_dot_nt(q_ref[0, :, hs], _bf(kv_ref[0, :, hs])) * (XA_HEAD_DIM ** -0.5) for hs in heads]
    e = [jnp.exp(t - jnp.max(t, axis=-1, keepdims=True)) for t in s]
    p = [t / jnp.sum(t, axis=-1, keepdims=True) for t in e]
    for h, hs in enumerate(heads):
        vs = slice(XA_WIDTH + h * XA_HEAD_DIM, XA_WIDTH + (h + 1) * XA_HEAD_DIM)
        o_ref[0, :, hs] = _dot(_bf(p[h]), _bf(kv_ref[0, :, vs])).astype(o_ref.dtype)


def _attn_short_kernel(q_ref, k_ref, v_ref, o_ref, *, bt):
    L = q_ref.shape[1]
    rows, cols = XA_HEADS * L, MEM_LEN * XA_HEADS
    own = (lax.broadcasted_iota(jnp.int32, (rows, cols), 0) // L
           == lax.broadcasted_iota(jnp.int32, (rows, cols), 1) % XA_HEADS)
    heads = [slice(h * XA_HEAD_DIM, (h + 1) * XA_HEAD_DIM) for h in range(XA_HEADS)]
    qs = [jnp.concatenate([q_ref[b, :, hs] for hs in heads], axis=0) for b in range(bt)]
    s = [_dot_nt(qs[b], _bf(k_ref[b].reshape(cols, XA_HEAD_DIM))) * (XA_HEAD_DIM ** -0.5) for b in range(bt)]
    s = [jnp.where(own, t, -jnp.inf) for t in s]
    e = [jnp.exp(t - jnp.max(t, axis=-1, keepdims=True)) for t in s]
    p = [t / jnp.sum(t, axis=-1, keepdims=True) for t in e]
    for b in range(bt):
        out = _dot(_bf(p[b]), _bf(v_ref[b].reshape(cols, XA_HEAD_DIM)))
        for h, hs in enumerate(heads):
            o_ref[b, :, hs] = out[h * L:(h + 1) * L].astype(o_ref.dtype)


def cross_attention(q, mem, l, *, B, L):
    q3 = q.reshape(B, L, XA_WIDTH)
    if l is None:
        bt, tq, body, mems = 1, min(L, 512), _attn_kernel, [mem]
        mem_specs = [pl.BlockSpec((bt, MEM_LEN, 2 * XA_WIDTH), lambda b, i: (b, 0, 0))]
    else:
        bt, tq, mems = ATTN_SEQS_PER_STEP, L, list(mem)
        body = functools.partial(_attn_short_kernel, bt=bt)
        mem_specs = [pl.BlockSpec((None, bt, MEM_LEN, XA_HEADS, XA_HEAD_DIM), lambda b, i: (l, b, 0, 0, 0))] * 2
    qo_spec = pl.BlockSpec((bt, tq, XA_WIDTH), lambda b, i: (b, i, 0))
    out = pl.pallas_call(
        body,
        grid=(B // bt, L // tq),
        in_specs=[qo_spec] + mem_specs,
        out_specs=qo_spec,
        out_shape=jax.ShapeDtypeStruct((B, L, XA_WIDTH), BF16),
        compiler_params=_cparams(2),
        name="cross_attention",
    )(q3, *mems)
    return out.reshape(B * L, XA_WIDTH)


ROUTER_GROUP_LANE = MOE_EXPERTS


MOE_TB = 512
MOE_TB_GATHER = 1024
MOE_SUB = 256
MOE_TE_LONG = 1024
MOE_TE_SHORT = 512
ROUTE_GROUP_LANE = 0
ROUTE_RANK_LANE = 1
COMB_PIECES = 3


def _router_kernel(x_ref, g_ref, w_ref, b_ref, xn_ref, comb_ref, route_ref, cnt_s):
    @pl.when(pl.program_id(0) == 0)
    def _():
        cnt_s[...] = jnp.zeros_like(cnt_s)

    xn = _rms(x_ref[...], g_ref[...])
    xn_ref[...] = _bf(xn)
    z = _dot(xn, w_ref[...], HI) + b_ref[...]
    tm = z.shape[0]
    lane = lax.broadcasted_iota(jnp.int32, z.shape, 1).astype(F32)
    big = float(LANES)
    neg = -jnp.inf
    first = lambda mask: jnp.min(jnp.where(mask, lane, big), axis=-1, keepdims=True)
    is_g = jnp.logical_and(lane >= ROUTER_GROUP_LANE, lane < ROUTER_GROUP_LANE + MOE_GROUPS)
    zg = jnp.where(is_g, z, neg)
    mg = jnp.max(zg, axis=-1, keepdims=True)
    grp = first(zg == mg) - ROUTER_GROUP_LANE
    p_grp = 1.0 / jnp.sum(jnp.exp(zg - mg), axis=-1, keepdims=True)
    lo = grp * MOE_PER_GROUP
    ze = jnp.where(jnp.logical_and(lane >= lo, lane < lo + MOE_PER_GROUP), z, neg)
    t1 = jnp.max(ze, axis=-1, keepdims=True)
    i1 = first(ze == t1)
    ze2 = jnp.where(lane == i1, neg, ze)
    t2 = jnp.max(ze2, axis=-1, keepdims=True)
    i2 = first(ze2 == t2)
    e2 = jnp.exp(t2 - t1)
    g1 = p_grp / (1.0 + e2)
    comb = jnp.where(lane == i1, g1, 0.0) + jnp.where(lane == i2, g1 * e2, 0.0)
    comb_ref[...] = _split3(comb)
    onehot = jnp.where(lane == grp, 1.0, 0.0)
    r = lax.broadcasted_iota(jnp.int32, (tm, tm), 0)
    c = lax.broadcasted_iota(jnp.int32, (tm, tm), 1)
    before = _dot(_bf(jnp.where(c < r, 1.0, 0.0)), _bf(onehot)) + cnt_s[...]
    rank = jnp.sum(onehot * before, axis=-1, keepdims=True)
    cnt_s[...] = cnt_s[...] + jnp.sum(onehot, axis=0, keepdims=True)
    route_ref[...] = jnp.where(lane == ROUTE_GROUP_LANE, grp, 0.0) + jnp.where(lane == ROUTE_RANK_LANE, rank, 0.0)


def moe_router(x, g, w_r1, b_r1, w_r2, b_r2):
    T, D = x.shape
    tm = MOE_TB
    pad = LANES - MOE_EXPERTS - MOE_GROUPS
    w = jnp.concatenate([w_r2, w_r1, jnp.zeros((D, pad), F32)], axis=1)
    b = jnp.concatenate([b_r2, b_r1, jnp.zeros((pad,), F32)]).reshape(1, LANES)
    return pl.pallas_call(
        _router_kernel,
        grid=(T // tm,),
        in_specs=[pl.BlockSpec((tm, D), lambda i: (i, 0)),
                  pl.BlockSpec((1, D), lambda i: (0, 0)),
                  pl.BlockSpec((D, LANES), lambda i: (0, 0)),
                  pl.BlockSpec((1, LANES), lambda i: (0, 0))],
        out_specs=[pl.BlockSpec((tm, D), lambda i: (i, 0)),
                   pl.BlockSpec((tm, COMB_PIECES * LANES), lambda i: (i, 0)),
                   pl.BlockSpec((tm, LANES), lambda i: (i, 0))],
        out_shape=[jax.ShapeDtypeStruct((T, D), BF16), jax.ShapeDtypeStruct((T, COMB_PIECES * LANES), BF16),
                   jax.ShapeDtypeStruct((T, LANES), F32)],
        scratch_shapes=[pltpu.VMEM((1, LANES), F32)],
        compiler_params=_cparams(1),
        name="moe_router",
    )(x, g.reshape(1, D), w, b)


def _moe_plan(route, T, te):
    i32 = jnp.int32
    grp = route[:, ROUTE_GROUP_LANE].astype(i32)
    rank = route[:, ROUTE_RANK_LANE].astype(i32)
    rows = T + MOE_GROUPS * te
    n_sub = rows // MOE_SUB
    n_tiles = rows // te
    onehot = (grp[:, None] == jnp.arange(MOE_GROUPS, dtype=i32)[None]).astype(i32)
    seg_rows = (onehot.sum(axis=0) + te - 1) // te * te
    seg_end = jnp.cumsum(seg_rows)
    seg_start = seg_end - seg_rows
    dest = seg_start[grp] + rank
    group_of = lambda row: jnp.minimum(jnp.sum(row[:, None] >= seg_end[None], axis=1), MOE_GROUPS - 1).astype(i32)
    sub_row = jnp.arange(n_sub, dtype=i32) * MOE_SUB
    sub_g = group_of(sub_row)
    sub_valid = sub_row < seg_end[-1]
    r0 = sub_row - seg_start[sub_g]

    def overlap(tb):
        blk_cnt = onehot.reshape(T // tb, tb, MOE_GROUPS).sum(axis=1)
        cum_blk = jnp.concatenate([jnp.zeros((1, MOE_GROUPS), i32), jnp.cumsum(blk_cnt, axis=0)])
        lo = cum_blk[:-1][:, sub_g].T
        hi = cum_blk[1:][:, sub_g].T
        return (lo < (r0 + MOE_SUB)[:, None]) & (hi > r0[:, None]) & sub_valid[:, None]

    g_mask = overlap(MOE_TB_GATHER)
    first_col = (jnp.arange(T // MOE_TB_GATHER) == 0)[None]
    g_mask = g_mask | (first_col & ~g_mask.any(axis=1, keepdims=True))

    def items(mask, ncol, nblk):
        n_items = n_sub + MOE_GROUPS * nblk
        flat = jnp.nonzero(mask.reshape(-1), size=n_items, fill_value=-1)[0].astype(i32)
        valid = flat >= 0
        flat = jnp.where(valid, flat, jnp.max(flat))
        major, minor = flat // ncol, flat % ncol
        prev = jnp.concatenate([jnp.full((1,), -1, i32), major[:-1]])
        nxt = jnp.concatenate([major[1:], jnp.full((1,), -1, i32)])
        nvalid = jnp.concatenate([valid[1:], jnp.zeros((1,), bool)])
        first = (major != prev) & valid
        last = ((major != nxt) | ~nvalid) & valid
        return major, minor, first.astype(i32), last.astype(i32), valid.astype(i32)

    g_sub, g_blk, g_first, _, g_valid = items(g_mask, T // MOE_TB_GATHER, T // MOE_TB_GATHER)
    s_blk, s_sub, s_first, s_last, s_valid = items(overlap(MOE_TB).T, n_sub, T // MOE_TB)
    tile_row = jnp.arange(n_tiles, dtype=i32) * te
    tile_valid = tile_row < seg_end[-1]
    tile_group = group_of(jnp.where(tile_valid, tile_row, seg_end[-1] - 1))
    return dict(dest=dest, rows=rows, gather=(g_sub, g_blk, g_first, g_valid),
                scatter=(s_sub, s_blk, s_first, s_last, s_valid),
                tile_group=tile_group, tile_valid=tile_valid.astype(i32))


def _moe_gather_kernel(sub_ref, blk_ref, first_ref, valid_ref, dest_ref, xn_ref, comb_ref, xs_ref, cs_ref):
    w = pl.program_id(0)

    @pl.when(valid_ref[w] == 1)
    def _():
        rows = sub_ref[w] * MOE_SUB + lax.broadcasted_iota(jnp.int32, (MOE_SUB, MOE_TB_GATHER), 0)
        hit = _bf(jnp.where(dest_ref[0] == rows, 1.0, 0.0))
        gx = _bf(_dot(hit, xn_ref[...]))
        g3 = _dot(hit, comb_ref[...])
        gc = g3[:, :LANES] + g3[:, LANES:2 * LANES] + g3[:, 2 * LANES:]

        @pl.when(first_ref[w] == 1)
        def _():
            xs_ref[...] = gx
            cs_ref[...] = gc

        @pl.when(first_ref[w] == 0)
        def _():
            xs_ref[...] = xs_ref[...] + gx
            cs_ref[...] = cs_ref[...] + gc


def moe_gather(xn, comb, plan):
    T, D = xn.shape
    g_sub, g_blk, g_first, g_valid = plan["gather"]
    rows = plan["rows"]
    tb = MOE_TB_GATHER
    dest3 = plan["dest"].reshape(T // tb, 1, tb)
    grid_spec = pltpu.PrefetchScalarGridSpec(
        num_scalar_prefetch=4,
        grid=(g_sub.shape[0],),
        in_specs=[pl.BlockSpec((1, 1, tb), lambda w, s, b, f, v: (b[w], 0, 0)),
                  pl.BlockSpec((tb, D), lambda w, s, b, f, v: (b[w], 0)),
                  pl.BlockSpec((tb, COMB_PIECES * LANES), lambda w, s, b, f, v: (b[w], 0))],
        out_specs=[pl.BlockSpec((MOE_SUB, D), lambda w, s, b, f, v: (s[w], 0)),
                   pl.BlockSpec((MOE_SUB, LANES), lambda w, s, b, f, v: (s[w], 0))])
    return pl.pallas_call(
        _moe_gather_kernel,
        grid_spec=grid_spec,
        out_shape=[jax.ShapeDtypeStruct((rows, D), BF16), jax.ShapeDtypeStruct((rows, LANES), F32)],
        compiler_params=_cparams(1),
        name="moe_gather",
    )(g_sub, g_blk, g_first, g_valid, dest3, xn, comb)


MOE_EXPERTS_PER_STEP = 2


def _moe_group_experts_kernel(tg_ref, tv_ref, xs_ref, cs_ref, wg_ref, wu_ref, wd_ref, y_ref, acc):
    i = pl.program_id(0)
    e = pl.program_id(1)

    @pl.when(e == 0)
    def _():
        acc[...] = jnp.zeros_like(acc)

    @pl.when(tv_ref[i] == 1)
    def _():
        xs = xs_ref[...]
        cs = cs_ref[...]
        lane = lax.broadcasted_iota(jnp.int32, cs.shape, 1)
        first = tg_ref[i] * MOE_PER_GROUP + e * MOE_EXPERTS_PER_STEP
        hids = []
        for j in range(MOE_EXPERTS_PER_STEP):
            ce = jnp.sum(jnp.where(lane == first + j, cs, 0.0), axis=-1, keepdims=True)
            hids.append(_bf(jax.nn.silu(_dot(xs, _bf(wg_ref[j]))) * _dot(xs, _bf(wu_ref[j])) * ce))
        hid = jnp.concatenate(hids, axis=1)
        acc[...] += _dot(hid, _bf(wd_ref[...].reshape(MOE_EXPERTS_PER_STEP * MOE_HIDDEN, wd_ref.shape[-1])))

    @pl.when(e == pl.num_programs(1) - 1)
    def _():
        y_ref[...] = _bf(acc[...])


def moe_group_experts(xs, cs, plan, wg, wu, wd, l, te):
    rows, D = xs.shape
    eps = MOE_EXPERTS_PER_STEP
    steps = MOE_PER_GROUP // eps

    def expert(i, e, tg, tv):
        return (l, tg[i] * steps + jnp.where(tv[i] == 1, e, steps - 1), 0, 0)

    w_in = pl.BlockSpec((None, eps, D, MOE_HIDDEN), expert)
    grid_spec = pltpu.PrefetchScalarGridSpec(
        num_scalar_prefetch=2,
        grid=(rows // te, steps),
        in_specs=[pl.BlockSpec((te, D), lambda i, e, tg, tv: (i, 0)),
                  pl.BlockSpec((te, LANES), lambda i, e, tg, tv: (i, 0)),
                  w_in, w_in,
                  pl.BlockSpec((None, eps, MOE_HIDDEN, D), expert)],
        out_specs=pl.BlockSpec((te, D), lambda i, e, tg, tv: (i, 0)),
        scratch_shapes=[pltpu.VMEM((te, D), F32)])
    return pl.pallas_call(
        _moe_group_experts_kernel,
        grid_spec=grid_spec,
        out_shape=jax.ShapeDtypeStruct((rows, D), BF16),
        compiler_params=_cparams(2),
        name="moe_group_experts",
    )(plan["tile_group"], plan["tile_valid"], xs, cs, wg, wu, wd)


def _moe_scatter_kernel(sub_ref, blk_ref, first_ref, last_ref, valid_ref, dest_ref, y_ref, x_ref, gf_ref, o_ref,
                        *, final_norm):
    w = pl.program_id(0)

    @pl.when(valid_ref[w] == 1)
    def _():
        cols = sub_ref[w] * MOE_SUB + lax.broadcasted_iota(jnp.int32, (MOE_TB, MOE_SUB), 1)
        pick = _bf(jnp.where(dest_ref[...] == cols, 1.0, 0.0))
        upd = _dot(pick, y_ref[...])

        @pl.when(first_ref[w] == 1)
        def _():
            o_ref[...] = x_ref[...] + upd

        @pl.when(first_ref[w] == 0)
        def _():
            o_ref[...] += upd

        if final_norm:
            @pl.when(last_ref[w] == 1)
            def _():
                o_ref[...] = _rms(o_ref[...], gf_ref[...])


def moe_scatter(ys, x, plan, g_final, *, final_norm):
    T, D = x.shape
    s_sub, s_blk, s_first, s_last, s_valid = plan["scatter"]
    dest_col = plan["dest"].reshape(T, 1)
    im = lambda f: (lambda w, s, b, fi, la, v: f(w, s, b))
    grid_spec = pltpu.PrefetchScalarGridSpec(
        num_scalar_prefetch=5,
        grid=(s_sub.shape[0],),
        in_specs=[pl.BlockSpec((MOE_TB, 1), im(lambda w, s, b: (b[w], 0))),
                  pl.BlockSpec((MOE_SUB, D), im(lambda w, s, b: (s[w], 0))),
                  pl.BlockSpec((MOE_TB, D), im(lambda w, s, b: (b[w], 0))),
                  pl.BlockSpec((1, D), im(lambda w, s, b: (0, 0)))],
        out_specs=pl.BlockSpec((MOE_TB, D), im(lambda w, s, b: (b[w], 0))))
    return pl.pallas_call(
        functools.partial(_moe_scatter_kernel, final_norm=final_norm),
        grid_spec=grid_spec,
        out_shape=jax.ShapeDtypeStruct((T, D), F32),
        compiler_params=_cparams(1),
        name="moe_scatter",
    )(s_sub, s_blk, s_first, s_last, s_valid, dest_col, ys, x, g_final.reshape(1, D))


def hmoe_block(x, p, l, *, final_norm):
    T = x.shape[0]
    te = MOE_TE_LONG if T >= 4 * MOE_TE_LONG else MOE_TE_SHORT
    xn, comb, route = moe_router(x, p["g_moe"][l], p["moe_wr1"][l], p["moe_br1"][l], p["moe_wr2"][l],
                                 p["moe_br2"][l])
    plan = _moe_plan(route, T, te)
    xs, cs = moe_gather(xn, comb, plan)
    ys = moe_group_experts(xs, cs, plan, p["moe_wg"], p["moe_wu"], p["moe_wd"], l, te)
    return moe_scatter(ys, x, plan, p["g_final"], final_norm=final_norm)


def _layer_weights(l, p):
    w_in = p["w_in"][l]
    c0 = RW_COLS
    c1 = c0 + ML_MAIN
    c2 = c1 + 2 * ML_HEADS
    c3 = c2 + POOL_WIDTH
    w_pl = jnp.concatenate([w_in[:, c2:c3], w_in[:, c1:c2], jnp.zeros((D_MODEL, LANES - 2 * ML_HEADS), F32)], axis=1)
    return dict(
        w_rw=_bf(w_in[:, :c0]), w_ml=_bf(w_in[:, c0:c1]), w_pl=_bf(w_pl), w_gate=_bf(w_in[:, c3:]),
        w_up_rwkv=_bf(p["w_up_rwkv"][l]), w_up_mlstm=_bf(p["w_up_mlstm"][l]), w_up_pool=_bf(p["w_up_pool"][l]),
        w_out=_bf(p["w_out"][l]), pool_w=_bf(p["pool_w"][l]),
        xa_wq=_bf(p["xa_wq"][l]), xa_wo=_bf(p["xa_wo"][l]),
        xa_wkv=_bf(jnp.concatenate([p["xa_wk"][l], p["xa_wv"][l]], axis=1)),
    )


def _trunk_layer(x, mem, mem_l, st, carry, *, l, sl, p, lw, B, L, l_valid, start, final_norm):
    g_mix = p["g_mix"][l]
    u_rw = norm_matmul(x, g_mix, lw["w_rw"], tn=TN_RWKV, tm=TM_WHOLE_SEGMENT, name="in_rwkv")
    u_ml = norm_matmul(x, g_mix, lw["w_ml"], tn=TN_MLSTM, tm=TM_WHOLE_SEGMENT, name="in_mlstm")
    u_pl = norm_matmul(x, g_mix, lw["w_pl"], tn=POOL_WIDTH + LANES, name="in_pool")
    gates = norm_matmul(x, g_mix, lw["w_gate"], tn=TN_GATES, tm=TM_WHOLE_SEGMENT, out_dtype=BF16, act="sigmoid",
                        name="in_gates")

    rwkv = rwkv7_short if L < CHUNK_ROWS else rwkv7_mix
    o_rw, rw_s = rwkv(u_rw, st["rw_shift"], st["rw_s"], sl, l, carry["rw_s"], p["rw_mu"][l], p["rw_w0"][l],
                      p["rw_w2"][l], p["rw_a0"][l], p["rw_a2"][l], p["rw_k_k"][l], p["rw_k_a"][l],
                      p["rw_r_k"][l], p["rw_gn_w"][l], p["rw_gn_b"][l], B=B, L=L, l_valid=l_valid)
    rw_shift = u_rw.reshape(B, L, RW_COLS)[:, l_valid - 1]
    o_ml, ml_c, ml_n, ml_m = mlstm_mix(u_ml, u_pl, st["ml_c"], st["ml_n"], st["ml_m"], sl, l, carry["ml_c"],
                                       p["ml_b_i"][l], p["ml_b_f"][l], p["ml_gn_w"][l], B=B, L=L, l_valid=l_valid)
    o_pl, pool_buf = pool_mix(u_pl, st["pool"], sl, lw["pool_w"], p["pool_scale"][l], B=B, L=L, l_valid=l_valid,
                              start=start)
    merged = merge_branches(o_rw, o_ml, o_pl, gates, lw["w_up_rwkv"], lw["w_up_mlstm"], lw["w_up_pool"])
    x = matmul_residual(merged, lw["w_out"], x)

    q = norm_matmul(x, p["g_xa"][l], lw["xa_wq"], tn=XA_WIDTH, out_dtype=BF16, name="xa_q")
    att = cross_attention(q, mem, mem_l, B=B, L=L)
    x = matmul_residual(att, lw["xa_wo"], x)

    x = hmoe_block(x, p, l, final_norm=final_norm)
    return x, dict(rw_s=rw_s, ml_c=ml_c), (rw_shift, ml_n, ml_m, pool_buf)


def kernel(x_prompt, x_sample, cache_mem_k, cache_mem_v, state_rwkv_s, state_rwkv_shift, state_mlstm_c, state_mlstm_n, state_mlstm_m, state_pool, mem_prompt, g_mix, w_in, rw_mu, rw_w0, rw_w2, rw_a0, rw_a2, rw_k_k, rw_k_a, rw_r_k, rw_gn_w, rw_gn_b, ml_b_i, ml_b_f, ml_gn_w, pool_w, pool_scale, w_up_rwkv, w_up_mlstm, w_up_pool, w_out, g_xa, g_mem, xa_wq, xa_wk, xa_wv, xa_wo, g_moe, moe_wr1, moe_br1, moe_wr2, moe_br2, moe_wg, moe_wu, moe_wd, g_final):
    p = dict(g_mix=g_mix, w_in=w_in, rw_mu=rw_mu, rw_w0=rw_w0, rw_w2=rw_w2, rw_a0=rw_a0, rw_a2=rw_a2, rw_k_k=rw_k_k,
             rw_k_a=rw_k_a, rw_r_k=rw_r_k, rw_gn_w=rw_gn_w, rw_gn_b=rw_gn_b, ml_b_i=ml_b_i, ml_b_f=ml_b_f,
             ml_gn_w=ml_gn_w, pool_w=pool_w, pool_scale=pool_scale, w_up_rwkv=w_up_rwkv, w_up_mlstm=w_up_mlstm,
             w_up_pool=w_up_pool, w_out=w_out, g_xa=g_xa, g_mem=g_mem, xa_wq=xa_wq, xa_wk=xa_wk, xa_wv=xa_wv,
             xa_wo=xa_wo, g_moe=g_moe, moe_wr1=moe_wr1, moe_br1=moe_br1, moe_wr2=moe_wr2, moe_br2=moe_br2,
             moe_wg=moe_wg, moe_wu=moe_wu, moe_wd=moe_wd, g_final=g_final)
    Bp, Lp, D = x_prompt.shape
    Bs, Ls, _ = x_sample.shape
    Ls_pad = -(-Ls // SUBLANES) * SUBLANES
    yp = x_prompt.reshape(Bp * Lp, D)
    ys = jnp.pad(x_sample, ((0, 0), (0, Ls_pad - Ls), (0, 0))).reshape(Bs * Ls_pad, D)
    zeros = lambda *s: jnp.zeros((1,) + s, F32)
    st_p = dict(rw_shift=zeros(Bp, RW_COLS), rw_s=zeros(Bp, RW_HEADS, RW_HEAD, RW_HEAD),
                ml_c=zeros(Bp, ML_HEADS, ML_DQK, ML_DV), ml_n=zeros(Bp, ML_HEADS, ML_DQK), ml_m=zeros(Bp, ML_HEADS),
                pool=zeros(Bp, POOL_BUF, POOL_WIDTH))
    batch_minor = (0, 2, 3, 4, 1)
    st_s = dict(rw_shift=state_rwkv_shift, rw_s=jnp.transpose(state_rwkv_s, batch_minor), ml_c=state_mlstm_c,
                ml_n=state_mlstm_n, ml_m=state_mlstm_m, pool=state_pool)
    carry_p = carry_s = dict(rw_s=None, ml_c=None)
    small_p = [[] for _ in range(6)]
    small_s = [[] for _ in range(4)]
    for l in range(DEPTH):
        lw = _layer_weights(l, p)
        final = l == DEPTH - 1
        kv = norm_matmul(mem_prompt.reshape(Bp * MEM_LEN, D), g_mem[l], lw["xa_wkv"], tn=2 * XA_WIDTH,
                         name="memory_kv")
        kv3 = kv.reshape(Bp, MEM_LEN, 2 * XA_WIDTH)
        yp, carry_p, small = _trunk_layer(yp, kv3, None, st_p, carry_p, l=l, sl=0, p=p, lw=lw, B=Bp, L=Lp,
                                          l_valid=Lp, start=0, final_norm=final)
        mk = kv[:, :XA_WIDTH].reshape(Bp, MEM_LEN, XA_HEADS, XA_HEAD_DIM)
        mv = kv[:, XA_WIDTH:].reshape(Bp, MEM_LEN, XA_HEADS, XA_HEAD_DIM)
        for acc, t in zip(small_p, small + (mk, mv)):
            acc.append(t)
        ys, carry_s, small = _trunk_layer(ys, (cache_mem_k, cache_mem_v), l, st_s, carry_s, l=l, sl=l, p=p, lw=lw,
                                          B=Bs, L=Ls_pad, l_valid=Ls, start=PAST_LEN, final_norm=final)
        for acc, t in zip(small_s, small):
            acc.append(t)
    y_prompt = yp.reshape(Bp, Lp, D)
    y_sample = ys.reshape(Bs, Ls_pad, D)[:, :Ls]
    p_sh, p_n, p_m, p_pool, p_mk, p_mv = [jnp.stack(t) for t in small_p]
    s_sh, s_n, s_m, s_pool = [jnp.stack(t) for t in small_s]
    return (y_prompt, y_sample, carry_p["rw_s"], p_sh, carry_p["ml_c"], p_n, p_m, p_pool, p_mk, p_mv,
            jnp.transpose(carry_s["rw_s"], (0, 4, 1, 2, 3)), s_sh, carry_s["ml_c"], s_n, s_m, s_pool)
```

```python
import functools
import math

import jax
import jax.numpy as jnp
from jax import lax
from jax.experimental import pallas as pl
from jax.experimental.pallas import tpu as pltpu

F32 = jnp.float32
BF16 = jnp.bfloat16
HI = lax.Precision.HIGHEST

D_MODEL = 2048
DEPTH = 2
PAST_LEN = 16384
RW_HEAD = 64
RW_WIDTH = 1024
RW_HEADS = 16
RW_RANK = 64
RW_COLS = 3 * RW_WIDTH + 2 * RW_RANK
GN_EPS = 64e-5
ML_HEADS = 4
ML_DQK = 128
ML_DV = 256
ML_QK_WIDTH = 512
ML_V_WIDTH = 1024
ML_MAIN = 2 * ML_QK_WIDTH + 2 * ML_V_WIDTH
POOL_WIDTH = 1024
POOL_WINDOWS = (2, 4, 8, 16)
POOL_GW = 256
POOL_BUF = 15
POOL_HIST = 16
POOL_TAIL = 24
MEM_LEN = 256
XA_HEADS = 4
XA_HEAD_DIM = 128
XA_WIDTH = 512
MOE_GROUPS = 4
MOE_PER_GROUP = 8
MOE_EXPERTS = 32
MOE_HIDDEN = 256
RMS_EPS = 1e-6
LANES = 128
SUBLANES = 8
VMEM_LIMIT = 56 * 1024 * 1024
ROW_TILE = 1024
CHUNK_ROWS = 64
TN_RWKV = RW_COLS
TN_MLSTM = ML_MAIN
TN_GATES = 3 * D_MODEL // 2
TM_WHOLE_SEGMENT = ROW_TILE // 2


def _cparams(n_axes):
    return pltpu.CompilerParams(dimension_semantics=("arbitrary",) * n_axes, vmem_limit_bytes=VMEM_LIMIT)


def _dot(a, b, precision=None):
    return jnp.dot(a, b, preferred_element_type=F32, precision=precision)


def _dot_nt(a, b, precision=None):
    return lax.dot_general(a, b, (((1,), (1,)), ((), ())), preferred_element_type=F32, precision=precision)


def _dot_tn(a, b, precision=None):
    return lax.dot_general(a, b, (((0,), (0,)), ((), ())), preferred_element_type=F32, precision=precision)


def _bf(x):
    return x.astype(BF16)


def _split3(x):
    hi = _bf(x)
    rest = x - hi.astype(F32)
    mid = _bf(rest)
    return jnp.concatenate([hi, mid, _bf(rest - mid.astype(F32))], axis=1)


def _rms(x, g):
    return x * lax.rsqrt(jnp.mean(x * x, axis=-1, keepdims=True) + RMS_EPS) * g


def _log_sigmoid(x):
    return jnp.minimum(x, 0.0) - jnp.log1p(jnp.exp(-jnp.abs(x)))


def _softplus(x):
    return jnp.maximum(x, 0.0) + jnp.log1p(jnp.exp(-jnp.abs(x)))


def _seq_prefix_mask(n, c):
    r = lax.broadcasted_iota(jnp.int32, (n, n), 0)
    q = lax.broadcasted_iota(jnp.int32, (n, n), 1)
    return jnp.logical_and(q <= r, q // c == r // c)


def _layer_block(l, blk, idx):
    return pl.BlockSpec((None,) + tuple(blk), lambda b, i: (l,) + tuple(idx(b, i)))


def _norm_matmul_kernel(x_ref, g_ref, w_ref, o_ref, xn_ref, *, act):
    @pl.when(pl.program_id(1) == 0)
    def _():
        xn_ref[...] = _bf(_rms(x_ref[...], g_ref[...]))

    acc = _dot(xn_ref[...], w_ref[...])
    if act == "sigmoid":
        acc = jax.nn.sigmoid(acc)
    o_ref[...] = acc.astype(o_ref.dtype)


def norm_matmul(x, g, w, *, tn, name, tm=ROW_TILE, out_dtype=F32, act=None):
    T, D = x.shape
    N = w.shape[1]
    tm = min(T, tm)
    return pl.pallas_call(
        functools.partial(_norm_matmul_kernel, act=act),
        grid=(T // tm, N // tn),
        in_specs=[pl.BlockSpec((tm, D), lambda i, j: (i, 0)),
                  pl.BlockSpec((1, D), lambda i, j: (0, 0)),
                  pl.BlockSpec((D, tn), lambda i, j: (0, j))],
        out_specs=pl.BlockSpec((tm, tn), lambda i, j: (i, j)),
        out_shape=jax.ShapeDtypeStruct((T, N), out_dtype),
        scratch_shapes=[pltpu.VMEM((tm, D), BF16)],
        compiler_params=_cparams(2),
        name=name,
    )(x, g.reshape(1, D), w)


def _matmul_residual_kernel(a_ref, w_ref, r_ref, o_ref):
    o_ref[...] = r_ref[...] + _dot(a_ref[...], w_ref[...])


def matmul_residual(a, w, res):
    T, K = a.shape
    tn = w.shape[1]
    tm = min(T, TM_WHOLE_SEGMENT)
    return pl.pallas_call(
        _matmul_residual_kernel,
        grid=(T // tm, 1),
        in_specs=[pl.BlockSpec((tm, K), lambda i, j: (i, 0)),
                  pl.BlockSpec((K, tn), lambda i, j: (0, j)),
                  pl.BlockSpec((tm, tn), lambda i, j: (i, j))],
        out_specs=pl.BlockSpec((tm, tn), lambda i, j: (i, j)),
        out_shape=jax.ShapeDtypeStruct((T, tn), F32),
        compiler_params=_cparams(2),
        name="matmul_residual",
    )(a, w, res)


def _merge_kernel(orw_ref, oml_ref, opl_ref, g0_ref, g1_ref, g2_ref, wr_ref, wm_ref, wp_ref, o_ref):
    m = (g0_ref[...].astype(F32) * _dot(orw_ref[...], wr_ref[...])
         + g1_ref[...].astype(F32) * _dot(oml_ref[...], wm_ref[...])
         + g2_ref[...].astype(F32) * _dot(opl_ref[...], wp_ref[...]))
    o_ref[...] = _bf(m)


def merge_branches(o_rw, o_ml, o_pl, gates, w_rw, w_ml, w_pl, *, tn=1024):
    T, K = o_rw.shape
    tm = min(T, ROW_TILE)
    nj = D_MODEL // tn
    act = pl.BlockSpec((tm, K), lambda i, j: (i, 0))
    wsp = pl.BlockSpec((K, tn), lambda i, j: (0, j))
    gate = lambda b: pl.BlockSpec((tm, tn), lambda i, j, b=b: (i, b * nj + j))
    return pl.pallas_call(
        _merge_kernel,
        grid=(T // tm, nj),
        in_specs=[act, act, act, gate(0), gate(1), gate(2), wsp, wsp, wsp],
        out_specs=pl.BlockSpec((tm, tn), lambda i, j: (i, j)),
        out_shape=jax.ShapeDtypeStruct((T, D_MODEL), BF16),
        compiler_params=_cparams(2),
        name="merge_branches",
    )(o_rw, o_ml, o_pl, gates, gates, gates, w_rw, w_ml, w_pl)


def _pool_kernel(hist_ref, u_ref, w_ref, sc_ref, o_ref, tail_ref, e_ref, *, tl, bt, start):
    li = pl.program_id(1)

    @pl.when(li == 0)
    def _():
        e_ref[:, 0:POOL_HIST, :] = hist_ref[...]

    pos = start + li * tl + lax.broadcasted_iota(jnp.int32, (tl, 1), 0)
    for b in range(bt):
        u = u_ref[b]
        e_ref[b, POOL_HIST:, :] = u
        e = e_ref[b]
        s2 = e + pltpu.roll(e, 1, 0)
        s4 = s2[:, POOL_GW:] + pltpu.roll(s2[:, POOL_GW:], 2, 0)
        s8 = s4[:, POOL_GW:] + pltpu.roll(s4[:, POOL_GW:], 4, 0)
        s16 = s8[:, POOL_GW:] + pltpu.roll(s8[:, POOL_GW:], 8, 0)
        sums = (s2[:, :POOL_GW], s4[:, :POOL_GW], s8[:, :POOL_GW], s16)
        for g, win in enumerate(POOL_WINDOWS):
            cols = slice(g * POOL_GW, (g + 1) * POOL_GW)
            cnt = jnp.minimum(win, pos + 1).astype(F32)
            d = sums[g][POOL_HIST:, :] / cnt - u[:, cols]
            out = _dot(_bf(d), w_ref[g]) * sc_ref[:, cols]
            o_ref[b, :, cols] = out.astype(o_ref.dtype)
        tail_ref[b] = e[tl + POOL_HIST - POOL_TAIL:, :]
        e_ref[b, 0:POOL_HIST, :] = e[tl:, :]


def pool_mix(u_pl, buf, l, w_grp, scale, *, B, L, l_valid, start):
    tl = min(L, 256)
    bt = max(CHUNK_ROWS // L, 1)
    hist = jnp.pad(buf[l], ((0, 0), (POOL_HIST - POOL_BUF, 0), (0, 0)))
    u3 = u_pl.reshape(B, L, u_pl.shape[-1])
    out, tail = pl.pallas_call(
        functools.partial(_pool_kernel, tl=tl, bt=bt, start=start),
        grid=(B // bt, L // tl),
        in_specs=[pl.BlockSpec((bt, POOL_HIST, POOL_WIDTH), lambda b, i: (b, 0, 0)),
                  pl.BlockSpec((bt, tl, POOL_WIDTH), lambda b, i: (b, i, 0)),
                  pl.BlockSpec((4, POOL_GW, POOL_GW), lambda b, i: (0, 0, 0)),
                  pl.BlockSpec((1, POOL_WIDTH), lambda b, i: (0, 0))],
        out_specs=[pl.BlockSpec((bt, tl, POOL_WIDTH), lambda b, i: (b, i, 0)),
                   pl.BlockSpec((bt, POOL_TAIL, POOL_WIDTH), lambda b, i: (b, 0, 0))],
        out_shape=[jax.ShapeDtypeStruct((B, L, POOL_WIDTH), BF16),
                   jax.ShapeDtypeStruct((B, POOL_TAIL, POOL_WIDTH), F32)],
        scratch_shapes=[pltpu.VMEM((bt, tl + POOL_HIST, POOL_WIDTH), F32)],
        compiler_params=_cparams(2),
        name="pool_mix",
    )(hist, u3, w_grp, scale.reshape(1, POOL_WIDTH))
    pad = L - l_valid
    new_buf = tail[:, POOL_TAIL - pad - POOL_BUF:POOL_TAIL - pad, :]
    return out.reshape(B * L, POOL_WIDTH), new_buf


def _mlstm_kernel(*refs, c, bt, l_valid, has_carry):
    u_ref, gif_ref, c0_ref, n0_ref, m0_ref, bi_ref, bf_ref, gnw_ref = refs[:8]
    o_ref, cN_ref, nN_ref, mN_ref, c_s, n_s, m_s = refs[8 + has_carry:]
    ci = pl.program_id(1)

    @pl.when(ci == 0)
    def _():
        c_s[...] = c0_ref[...]
        n_s[...] = n0_ref[...]
        m_s[...] = m0_ref[...]

    N = bt * c
    gif = gif_ref[...].reshape(N, LANES)
    tok = lax.broadcasted_iota(jnp.int32, (N, 1), 0) % c
    valid = tok < l_valid
    ig_all = jnp.where(valid, gif + bi_ref[...], -jnp.inf)
    lf_all = jnp.where(valid, _log_sigmoid(gif + bf_ref[...]), 0.0)
    bcum_all = _dot(_seq_prefix_mask(N, c).astype(F32), lf_all, HI)
    ig_t = ig_all.T
    bcum_t = bcum_all.T
    ri = lax.broadcasted_iota(jnp.int32, (c, c), 0)
    causal = lax.broadcasted_iota(jnp.int32, (c, c), 1) <= ri
    st = []
    for b in range(bt):
        rs = slice(b * c, (b + 1) * c)
        for h in range(ML_HEADS):
            qs = slice(h * ML_DQK, (h + 1) * ML_DQK)
            ks = slice(ML_QK_WIDTH + h * ML_DQK, ML_QK_WIDTH + (h + 1) * ML_DQK)
            vs = slice(2 * ML_QK_WIDTH + h * ML_DV, 2 * ML_QK_WIDTH + (h + 1) * ML_DV)
            os_ = slice(2 * ML_QK_WIDTH + ML_V_WIDTH + h * ML_DV, 2 * ML_QK_WIDTH + ML_V_WIDTH + (h + 1) * ML_DV)
            i_c = ig_all[rs, h:h + 1]
            b_c = bcum_all[rs, ML_HEADS + h:ML_HEADS + h + 1]
            i_r = ig_t[h:h + 1, rs]
            b_r = bcum_t[ML_HEADS + h:ML_HEADS + h + 1, rs]
            m_prev = m_s[b, h]
            dlog = jnp.where(causal, b_c - b_r + i_r, -jnp.inf)
            inter = b_c + m_prev
            m_t = jnp.maximum(inter, jnp.max(dlog, axis=-1, keepdims=True))
            b_last = b_c[c - 1:c, :]
            s_log = b_last - b_c + i_c
            m_new = jnp.maximum(b_last + m_prev, jnp.max(s_log, axis=0, keepdims=True))
            q = u_ref[b, :, qs]
            k = u_ref[b, :, ks] * (ML_DQK ** -0.5)
            st.append(dict(b=b, h=h, hs=slice(h * ML_DV, (h + 1) * ML_DV), os=os_, q=q, k=k, qb=_bf(q), vb=_bf(u_ref[b, :, vs]),
                           dlog=dlog, m_t=m_t, w_prev=jnp.exp(inter - m_t), m_new=m_new,
                           kw=k * jnp.exp(s_log - m_new), wp=jnp.exp(b_last + m_prev - m_new)))
    for s in st:
        s["wts"] = jnp.exp(s["dlog"] - s["m_t"]) * _dot_nt(s["qb"], _bf(s["k"]))
    for s in st:
        b, h = s["b"], s["h"]
        c_prev = c_s[b, h]
        n_prev = n_s[b, h]
        num = s["w_prev"] * _dot(s["qb"], _bf(c_prev)) + _dot(_bf(s["wts"]), s["vb"])
        den = (s["w_prev"] * jnp.sum(s["q"] * n_prev, axis=-1, keepdims=True)
               + jnp.sum(s["wts"], axis=-1, keepdims=True))
        s["hh"] = num / jnp.maximum(jnp.abs(den), jnp.exp(-s["m_t"]))
        c_s[b, h] = s["wp"] * c_prev + _dot_tn(_bf(s["kw"]), s["vb"])
        n_s[b, h] = s["wp"] * n_prev + jnp.sum(s["kw"], axis=0, keepdims=True)
        m_s[b, h] = s["m_new"]
    for s in st:
        hh = s["hh"]
        hn = hh * lax.rsqrt(jnp.mean(hh * hh, axis=-1, keepdims=True) + RMS_EPS) * gnw_ref[:, s["hs"]]
        o_ref[s["b"], :, s["hs"]] =(hn * jax.nn.sigmoid(u_ref[s["b"], :, s["os"]])).astype(o_ref.dtype)

    @pl.when(ci == pl.num_programs(1) - 1)
    def _():
        cN_ref[...] = c_s[...]
        nN_ref[...] = n_s[...]
        mN_ref[...] = m_s[...]


def mlstm_mix(u_ml, u_pl, c0, n0, m0, l, l_out, c_carry, b_i, b_f, gn_w, *, B, L, l_valid):
    c = min(L, CHUNK_ROWS)
    bt = CHUNK_ROWS // c
    u3 = u_ml.reshape(B, L, ML_MAIN)
    g3 = u_pl.reshape(B, L, POOL_WIDTH + LANES)
    zeros = jnp.zeros((LANES - 2 * ML_HEADS,), F32)
    bi = jnp.concatenate([b_i, jnp.zeros((ML_HEADS,), F32), zeros]).reshape(1, LANES)
    bf = jnp.concatenate([jnp.zeros((ML_HEADS,), F32), b_f, zeros]).reshape(1, LANES)
    nl = n0.shape[0]
    c_blk, n_blk, m_blk = (bt, ML_HEADS, ML_DQK, ML_DV), (bt, ML_HEADS, 1, ML_DQK), (bt, ML_HEADS, 1, 1)
    at_b = lambda b, i: (b, 0, 0, 0)
    in_specs = [pl.BlockSpec((bt, c, ML_MAIN), lambda b, i: (b, i, 0)),
                pl.BlockSpec((bt, c, LANES), lambda b, i: (b, i, POOL_WIDTH // LANES)),
                _layer_block(l, c_blk, at_b), _layer_block(l, n_blk, at_b), _layer_block(l, m_blk, at_b),
                pl.BlockSpec((1, LANES), lambda b, i: (0, 0)),
                pl.BlockSpec((1, LANES), lambda b, i: (0, 0)),
                pl.BlockSpec((1, ML_V_WIDTH), lambda b, i: (0, 0))]
    args = [u3, g3, c0, n0.reshape(nl, B, ML_HEADS, 1, ML_DQK), m0.reshape(nl, B, ML_HEADS, 1, 1), bi, bf,
            gn_w.reshape(1, ML_V_WIDTH)]
    aliases = {}
    if c_carry is not None:
        in_specs.append(pl.BlockSpec(memory_space=pl.ANY))
        args.append(c_carry)
        aliases = {len(args) - 1: 1}
    out, cN, nN, mN = pl.pallas_call(
        functools.partial(_mlstm_kernel, c=c, bt=bt, l_valid=l_valid, has_carry=c_carry is not None),
        grid=(B // bt, L // c),
        in_specs=in_specs,
        out_specs=[pl.BlockSpec((bt, c, ML_V_WIDTH), lambda b, i: (b, i, 0)),
                   _layer_block(l_out, c_blk, at_b),
                   pl.BlockSpec(n_blk, at_b), pl.BlockSpec(m_blk, at_b)],
        out_shape=[jax.ShapeDtypeStruct((B, L, ML_V_WIDTH), BF16),
                   jax.ShapeDtypeStruct((DEPTH, B, ML_HEADS, ML_DQK, ML_DV), F32),
                   jax.ShapeDtypeStruct((B, ML_HEADS, 1, ML_DQK), F32),
                   jax.ShapeDtypeStruct((B, ML_HEADS, 1, 1), F32)],
        scratch_shapes=[pltpu.VMEM(c_blk, F32), pltpu.VMEM(n_blk, F32), pltpu.VMEM(m_blk, F32)],
        input_output_aliases=aliases,
        compiler_params=_cparams(2),
        name="mlstm_mix",
    )(*args)
    return (out.reshape(B * L, ML_V_WIDTH), cN, nN.reshape(B, ML_HEADS, ML_DQK), mN.reshape(B, ML_HEADS))


RW_CHAIN_GROUP = 16

def _rwkv_chain_group(chains, seq, s_s, o_ref, rk_ref, gnw_ref, gnb_ref, *, C):
    rowi = lax.broadcasted_iota(jnp.int32, (C, C), 0)
    coli = lax.broadcasted_iota(jnp.int32, (C, C), 1)
    upper = rowi < coli
    col2 = lax.broadcasted_iota(jnp.int32, (C, 2 * C), 1)
    incl2 = jnp.where(col2 >= C, col2 - C, col2) <= lax.broadcasted_iota(jnp.int32, (C, 2 * C), 0)
    n_sq = max(int(math.log2(C)), 1)
    st = []
    for b, h in chains:
        rs = slice(b * C, (b + 1) * C)
        sl = slice(h * RW_HEAD, (h + 1) * RW_HEAD)
        cut = lambda t, rs=rs, sl=sl: t[rs, sl]
        kk_h = cut(seq["kkv"])
        nrm = jnp.sqrt(jnp.sum(kk_h * kk_h, axis=-1, keepdims=True))
        kap = jnp.where(seq["valid"][rs], kk_h / jnp.maximum(nrm, 1e-12), 0.0)
        k_h, v_h, r_h = cut(seq["kmod"]), cut(seq["v"]), cut(seq["r"])
        gi = cut(seq["gi"])
        b_t = kap * cut(seq["a"]) * gi
        k_t = k_h * gi
        st.append(dict(b=b, h=h, sl=sl, k_h=k_h, v_h=v_h, r_h=r_h,
                       a_t=_bf(-kap * cut(seq["gp"])), r_t=_bf(r_h * cut(seq["g"])),
                       bk=jnp.concatenate([b_t, k_t], axis=0),
                       ge=seq["g"][(b + 1) * C - 1:(b + 1) * C, sl], s0=s_s[b, h]))
    for c in st:
        bkb = _bf(c["bk"])
        mt = _dot_nt(bkb, c["a_t"])
        c["pt"] = jnp.where(upper, mt[:C], 0.0)
        c["akt"] = _bf(jnp.where(upper, mt[C:], 0.0))
        c["a_r"] = jnp.where(incl2, _dot_nt(c["r_t"], bkb), 0.0)
        c["s0b"] = _bf(c["s0"])
        c["vb"] = _bf(c["v_h"])
    for c in st:
        c["xt"] = _dot_nt(c["s0b"], c["a_t"]) + _dot_tn(c["vb"], c["akt"])
    for _ in range(n_sq - 1):
        for c in st:
            z = _dot(_bf(jnp.concatenate([c["pt"], c["xt"]], axis=0)), _bf(c["pt"]))
            c["pt"] = z[:C]
            c["xt"] = c["xt"] + z[C:]
    for c in st:
        c["ut"] = _bf(c["xt"] + _dot(_bf(c["xt"]), _bf(c["pt"])))
    for c in st:
        a_r = c["a_r"]
        c["y"] = (_dot_nt(c["r_t"], c["s0b"]) + _dot_nt(_bf(a_r[:, :C]), c["ut"])
                  + _dot(_bf(a_r[:, C:]), c["vb"]))
        bkg = c["bk"] * c["ge"]
        s_s[c["b"], c["h"]] = (c["s0"] * c["ge"] + _dot(c["ut"], _bf(bkg[:C]))
                               + _dot_tn(c["vb"], _bf(bkg[C:])))
    for c in st:
        y, sl = c["y"], c["sl"]
        mean = jnp.mean(y, axis=-1, keepdims=True)
        yc = y - mean
        var = jnp.mean(yc * yc, axis=-1, keepdims=True)
        yn = yc * lax.rsqrt(var + GN_EPS) * gnw_ref[:, sl] + gnb_ref[:, sl]
        bonus = jnp.sum(c["r_h"] * c["k_h"] * rk_ref[:, sl], axis=-1, keepdims=True) * c["v_h"]
        o_ref[c["b"], :, sl] = (yn + bonus).astype(o_ref.dtype)


def _rwkv_kernel(*refs, C, bt, l_valid, has_carry):
    (u_ref, sh_ref, s0_ref, mu_ref, w0_ref, w2_ref, a0_ref, a2_ref, kk_ref, ka_ref, rk_ref, gnw_ref,
     gnb_ref) = refs[:13]
    o_ref, sN_ref, prev_s, s_s = refs[13 + has_carry:]
    ci = pl.program_id(1)

    @pl.when(ci == 0)
    def _():
        prev_s[...] = sh_ref[...]
        s_s[...] = s0_ref[...]

    N = bt * C
    u = u_ref[...].reshape(N, RW_COLS)
    tok = lax.broadcasted_iota(jnp.int32, (N, 1), 0) % C
    valid = tok < l_valid
    prev = jnp.concatenate([jnp.broadcast_to(prev_s[b], (C, RW_COLS)) for b in range(bt)], axis=0)
    u_prev = jnp.where(tok == 0, prev, pltpu.roll(u, 1, 0))
    for b in range(bt):
        prev_s[b] = u[(b + 1) * C - 1:(b + 1) * C, :]
    us = u + (u_prev - u) * mu_ref[...]
    W = RW_WIDTH
    k = us[:, W:2 * W]
    wd = us[:, 3 * W:3 * W + RW_RANK]
    ad = us[:, 3 * W + RW_RANK:3 * W + 2 * RW_RANK]
    xw = w0_ref[...] + _dot(_bf(jnp.tanh(wd)), _bf(w2_ref[...]))
    log_w = -_softplus(-xw) - 0.5
    ld = jnp.where(valid, -jnp.exp(log_w), 0.0)
    a = jax.nn.sigmoid(a0_ref[...] + _dot(_bf(ad), _bf(a2_ref[...])))
    cum = _dot(_seq_prefix_mask(N, C).astype(F32), ld, HI)
    seq = dict(valid=valid, r=us[:, 0:W], v=jnp.where(valid, us[:, 2 * W:3 * W], 0.0), a=a,
               g=jnp.exp(cum), gi=jnp.exp(-cum), gp=jnp.exp(cum - ld), kkv=k * kk_ref[...],
               kmod=jnp.where(valid, k * (1.0 + (a - 1.0) * ka_ref[...]), 0.0))
    chains = [(b, h) for b in range(bt) for h in range(RW_HEADS)]
    group = RW_CHAIN_GROUP * bt
    for i in range(0, len(chains), group):
        _rwkv_chain_group(chains[i:i + group], seq, s_s, o_ref, rk_ref, gnw_ref, gnb_ref, C=C)

    @pl.when(ci == pl.num_programs(1) - 1)
    def _():
        sN_ref[...] = s_s[...]


def rwkv7_mix(u_rw, shift_prev, s_prev, l, l_out, s_carry, mu, w0, w2, a0, a2, k_k, k_a, r_k, gn_w, gn_b, *, B, L,
              l_valid):
    C = min(L, CHUNK_ROWS)
    bt = CHUNK_ROWS // C
    u3 = u_rw.reshape(B, L, RW_COLS)
    vec = lambda n: pl.BlockSpec((1, n), lambda b, i: (0, 0))
    row = lambda t: t.reshape(1, -1)
    s_blk = (bt, RW_HEADS, RW_HEAD, RW_HEAD)
    at_b = lambda b, i: (b, 0, 0, 0)
    in_specs = [pl.BlockSpec((bt, C, RW_COLS), lambda b, i: (b, i, 0)),
                _layer_block(l, (bt, 1, RW_COLS), lambda b, i: (b, 0, 0)),
                _layer_block(l, s_blk, at_b),
                vec(RW_COLS), vec(RW_WIDTH),
                pl.BlockSpec((RW_RANK, RW_WIDTH), lambda b, i: (0, 0)),
                vec(RW_WIDTH),
                pl.BlockSpec((RW_RANK, RW_WIDTH), lambda b, i: (0, 0)),
                vec(RW_WIDTH), vec(RW_WIDTH), vec(RW_WIDTH), vec(RW_WIDTH), vec(RW_WIDTH)]
    args = [u3, shift_prev.reshape(shift_prev.shape[0], B, 1, RW_COLS), s_prev, row(mu), row(w0), w2, row(a0), a2,
            row(k_k), row(k_a), row(r_k), row(gn_w), row(gn_b)]
    aliases = {}
    if s_carry is not None:
        in_specs.append(pl.BlockSpec(memory_space=pl.ANY))
        args.append(s_carry)
        aliases = {len(args) - 1: 1}
    out, sN = pl.pallas_call(
        functools.partial(_rwkv_kernel, C=C, bt=bt, l_valid=l_valid, has_carry=s_carry is not None),
        grid=(B // bt, L // C),
        in_specs=in_specs,
        out_specs=[pl.BlockSpec((bt, C, RW_WIDTH), lambda b, i: (b, i, 0)), _layer_block(l_out, s_blk, at_b)],
        out_shape=[jax.ShapeDtypeStruct((B, L, RW_WIDTH), BF16),
                   jax.ShapeDtypeStruct((DEPTH, B, RW_HEADS, RW_HEAD, RW_HEAD), F32)],
        scratch_shapes=[pltpu.VMEM((bt, 1, RW_COLS), F32), pltpu.VMEM(s_blk, F32)],
        input_output_aliases=aliases,
        compiler_params=_cparams(2),
        name="rwkv7_mix",
    )(*args)
    return out.reshape(B * L, RW_WIDTH), sN


RWS_VECS = 5
RWS_ROWS = 2


def _rwkv_short_kernel(*refs, l_valid, has_carry):
    (ur_ref, uk_ref, uv_ref, ul_ref, shr_ref, shk_ref, shv_ref, shl_ref, mur_ref, muk_ref, muv_ref, mul_ref,
     w0_ref, w2_ref, a0_ref, a2_ref, kk_ref, ka_ref, rk_ref, gnw_ref, gnb_ref, s_ref) = refs[:22]
    o_ref, sN_ref, vec_s, val_s, y_s = refs[22 + has_carry:]
    H2 = LANES // RW_HEAD
    heads = [slice(h * RW_HEAD, (h + 1) * RW_HEAD) for h in range(H2)]
    o_ref[...] = jnp.zeros_like(o_ref)

    def shifted(u_ref, sh_ref, mu_ref, t):
        u = u_ref[:, t, :]
        prev = sh_ref[...] if t == 0 else u_ref[:, t - 1, :]
        return u + (prev - u) * mu_ref[...]

    rows = []
    for t in range(l_valid):
        r = shifted(ur_ref, shr_ref, mur_ref, t)
        k = shifted(uk_ref, shk_ref, muk_ref, t)
        v = shifted(uv_ref, shv_ref, muv_ref, t)
        lo = shifted(ul_ref, shl_ref, mul_ref, t)
        xw = w0_ref[...] + _dot(jnp.tanh(lo[:, :RW_RANK]), w2_ref[...], HI)
        w = jnp.exp(-jnp.exp(-_softplus(-xw) - 0.5))
        a = jax.nn.sigmoid(a0_ref[...] + _dot(lo[:, RW_RANK:], a2_ref[...], HI))
        kkv = k * kk_ref[...]
        kmod = k * (1.0 + (a - 1.0) * ka_ref[...])
        kap = jnp.concatenate(
            [kkv[:, sl] / jnp.maximum(jnp.sqrt(jnp.sum(kkv[:, sl] * kkv[:, sl], axis=-1, keepdims=True)), 1e-12)
             for sl in heads], axis=1)
        for j, x in enumerate((w, kap, kap * a, kmod, r)):
            vec_s[t, j] = x.T.reshape(H2, RW_HEAD, x.shape[0])
        val_s[t] = v.T.reshape(H2, RW_HEAD, v.shape[0])
        rows.append((r, kmod, v))

    for h in range(H2):
        def body(vi, carry, h=h):
            vs = [vi * RWS_ROWS + j for j in range(RWS_ROWS)]
            ss = [s_ref[h, v] for v in vs]
            for t in range(l_valid):
                w, kap, bb, kk, rr = (vec_s[t, j, h] for j in range(RWS_VECS))
                for j, v in enumerate(vs):
                    sa = jnp.sum(ss[j] * kap, axis=0, keepdims=True)
                    ss[j] = ss[j] * w - sa * bb + val_s[t, h, pl.ds(v, 1), :] * kk
                    y_s[t, h, pl.ds(v, 1), :] = jnp.sum(ss[j] * rr, axis=0, keepdims=True)
            for j, v in enumerate(vs):
                sN_ref[h, v] = ss[j]
            return carry
        lax.fori_loop(0, RW_HEAD // RWS_ROWS, body, 0)

    for t in range(l_valid):
        r, kmod, v = rows[t]
        y_all = y_s[t].reshape(LANES, y_s.shape[-1]).T
        outs = []
        for sl in heads:
            y = y_all[:, sl]
            mean = jnp.mean(y, axis=-1, keepdims=True)
            yc = y - mean
            var = jnp.mean(yc * yc, axis=-1, keepdims=True)
            yn = yc * lax.rsqrt(var + GN_EPS) * gnw_ref[:, sl] + gnb_ref[:, sl]
            bonus = jnp.sum(r[:, sl] * kmod[:, sl] * rk_ref[:, sl], axis=-1, keepdims=True) * v[:, sl]
            outs.append(yn + bonus)
        o_ref[:, t, :] = jnp.concatenate(outs, axis=1).astype(o_ref.dtype)


def rwkv7_short(u_rw, shift_prev, s_prev_t, l, l_out, s_carry, mu, w0, w2, a0, a2, k_k, k_a, r_k, gn_w, gn_b, *, B, L,
                l_valid):
    assert B == LANES, "the batch must fill the lane dimension"
    H2 = LANES // RW_HEAD
    nblk = RW_WIDTH // LANES
    u3 = u_rw.reshape(B, L, RW_COLS)
    seg = lambda off: pl.BlockSpec((B, L, LANES), lambda hp, off=off: (0, 0, off + hp))
    lora = pl.BlockSpec((B, L, LANES), lambda hp: (0, 0, 3 * nblk))
    sh = lambda off: pl.BlockSpec((None, B, LANES), lambda hp, off=off: (l, 0, off + hp))
    sh_lora = pl.BlockSpec((None, B, LANES), lambda hp: (l, 0, 3 * nblk))
    mus = lambda off: pl.BlockSpec((1, LANES), lambda hp, off=off: (0, off + hp))
    mu_lora = pl.BlockSpec((1, LANES), lambda hp: (0, 3 * nblk))
    vec = pl.BlockSpec((1, LANES), lambda hp: (0, hp))
    mat = pl.BlockSpec((RW_RANK, LANES), lambda hp: (0, hp))
    s_blk = (H2, RW_HEAD, RW_HEAD, B)
    in_specs = [seg(0), seg(nblk), seg(2 * nblk), lora, sh(0), sh(nblk), sh(2 * nblk), sh_lora,
                mus(0), mus(nblk), mus(2 * nblk), mu_lora,
                vec, mat, vec, mat, vec, vec, vec, vec, vec,
                pl.BlockSpec((None,) + s_blk, lambda hp: (l, hp, 0, 0, 0))]
    row = lambda t: t.reshape(1, -1)
    args = [u3, u3, u3, u3, shift_prev, shift_prev, shift_prev, shift_prev, row(mu), row(mu), row(mu), row(mu),
            row(w0), w2, row(a0), a2, row(k_k), row(k_a), row(r_k), row(gn_w), row(gn_b), s_prev_t]
    aliases = {}
    if s_carry is not None:
        in_specs.append(pl.BlockSpec(memory_space=pl.ANY))
        args.append(s_carry)
        aliases = {len(args) - 1: 1}
    out, sN = pl.pallas_call(
        functools.partial(_rwkv_short_kernel, l_valid=l_valid, has_carry=s_carry is not None),
        grid=(RW_HEADS // H2,),
        in_specs=in_specs,
        out_specs=[pl.BlockSpec((B, L, LANES), lambda hp: (0, 0, hp)),
                   pl.BlockSpec((None,) + s_blk, lambda hp: (l_out, hp, 0, 0, 0))],
        out_shape=[jax.ShapeDtypeStruct((B, L, RW_WIDTH), BF16),
                   jax.ShapeDtypeStruct((DEPTH, RW_HEADS, RW_HEAD, RW_HEAD, B), F32)],
        scratch_shapes=[pltpu.VMEM((l_valid, RWS_VECS, H2, RW_HEAD, B), F32),
                        pltpu.VMEM((l_valid, H2, RW_HEAD, B), F32),
                        pltpu.VMEM((l_valid, H2, RW_HEAD, B), F32)],
        input_output_aliases=aliases,
        compiler_params=_cparams(1),
        name="rwkv7_short",
    )(*args)
    return out.reshape(B * L, RW_WIDTH), sN


ATTN_SEQS_PER_STEP = 8


def _attn_kernel(q_ref, kv_ref, o_ref):
    heads = [slice(h * XA_HEAD_DIM, (h + 1) * XA_HEAD_DIM) for h in range(XA_HEADS)]
    s = [_dot_nt(q_ref[0, :, hs], _bf(kv_ref[0, :, hs])) * (XA_HEAD_DIM ** -0.5) for hs in heads]
    e = [jnp.exp(t - jnp.max(t, axis=-1, keepdims=True)) for t in s]
    p = [t / jnp.sum(t, axis=-1, keepdims=True) for t in e]
    for h, hs in enumerate(heads):
        vs = slice(XA_WIDTH + h * XA_HEAD_DIM, XA_WIDTH + (h + 1) * XA_HEAD_DIM)
        o_ref[0, :, hs] = _dot(_bf(p[h]), _bf(kv_ref[0, :, vs])).astype(o_ref.dtype)


def _attn_short_kernel(q_ref, k_ref, v_ref, o_ref, *, bt):
    L = q_ref.shape[1]
    rows, cols = XA_HEADS * L, MEM_LEN * XA_HEADS
    own = (lax.broadcasted_iota(jnp.int32, (rows, cols), 0) // L
           == lax.broadcasted_iota(jnp.int32, (rows, cols), 1) % XA_HEADS)
    heads = [slice(h * XA_HEAD_DIM, (h + 1) * XA_HEAD_DIM) for h in range(XA_HEADS)]
    qs = [jnp.concatenate([q_ref[b, :, hs] for hs in heads], axis=0) for b in range(bt)]
    s = [_dot_nt(qs[b], _bf(k_ref[b].reshape(cols, XA_HEAD_DIM))) * (XA_HEAD_DIM ** -0.5) for b in range(bt)]
    s = [jnp.where(own, t, -jnp.inf) for t in s]
    e = [jnp.exp(t - jnp.max(t, axis=-1, keepdims=True)) for t in s]
    p = [t / jnp.sum(t, axis=-1, keepdims=True) for t in e]
    for b in range(bt):
        out = _dot(_bf(p[b]), _bf(v_ref[b].reshape(cols, XA_HEAD_DIM)))
        for h, hs in enumerate(heads):
            o_ref[b, :, hs] = out[h * L:(h + 1) * L].astype(o_ref.dtype)


def cross_attention(q, mem, l, *, B, L):
    q3 = q.reshape(B, L, XA_WIDTH)
    if l is None:
        bt, tq, body, mems = 1, min(L, 512), _attn_kernel, [mem]
        mem_specs = [pl.BlockSpec((bt, MEM_LEN, 2 * XA_WIDTH), lambda b, i: (b, 0, 0))]
    else:
        bt, tq, mems = ATTN_SEQS_PER_STEP, L, list(mem)
        body = functools.partial(_attn_short_kernel, bt=bt)
        mem_specs = [pl.BlockSpec((None, bt, MEM_LEN, XA_HEADS, XA_HEAD_DIM), lambda b, i: (l, b, 0, 0, 0))] * 2
    qo_spec = pl.BlockSpec((bt, tq, XA_WIDTH), lambda b, i: (b, i, 0))
    out = pl.pallas_call(
        body,
        grid=(B // bt, L // tq),
        in_specs=[qo_spec] + mem_specs,
        out_specs=qo_spec,
        out_shape=jax.ShapeDtypeStruct((B, L, XA_WIDTH), BF16),
        compiler_params=_cparams(2),
        name="cross_attention",
    )(q3, *mems)
    return out.reshape(B * L, XA_WIDTH)


ROUTER_GROUP_LANE = MOE_EXPERTS


MOE_TB = 512
MOE_TB_GATHER = 1024
MOE_SUB = 256
MOE_TE_LONG = 1024
MOE_TE_SHORT = 512
ROUTE_GROUP_LANE = 0
ROUTE_RANK_LANE = 1
COMB_PIECES = 3


def _router_kernel(x_ref, g_ref, w_ref, b_ref, xn_ref, comb_ref, route_ref, cnt_s):
    @pl.when(pl.program_id(0) == 0)
    def _():
        cnt_s[...] = jnp.zeros_like(cnt_s)

    xn = _rms(x_ref[...], g_ref[...])
    xn_ref[...] = _bf(xn)
    z = _dot(xn, w_ref[...], HI) + b_ref[...]
    tm = z.shape[0]
    lane = lax.broadcasted_iota(jnp.int32, z.shape, 1).astype(F32)
    big = float(LANES)
    neg = -jnp.inf
    first = lambda mask: jnp.min(jnp.where(mask, lane, big), axis=-1, keepdims=True)
    is_g = jnp.logical_and(lane >= ROUTER_GROUP_LANE, lane < ROUTER_GROUP_LANE + MOE_GROUPS)
    zg = jnp.where(is_g, z, neg)
    mg = jnp.max(zg, axis=-1, keepdims=True)
    grp = first(zg == mg) - ROUTER_GROUP_LANE
    p_grp = 1.0 / jnp.sum(jnp.exp(zg - mg), axis=-1, keepdims=True)
    lo = grp * MOE_PER_GROUP
    ze = jnp.where(jnp.logical_and(lane >= lo, lane < lo + MOE_PER_GROUP), z, neg)
    t1 = jnp.max(ze, axis=-1, keepdims=True)
    i1 = first(ze == t1)
    ze2 = jnp.where(lane == i1, neg, ze)
    t2 = jnp.max(ze2, axis=-1, keepdims=True)
    i2 = first(ze2 == t2)
    e2 = jnp.exp(t2 - t1)
    g1 = p_grp / (1.0 + e2)
    comb = jnp.where(lane == i1, g1, 0.0) + jnp.where(lane == i2, g1 * e2, 0.0)
    comb_ref[...] = _split3(comb)
    onehot = jnp.where(lane == grp, 1.0, 0.0)
    r = lax.broadcasted_iota(jnp.int32, (tm, tm), 0)
    c = lax.broadcasted_iota(jnp.int32, (tm, tm), 1)
    before = _dot(_bf(jnp.where(c < r, 1.0, 0.0)), _bf(onehot)) + cnt_s[...]
    rank = jnp.sum(onehot * before, axis=-1, keepdims=True)
    cnt_s[...] = cnt_s[...] + jnp.sum(onehot, axis=0, keepdims=True)
    route_ref[...] = jnp.where(lane == ROUTE_GROUP_LANE, grp, 0.0) + jnp.where(lane == ROUTE_RANK_LANE, rank, 0.0)


def moe_router(x, g, w_r1, b_r1, w_r2, b_r2):
    T, D = x.shape
    tm = MOE_TB
    pad = LANES - MOE_EXPERTS - MOE_GROUPS
    w = jnp.concatenate([w_r2, w_r1, jnp.zeros((D, pad), F32)], axis=1)
    b = jnp.concatenate([b_r2, b_r1, jnp.zeros((pad,), F32)]).reshape(1, LANES)
    return pl.pallas_call(
        _router_kernel,
        grid=(T // tm,),
        in_specs=[pl.BlockSpec((tm, D), lambda i: (i, 0)),
                  pl.BlockSpec((1, D), lambda i: (0, 0)),
                  pl.BlockSpec((D, LANES), lambda i: (0, 0)),
                  pl.BlockSpec((1, LANES), lambda i: (0, 0))],
        out_specs=[pl.BlockSpec((tm, D), lambda i: (i, 0)),
                   pl.BlockSpec((tm, COMB_PIECES * LANES), lambda i: (i, 0)),
                   pl.BlockSpec((tm, LANES), lambda i: (i, 0))],
        out_shape=[jax.ShapeDtypeStruct((T, D), BF16), jax.ShapeDtypeStruct((T, COMB_PIECES * LANES), BF16),
                   jax.ShapeDtypeStruct((T, LANES), F32)],
        scratch_shapes=[pltpu.VMEM((1, LANES), F32)],
        compiler_params=_cparams(1),
        name="moe_router",
    )(x, g.reshape(1, D), w, b)


def _moe_plan(route, T, te):
    i32 = jnp.int32
    grp = route[:, ROUTE_GROUP_LANE].astype(i32)
    rank = route[:, ROUTE_RANK_LANE].astype(i32)
    rows = T + MOE_GROUPS * te
    n_sub = rows // MOE_SUB
    n_tiles = rows // te
    onehot = (grp[:, None] == jnp.arange(MOE_GROUPS, dtype=i32)[None]).astype(i32)
    seg_rows = (onehot.sum(axis=0) + te - 1) // te * te
    seg_end = jnp.cumsum(seg_rows)
    seg_start = seg_end - seg_rows
    dest = seg_start[grp] + rank
    group_of = lambda row: jnp.minimum(jnp.sum(row[:, None] >= seg_end[None], axis=1), MOE_GROUPS - 1).astype(i32)
    sub_row = jnp.arange(n_sub, dtype=i32) * MOE_SUB
    sub_g = group_of(sub_row)
    sub_valid = sub_row < seg_end[-1]
    r0 = sub_row - seg_start[sub_g]

    def overlap(tb):
        blk_cnt = onehot.reshape(T // tb, tb, MOE_GROUPS).sum(axis=1)
        cum_blk = jnp.concatenate([jnp.zeros((1, MOE_GROUPS), i32), jnp.cumsum(blk_cnt, axis=0)])
        lo = cum_blk[:-1][:, sub_g].T
        hi = cum_blk[1:][:, sub_g].T
        return (lo < (r0 + MOE_SUB)[:, None]) & (hi > r0[:, None]) & sub_valid[:, None]

    g_mask = overlap(MOE_TB_GATHER)
    first_col = (jnp.arange(T // MOE_TB_GATHER) == 0)[None]
    g_mask = g_mask | (first_col & ~g_mask.any(axis=1, keepdims=True))

    def items(mask, ncol, nblk):
        n_items = n_sub + MOE_GROUPS * nblk
        flat = jnp.nonzero(mask.reshape(-1), size=n_items, fill_value=-1)[0].astype(i32)
        valid = flat >= 0
        flat = jnp.where(valid, flat, jnp.max(flat))
        major, minor = flat // ncol, flat % ncol
        prev = jnp.concatenate([jnp.full((1,), -1, i32), major[:-1]])
        nxt = jnp.concatenate([major[1:], jnp.full((1,), -1, i32)])
        nvalid = jnp.concatenate([valid[1:], jnp.zeros((1,), bool)])
        first = (major != prev) & valid
        last = ((major != nxt) | ~nvalid) & valid
        return major, minor, first.astype(i32), last.astype(i32), valid.astype(i32)

    g_sub, g_blk, g_first, _, g_valid = items(g_mask, T // MOE_TB_GATHER, T // MOE_TB_GATHER)
    s_blk, s_sub, s_first, s_last, s_valid = items(overlap(MOE_TB).T, n_sub, T // MOE_TB)
    tile_row = jnp.arange(n_tiles, dtype=i32) * te
    tile_valid = tile_row < seg_end[-1]
    tile_group = group_of(jnp.where(tile_valid, tile_row, seg_end[-1] - 1))
    return dict(dest=dest, rows=rows, gather=(g_sub, g_blk, g_first, g_valid),
                scatter=(s_sub, s_blk, s_first, s_last, s_valid),
                tile_group=tile_group, tile_valid=tile_valid.astype(i32))


def _moe_gather_kernel(sub_ref, blk_ref, first_ref, valid_ref, dest_ref, xn_ref, comb_ref, xs_ref, cs_ref):
    w = pl.program_id(0)

    @pl.when(valid_ref[w] == 1)
    def _():
        rows = sub_ref[w] * MOE_SUB + lax.broadcasted_iota(jnp.int32, (MOE_SUB, MOE_TB_GATHER), 0)
        hit = _bf(jnp.where(dest_ref[0] == rows, 1.0, 0.0))
        gx = _bf(_dot(hit, xn_ref[...]))
        g3 = _dot(hit, comb_ref[...])
        gc = g3[:, :LANES] + g3[:, LANES:2 * LANES] + g3[:, 2 * LANES:]

        @pl.when(first_ref[w] == 1)
        def _():
            xs_ref[...] = gx
            cs_ref[...] = gc

        @pl.when(first_ref[w] == 0)
        def _():
            xs_ref[...] = xs_ref[...] + gx
            cs_ref[...] = cs_ref[...] + gc


def moe_gather(xn, comb, plan):
    T, D = xn.shape
    g_sub, g_blk, g_first, g_valid = plan["gather"]
    rows = plan["rows"]
    tb = MOE_TB_GATHER
    dest3 = plan["dest"].reshape(T // tb, 1, tb)
    grid_spec = pltpu.PrefetchScalarGridSpec(
        num_scalar_prefetch=4,
        grid=(g_sub.shape[0],),
        in_specs=[pl.BlockSpec((1, 1, tb), lambda w, s, b, f, v: (b[w], 0, 0)),
                  pl.BlockSpec((tb, D), lambda w, s, b, f, v: (b[w], 0)),
                  pl.BlockSpec((tb, COMB_PIECES * LANES), lambda w, s, b, f, v: (b[w], 0))],
        out_specs=[pl.BlockSpec((MOE_SUB, D), lambda w, s, b, f, v: (s[w], 0)),
                   pl.BlockSpec((MOE_SUB, LANES), lambda w, s, b, f, v: (s[w], 0))])
    return pl.pallas_call(
        _moe_gather_kernel,
        grid_spec=grid_spec,
        out_shape=[jax.ShapeDtypeStruct((rows, D), BF16), jax.ShapeDtypeStruct((rows, LANES), F32)],
        compiler_params=_cparams(1),
        name="moe_gather",
    )(g_sub, g_blk, g_first, g_valid, dest3, xn, comb)


MOE_EXPERTS_PER_STEP = 2


def _moe_group_experts_kernel(tg_ref, tv_ref, xs_ref, cs_ref, wg_ref, wu_ref, wd_ref, y_ref, acc):
    i = pl.program_id(0)
    e = pl.program_id(1)

    @pl.when(e == 0)
    def _():
        acc[...] = jnp.zeros_like(acc)

    @pl.when(tv_ref[i] == 1)
    def _():
        xs = xs_ref[...]
        cs = cs_ref[...]
        lane = lax.broadcasted_iota(jnp.int32, cs.shape, 1)
        first = tg_ref[i] * MOE_PER_GROUP + e * MOE_EXPERTS_PER_STEP
        hids = []
        for j in range(MOE_EXPERTS_PER_STEP):
            ce = jnp.sum(jnp.where(lane == first + j, cs, 0.0), axis=-1, keepdims=True)
            hids.append(_bf(jax.nn.silu(_dot(xs, _bf(wg_ref[j]))) * _dot(xs, _bf(wu_ref[j])) * ce))
        hid = jnp.concatenate(hids, axis=1)
        acc[...] += _dot(hid, _bf(wd_ref[...].reshape(MOE_EXPERTS_PER_STEP * MOE_HIDDEN, wd_ref.shape[-1])))

    @pl.when(e == pl.num_programs(1) - 1)
    def _():
        y_ref[...] = _bf(acc[...])


def moe_group_experts(xs, cs, plan, wg, wu, wd, l, te):
    rows, D = xs.shape
    eps = MOE_EXPERTS_PER_STEP
    steps = MOE_PER_GROUP // eps

    def expert(i, e, tg, tv):
        return (l, tg[i] * steps + jnp.where(tv[i] == 1, e, steps - 1), 0, 0)

    w_in = pl.BlockSpec((None, eps, D, MOE_HIDDEN), expert)
    grid_spec = pltpu.PrefetchScalarGridSpec(
        num_scalar_prefetch=2,
        grid=(rows // te, steps),
        in_specs=[pl.BlockSpec((te, D), lambda i, e, tg, tv: (i, 0)),
                  pl.BlockSpec((te, LANES), lambda i, e, tg, tv: (i, 0)),
                  w_in, w_in,
                  pl.BlockSpec((None, eps, MOE_HIDDEN, D), expert)],
        out_specs=pl.BlockSpec((te, D), lambda i, e, tg, tv: (i, 0)),
        scratch_shapes=[pltpu.VMEM((te, D), F32)])
    return pl.pallas_call(
        _moe_group_experts_kernel,
        grid_spec=grid_spec,
        out_shape=jax.ShapeDtypeStruct((rows, D), BF16),
        compiler_params=_cparams(2),
        name="moe_group_experts",
    )(plan["tile_group"], plan["tile_valid"], xs, cs, wg, wu, wd)


def _moe_scatter_kernel(sub_ref, blk_ref, first_ref, last_ref, valid_ref, dest_ref, y_ref, x_ref, gf_ref, o_ref,
                        *, final_norm):
    w = pl.program_id(0)

    @pl.when(valid_ref[w] == 1)
    def _():
        cols = sub_ref[w] * MOE_SUB + lax.broadcasted_iota(jnp.int32, (MOE_TB, MOE_SUB), 1)
        pick = _bf(jnp.where(dest_ref[...] == cols, 1.0, 0.0))
        upd = _dot(pick, y_ref[...])

        @pl.when(first_ref[w] == 1)
        def _():
            o_ref[...] = x_ref[...] + upd

        @pl.when(first_ref[w] == 0)
        def _():
            o_ref[...] += upd

        if final_norm:
            @pl.when(last_ref[w] == 1)
            def _():
                o_ref[...] = _rms(o_ref[...], gf_ref[...])


def moe_scatter(ys, x, plan, g_final, *, final_norm):
    T, D = x.shape
    s_sub, s_blk, s_first, s_last, s_valid = plan["scatter"]
    dest_col = plan["dest"].reshape(T, 1)
    im = lambda f: (lambda w, s, b, fi, la, v: f(w, s, b))
    grid_spec = pltpu.PrefetchScalarGridSpec(
        num_scalar_prefetch=5,
        grid=(s_sub.shape[0],),
        in_specs=[pl.BlockSpec((MOE_TB, 1), im(lambda w, s, b: (b[w], 0))),
                  pl.BlockSpec((MOE_SUB, D), im(lambda w, s, b: (s[w], 0))),
                  pl.BlockSpec((MOE_TB, D), im(lambda w, s, b: (b[w], 0))),
                  pl.BlockSpec((1, D), im(lambda w, s, b: (0, 0)))],
        out_specs=pl.BlockSpec((MOE_TB, D), im(lambda w, s, b: (b[w], 0))))
    return pl.pallas_call(
        functools.partial(_moe_scatter_kernel, final_norm=final_norm),
        grid_spec=grid_spec,
        out_shape=jax.ShapeDtypeStruct((T, D), F32),
        compiler_params=_cparams(1),
        name="moe_scatter",
    )(s_sub, s_blk, s_first, s_last, s_valid, dest_col, ys, x, g_final.reshape(1, D))


def hmoe_block(x, p, l, *, final_norm):
    T = x.shape[0]
    te = MOE_TE_LONG if T >= 4 * MOE_TE_LONG else MOE_TE_SHORT
    xn, comb, route = moe_router(x, p["g_moe"][l], p["moe_wr1"][l], p["moe_br1"][l], p["moe_wr2"][l],
                                 p["moe_br2"][l])
    plan = _moe_plan(route, T, te)
    xs, cs = moe_gather(xn, comb, plan)
    ys = moe_group_experts(xs, cs, plan, p["moe_wg"], p["moe_wu"], p["moe_wd"], l, te)
    return moe_scatter(ys, x, plan, p["g_final"], final_norm=final_norm)


def _layer_weights(l, p):
    w_in = p["w_in"][l]
    c0 = RW_COLS
    c1 = c0 + ML_MAIN
    c2 = c1 + 2 * ML_HEADS
    c3 = c2 + POOL_WIDTH
    w_pl = jnp.concatenate([w_in[:, c2:c3], w_in[:, c1:c2], jnp.zeros((D_MODEL, LANES - 2 * ML_HEADS), F32)], axis=1)
    return dict(
        w_rw=_bf(w_in[:, :c0]), w_ml=_bf(w_in[:, c0:c1]), w_pl=_bf(w_pl), w_gate=_bf(w_in[:, c3:]),
        w_up_rwkv=_bf(p["w_up_rwkv"][l]), w_up_mlstm=_bf(p["w_up_mlstm"][l]), w_up_pool=_bf(p["w_up_pool"][l]),
        w_out=_bf(p["w_out"][l]), pool_w=_bf(p["pool_w"][l]),
        xa_wq=_bf(p["xa_wq"][l]), xa_wo=_bf(p["xa_wo"][l]),
        xa_wkv=_bf(jnp.concatenate([p["xa_wk"][l], p["xa_wv"][l]], axis=1)),
    )


def _trunk_layer(x, mem, mem_l, st, carry, *, l, sl, p, lw, B, L, l_valid, start, final_norm):
    g_mix = p["g_mix"][l]
    u_rw = norm_matmul(x, g_mix, lw["w_rw"], tn=TN_RWKV, tm=TM_WHOLE_SEGMENT, name="in_rwkv")
    u_ml = norm_matmul(x, g_mix, lw["w_ml"], tn=TN_MLSTM, tm=TM_WHOLE_SEGMENT, name="in_mlstm")
    u_pl = norm_matmul(x, g_mix, lw["w_pl"], tn=POOL_WIDTH + LANES, name="in_pool")
    gates = norm_matmul(x, g_mix, lw["w_gate"], tn=TN_GATES, tm=TM_WHOLE_SEGMENT, out_dtype=BF16, act="sigmoid",
                        name="in_gates")

    rwkv = rwkv7_short if L < CHUNK_ROWS else rwkv7_mix
    o_rw, rw_s = rwkv(u_rw, st["rw_shift"], st["rw_s"], sl, l, carry["rw_s"], p["rw_mu"][l], p["rw_w0"][l],
                      p["rw_w2"][l], p["rw_a0"][l], p["rw_a2"][l], p["rw_k_k"][l], p["rw_k_a"][l],
                      p["rw_r_k"][l], p["rw_gn_w"][l], p["rw_gn_b"][l], B=B, L=L, l_valid=l_valid)
    rw_shift = u_rw.reshape(B, L, RW_COLS)[:, l_valid - 1]
    o_ml, ml_c, ml_n, ml_m = mlstm_mix(u_ml, u_pl, st["ml_c"], st["ml_n"], st["ml_m"], sl, l, carry["ml_c"],
                                       p["ml_b_i"][l], p["ml_b_f"][l], p["ml_gn_w"][l], B=B, L=L, l_valid=l_valid)
    o_pl, pool_buf = pool_mix(u_pl, st["pool"], sl, lw["pool_w"], p["pool_scale"][l], B=B, L=L, l_valid=l_valid,
                              start=start)
    merged = merge_branches(o_rw, o_ml, o_pl, gates, lw["w_up_rwkv"], lw["w_up_mlstm"], lw["w_up_pool"])
    x = matmul_residual(merged, lw["w_out"], x)

    q = norm_matmul(x, p["g_xa"][l], lw["xa_wq"], tn=XA_WIDTH, out_dtype=BF16, name="xa_q")
    att = cross_attention(q, mem, mem_l, B=B, L=L)
    x = matmul_residual(att, lw["xa_wo"], x)

    x = hmoe_block(x, p, l, final_norm=final_norm)
    return x, dict(rw_s=rw_s, ml_c=ml_c), (rw_shift, ml_n, ml_m, pool_buf)


def kernel(x_prompt, x_sample, cache_mem_k, cache_mem_v, state_rwkv_s, state_rwkv_shift, state_mlstm_c, state_mlstm_n, state_mlstm_m, state_pool, mem_prompt, g_mix, w_in, rw_mu, rw_w0, rw_w2, rw_a0, rw_a2, rw_k_k, rw_k_a, rw_r_k, rw_gn_w, rw_gn_b, ml_b_i, ml_b_f, ml_gn_w, pool_w, pool_scale, w_up_rwkv, w_up_mlstm, w_up_pool, w_out, g_xa, g_mem, xa_wq, xa_wk, xa_wv, xa_wo, g_moe, moe_wr1, moe_br1, moe_wr2, moe_br2, moe_wg, moe_wu, moe_wd, g_final):
    p = dict(g_mix=g_mix, w_in=w_in, rw_mu=rw_mu, rw_w0=rw_w0, rw_w2=rw_w2, rw_a0=rw_a0, rw_a2=rw_a2, rw_k_k=rw_k_k,
             rw_k_a=rw_k_a, rw_r_k=rw_r_k, rw_gn_w=rw_gn_w, rw_gn_b=rw_gn_b, ml_b_i=ml_b_i, ml_b_f=ml_b_f,
             ml_gn_w=ml_gn_w, pool_w=pool_w, pool_scale=pool_scale, w_up_rwkv=w_up_rwkv, w_up_mlstm=w_up_mlstm,
             w_up_pool=w_up_pool, w_out=w_out, g_xa=g_xa, g_mem=g_mem, xa_wq=xa_wq, xa_wk=xa_wk, xa_wv=xa_wv,
             xa_wo=xa_wo, g_moe=g_moe, moe_wr1=moe_wr1, moe_br1=moe_br1, moe_wr2=moe_wr2, moe_br2=moe_br2,
             moe_wg=moe_wg, moe_wu=moe_wu, moe_wd=moe_wd, g_final=g_final)
    Bp, Lp, D = x_prompt.shape
    Bs, Ls, _ = x_sample.shape
    Ls_pad = -(-Ls // SUBLANES) * SUBLANES
    yp = x_prompt.reshape(Bp * Lp, D)
    ys = jnp.pad(x_sample, ((0, 0), (0, Ls_pad - Ls), (0, 0))).reshape(Bs * Ls_pad, D)
    zeros = lambda *s: jnp.zeros((1,) + s, F32)
    st_p = dict(rw_shift=zeros(Bp, RW_COLS), rw_s=zeros(Bp, RW_HEADS, RW_HEAD, RW_HEAD),
                ml_c=zeros(Bp, ML_HEADS, ML_DQK, ML_DV), ml_n=zeros(Bp, ML_HEADS, ML_DQK), ml_m=zeros(Bp, ML_HEADS),
                pool=zeros(Bp, POOL_BUF, POOL_WIDTH))
    batch_minor = (0, 2, 3, 4, 1)
    st_s = dict(rw_shift=state_rwkv_shift, rw_s=jnp.transpose(state_rwkv_s, batch_minor), ml_c=state_mlstm_c,
                ml_n=state_mlstm_n, ml_m=state_mlstm_m, pool=state_pool)
    carry_p = carry_s = dict(rw_s=None, ml_c=None)
    small_p = [[] for _ in range(6)]
    small_s = [[] for _ in range(4)]
    for l in range(DEPTH):
        lw = _layer_weights(l, p)
        final = l == DEPTH - 1
        kv = norm_matmul(mem_prompt.reshape(Bp * MEM_LEN, D), g_mem[l], lw["xa_wkv"], tn=2 * XA_WIDTH,
                         name="memory_kv")
        kv3 = kv.reshape(Bp, MEM_LEN, 2 * XA_WIDTH)
        yp, carry_p, small = _trunk_layer(yp, kv3, None, st_p, carry_p, l=l, sl=0, p=p, lw=lw, B=Bp, L=Lp,
                                          l_valid=Lp, start=0, final_norm=final)
        mk = kv[:, :XA_WIDTH].reshape(Bp, MEM_LEN, XA_HEADS, XA_HEAD_DIM)
        mv = kv[:, XA_WIDTH:].reshape(Bp, MEM_LEN, XA_HEADS, XA_HEAD_DIM)
        for acc, t in zip(small_p, small + (mk, mv)):
            acc.append(t)
        ys, carry_s, small = _trunk_layer(ys, (cache_mem_k, cache_mem_v), l, st_s, carry_s, l=l, sl=l, p=p, lw=lw,
                                          B=Bs, L=Ls_pad, l_valid=Ls, start=PAST_LEN, final_norm=final)
        for acc, t in zip(small_s, small):
            acc.append(t)
    y_prompt = yp.reshape(Bp, Lp, D)
    y_sample = ys.reshape(Bs, Ls_pad, D)[:, :Ls]
    p_sh, p_n, p_m, p_pool, p_mk, p_mv = [jnp.stack(t) for t in small_p]
    s_sh, s_n, s_m, s_pool = [jnp.stack(t) for t in small_s]
    return (y_prompt, y_sample, carry_p["rw_s"], p_sh, carry_p["ml_c"], p_n, p_m, p_pool, p_mk, p_mv,
            jnp.transpose(carry_s["rw_s"], (0, 4, 1, 2, 3)), s_sh, carry_s["ml_c"], s_n, s_m, s_pool)
```

```python
import functools
import math

import jax
import jax.numpy as jnp
from jax import lax
from jax.experimental import pallas as pl
from jax.experimental.pallas import tpu as pltpu

F32 = jnp.float32
BF16 = jnp.bfloat16
HI = lax.Precision.HIGHEST

D_MODEL = 2048
DEPTH = 2
PAST_LEN = 16384
RW_HEAD = 64
RW_WIDTH = 1024
RW_HEADS = 16
RW_RANK = 64
RW_COLS = 3 * RW_WIDTH + 2 * RW_RANK
GN_EPS = 64e-5
ML_HEADS = 4
ML_DQK = 128
ML_DV = 256
ML_QK_WIDTH = 512
ML_V_WIDTH = 1024
ML_MAIN = 2 * ML_QK_WIDTH + 2 * ML_V_WIDTH
POOL_WIDTH = 1024
POOL_WINDOWS = (2, 4, 8, 16)
POOL_GW = 256
POOL_BUF = 15
POOL_HIST = 16
POOL_TAIL = 24
MEM_LEN = 256
XA_HEADS = 4
XA_HEAD_DIM = 128
XA_WIDTH = 512
MOE_GROUPS = 4
MOE_PER_GROUP = 8
MOE_EXPERTS = 32
MOE_HIDDEN = 256
RMS_EPS = 1e-6
LANES = 128
SUBLANES = 8
VMEM_LIMIT = 56 * 1024 * 1024
ROW_TILE = 1024
CHUNK_ROWS = 64
TN_RWKV = RW_COLS
TN_MLSTM = ML_MAIN
TN_GATES = 3 * D_MODEL
TM_WHOLE_SEGMENT = ROW_TILE // 2


def _cparams(n_axes):
    return pltpu.CompilerParams(dimension_semantics=("arbitrary",) * n_axes, vmem_limit_bytes=VMEM_LIMIT)


def _dot(a, b, precision=None):
    return jnp.dot(a, b, preferred_element_type=F32, precision=precision)


def _dot_nt(a, b, precision=None):
    return lax.dot_general(a, b, (((1,), (1,)), ((), ())), preferred_element_type=F32, precision=precision)


def _dot_tn(a, b, precision=None):
    return lax.dot_general(a, b, (((0,), (0,)), ((), ())), preferred_element_type=F32, precision=precision)


def _bf(x):
    return x.astype(BF16)


def _split3(x):
    hi = _bf(x)
    rest = x - hi.astype(F32)
    mid = _bf(rest)
    return jnp.concatenate([hi, mid, _bf(rest - mid.astype(F32))], axis=1)


def _rms(x, g):
    return x * lax.rsqrt(jnp.mean(x * x, axis=-1, keepdims=True) + RMS_EPS) * g


def _log_sigmoid(x):
    return jnp.minimum(x, 0.0) - jnp.log1p(jnp.exp(-jnp.abs(x)))


def _softplus(x):
    return jnp.maximum(x, 0.0) + jnp.log1p(jnp.exp(-jnp.abs(x)))


def _seq_prefix_mask(n, c):
    r = lax.broadcasted_iota(jnp.int32, (n, n), 0)
    q = lax.broadcasted_iota(jnp.int32, (n, n), 1)
    return jnp.logical_and(q <= r, q // c == r // c)


def _layer_block(l, blk, idx):
    return pl.BlockSpec((None,) + tuple(blk), lambda b, i: (l,) + tuple(idx(b, i)))


def _norm_matmul_kernel(x_ref, g_ref, w_ref, o_ref, xn_ref, *, act):
    @pl.when(pl.program_id(1) == 0)
    def _():
        xn_ref[...] = _bf(_rms(x_ref[...], g_ref[...]))

    acc = _dot(xn_ref[...], w_ref[...])
    if act == "sigmoid":
        acc = jax.nn.sigmoid(acc)
    o_ref[...] = acc.astype(o_ref.dtype)


def norm_matmul(x, g, w, *, tn, name, tm=ROW_TILE, out_dtype=F32, act=None):
    T, D = x.shape
    N = w.shape[1]
    tm = min(T, tm)
    w_mode = pl.Buffered(1) if tn == N else None
    return pl.pallas_call(
        functools.partial(_norm_matmul_kernel, act=act),
        grid=(T // tm, N // tn),
        in_specs=[pl.BlockSpec((tm, D), lambda i, j: (i, 0)),
                  pl.BlockSpec((1, D), lambda i, j: (0, 0)),
                  pl.BlockSpec((D, tn), lambda i, j: (0, j), pipeline_mode=w_mode)],
        out_specs=pl.BlockSpec((tm, tn), lambda i, j: (i, j)),
        out_shape=jax.ShapeDtypeStruct((T, N), out_dtype),
        scratch_shapes=[pltpu.VMEM((tm, D), BF16)],
        compiler_params=_cparams(2),
        name=name,
    )(x, g.reshape(1, D), w)


def _matmul_residual_kernel(a_ref, w_ref, r_ref, o_ref):
    o_ref[...] = r_ref[...] + _dot(a_ref[...], w_ref[...])


def matmul_residual(a, w, res):
    T, K = a.shape
    tn = w.shape[1]
    tm = min(T, TM_WHOLE_SEGMENT)
    return pl.pallas_call(
        _matmul_residual_kernel,
        grid=(T // tm, 1),
        in_specs=[pl.BlockSpec((tm, K), lambda i, j: (i, 0)),
                  pl.BlockSpec((K, tn), lambda i, j: (0, j)),
                  pl.BlockSpec((tm, tn), lambda i, j: (i, j))],
        out_specs=pl.BlockSpec((tm, tn), lambda i, j: (i, j)),
        out_shape=jax.ShapeDtypeStruct((T, tn), F32),
        compiler_params=_cparams(2),
        name="matmul_residual",
    )(a, w, res)


def _merge_kernel(orw_ref, oml_ref, opl_ref, g0_ref, g1_ref, g2_ref, wr_ref, wm_ref, wp_ref, o_ref):
    m = (g0_ref[...].astype(F32) * _dot(orw_ref[...], wr_ref[...])
         + g1_ref[...].astype(F32) * _dot(oml_ref[...], wm_ref[...])
         + g2_ref[...].astype(F32) * _dot(opl_ref[...], wp_ref[...]))
    o_ref[...] = _bf(m)


def merge_branches(o_rw, o_ml, o_pl, gates, w_rw, w_ml, w_pl, *, tn=1024):
    T, K = o_rw.shape
    tm = min(T, ROW_TILE)
    nj = D_MODEL // tn
    act = pl.BlockSpec((tm, K), lambda i, j: (i, 0))
    wsp = pl.BlockSpec((K, tn), lambda i, j: (0, j))
    gate = lambda b: pl.BlockSpec((tm, tn), lambda i, j, b=b: (i, b * nj + j))
    return pl.pallas_call(
        _merge_kernel,
        grid=(T // tm, nj),
        in_specs=[act, act, act, gate(0), gate(1), gate(2), wsp, wsp, wsp],
        out_specs=pl.BlockSpec((tm, tn), lambda i, j: (i, j)),
        out_shape=jax.ShapeDtypeStruct((T, D_MODEL), BF16),
        compiler_params=_cparams(2),
        name="merge_branches",
    )(o_rw, o_ml, o_pl, gates, gates, gates, w_rw, w_ml, w_pl)


def _pool_kernel(hist_ref, u_ref, w_ref, sc_ref, o_ref, tail_ref, e_ref, *, tl, bt, start):
    li = pl.program_id(1)

    @pl.when(li == 0)
    def _():
        e_ref[:, 0:POOL_HIST, :] = hist_ref[...]

    pos = start + li * tl + lax.broadcasted_iota(jnp.int32, (tl, 1), 0)
    for b in range(bt):
        u = u_ref[b]
        e_ref[b, POOL_HIST:, :] = u
        e = e_ref[b]
        s2 = e + pltpu.roll(e, 1, 0)
        s4 = s2[:, POOL_GW:] + pltpu.roll(s2[:, POOL_GW:], 2, 0)
        s8 = s4[:, POOL_GW:] + pltpu.roll(s4[:, POOL_GW:], 4, 0)
        s16 = s8[:, POOL_GW:] + pltpu.roll(s8[:, POOL_GW:], 8, 0)
        sums = (s2[:, :POOL_GW], s4[:, :POOL_GW], s8[:, :POOL_GW], s16)
        for g, win in enumerate(POOL_WINDOWS):
            cols = slice(g * POOL_GW, (g + 1) * POOL_GW)
            cnt = jnp.minimum(win, pos + 1).astype(F32)
            d = sums[g][POOL_HIST:, :] / cnt - u[:, cols]
            out = _dot(_bf(d), w_ref[g]) * sc_ref[:, cols]
            o_ref[b, :, cols] = out.astype(o_ref.dtype)
        tail_ref[b] = e[tl + POOL_HIST - POOL_TAIL:, :]
        e_ref[b, 0:POOL_HIST, :] = e[tl:, :]


def pool_mix(u_pl, buf, l, w_grp, scale, *, B, L, l_valid, start):
    tl = min(L, 256)
    bt = max(CHUNK_ROWS // L, 1)
    hist = jnp.pad(buf[l], ((0, 0), (POOL_HIST - POOL_BUF, 0), (0, 0)))
    u3 = u_pl.reshape(B, L, u_pl.shape[-1])
    out, tail = pl.pallas_call(
        functools.partial(_pool_kernel, tl=tl, bt=bt, start=start),
        grid=(B // bt, L // tl),
        in_specs=[pl.BlockSpec((bt, POOL_HIST, POOL_WIDTH), lambda b, i: (b, 0, 0)),
                  pl.BlockSpec((bt, tl, POOL_WIDTH), lambda b, i: (b, i, 0)),
                  pl.BlockSpec((4, POOL_GW, POOL_GW), lambda b, i: (0, 0, 0)),
                  pl.BlockSpec((1, POOL_WIDTH), lambda b, i: (0, 0))],
        out_specs=[pl.BlockSpec((bt, tl, POOL_WIDTH), lambda b, i: (b, i, 0)),
                   pl.BlockSpec((bt, POOL_TAIL, POOL_WIDTH), lambda b, i: (b, 0, 0))],
        out_shape=[jax.ShapeDtypeStruct((B, L, POOL_WIDTH), BF16),
                   jax.ShapeDtypeStruct((B, POOL_TAIL, POOL_WIDTH), F32)],
        scratch_shapes=[pltpu.VMEM((bt, tl + POOL_HIST, POOL_WIDTH), F32)],
        compiler_params=_cparams(2),
        name="pool_mix",
    )(hist, u3, w_grp, scale.reshape(1, POOL_WIDTH))
    pad = L - l_valid
    new_buf = tail[:, POOL_TAIL - pad - POOL_BUF:POOL_TAIL - pad, :]
    return out.reshape(B * L, POOL_WIDTH), new_buf


def _mlstm_kernel(*refs, c, bt, l_valid, has_carry):
    u_ref, gif_ref, c0_ref, n0_ref, m0_ref, bi_ref, bf_ref, gnw_ref = refs[:8]
    o_ref, cN_ref, nN_ref, mN_ref, c_s, n_s, m_s = refs[8 + has_carry:]
    ci = pl.program_id(1)

    @pl.when(ci == 0)
    def _():
        c_s[...] = c0_ref[...]
        n_s[...] = n0_ref[...]
        m_s[...] = m0_ref[...]

    N = bt * c
    gif = gif_ref[...].reshape(N, LANES)
    tok = lax.broadcasted_iota(jnp.int32, (N, 1), 0) % c
    valid = tok < l_valid
    ig_all = jnp.where(valid, gif + bi_ref[...], -jnp.inf)
    lf_all = jnp.where(valid, _log_sigmoid(gif + bf_ref[...]), 0.0)
    bcum_all = _dot(_seq_prefix_mask(N, c).astype(F32), lf_all, HI)
    ig_t = ig_all.T
    bcum_t = bcum_all.T
    ri = lax.broadcasted_iota(jnp.int32, (c, c), 0)
    causal = lax.broadcasted_iota(jnp.int32, (c, c), 1) <= ri
    st = []
    for b in range(bt):
        rs = slice(b * c, (b + 1) * c)
        for h in range(ML_HEADS):
            qs = slice(h * ML_DQK, (h + 1) * ML_DQK)
            ks = slice(ML_QK_WIDTH + h * ML_DQK, ML_QK_WIDTH + (h + 1) * ML_DQK)
            vs = slice(2 * ML_QK_WIDTH + h * ML_DV, 2 * ML_QK_WIDTH + (h + 1) * ML_DV)
            os_ = slice(2 * ML_QK_WIDTH + ML_V_WIDTH + h * ML_DV, 2 * ML_QK_WIDTH + ML_V_WIDTH + (h + 1) * ML_DV)
            i_c = ig_all[rs, h:h + 1]
            b_c = bcum_all[rs, ML_HEADS + h:ML_HEADS + h + 1]
            i_r = ig_t[h:h + 1, rs]
            b_r = bcum_t[ML_HEADS + h:ML_HEADS + h + 1, rs]
            m_prev = m_s[b, h]
            dlog = jnp.where(causal, b_c - b_r + i_r, -jnp.inf)
            inter = b_c + m_prev
            m_t = jnp.maximum(inter, jnp.max(dlog, axis=-1, keepdims=True))
            b_last = b_c[c - 1:c, :]
            s_log = b_last - b_c + i_c
            m_new = jnp.maximum(b_last + m_prev, jnp.max(s_log, axis=0, keepdims=True))
            q = u_ref[b, :, qs]
            k = u_ref[b, :, ks] * (ML_DQK ** -0.5)
            st.append(dict(b=b, h=h, hs=slice(h * ML_DV, (h + 1) * ML_DV), os=os_, q=q, k=k, qb=_bf(q), vb=_bf(u_ref[b, :, vs]),
                           dlog=dlog, m_t=m_t, w_prev=jnp.exp(inter - m_t), m_new=m_new,
                           kw=k * jnp.exp(s_log - m_new), wp=jnp.exp(b_last + m_prev - m_new)))
    for s in st:
        s["wts"] = jnp.exp(s["dlog"] - s["m_t"]) * _dot_nt(s["qb"], _bf(s["k"]))
    for s in st:
        b, h = s["b"], s["h"]
        c_prev = c_s[b, h]
        n_prev = n_s[b, h]
        num = s["w_prev"] * _dot(s["qb"], _bf(c_prev)) + _dot(_bf(s["wts"]), s["vb"])
        den = (s["w_prev"] * jnp.sum(s["q"] * n_prev, axis=-1, keepdims=True)
               + jnp.sum(s["wts"], axis=-1, keepdims=True))
        s["hh"] = num / jnp.maximum(jnp.abs(den), jnp.exp(-s["m_t"]))
        c_s[b, h] = s["wp"] * c_prev + _dot_tn(_bf(s["kw"]), s["vb"])
        n_s[b, h] = s["wp"] * n_prev + jnp.sum(s["kw"], axis=0, keepdims=True)
        m_s[b, h] = s["m_new"]
    for s in st:
        hh = s["hh"]
        hn = hh * lax.rsqrt(jnp.mean(hh * hh, axis=-1, keepdims=True) + RMS_EPS) * gnw_ref[:, s["hs"]]
        o_ref[s["b"], :, s["hs"]] =(hn * jax.nn.sigmoid(u_ref[s["b"], :, s["os"]])).astype(o_ref.dtype)

    @pl.when(ci == pl.num_programs(1) - 1)
    def _():
        cN_ref[...] = c_s[...]
        nN_ref[...] = n_s[...]
        mN_ref[...] = m_s[...]


def mlstm_mix(u_ml, u_pl, c0, n0, m0, l, l_out, c_carry, b_i, b_f, gn_w, *, B, L, l_valid):
    c = min(L, CHUNK_ROWS)
    bt = CHUNK_ROWS // c
    u3 = u_ml.reshape(B, L, ML_MAIN)
    g3 = u_pl.reshape(B, L, POOL_WIDTH + LANES)
    zeros = jnp.zeros((LANES - 2 * ML_HEADS,), F32)
    bi = jnp.concatenate([b_i, jnp.zeros((ML_HEADS,), F32), zeros]).reshape(1, LANES)
    bf = jnp.concatenate([jnp.zeros((ML_HEADS,), F32), b_f, zeros]).reshape(1, LANES)
    nl = n0.shape[0]
    c_blk, n_blk, m_blk = (bt, ML_HEADS, ML_DQK, ML_DV), (bt, ML_HEADS, 1, ML_DQK), (bt, ML_HEADS, 1, 1)
    at_b = lambda b, i: (b, 0, 0, 0)
    in_specs = [pl.BlockSpec((bt, c, ML_MAIN), lambda b, i: (b, i, 0)),
                pl.BlockSpec((bt, c, LANES), lambda b, i: (b, i, POOL_WIDTH // LANES)),
                _layer_block(l, c_blk, at_b), _layer_block(l, n_blk, at_b), _layer_block(l, m_blk, at_b),
                pl.BlockSpec((1, LANES), lambda b, i: (0, 0)),
                pl.BlockSpec((1, LANES), lambda b, i: (0, 0)),
                pl.BlockSpec((1, ML_V_WIDTH), lambda b, i: (0, 0))]
    args = [u3, g3, c0, n0.reshape(nl, B, ML_HEADS, 1, ML_DQK), m0.reshape(nl, B, ML_HEADS, 1, 1), bi, bf,
            gn_w.reshape(1, ML_V_WIDTH)]
    aliases = {}
    if c_carry is not None:
        in_specs.append(pl.BlockSpec(memory_space=pl.ANY))
        args.append(c_carry)
        aliases = {len(args) - 1: 1}
    out, cN, nN, mN = pl.pallas_call(
        functools.partial(_mlstm_kernel, c=c, bt=bt, l_valid=l_valid, has_carry=c_carry is not None),
        grid=(B // bt, L // c),
        in_specs=in_specs,
        out_specs=[pl.BlockSpec((bt, c, ML_V_WIDTH), lambda b, i: (b, i, 0)),
                   _layer_block(l_out, c_blk, at_b),
                   pl.BlockSpec(n_blk, at_b), pl.BlockSpec(m_blk, at_b)],
        out_shape=[jax.ShapeDtypeStruct((B, L, ML_V_WIDTH), BF16),
                   jax.ShapeDtypeStruct((DEPTH, B, ML_HEADS, ML_DQK, ML_DV), F32),
                   jax.ShapeDtypeStruct((B, ML_HEADS, 1, ML_DQK), F32),
                   jax.ShapeDtypeStruct((B, ML_HEADS, 1, 1), F32)],
        scratch_shapes=[pltpu.VMEM(c_blk, F32), pltpu.VMEM(n_blk, F32), pltpu.VMEM(m_blk, F32)],
        input_output_aliases=aliases,
        compiler_params=_cparams(2),
        name="mlstm_mix",
    )(*args)
    return (out.reshape(B * L, ML_V_WIDTH), cN, nN.reshape(B, ML_HEADS, ML_DQK), mN.reshape(B, ML_HEADS))


RW_CHAIN_GROUP = 16

def _rwkv_chain_group(chains, seq, s_s, o_ref, rk_ref, gnw_ref, gnb_ref, *, C):
    rowi = lax.broadcasted_iota(jnp.int32, (C, C), 0)
    coli = lax.broadcasted_iota(jnp.int32, (C, C), 1)
    upper = rowi < coli
    col2 = lax.broadcasted_iota(jnp.int32, (C, 2 * C), 1)
    incl2 = jnp.where(col2 >= C, col2 - C, col2) <= lax.broadcasted_iota(jnp.int32, (C, 2 * C), 0)
    n_sq = max(int(math.log2(C)), 1)
    st = []
    for b, h in chains:
        rs = slice(b * C, (b + 1) * C)
        sl = slice(h * RW_HEAD, (h + 1) * RW_HEAD)
        cut = lambda t, rs=rs, sl=sl: t[rs, sl]
        kk_h = cut(seq["kkv"])
        nrm = jnp.sqrt(jnp.sum(kk_h * kk_h, axis=-1, keepdims=True))
        kap = jnp.where(seq["valid"][rs], kk_h / jnp.maximum(nrm, 1e-12), 0.0)
        k_h, v_h, r_h = cut(seq["kmod"]), cut(seq["v"]), cut(seq["r"])
        gi = cut(seq["gi"])
        b_t = kap * cut(seq["a"]) * gi
        k_t = k_h * gi
        st.append(dict(b=b, h=h, sl=sl, k_h=k_h, v_h=v_h, r_h=r_h,
                       a_t=_bf(-kap * cut(seq["gp"])), r_t=_bf(r_h * cut(seq["g"])),
                       bk=jnp.concatenate([b_t, k_t], axis=0),
                       ge=seq["g"][(b + 1) * C - 1:(b + 1) * C, sl], s0=s_s[b, h]))
    for c in st:
        bkb = _bf(c["bk"])
        mt = _dot_nt(bkb, c["a_t"])
        c["pt"] = jnp.where(upper, mt[:C], 0.0)
        c["akt"] = _bf(jnp.where(upper, mt[C:], 0.0))
        c["a_r"] = jnp.where(incl2, _dot_nt(c["r_t"], bkb), 0.0)
        c["s0b"] = _bf(c["s0"])
        c["vb"] = _bf(c["v_h"])
    for c in st:
        c["xt"] = _dot_nt(c["s0b"], c["a_t"]) + _dot_tn(c["vb"], c["akt"])
    for _ in range(n_sq - 1):
        for c in st:
            z = _dot(_bf(jnp.concatenate([c["pt"], c["xt"]], axis=0)), _bf(c["pt"]))
            c["pt"] = z[:C]
            c["xt"] = c["xt"] + z[C:]
    for c in st:
        c["ut"] = _bf(c["xt"] + _dot(_bf(c["xt"]), _bf(c["pt"])))
    for c in st:
        a_r = c["a_r"]
        c["y"] = (_dot_nt(c["r_t"], c["s0b"]) + _dot_nt(_bf(a_r[:, :C]), c["ut"])
                  + _dot(_bf(a_r[:, C:]), c["vb"]))
        bkg = c["bk"] * c["ge"]
        s_s[c["b"], c["h"]] = (c["s0"] * c["ge"] + _dot(c["ut"], _bf(bkg[:C]))
                               + _dot_tn(c["vb"], _bf(bkg[C:])))
    for c in st:
        y, sl = c["y"], c["sl"]
        mean = jnp.mean(y, axis=-1, keepdims=True)
        yc = y - mean
        var = jnp.mean(yc * yc, axis=-1, keepdims=True)
        yn = yc * lax.rsqrt(var + GN_EPS) * gnw_ref[:, sl] + gnb_ref[:, sl]
        bonus = jnp.sum(c["r_h"] * c["k_h"] * rk_ref[:, sl], axis=-1, keepdims=True) * c["v_h"]
        o_ref[c["b"], :, sl] = (yn + bonus).astype(o_ref.dtype)


def _rwkv_kernel(*refs, C, bt, l_valid, has_carry):
    (u_ref, sh_ref, s0_ref, mu_ref, w0_ref, w2_ref, a0_ref, a2_ref, kk_ref, ka_ref, rk_ref, gnw_ref,
     gnb_ref) = refs[:13]
    o_ref, sN_ref, prev_s, s_s = refs[13 + has_carry:]
    ci = pl.program_id(1)

    @pl.when(ci == 0)
    def _():
        prev_s[...] = sh_ref[...]
        s_s[...] = s0_ref[...]

    N = bt * C
    u = u_ref[...].reshape(N, RW_COLS)
    tok = lax.broadcasted_iota(jnp.int32, (N, 1), 0) % C
    valid = tok < l_valid
    prev = jnp.concatenate([jnp.broadcast_to(prev_s[b], (C, RW_COLS)) for b in range(bt)], axis=0)
    u_prev = jnp.where(tok == 0, prev, pltpu.roll(u, 1, 0))
    for b in range(bt):
        prev_s[b] = u[(b + 1) * C - 1:(b + 1) * C, :]
    us = u + (u_prev - u) * mu_ref[...]
    W = RW_WIDTH
    k = us[:, W:2 * W]
    wd = us[:, 3 * W:3 * W + RW_RANK]
    ad = us[:, 3 * W + RW_RANK:3 * W + 2 * RW_RANK]
    xw = w0_ref[...] + _dot(_bf(jnp.tanh(wd)), _bf(w2_ref[...]))
    log_w = -_softplus(-xw) - 0.5
    ld = jnp.where(valid, -jnp.exp(log_w), 0.0)
    a = jax.nn.sigmoid(a0_ref[...] + _dot(_bf(ad), _bf(a2_ref[...])))
    cum = _dot(_seq_prefix_mask(N, C).astype(F32), ld, HI)
    seq = dict(valid=valid, r=us[:, 0:W], v=jnp.where(valid, us[:, 2 * W:3 * W], 0.0), a=a,
               g=jnp.exp(cum), gi=jnp.exp(-cum), gp=jnp.exp(cum - ld), kkv=k * kk_ref[...],
               kmod=jnp.where(valid, k * (1.0 + (a - 1.0) * ka_ref[...]), 0.0))
    chains = [(b, h) for b in range(bt) for h in range(RW_HEADS)]
    group = RW_CHAIN_GROUP * bt
    for i in range(0, len(chains), group):
        _rwkv_chain_group(chains[i:i + group], seq, s_s, o_ref, rk_ref, gnw_ref, gnb_ref, C=C)

    @pl.when(ci == pl.num_programs(1) - 1)
    def _():
        sN_ref[...] = s_s[...]


def rwkv7_mix(u_rw, shift_prev, s_prev, l, l_out, s_carry, mu, w0, w2, a0, a2, k_k, k_a, r_k, gn_w, gn_b, *, B, L,
              l_valid):
    C = min(L, CHUNK_ROWS)
    bt = CHUNK_ROWS // C
    u3 = u_rw.reshape(B, L, RW_COLS)
    vec = lambda n: pl.BlockSpec((1, n), lambda b, i: (0, 0))
    row = lambda t: t.reshape(1, -1)
    s_blk = (bt, RW_HEADS, RW_HEAD, RW_HEAD)
    at_b = lambda b, i: (b, 0, 0, 0)
    in_specs = [pl.BlockSpec((bt, C, RW_COLS), lambda b, i: (b, i, 0)),
                _layer_block(l, (bt, 1, RW_COLS), lambda b, i: (b, 0, 0)),
                _layer_block(l, s_blk, at_b),
                vec(RW_COLS), vec(RW_WIDTH),
                pl.BlockSpec((RW_RANK, RW_WIDTH), lambda b, i: (0, 0)),
                vec(RW_WIDTH),
                pl.BlockSpec((RW_RANK, RW_WIDTH), lambda b, i: (0, 0)),
                vec(RW_WIDTH), vec(RW_WIDTH), vec(RW_WIDTH), vec(RW_WIDTH), vec(RW_WIDTH)]
    args = [u3, shift_prev.reshape(shift_prev.shape[0], B, 1, RW_COLS), s_prev, row(mu), row(w0), w2, row(a0), a2,
            row(k_k), row(k_a), row(r_k), row(gn_w), row(gn_b)]
    aliases = {}
    if s_carry is not None:
        in_specs.append(pl.BlockSpec(memory_space=pl.ANY))
        args.append(s_carry)
        aliases = {len(args) - 1: 1}
    out, sN = pl.pallas_call(
        functools.partial(_rwkv_kernel, C=C, bt=bt, l_valid=l_valid, has_carry=s_carry is not None),
        grid=(B // bt, L // C),
        in_specs=in_specs,
        out_specs=[pl.BlockSpec((bt, C, RW_WIDTH), lambda b, i: (b, i, 0)), _layer_block(l_out, s_blk, at_b)],
        out_shape=[jax.ShapeDtypeStruct((B, L, RW_WIDTH), BF16),
                   jax.ShapeDtypeStruct((DEPTH, B, RW_HEADS, RW_HEAD, RW_HEAD), F32)],
        scratch_shapes=[pltpu.VMEM((bt, 1, RW_COLS), F32), pltpu.VMEM(s_blk, F32)],
        input_output_aliases=aliases,
        compiler_params=_cparams(2),
        name="rwkv7_mix",
    )(*args)
    return out.reshape(B * L, RW_WIDTH), sN


RWS_VECS = 5
RWS_ROWS = 2


def _rwkv_short_kernel(*refs, l_valid, has_carry):
    (ur_ref, uk_ref, uv_ref, ul_ref, shr_ref, shk_ref, shv_ref, shl_ref, mur_ref, muk_ref, muv_ref, mul_ref,
     w0_ref, w2_ref, a0_ref, a2_ref, kk_ref, ka_ref, rk_ref, gnw_ref, gnb_ref, s_ref) = refs[:22]
    o_ref, sN_ref, vec_s, val_s, y_s = refs[22 + has_carry:]
    H2 = LANES // RW_HEAD
    heads = [slice(h * RW_HEAD, (h + 1) * RW_HEAD) for h in range(H2)]
    o_ref[...] = jnp.zeros_like(o_ref)

    def shifted(u_ref, sh_ref, mu_ref, t):
        u = u_ref[:, t, :]
        prev = sh_ref[...] if t == 0 else u_ref[:, t - 1, :]
        return u + (prev - u) * mu_ref[...]

    rows = []
    for t in range(l_valid):
        r = shifted(ur_ref, shr_ref, mur_ref, t)
        k = shifted(uk_ref, shk_ref, muk_ref, t)
        v = shifted(uv_ref, shv_ref, muv_ref, t)
        lo = shifted(ul_ref, shl_ref, mul_ref, t)
        xw = w0_ref[...] + _dot(jnp.tanh(lo[:, :RW_RANK]), w2_ref[...], HI)
        w = jnp.exp(-jnp.exp(-_softplus(-xw) - 0.5))
        a = jax.nn.sigmoid(a0_ref[...] + _dot(lo[:, RW_RANK:], a2_ref[...], HI))
        kkv = k * kk_ref[...]
        kmod = k * (1.0 + (a - 1.0) * ka_ref[...])
        kap = jnp.concatenate(
            [kkv[:, sl] / jnp.maximum(jnp.sqrt(jnp.sum(kkv[:, sl] * kkv[:, sl], axis=-1, keepdims=True)), 1e-12)
             for sl in heads], axis=1)
        for j, x in enumerate((w, kap, kap * a, kmod, r)):
            vec_s[t, j] = x.T.reshape(H2, RW_HEAD, x.shape[0])
        val_s[t] = v.T.reshape(H2, RW_HEAD, v.shape[0])
        rows.append((r, kmod, v))

    for h in range(H2):
        def body(vi, carry, h=h):
            vs = [vi * RWS_ROWS + j for j in range(RWS_ROWS)]
            ss = [s_ref[h, v] for v in vs]
            for t in range(l_valid):
                w, kap, bb, kk, rr = (vec_s[t, j, h] for j in range(RWS_VECS))
                for j, v in enumerate(vs):
                    sa = jnp.sum(ss[j] * kap, axis=0, keepdims=True)
                    ss[j] = ss[j] * w - sa * bb + val_s[t, h, pl.ds(v, 1), :] * kk
                    y_s[t, h, pl.ds(v, 1), :] = jnp.sum(ss[j] * rr, axis=0, keepdims=True)
            for j, v in enumerate(vs):
                sN_ref[h, v] = ss[j]
            return carry
        lax.fori_loop(0, RW_HEAD // RWS_ROWS, body, 0)

    for t in range(l_valid):
        r, kmod, v = rows[t]
        y_all = y_s[t].reshape(LANES, y_s.shape[-1]).T
        outs = []
        for sl in heads:
            y = y_all[:, sl]
            mean = jnp.mean(y, axis=-1, keepdims=True)
            yc = y - mean
            var = jnp.mean(yc * yc, axis=-1, keepdims=True)
            yn = yc * lax.rsqrt(var + GN_EPS) * gnw_ref[:, sl] + gnb_ref[:, sl]
            bonus = jnp.sum(r[:, sl] * kmod[:, sl] * rk_ref[:, sl], axis=-1, keepdims=True) * v[:, sl]
            outs.append(yn + bonus)
        o_ref[:, t, :] = jnp.concatenate(outs, axis=1).astype(o_ref.dtype)


def rwkv7_short(u_rw, shift_prev, s_prev_t, l, l_out, s_carry, mu, w0, w2, a0, a2, k_k, k_a, r_k, gn_w, gn_b, *, B, L,
                l_valid):
    assert B == LANES, "the batch must fill the lane dimension"
    H2 = LANES // RW_HEAD
    nblk = RW_WIDTH // LANES
    u3 = u_rw.reshape(B, L, RW_COLS)
    seg = lambda off: pl.BlockSpec((B, L, LANES), lambda hp, off=off: (0, 0, off + hp))
    lora = pl.BlockSpec((B, L, LANES), lambda hp: (0, 0, 3 * nblk))
    sh = lambda off: pl.BlockSpec((None, B, LANES), lambda hp, off=off: (l, 0, off + hp))
    sh_lora = pl.BlockSpec((None, B, LANES), lambda hp: (l, 0, 3 * nblk))
    mus = lambda off: pl.BlockSpec((1, LANES), lambda hp, off=off: (0, off + hp))
    mu_lora = pl.BlockSpec((1, LANES), lambda hp: (0, 3 * nblk))
    vec = pl.BlockSpec((1, LANES), lambda hp: (0, hp))
    mat = pl.BlockSpec((RW_RANK, LANES), lambda hp: (0, hp))
    s_blk = (H2, RW_HEAD, RW_HEAD, B)
    in_specs = [seg(0), seg(nblk), seg(2 * nblk), lora, sh(0), sh(nblk), sh(2 * nblk), sh_lora,
                mus(0), mus(nblk), mus(2 * nblk), mu_lora,
                vec, mat, vec, mat, vec, vec, vec, vec, vec,
                pl.BlockSpec((None,) + s_blk, lambda hp: (l, hp, 0, 0, 0))]
    row = lambda t: t.reshape(1, -1)
    args = [u3, u3, u3, u3, shift_prev, shift_prev, shift_prev, shift_prev, row(mu), row(mu), row(mu), row(mu),
            row(w0), w2, row(a0), a2, row(k_k), row(k_a), row(r_k), row(gn_w), row(gn_b), s_prev_t]
    aliases = {}
    if s_carry is not None:
        in_specs.append(pl.BlockSpec(memory_space=pl.ANY))
        args.append(s_carry)
        aliases = {len(args) - 1: 1}
    out, sN = pl.pallas_call(
        functools.partial(_rwkv_short_kernel, l_valid=l_valid, has_carry=s_carry is not None),
        grid=(RW_HEADS // H2,),
        in_specs=in_specs,
        out_specs=[pl.BlockSpec((B, L, LANES), lambda hp: (0, 0, hp)),
                   pl.BlockSpec((None,) + s_blk, lambda hp: (l_out, hp, 0, 0, 0))],
        out_shape=[jax.ShapeDtypeStruct((B, L, RW_WIDTH), BF16),
                   jax.ShapeDtypeStruct((DEPTH, RW_HEADS, RW_HEAD, RW_HEAD, B), F32)],
        scratch_shapes=[pltpu.VMEM((l_valid, RWS_VECS, H2, RW_HEAD, B), F32),
                        pltpu.VMEM((l_valid, H2, RW_HEAD, B), F32),
                        pltpu.VMEM((l_valid, H2, RW_HEAD, B), F32)],
        input_output_aliases=aliases,
        compiler_params=_cparams(1),
        name="rwkv7_short",
    )(*args)
    return out.reshape(B * L, RW_WIDTH), sN


ATTN_SEQS_PER_STEP = 8


def _attn_kernel(q_ref, kv_ref, o_ref):
    heads = [slice(h * XA_HEAD_DIM, (h + 1) * XA_HEAD_DIM) for h in range(XA_HEADS)]
    s = [_dot_nt(q_ref[0, :, hs], _bf(kv_ref[0, :, hs])) * (XA_HEAD_DIM ** -0.5) for hs in heads]
    e = [jnp.exp(t - jnp.max(t, axis=-1, keepdims=True)) for t in s]
    p = [t / jnp.sum(t, axis=-1, keepdims=True) for t in e]
    for h, hs in enumerate(heads):
        vs = slice(XA_WIDTH + h * XA_HEAD_DIM, XA_WIDTH + (h + 1) * XA_HEAD_DIM)
        o_ref[0, :, hs] = _dot(_bf(p[h]), _bf(kv_ref[0, :, vs])).astype(o_ref.dtype)


def _attn_short_kernel(q_ref, k_ref, v_ref, o_ref, *, bt):
    L = q_ref.shape[1]
    rows, cols = XA_HEADS * L, MEM_LEN * XA_HEADS
    own = (lax.broadcasted_iota(jnp.int32, (rows, cols), 0) // L
           == lax.broadcasted_iota(jnp.int32, (rows, cols), 1) % XA_HEADS)
    heads = [slice(h * XA_HEAD_DIM, (h + 1) * XA_HEAD_DIM) for h in range(XA_HEADS)]
    qs = [jnp.concatenate([q_ref[b, :, hs] for hs in heads], axis=0) for b in range(bt)]
    s = [_dot_nt(qs[b], _bf(k_ref[b].reshape(cols, XA_HEAD_DIM))) * (XA_HEAD_DIM ** -0.5) for b in range(bt)]
    s = [jnp.where(own, t, -jnp.inf) for t in s]
    e = [jnp.exp(t - jnp.max(t, axis=-1, keepdims=True)) for t in s]
    p = [t / jnp.sum(t, axis=-1, keepdims=True) for t in e]
    for b in range(bt):
        out = _dot(_bf(p[b]), _bf(v_ref[b].reshape(cols, XA_HEAD_DIM)))
        for h, hs in enumerate(heads):
            o_ref[b, :, hs] = out[h * L:(h + 1) * L].astype(o_ref.dtype)


def cross_attention(q, mem, l, *, B, L):
    q3 = q.reshape(B, L, XA_WIDTH)
    if l is None:
        bt, tq, body, mems = 1, min(L, 512), _attn_kernel, [mem]
        mem_specs = [pl.BlockSpec((bt, MEM_LEN, 2 * XA_WIDTH), lambda b, i: (b, 0, 0))]
    else:
        bt, tq, mems = ATTN_SEQS_PER_STEP, L, list(mem)
        body = functools.partial(_attn_short_kernel, bt=bt)
        mem_specs = [pl.BlockSpec((None, bt, MEM_LEN, XA_HEADS, XA_HEAD_DIM), lambda b, i: (l, b, 0, 0, 0))] * 2
    qo_spec = pl.BlockSpec((bt, tq, XA_WIDTH), lambda b, i: (b, i, 0))
    out = pl.pallas_call(
        body,
        grid=(B // bt, L // tq),
        in_specs=[qo_spec] + mem_specs,
        out_specs=qo_spec,
        out_shape=jax.ShapeDtypeStruct((B, L, XA_WIDTH), BF16),
        compiler_params=_cparams(2),
        name="cross_attention",
    )(q3, *mems)
    return out.reshape(B * L, XA_WIDTH)


ROUTER_GROUP_LANE = MOE_EXPERTS


MOE_TB = 512
MOE_TB_GATHER = 1024
MOE_SUB = 256
MOE_TE_LONG = 1024
MOE_TE_SHORT = 512
ROUTE_GROUP_LANE = 0
ROUTE_RANK_LANE = 1
COMB_PIECES = 3


def _router_kernel(x_ref, g_ref, w_ref, b_ref, xn_ref, comb_ref, route_ref, cnt_s):
    @pl.when(pl.program_id(0) == 0)
    def _():
        cnt_s[...] = jnp.zeros_like(cnt_s)

    xn = _rms(x_ref[...], g_ref[...])
    xn_ref[...] = _bf(xn)
    z = _dot(xn, w_ref[...], HI) + b_ref[...]
    tm = z.shape[0]
    lane = lax.broadcasted_iota(jnp.int32, z.shape, 1).astype(F32)
    big = float(LANES)
    neg = -jnp.inf
    first = lambda mask: jnp.min(jnp.where(mask, lane, big), axis=-1, keepdims=True)
    is_g = jnp.logical_and(lane >= ROUTER_GROUP_LANE, lane < ROUTER_GROUP_LANE + MOE_GROUPS)
    zg = jnp.where(is_g, z, neg)
    mg = jnp.max(zg, axis=-1, keepdims=True)
    grp = first(zg == mg) - ROUTER_GROUP_LANE
    p_grp = 1.0 / jnp.sum(jnp.exp(zg - mg), axis=-1, keepdims=True)
    lo = grp * MOE_PER_GROUP
    ze = jnp.where(jnp.logical_and(lane >= lo, lane < lo + MOE_PER_GROUP), z, neg)
    t1 = jnp.max(ze, axis=-1, keepdims=True)
    i1 = first(ze == t1)
    ze2 = jnp.where(lane == i1, neg, ze)
    t2 = jnp.max(ze2, axis=-1, keepdims=True)
    i2 = first(ze2 == t2)
    e2 = jnp.exp(t2 - t1)
    g1 = p_grp / (1.0 + e2)
    comb = jnp.where(lane == i1, g1, 0.0) + jnp.where(lane == i2, g1 * e2, 0.0)
    comb_ref[...] = _split3(comb)
    onehot = jnp.where(lane == grp, 1.0, 0.0)
    r = lax.broadcasted_iota(jnp.int32, (tm, tm), 0)
    c = lax.broadcasted_iota(jnp.int32, (tm, tm), 1)
    before = _dot(_bf(jnp.where(c < r, 1.0, 0.0)), _bf(onehot)) + cnt_s[...]
    rank = jnp.sum(onehot * before, axis=-1, keepdims=True)
    cnt_s[...] = cnt_s[...] + jnp.sum(onehot, axis=0, keepdims=True)
    route_ref[...] = jnp.where(lane == ROUTE_GROUP_LANE, grp, 0.0) + jnp.where(lane == ROUTE_RANK_LANE, rank, 0.0)


def moe_router(x, g, w_r1, b_r1, w_r2, b_r2):
    T, D = x.shape
    tm = MOE_TB
    pad = LANES - MOE_EXPERTS - MOE_GROUPS
    w = jnp.concatenate([w_r2, w_r1, jnp.zeros((D, pad), F32)], axis=1)
    b = jnp.concatenate([b_r2, b_r1, jnp.zeros((pad,), F32)]).reshape(1, LANES)
    return pl.pallas_call(
        _router_kernel,
        grid=(T // tm,),
        in_specs=[pl.BlockSpec((tm, D), lambda i: (i, 0)),
                  pl.BlockSpec((1, D), lambda i: (0, 0)),
                  pl.BlockSpec((D, LANES), lambda i: (0, 0)),
                  pl.BlockSpec((1, LANES), lambda i: (0, 0))],
        out_specs=[pl.BlockSpec((tm, D), lambda i: (i, 0)),
                   pl.BlockSpec((tm, COMB_PIECES * LANES), lambda i: (i, 0)),
                   pl.BlockSpec((tm, LANES), lambda i: (i, 0))],
        out_shape=[jax.ShapeDtypeStruct((T, D), BF16), jax.ShapeDtypeStruct((T, COMB_PIECES * LANES), BF16),
                   jax.ShapeDtypeStruct((T, LANES), F32)],
        scratch_shapes=[pltpu.VMEM((1, LANES), F32)],
        compiler_params=_cparams(1),
        name="moe_router",
    )(x, g.reshape(1, D), w, b)


def _moe_plan(route, T, te):
    i32 = jnp.int32
    grp = route[:, ROUTE_GROUP_LANE].astype(i32)
    rank = route[:, ROUTE_RANK_LANE].astype(i32)
    rows = T + MOE_GROUPS * te
    n_sub = rows // MOE_SUB
    n_tiles = rows // te
    onehot = (grp[:, None] == jnp.arange(MOE_GROUPS, dtype=i32)[None]).astype(i32)
    seg_rows = (onehot.sum(axis=0) + te - 1) // te * te
    seg_end = jnp.cumsum(seg_rows)
    seg_start = seg_end - seg_rows
    dest = seg_start[grp] + rank
    group_of = lambda row: jnp.minimum(jnp.sum(row[:, None] >= seg_end[None], axis=1), MOE_GROUPS - 1).astype(i32)
    sub_row = jnp.arange(n_sub, dtype=i32) * MOE_SUB
    sub_g = group_of(sub_row)
    sub_valid = sub_row < seg_end[-1]
    r0 = sub_row - seg_start[sub_g]

    def overlap(tb):
        blk_cnt = onehot.reshape(T // tb, tb, MOE_GROUPS).sum(axis=1)
        cum_blk = jnp.concatenate([jnp.zeros((1, MOE_GROUPS), i32), jnp.cumsum(blk_cnt, axis=0)])
        lo = cum_blk[:-1][:, sub_g].T
        hi = cum_blk[1:][:, sub_g].T
        return (lo < (r0 + MOE_SUB)[:, None]) & (hi > r0[:, None]) & sub_valid[:, None]

    g_mask = overlap(MOE_TB_GATHER)
    first_col = (jnp.arange(T // MOE_TB_GATHER) == 0)[None]
    g_mask = g_mask | (first_col & ~g_mask.any(axis=1, keepdims=True))

    def items(mask, ncol, nblk):
        n_items = n_sub + MOE_GROUPS * nblk
        flat = jnp.nonzero(mask.reshape(-1), size=n_items, fill_value=-1)[0].astype(i32)
        valid = flat >= 0
        flat = jnp.where(valid, flat, jnp.max(flat))
        major, minor = flat // ncol, flat % ncol
        prev = jnp.concatenate([jnp.full((1,), -1, i32), major[:-1]])
        nxt = jnp.concatenate([major[1:], jnp.full((1,), -1, i32)])
        nvalid = jnp.concatenate([valid[1:], jnp.zeros((1,), bool)])
        first = (major != prev) & valid
        last = ((major != nxt) | ~nvalid) & valid
        return major, minor, first.astype(i32), last.astype(i32), valid.astype(i32)

    g_sub, g_blk, g_first, _, g_valid = items(g_mask, T // MOE_TB_GATHER, T // MOE_TB_GATHER)
    s_blk, s_sub, s_first, s_last, s_valid = items(overlap(MOE_TB).T, n_sub, T // MOE_TB)
    tile_row = jnp.arange(n_tiles, dtype=i32) * te
    tile_valid = tile_row < seg_end[-1]
    tile_group = group_of(jnp.where(tile_valid, tile_row, seg_end[-1] - 1))
    return dict(dest=dest, rows=rows, gather=(g_sub, g_blk, g_first, g_valid),
                scatter=(s_sub, s_blk, s_first, s_last, s_valid),
                tile_group=tile_group, tile_valid=tile_valid.astype(i32))


def _moe_gather_kernel(sub_ref, blk_ref, first_ref, valid_ref, dest_ref, xn_ref, comb_ref, xs_ref, cs_ref):
    w = pl.program_id(0)

    @pl.when(valid_ref[w] == 1)
    def _():
        rows = sub_ref[w] * MOE_SUB + lax.broadcasted_iota(jnp.int32, (MOE_SUB, MOE_TB_GATHER), 0)
        hit = _bf(jnp.where(dest_ref[0] == rows, 1.0, 0.0))
        gx = _bf(_dot(hit, xn_ref[...]))
        g3 = _dot(hit, comb_ref[...])
        gc = g3[:, :LANES] + g3[:, LANES:2 * LANES] + g3[:, 2 * LANES:]

        @pl.when(first_ref[w] == 1)
        def _():
            xs_ref[...] = gx
            cs_ref[...] = gc

        @pl.when(first_ref[w] == 0)
        def _():
            xs_ref[...] = xs_ref[...] + gx
            cs_ref[...] = cs_ref[...] + gc


def moe_gather(xn, comb, plan):
    T, D = xn.shape
    g_sub, g_blk, g_first, g_valid = plan["gather"]
    rows = plan["rows"]
    tb = MOE_TB_GATHER
    dest3 = plan["dest"].reshape(T // tb, 1, tb)
    grid_spec = pltpu.PrefetchScalarGridSpec(
        num_scalar_prefetch=4,
        grid=(g_sub.shape[0],),
        in_specs=[pl.BlockSpec((1, 1, tb), lambda w, s, b, f, v: (b[w], 0, 0)),
                  pl.BlockSpec((tb, D), lambda w, s, b, f, v: (b[w], 0)),
                  pl.BlockSpec((tb, COMB_PIECES * LANES), lambda w, s, b, f, v: (b[w], 0))],
        out_specs=[pl.BlockSpec((MOE_SUB, D), lambda w, s, b, f, v: (s[w], 0)),
                   pl.BlockSpec((MOE_SUB, LANES), lambda w, s, b, f, v: (s[w], 0))])
    return pl.pallas_call(
        _moe_gather_kernel,
        grid_spec=grid_spec,
        out_shape=[jax.ShapeDtypeStruct((rows, D), BF16), jax.ShapeDtypeStruct((rows, LANES), F32)],
        compiler_params=_cparams(1),
        name="moe_gather",
    )(g_sub, g_blk, g_first, g_valid, dest3, xn, comb)


MOE_EXPERTS_PER_STEP = 2


def _moe_group_experts_kernel(tg_ref, tv_ref, xs_ref, cs_ref, wg_ref, wu_ref, wd_ref, y_ref, acc):
    i = pl.program_id(0)
    e = pl.program_id(1)

    @pl.when(e == 0)
    def _():
        acc[...] = jnp.zeros_like(acc)

    @pl.when(tv_ref[i] == 1)
    def _():
        xs = xs_ref[...]
        cs = cs_ref[...]
        lane = lax.broadcasted_iota(jnp.int32, cs.shape, 1)
        first = tg_ref[i] * MOE_PER_GROUP + e * MOE_EXPERTS_PER_STEP
        hids = []
        for j in range(MOE_EXPERTS_PER_STEP):
            ce = jnp.sum(jnp.where(lane == first + j, cs, 0.0), axis=-1, keepdims=True)
            hids.append(_bf(jax.nn.silu(_dot(xs, _bf(wg_ref[j]))) * _dot(xs, _bf(wu_ref[j])) * ce))
        hid = jnp.concatenate(hids, axis=1)
        acc[...] += _dot(hid, _bf(wd_ref[...].reshape(MOE_EXPERTS_PER_STEP * MOE_HIDDEN, wd_ref.shape[-1])))

    @pl.when(e == pl.num_programs(1) - 1)
    def _():
        y_ref[...] = _bf(acc[...])


def moe_group_experts(xs, cs, plan, wg, wu, wd, l, te):
    rows, D = xs.shape
    eps = MOE_EXPERTS_PER_STEP
    steps = MOE_PER_GROUP // eps

    def expert(i, e, tg, tv):
        return (l, tg[i] * steps + jnp.where(tv[i] == 1, e, steps - 1), 0, 0)

    w_in = pl.BlockSpec((None, eps, D, MOE_HIDDEN), expert)
    grid_spec = pltpu.PrefetchScalarGridSpec(
        num_scalar_prefetch=2,
        grid=(rows // te, steps),
        in_specs=[pl.BlockSpec((te, D), lambda i, e, tg, tv: (i, 0)),
                  pl.BlockSpec((te, LANES), lambda i, e, tg, tv: (i, 0)),
                  w_in, w_in,
                  pl.BlockSpec((None, eps, MOE_HIDDEN, D), expert)],
        out_specs=pl.BlockSpec((te, D), lambda i, e, tg, tv: (i, 0)),
        scratch_shapes=[pltpu.VMEM((te, D), F32)])
    return pl.pallas_call(
        _moe_group_experts_kernel,
        grid_spec=grid_spec,
        out_shape=jax.ShapeDtypeStruct((rows, D), BF16),
        compiler_params=_cparams(2),
        name="moe_group_experts",
    )(plan["tile_group"], plan["tile_valid"], xs, cs, wg, wu, wd)


def _moe_scatter_kernel(sub_ref, blk_ref, first_ref, last_ref, valid_ref, dest_ref, y_ref, x_ref, gf_ref, o_ref,
                        *, final_norm):
    w = pl.program_id(0)

    @pl.when(valid_ref[w] == 1)
    def _():
        cols = sub_ref[w] * MOE_SUB + lax.broadcasted_iota(jnp.int32, (MOE_TB, MOE_SUB), 1)
        pick = _bf(jnp.where(dest_ref[...] == cols, 1.0, 0.0))
        upd = _dot(pick, y_ref[...])

        @pl.when(first_ref[w] == 1)
        def _():
            o_ref[...] = x_ref[...] + upd

        @pl.when(first_ref[w] == 0)
        def _():
            o_ref[...] += upd

        if final_norm:
            @pl.when(last_ref[w] == 1)
            def _():
                o_ref[...] = _rms(o_ref[...], gf_ref[...])


def moe_scatter(ys, x, plan, g_final, *, final_norm):
    T, D = x.shape
    s_sub, s_blk, s_first, s_last, s_valid = plan["scatter"]
    dest_col = plan["dest"].reshape(T, 1)
    im = lambda f: (lambda w, s, b, fi, la, v: f(w, s, b))
    grid_spec = pltpu.PrefetchScalarGridSpec(
        num_scalar_prefetch=5,
        grid=(s_sub.shape[0],),
        in_specs=[pl.BlockSpec((MOE_TB, 1), im(lambda w, s, b: (b[w], 0))),
                  pl.BlockSpec((MOE_SUB, D), im(lambda w, s, b: (s[w], 0))),
                  pl.BlockSpec((MOE_TB, D), im(lambda w, s, b: (b[w], 0))),
                  pl.BlockSpec((1, D), im(lambda w, s, b: (0, 0)))],
        out_specs=pl.BlockSpec((MOE_TB, D), im(lambda w, s, b: (b[w], 0))))
    return pl.pallas_call(
        functools.partial(_moe_scatter_kernel, final_norm=final_norm),
        grid_spec=grid_spec,
        out_shape=jax.ShapeDtypeStruct((T, D), F32),
        compiler_params=_cparams(1),
        name="moe_scatter",
    )(s_sub, s_blk, s_first, s_last, s_valid, dest_col, ys, x, g_final.reshape(1, D))


def hmoe_block(x, p, l, *, final_norm):
    T = x.shape[0]
    te = MOE_TE_LONG if T >= 4 * MOE_TE_LONG else MOE_TE_SHORT
    xn, comb, route = moe_router(x, p["g_moe"][l], p["moe_wr1"][l], p["moe_br1"][l], p["moe_wr2"][l],
                                 p["moe_br2"][l])
    plan = _moe_plan(route, T, te)
    xs, cs = moe_gather(xn, comb, plan)
    ys = moe_group_experts(xs, cs, plan, p["moe_wg"], p["moe_wu"], p["moe_wd"], l, te)
    return moe_scatter(ys, x, plan, p["g_final"], final_norm=final_norm)


def _layer_weights(l, p):
    w_in = p["w_in"][l]
    c0 = RW_COLS
    c1 = c0 + ML_MAIN
    c2 = c1 + 2 * ML_HEADS
    c3 = c2 + POOL_WIDTH
    w_pl = jnp.concatenate([w_in[:, c2:c3], w_in[:, c1:c2], jnp.zeros((D_MODEL, LANES - 2 * ML_HEADS), F32)], axis=1)
    return dict(
        w_rw=_bf(w_in[:, :c0]), w_ml=_bf(w_in[:, c0:c1]), w_pl=_bf(w_pl), w_gate=_bf(w_in[:, c3:]),
        w_up_rwkv=_bf(p["w_up_rwkv"][l]), w_up_mlstm=_bf(p["w_up_mlstm"][l]), w_up_pool=_bf(p["w_up_pool"][l]),
        w_out=_bf(p["w_out"][l]), pool_w=_bf(p["pool_w"][l]),
        xa_wq=_bf(p["xa_wq"][l]), xa_wo=_bf(p["xa_wo"][l]),
        xa_wkv=_bf(jnp.concatenate([p["xa_wk"][l], p["xa_wv"][l]], axis=1)),
    )


def _trunk_layer(x, mem, mem_l, st, carry, *, l, sl, p, lw, B, L, l_valid, start, final_norm):
    g_mix = p["g_mix"][l]
    u_rw = norm_matmul(x, g_mix, lw["w_rw"], tn=TN_RWKV, tm=TM_WHOLE_SEGMENT, name="in_rwkv")
    u_ml = norm_matmul(x, g_mix, lw["w_ml"], tn=TN_MLSTM, tm=TM_WHOLE_SEGMENT, name="in_mlstm")
    u_pl = norm_matmul(x, g_mix, lw["w_pl"], tn=POOL_WIDTH + LANES, name="in_pool")
    gates = norm_matmul(x, g_mix, lw["w_gate"], tn=TN_GATES, tm=TM_WHOLE_SEGMENT, out_dtype=BF16, act="sigmoid",
                        name="in_gates")

    rwkv = rwkv7_short if L < CHUNK_ROWS else rwkv7_mix
    o_rw, rw_s = rwkv(u_rw, st["rw_shift"], st["rw_s"], sl, l, carry["rw_s"], p["rw_mu"][l], p["rw_w0"][l],
                      p["rw_w2"][l], p["rw_a0"][l], p["rw_a2"][l], p["rw_k_k"][l], p["rw_k_a"][l],
                      p["rw_r_k"][l], p["rw_gn_w"][l], p["rw_gn_b"][l], B=B, L=L, l_valid=l_valid)
    rw_shift = u_rw.reshape(B, L, RW_COLS)[:, l_valid - 1]
    o_ml, ml_c, ml_n, ml_m = mlstm_mix(u_ml, u_pl, st["ml_c"], st["ml_n"], st["ml_m"], sl, l, carry["ml_c"],
                                       p["ml_b_i"][l], p["ml_b_f"][l], p["ml_gn_w"][l], B=B, L=L, l_valid=l_valid)
    o_pl, pool_buf = pool_mix(u_pl, st["pool"], sl, lw["pool_w"], p["pool_scale"][l], B=B, L=L, l_valid=l_valid,
                              start=start)
    merged = merge_branches(o_rw, o_ml, o_pl, gates, lw["w_up_rwkv"], lw["w_up_mlstm"], lw["w_up_pool"])
    x = matmul_residual(merged, lw["w_out"], x)

    q = norm_matmul(x, p["g_xa"][l], lw["xa_wq"], tn=XA_WIDTH, out_dtype=BF16, name="xa_q")
    att = cross_attention(q, mem, mem_l, B=B, L=L)
    x = matmul_residual(att, lw["xa_wo"], x)

    x = hmoe_block(x, p, l, final_norm=final_norm)
    return x, dict(rw_s=rw_s, ml_c=ml_c), (rw_shift, ml_n, ml_m, pool_buf)


def kernel(x_prompt, x_sample, cache_mem_k, cache_mem_v, state_rwkv_s, state_rwkv_shift, state_mlstm_c, state_mlstm_n, state_mlstm_m, state_pool, mem_prompt, g_mix, w_in, rw_mu, rw_w0, rw_w2, rw_a0, rw_a2, rw_k_k, rw_k_a, rw_r_k, rw_gn_w, rw_gn_b, ml_b_i, ml_b_f, ml_gn_w, pool_w, pool_scale, w_up_rwkv, w_up_mlstm, w_up_pool, w_out, g_xa, g_mem, xa_wq, xa_wk, xa_wv, xa_wo, g_moe, moe_wr1, moe_br1, moe_wr2, moe_br2, moe_wg, moe_wu, moe_wd, g_final):
    p = dict(g_mix=g_mix, w_in=w_in, rw_mu=rw_mu, rw_w0=rw_w0, rw_w2=rw_w2, rw_a0=rw_a0, rw_a2=rw_a2, rw_k_k=rw_k_k,
             rw_k_a=rw_k_a, rw_r_k=rw_r_k, rw_gn_w=rw_gn_w, rw_gn_b=rw_gn_b, ml_b_i=ml_b_i, ml_b_f=ml_b_f,
             ml_gn_w=ml_gn_w, pool_w=pool_w, pool_scale=pool_scale, w_up_rwkv=w_up_rwkv, w_up_mlstm=w_up_mlstm,
             w_up_pool=w_up_pool, w_out=w_out, g_xa=g_xa, g_mem=g_mem, xa_wq=xa_wq, xa_wk=xa_wk, xa_wv=xa_wv,
             xa_wo=xa_wo, g_moe=g_moe, moe_wr1=moe_wr1, moe_br1=moe_br1, moe_wr2=moe_wr2, moe_br2=moe_br2,
             moe_wg=moe_wg, moe_wu=moe_wu, moe_wd=moe_wd, g_final=g_final)
    Bp, Lp, D = x_prompt.shape
    Bs, Ls, _ = x_sample.shape
    Ls_pad = -(-Ls // SUBLANES) * SUBLANES
    yp = x_prompt.reshape(Bp * Lp, D)
    ys = jnp.pad(x_sample, ((0, 0), (0, Ls_pad - Ls), (0, 0))).reshape(Bs * Ls_pad, D)
    zeros = lambda *s: jnp.zeros((1,) + s, F32)
    st_p = dict(rw_shift=zeros(Bp, RW_COLS), rw_s=zeros(Bp, RW_HEADS, RW_HEAD, RW_HEAD),
                ml_c=zeros(Bp, ML_HEADS, ML_DQK, ML_DV), ml_n=zeros(Bp, ML_HEADS, ML_DQK), ml_m=zeros(Bp, ML_HEADS),
                pool=zeros(Bp, POOL_BUF, POOL_WIDTH))
    batch_minor = (0, 2, 3, 4, 1)
    st_s = dict(rw_shift=state_rwkv_shift, rw_s=jnp.transpose(state_rwkv_s, batch_minor), ml_c=state_mlstm_c,
                ml_n=state_mlstm_n, ml_m=state_mlstm_m, pool=state_pool)
    carry_p = carry_s = dict(rw_s=None, ml_c=None)
    small_p = [[] for _ in range(6)]
    small_s = [[] for _ in range(4)]
    for l in range(DEPTH):
        lw = _layer_weights(l, p)
        final = l == DEPTH - 1
        kv = norm_matmul(mem_prompt.reshape(Bp * MEM_LEN, D), g_mem[l], lw["xa_wkv"], tn=2 * XA_WIDTH,
                         name="memory_kv")
        kv3 = kv.reshape(Bp, MEM_LEN, 2 * XA_WIDTH)
        yp, carry_p, small = _trunk_layer(yp, kv3, None, st_p, carry_p, l=l, sl=0, p=p, lw=lw, B=Bp, L=Lp,
                                          l_valid=Lp, start=0, final_norm=final)
        mk = kv[:, :XA_WIDTH].reshape(Bp, MEM_LEN, XA_HEADS, XA_HEAD_DIM)
        mv = kv[:, XA_WIDTH:].reshape(Bp, MEM_LEN, XA_HEADS, XA_HEAD_DIM)
        for acc, t in zip(small_p, small + (mk, mv)):
            acc.append(t)
        ys, carry_s, small = _trunk_layer(ys, (cache_mem_k, cache_mem_v), l, st_s, carry_s, l=l, sl=l, p=p, lw=lw,
                                          B=Bs, L=Ls_pad, l_valid=Ls, start=PAST_LEN, final_norm=final)
        for acc, t in zip(small_s, small):
            acc.append(t)
    y_prompt = yp.reshape(Bp, Lp, D)
    y_sample = ys.reshape(Bs, Ls_pad, D)[:, :Ls]
    p_sh, p_n, p_m, p_pool, p_mk, p_mv = [jnp.stack(t) for t in small_p]
    s_sh, s_n, s_m, s_pool = [jnp.stack(t) for t in small_s]
    return (y_prompt, y_sample, carry_p["rw_s"], p_sh, carry_p["ml_c"], p_n, p_m, p_pool, p_mk, p_mv,
            jnp.transpose(carry_s["rw_s"], (0, 4, 1, 2, 3)), s_sh, carry_s["ml_c"], s_n, s_m, s_pool)
```

```python
import functools
import math

import jax
import jax.numpy as jnp
from jax import lax
from jax.experimental import pallas as pl
from jax.experimental.pallas import tpu as pltpu

F32 = jnp.float32
BF16 = jnp.bfloat16
HI = lax.Precision.HIGHEST

D_MODEL = 2048
DEPTH = 2
PAST_LEN = 16384
RW_HEAD = 64
RW_WIDTH = 1024
RW_HEADS = 16
RW_RANK = 64
RW_COLS = 3 * RW_WIDTH + 2 * RW_RANK
GN_EPS = 64e-5
ML_HEADS = 4
ML_DQK = 128
ML_DV = 256
ML_QK_WIDTH = 512
ML_V_WIDTH = 1024
ML_MAIN = 2 * ML_QK_WIDTH + 2 * ML_V_WIDTH
POOL_WIDTH = 1024
POOL_WINDOWS = (2, 4, 8, 16)
POOL_GW = 256
POOL_BUF = 15
POOL_HIST = 16
POOL_TAIL = 24
MEM_LEN = 256
XA_HEADS = 4
XA_HEAD_DIM = 128
XA_WIDTH = 512
MOE_GROUPS = 4
MOE_PER_GROUP = 8
MOE_EXPERTS = 32
MOE_HIDDEN = 256
RMS_EPS = 1e-6
LANES = 128
SUBLANES = 8
VMEM_LIMIT = 56 * 1024 * 1024
ROW_TILE = 1024
CHUNK_ROWS = 64
TN_RWKV = RW_COLS
TN_MLSTM = ML_MAIN
TN_GATES = 3 * D_MODEL
TM_WHOLE_SEGMENT = ROW_TILE // 2


def _cparams(n_axes):
    return pltpu.CompilerParams(dimension_semantics=("arbitrary",) * n_axes, vmem_limit_bytes=VMEM_LIMIT)


def _dot(a, b, precision=None):
    return jnp.dot(a, b, preferred_element_type=F32, precision=precision)


def _dot_nt(a, b, precision=None):
    return lax.dot_general(a, b, (((1,), (1,)), ((), ())), preferred_element_type=F32, precision=precision)


def _dot_tn(a, b, precision=None):
    return lax.dot_general(a, b, (((0,), (0,)), ((), ())), preferred_element_type=F32, precision=precision)


def _bf(x):
    return x.astype(BF16)


def _split3(x):
    hi = _bf(x)
    rest = x - hi.astype(F32)
    mid = _bf(rest)
    return jnp.concatenate([hi, mid, _bf(rest - mid.astype(F32))], axis=1)


def _rms(x, g):
    return x * lax.rsqrt(jnp.mean(x * x, axis=-1, keepdims=True) + RMS_EPS) * g


def _log_sigmoid(x):
    return jnp.minimum(x, 0.0) - jnp.log1p(jnp.exp(-jnp.abs(x)))


def _softplus(x):
    return jnp.maximum(x, 0.0) + jnp.log1p(jnp.exp(-jnp.abs(x)))


def _seq_prefix_mask(n, c):
    r = lax.broadcasted_iota(jnp.int32, (n, n), 0)
    q = lax.broadcasted_iota(jnp.int32, (n, n), 1)
    return jnp.logical_and(q <= r, q // c == r // c)


def _layer_block(l, blk, idx):
    return pl.BlockSpec((None,) + tuple(blk), lambda b, i: (l,) + tuple(idx(b, i)))


def _norm_matmul_kernel(x_ref, g_ref, w_ref, o_ref, xn_ref, *, act):
    @pl.when(pl.program_id(1) == 0)
    def _():
        xn_ref[...] = _bf(_rms(x_ref[...], g_ref[...]))

    acc = _dot(xn_ref[...], w_ref[...])
    if act == "sigmoid":
        acc = jax.nn.sigmoid(acc)
    o_ref[...] = acc.astype(o_ref.dtype)


def norm_matmul(x, g, w, *, tn, name, tm=ROW_TILE, out_dtype=F32, act=None):
    T, D = x.shape
    N = w.shape[1]
    tm = min(T, tm)
    w_mode = pl.Buffered(1) if tn == N else None
    return pl.pallas_call(
        functools.partial(_norm_matmul_kernel, act=act),
        grid=(T // tm, N // tn),
        in_specs=[pl.BlockSpec((tm, D), lambda i, j: (i, 0)),
                  pl.BlockSpec((1, D), lambda i, j: (0, 0)),
                  pl.BlockSpec((D, tn), lambda i, j: (0, j), pipeline_mode=w_mode)],
        out_specs=pl.BlockSpec((tm, tn), lambda i, j: (i, j)),
        out_shape=jax.ShapeDtypeStruct((T, N), out_dtype),
        scratch_shapes=[pltpu.VMEM((tm, D), BF16)],
        compiler_params=_cparams(2),
        name=name,
    )(x, g.reshape(1, D), w)


def _matmul_residual_kernel(a_ref, w_ref, r_ref, o_ref):
    o_ref[...] = r_ref[...] + _dot(a_ref[...], w_ref[...])


def matmul_residual(a, w, res):
    T, K = a.shape
    tn = w.shape[1]
    tm = min(T, TM_WHOLE_SEGMENT)
    return pl.pallas_call(
        _matmul_residual_kernel,
        grid=(T // tm, 1),
        in_specs=[pl.BlockSpec((tm, K), lambda i, j: (i, 0)),
                  pl.BlockSpec((K, tn), lambda i, j: (0, j)),
                  pl.BlockSpec((tm, tn), lambda i, j: (i, j))],
        out_specs=pl.BlockSpec((tm, tn), lambda i, j: (i, j)),
        out_shape=jax.ShapeDtypeStruct((T, tn), F32),
        compiler_params=_cparams(2),
        name="matmul_residual",
    )(a, w, res)


def _merge_kernel(orw_ref, oml_ref, opl_ref, g0_ref, g1_ref, g2_ref, wr_ref, wm_ref, wp_ref, o_ref):
    m = (g0_ref[...].astype(F32) * _dot(orw_ref[...], wr_ref[...])
         + g1_ref[...].astype(F32) * _dot(oml_ref[...], wm_ref[...])
         + g2_ref[...].astype(F32) * _dot(opl_ref[...], wp_ref[...]))
    o_ref[...] = _bf(m)


def merge_branches(o_rw, o_ml, o_pl, gates, w_rw, w_ml, w_pl, *, tn=1024):
    T, K = o_rw.shape
    tm = min(T, ROW_TILE)
    nj = D_MODEL // tn
    act = pl.BlockSpec((tm, K), lambda i, j: (i, 0))
    wsp = pl.BlockSpec((K, tn), lambda i, j: (0, j))
    gate = lambda b: pl.BlockSpec((tm, tn), lambda i, j, b=b: (i, b * nj + j))
    return pl.pallas_call(
        _merge_kernel,
        grid=(T // tm, nj),
        in_specs=[act, act, act, gate(0), gate(1), gate(2), wsp, wsp, wsp],
        out_specs=pl.BlockSpec((tm, tn), lambda i, j: (i, j)),
        out_shape=jax.ShapeDtypeStruct((T, D_MODEL), BF16),
        compiler_params=_cparams(2),
        name="merge_branches",
    )(o_rw, o_ml, o_pl, gates, gates, gates, w_rw, w_ml, w_pl)


def _merge_out_kernel(orw_ref, oml_ref, opl_ref, g0_ref, g1_ref, g2_ref, wr_ref, wm_ref, wp_ref, wo_ref, r_ref, o_ref):
    m = (g0_ref[...].astype(F32) * _dot(orw_ref[...], wr_ref[...])
         + g1_ref[...].astype(F32) * _dot(oml_ref[...], wm_ref[...])
         + g2_ref[...].astype(F32) * _dot(opl_ref[...], wp_ref[...]))
    o_ref[...] = r_ref[...] + _dot(_bf(m), wo_ref[...])


MERGE_ROWS = 256


def merge_out(o_rw, o_ml, o_pl, gates, w_rw, w_ml, w_pl, w_out, res):
    T, K = o_rw.shape
    tm = min(T, MERGE_ROWS)
    act = pl.BlockSpec((tm, K), lambda i: (i, 0))
    whole = lambda r: pl.BlockSpec((r, D_MODEL), lambda i: (0, 0), pipeline_mode=pl.Buffered(1))
    gate = lambda b: pl.BlockSpec((tm, D_MODEL), lambda i, b=b: (i, b))
    row = pl.BlockSpec((tm, D_MODEL), lambda i: (i, 0))
    return pl.pallas_call(
        _merge_out_kernel,
        grid=(T // tm,),
        in_specs=[act, act, act, gate(0), gate(1), gate(2), whole(K), whole(K), whole(K), whole(D_MODEL), row],
        out_specs=row,
        out_shape=jax.ShapeDtypeStruct((T, D_MODEL), F32),
        compiler_params=_cparams(1),
        name="merge_out",
    )(o_rw, o_ml, o_pl, gates, gates, gates, w_rw, w_ml, w_pl, w_out, res)


def _pool_kernel(hist_ref, u_ref, w_ref, sc_ref, o_ref, tail_ref, e_ref, *, tl, bt, start):
    li = pl.program_id(1)

    @pl.when(li == 0)
    def _():
        e_ref[:, 0:POOL_HIST, :] = hist_ref[...]

    pos = start + li * tl + lax.broadcasted_iota(jnp.int32, (tl, 1), 0)
    for b in range(bt):
        u = u_ref[b]
        e_ref[b, POOL_HIST:, :] = u
        e = e_ref[b]
        s2 = e + pltpu.roll(e, 1, 0)
        s4 = s2[:, POOL_GW:] + pltpu.roll(s2[:, POOL_GW:], 2, 0)
        s8 = s4[:, POOL_GW:] + pltpu.roll(s4[:, POOL_GW:], 4, 0)
        s16 = s8[:, POOL_GW:] + pltpu.roll(s8[:, POOL_GW:], 8, 0)
        sums = (s2[:, :POOL_GW], s4[:, :POOL_GW], s8[:, :POOL_GW], s16)
        for g, win in enumerate(POOL_WINDOWS):
            cols = slice(g * POOL_GW, (g + 1) * POOL_GW)
            cnt = jnp.minimum(win, pos + 1).astype(F32)
            d = sums[g][POOL_HIST:, :] / cnt - u[:, cols]
            out = _dot(_bf(d), w_ref[g]) * sc_ref[:, cols]
            o_ref[b, :, cols] = out.astype(o_ref.dtype)
        tail_ref[b] = e[tl + POOL_HIST - POOL_TAIL:, :]
        e_ref[b, 0:POOL_HIST, :] = e[tl:, :]


def pool_mix(u_pl, buf, l, w_grp, scale, *, B, L, l_valid, start):
    tl = min(L, 256)
    bt = max(CHUNK_ROWS // L, 1)
    hist = jnp.pad(buf[l], ((0, 0), (POOL_HIST - POOL_BUF, 0), (0, 0)))
    u3 = u_pl.reshape(B, L, u_pl.shape[-1])
    out, tail = pl.pallas_call(
        functools.partial(_pool_kernel, tl=tl, bt=bt, start=start),
        grid=(B // bt, L // tl),
        in_specs=[pl.BlockSpec((bt, POOL_HIST, POOL_WIDTH), lambda b, i: (b, 0, 0)),
                  pl.BlockSpec((bt, tl, POOL_WIDTH), lambda b, i: (b, i, 0)),
                  pl.BlockSpec((4, POOL_GW, POOL_GW), lambda b, i: (0, 0, 0)),
                  pl.BlockSpec((1, POOL_WIDTH), lambda b, i: (0, 0))],
        out_specs=[pl.BlockSpec((bt, tl, POOL_WIDTH), lambda b, i: (b, i, 0)),
                   pl.BlockSpec((bt, POOL_TAIL, POOL_WIDTH), lambda b, i: (b, 0, 0))],
        out_shape=[jax.ShapeDtypeStruct((B, L, POOL_WIDTH), BF16),
                   jax.ShapeDtypeStruct((B, POOL_TAIL, POOL_WIDTH), F32)],
        scratch_shapes=[pltpu.VMEM((bt, tl + POOL_HIST, POOL_WIDTH), F32)],
        compiler_params=_cparams(2),
        name="pool_mix",
    )(hist, u3, w_grp, scale.reshape(1, POOL_WIDTH))
    pad = L - l_valid
    new_buf = tail[:, POOL_TAIL - pad - POOL_BUF:POOL_TAIL - pad, :]
    return out.reshape(B * L, POOL_WIDTH), new_buf


def _mlstm_kernel(*refs, c, bt, l_valid, has_carry):
    u_ref, gif_ref, c0_ref, n0_ref, m0_ref, bi_ref, bf_ref, gnw_ref = refs[:8]
    o_ref, cN_ref, nN_ref, mN_ref, c_s, n_s, m_s = refs[8 + has_carry:]
    ci = pl.program_id(1)

    @pl.when(ci == 0)
    def _():
        c_s[...] = c0_ref[...]
        n_s[...] = n0_ref[...]
        m_s[...] = m0_ref[...]

    N = bt * c
    gif = gif_ref[...].reshape(N, LANES)
    tok = lax.broadcasted_iota(jnp.int32, (N, 1), 0) % c
    valid = tok < l_valid
    ig_all = jnp.where(valid, gif + bi_ref[...], -jnp.inf)
    lf_all = jnp.where(valid, _log_sigmoid(gif + bf_ref[...]), 0.0)
    bcum_all = _dot(_seq_prefix_mask(N, c).astype(F32), lf_all, HI)
    ig_t = ig_all.T
    bcum_t = bcum_all.T
    ri = lax.broadcasted_iota(jnp.int32, (c, c), 0)
    causal = lax.broadcasted_iota(jnp.int32, (c, c), 1) <= ri
    st = []
    for b in range(bt):
        rs = slice(b * c, (b + 1) * c)
        for h in range(ML_HEADS):
            qs = slice(h * ML_DQK, (h + 1) * ML_DQK)
            ks = slice(ML_QK_WIDTH + h * ML_DQK, ML_QK_WIDTH + (h + 1) * ML_DQK)
            vs = slice(2 * ML_QK_WIDTH + h * ML_DV, 2 * ML_QK_WIDTH + (h + 1) * ML_DV)
            os_ = slice(2 * ML_QK_WIDTH + ML_V_WIDTH + h * ML_DV, 2 * ML_QK_WIDTH + ML_V_WIDTH + (h + 1) * ML_DV)
            i_c = ig_all[rs, h:h + 1]
            b_c = bcum_all[rs, ML_HEADS + h:ML_HEADS + h + 1]
            i_r = ig_t[h:h + 1, rs]
            b_r = bcum_t[ML_HEADS + h:ML_HEADS + h + 1, rs]
            m_prev = m_s[b, h]
            dlog = jnp.where(causal, b_c - b_r + i_r, -jnp.inf)
            inter = b_c + m_prev
            m_t = jnp.maximum(inter, jnp.max(dlog, axis=-1, keepdims=True))
            b_last = b_c[c - 1:c, :]
            s_log = b_last - b_c + i_c
            m_new = jnp.maximum(b_last + m_prev, jnp.max(s_log, axis=0, keepdims=True))
            q = u_ref[b, :, qs]
            k = u_ref[b, :, ks] * (ML_DQK ** -0.5)
            st.append(dict(b=b, h=h, hs=slice(h * ML_DV, (h + 1) * ML_DV), os=os_, q=q, k=k, qb=_bf(q), vb=_bf(u_ref[b, :, vs]),
                           dlog=dlog, m_t=m_t, w_prev=jnp.exp(inter - m_t), m_new=m_new,
                           kw=k * jnp.exp(s_log - m_new), wp=jnp.exp(b_last + m_prev - m_new)))
    for s in st:
        s["wts"] = jnp.exp(s["dlog"] - s["m_t"]) * _dot_nt(s["qb"], _bf(s["k"]))
    for s in st:
        b, h = s["b"], s["h"]
        c_prev = c_s[b, h]
        n_prev = n_s[b, h]
        num = s["w_prev"] * _dot(s["qb"], _bf(c_prev)) + _dot(_bf(s["wts"]), s["vb"])
        den = (s["w_prev"] * jnp.sum(s["q"] * n_prev, axis=-1, keepdims=True)
               + jnp.sum(s["wts"], axis=-1, keepdims=True))
        s["hh"] = num / jnp.maximum(jnp.abs(den), jnp.exp(-s["m_t"]))
        c_s[b, h] = s["wp"] * c_prev + _dot_tn(_bf(s["kw"]), s["vb"])
        n_s[b, h] = s["wp"] * n_prev + jnp.sum(s["kw"], axis=0, keepdims=True)
        m_s[b, h] = s["m_new"]
    for s in st:
        hh = s["hh"]
        hn = hh * lax.rsqrt(jnp.mean(hh * hh, axis=-1, keepdims=True) + RMS_EPS) * gnw_ref[:, s["hs"]]
        o_ref[s["b"], :, s["hs"]] =(hn * jax.nn.sigmoid(u_ref[s["b"], :, s["os"]])).astype(o_ref.dtype)

    @pl.when(ci == pl.num_programs(1) - 1)
    def _():
        cN_ref[...] = c_s[...]
        nN_ref[...] = n_s[...]
        mN_ref[...] = m_s[...]


def mlstm_mix(u_ml, u_pl, c0, n0, m0, l, l_out, c_carry, b_i, b_f, gn_w, *, B, L, l_valid):
    c = min(L, CHUNK_ROWS)
    bt = CHUNK_ROWS // c
    u3 = u_ml.reshape(B, L, ML_MAIN)
    g3 = u_pl.reshape(B, L, POOL_WIDTH + LANES)
    zeros = jnp.zeros((LANES - 2 * ML_HEADS,), F32)
    bi = jnp.concatenate([b_i, jnp.zeros((ML_HEADS,), F32), zeros]).reshape(1, LANES)
    bf = jnp.concatenate([jnp.zeros((ML_HEADS,), F32), b_f, zeros]).reshape(1, LANES)
    nl = n0.shape[0]
    c_blk, n_blk, m_blk = (bt, ML_HEADS, ML_DQK, ML_DV), (bt, ML_HEADS, 1, ML_DQK), (bt, ML_HEADS, 1, 1)
    at_b = lambda b, i: (b, 0, 0, 0)
    in_specs = [pl.BlockSpec((bt, c, ML_MAIN), lambda b, i: (b, i, 0)),
                pl.BlockSpec((bt, c, LANES), lambda b, i: (b, i, POOL_WIDTH // LANES)),
                _layer_block(l, c_blk, at_b), _layer_block(l, n_blk, at_b), _layer_block(l, m_blk, at_b),
                pl.BlockSpec((1, LANES), lambda b, i: (0, 0)),
                pl.BlockSpec((1, LANES), lambda b, i: (0, 0)),
                pl.BlockSpec((1, ML_V_WIDTH), lambda b, i: (0, 0))]
    args = [u3, g3, c0, n0.reshape(nl, B, ML_HEADS, 1, ML_DQK), m0.reshape(nl, B, ML_HEADS, 1, 1), bi, bf,
            gn_w.reshape(1, ML_V_WIDTH)]
    aliases = {}
    if c_carry is not None:
        in_specs.append(pl.BlockSpec(memory_space=pl.ANY))
        args.append(c_carry)
        aliases = {len(args) - 1: 1}
    out, cN, nN, mN = pl.pallas_call(
        functools.partial(_mlstm_kernel, c=c, bt=bt, l_valid=l_valid, has_carry=c_carry is not None),
        grid=(B // bt, L // c),
        in_specs=in_specs,
        out_specs=[pl.BlockSpec((bt, c, ML_V_WIDTH), lambda b, i: (b, i, 0)),
                   _layer_block(l_out, c_blk, at_b),
                   pl.BlockSpec(n_blk, at_b), pl.BlockSpec(m_blk, at_b)],
        out_shape=[jax.ShapeDtypeStruct((B, L, ML_V_WIDTH), BF16),
                   jax.ShapeDtypeStruct((DEPTH, B, ML_HEADS, ML_DQK, ML_DV), F32),
                   jax.ShapeDtypeStruct((B, ML_HEADS, 1, ML_DQK), F32),
                   jax.ShapeDtypeStruct((B, ML_HEADS, 1, 1), F32)],
        scratch_shapes=[pltpu.VMEM(c_blk, F32), pltpu.VMEM(n_blk, F32), pltpu.VMEM(m_blk, F32)],
        input_output_aliases=aliases,
        compiler_params=_cparams(2),
        name="mlstm_mix",
    )(*args)
    return (out.reshape(B * L, ML_V_WIDTH), cN, nN.reshape(B, ML_HEADS, ML_DQK), mN.reshape(B, ML_HEADS))


RW_CHAIN_GROUP = 16

def _rwkv_chain_group(chains, seq, s_s, o_ref, rk_ref, gnw_ref, gnb_ref, *, C):
    rowi = lax.broadcasted_iota(jnp.int32, (C, C), 0)
    coli = lax.broadcasted_iota(jnp.int32, (C, C), 1)
    upper = rowi < coli
    col2 = lax.broadcasted_iota(jnp.int32, (C, 2 * C), 1)
    incl2 = jnp.where(col2 >= C, col2 - C, col2) <= lax.broadcasted_iota(jnp.int32, (C, 2 * C), 0)
    n_sq = max(int(math.log2(C)), 1)
    st = []
    for b, h in chains:
        rs = slice(b * C, (b + 1) * C)
        sl = slice(h * RW_HEAD, (h + 1) * RW_HEAD)
        cut = lambda t, rs=rs, sl=sl: t[rs, sl]
        kk_h = cut(seq["kkv"])
        nrm = jnp.sqrt(jnp.sum(kk_h * kk_h, axis=-1, keepdims=True))
        kap = jnp.where(seq["valid"][rs], kk_h / jnp.maximum(nrm, 1e-12), 0.0)
        k_h, v_h, r_h = cut(seq["kmod"]), cut(seq["v"]), cut(seq["r"])
        gi = cut(seq["gi"])
        b_t = kap * cut(seq["a"]) * gi
        k_t = k_h * gi
        st.append(dict(b=b, h=h, sl=sl, k_h=k_h, v_h=v_h, r_h=r_h,
                       a_t=_bf(-kap * cut(seq["gp"])), r_t=_bf(r_h * cut(seq["g"])),
                       bk=jnp.concatenate([b_t, k_t], axis=0),
                       ge=seq["g"][(b + 1) * C - 1:(b + 1) * C, sl], s0=s_s[b, h]))
    for c in st:
        bkb = _bf(c["bk"])
        mt = _dot_nt(bkb, c["a_t"])
        c["pt"] = jnp.where(upper, mt[:C], 0.0)
        c["akt"] = _bf(jnp.where(upper, mt[C:], 0.0))
        c["a_r"] = jnp.where(incl2, _dot_nt(c["r_t"], bkb), 0.0)
        c["s0b"] = _bf(c["s0"])
        c["vb"] = _bf(c["v_h"])
    for c in st:
        c["xt"] = _dot_nt(c["s0b"], c["a_t"]) + _dot_tn(c["vb"], c["akt"])
    for _ in range(n_sq - 1):
        for c in st:
            z = _dot(_bf(jnp.concatenate([c["pt"], c["xt"]], axis=0)), _bf(c["pt"]))
            c["pt"] = z[:C]
            c["xt"] = c["xt"] + z[C:]
    for c in st:
        c["ut"] = _bf(c["xt"] + _dot(_bf(c["xt"]), _bf(c["pt"])))
    for c in st:
        a_r = c["a_r"]
        c["y"] = (_dot_nt(c["r_t"], c["s0b"]) + _dot_nt(_bf(a_r[:, :C]), c["ut"])
                  + _dot(_bf(a_r[:, C:]), c["vb"]))
        bkg = c["bk"] * c["ge"]
        s_s[c["b"], c["h"]] = (c["s0"] * c["ge"] + _dot(c["ut"], _bf(bkg[:C]))
                               + _dot_tn(c["vb"], _bf(bkg[C:])))
    for c in st:
        y, sl = c["y"], c["sl"]
        mean = jnp.mean(y, axis=-1, keepdims=True)
        yc = y - mean
        var = jnp.mean(yc * yc, axis=-1, keepdims=True)
        yn = yc * lax.rsqrt(var + GN_EPS) * gnw_ref[:, sl] + gnb_ref[:, sl]
        bonus = jnp.sum(c["r_h"] * c["k_h"] * rk_ref[:, sl], axis=-1, keepdims=True) * c["v_h"]
        o_ref[c["b"], :, sl] = (yn + bonus).astype(o_ref.dtype)


def _rwkv_kernel(*refs, C, bt, l_valid, has_carry):
    (u_ref, sh_ref, s0_ref, mu_ref, w0_ref, w2_ref, a0_ref, a2_ref, kk_ref, ka_ref, rk_ref, gnw_ref,
     gnb_ref) = refs[:13]
    o_ref, sN_ref, prev_s, s_s = refs[13 + has_carry:]
    ci = pl.program_id(1)

    @pl.when(ci == 0)
    def _():
        prev_s[...] = sh_ref[...]
        s_s[...] = s0_ref[...]

    N = bt * C
    u = u_ref[...].reshape(N, RW_COLS)
    tok = lax.broadcasted_iota(jnp.int32, (N, 1), 0) % C
    valid = tok < l_valid
    prev = jnp.concatenate([jnp.broadcast_to(prev_s[b], (C, RW_COLS)) for b in range(bt)], axis=0)
    u_prev = jnp.where(tok == 0, prev, pltpu.roll(u, 1, 0))
    for b in range(bt):
        prev_s[b] = u[(b + 1) * C - 1:(b + 1) * C, :]
    us = u + (u_prev - u) * mu_ref[...]
    W = RW_WIDTH
    k = us[:, W:2 * W]
    wd = us[:, 3 * W:3 * W + RW_RANK]
    ad = us[:, 3 * W + RW_RANK:3 * W + 2 * RW_RANK]
    xw = w0_ref[...] + _dot(_bf(jnp.tanh(wd)), _bf(w2_ref[...]))
    log_w = -_softplus(-xw) - 0.5
    ld = jnp.where(valid, -jnp.exp(log_w), 0.0)
    a = jax.nn.sigmoid(a0_ref[...] + _dot(_bf(ad), _bf(a2_ref[...])))
    cum = _dot(_seq_prefix_mask(N, C).astype(F32), ld, HI)
    seq = dict(valid=valid, r=us[:, 0:W], v=jnp.where(valid, us[:, 2 * W:3 * W], 0.0), a=a,
               g=jnp.exp(cum), gi=jnp.exp(-cum), gp=jnp.exp(cum - ld), kkv=k * kk_ref[...],
               kmod=jnp.where(valid, k * (1.0 + (a - 1.0) * ka_ref[...]), 0.0))
    chains = [(b, h) for b in range(bt) for h in range(RW_HEADS)]
    group = RW_CHAIN_GROUP * bt
    for i in range(0, len(chains), group):
        _rwkv_chain_group(chains[i:i + group], seq, s_s, o_ref, rk_ref, gnw_ref, gnb_ref, C=C)

    @pl.when(ci == pl.num_programs(1) - 1)
    def _():
        sN_ref[...] = s_s[...]


def rwkv7_mix(u_rw, shift_prev, s_prev, l, l_out, s_carry, mu, w0, w2, a0, a2, k_k, k_a, r_k, gn_w, gn_b, *, B, L,
              l_valid):
    C = min(L, CHUNK_ROWS)
    bt = CHUNK_ROWS // C
    u3 = u_rw.reshape(B, L, RW_COLS)
    vec = lambda n: pl.BlockSpec((1, n), lambda b, i: (0, 0))
    row = lambda t: t.reshape(1, -1)
    s_blk = (bt, RW_HEADS, RW_HEAD, RW_HEAD)
    at_b = lambda b, i: (b, 0, 0, 0)
    in_specs = [pl.BlockSpec((bt, C, RW_COLS), lambda b, i: (b, i, 0)),
                _layer_block(l, (bt, 1, RW_COLS), lambda b, i: (b, 0, 0)),
                _layer_block(l, s_blk, at_b),
                vec(RW_COLS), vec(RW_WIDTH),
                pl.BlockSpec((RW_RANK, RW_WIDTH), lambda b, i: (0, 0)),
                vec(RW_WIDTH),
                pl.BlockSpec((RW_RANK, RW_WIDTH), lambda b, i: (0, 0)),
                vec(RW_WIDTH), vec(RW_WIDTH), vec(RW_WIDTH), vec(RW_WIDTH), vec(RW_WIDTH)]
    args = [u3, shift_prev.reshape(shift_prev.shape[0], B, 1, RW_COLS), s_prev, row(mu), row(w0), w2, row(a0), a2,
            row(k_k), row(k_a), row(r_k), row(gn_w), row(gn_b)]
    aliases = {}
    if s_carry is not None:
        in_specs.append(pl.BlockSpec(memory_space=pl.ANY))
        args.append(s_carry)
        aliases = {len(args) - 1: 1}
    out, sN = pl.pallas_call(
        functools.partial(_rwkv_kernel, C=C, bt=bt, l_valid=l_valid, has_carry=s_carry is not None),
        grid=(B // bt, L // C),
        in_specs=in_specs,
        out_specs=[pl.BlockSpec((bt, C, RW_WIDTH), lambda b, i: (b, i, 0)), _layer_block(l_out, s_blk, at_b)],
        out_shape=[jax.ShapeDtypeStruct((B, L, RW_WIDTH), BF16),
                   jax.ShapeDtypeStruct((DEPTH, B, RW_HEADS, RW_HEAD, RW_HEAD), F32)],
        scratch_shapes=[pltpu.VMEM((bt, 1, RW_COLS), F32), pltpu.VMEM(s_blk, F32)],
        input_output_aliases=aliases,
        compiler_params=_cparams(2),
        name="rwkv7_mix",
    )(*args)
    return out.reshape(B * L, RW_WIDTH), sN


RWS_VECS = 5
RWS_ROWS = 2


def _rwkv_short_kernel(*refs, l_valid, has_carry):
    (ur_ref, uk_ref, uv_ref, ul_ref, shr_ref, shk_ref, shv_ref, shl_ref, mur_ref, muk_ref, muv_ref, mul_ref,
     w0_ref, w2_ref, a0_ref, a2_ref, kk_ref, ka_ref, rk_ref, gnw_ref, gnb_ref, s_ref) = refs[:22]
    o_ref, sN_ref, vec_s, val_s, y_s = refs[22 + has_carry:]
    H2 = LANES // RW_HEAD
    heads = [slice(h * RW_HEAD, (h + 1) * RW_HEAD) for h in range(H2)]
    o_ref[...] = jnp.zeros_like(o_ref)

    def shifted(u_ref, sh_ref, mu_ref, t):
        u = u_ref[:, t, :]
        prev = sh_ref[...] if t == 0 else u_ref[:, t - 1, :]
        return u + (prev - u) * mu_ref[...]

    rows = []
    for t in range(l_valid):
        r = shifted(ur_ref, shr_ref, mur_ref, t)
        k = shifted(uk_ref, shk_ref, muk_ref, t)
        v = shifted(uv_ref, shv_ref, muv_ref, t)
        lo = shifted(ul_ref, shl_ref, mul_ref, t)
        xw = w0_ref[...] + _dot(jnp.tanh(lo[:, :RW_RANK]), w2_ref[...], HI)
        w = jnp.exp(-jnp.exp(-_softplus(-xw) - 0.5))
        a = jax.nn.sigmoid(a0_ref[...] + _dot(lo[:, RW_RANK:], a2_ref[...], HI))
        kkv = k * kk_ref[...]
        kmod = k * (1.0 + (a - 1.0) * ka_ref[...])
        kap = jnp.concatenate(
            [kkv[:, sl] / jnp.maximum(jnp.sqrt(jnp.sum(kkv[:, sl] * kkv[:, sl], axis=-1, keepdims=True)), 1e-12)
             for sl in heads], axis=1)
        for j, x in enumerate((w, kap, kap * a, kmod, r)):
            vec_s[t, j] = x.T.reshape(H2, RW_HEAD, x.shape[0])
        val_s[t] = v.T.reshape(H2, RW_HEAD, v.shape[0])
        rows.append((r, kmod, v))

    for h in range(H2):
        def body(vi, carry, h=h):
            vs = [vi * RWS_ROWS + j for j in range(RWS_ROWS)]
            ss = [s_ref[h, v] for v in vs]
            for t in range(l_valid):
                w, kap, bb, kk, rr = (vec_s[t, j, h] for j in range(RWS_VECS))
                for j, v in enumerate(vs):
                    sa = jnp.sum(ss[j] * kap, axis=0, keepdims=True)
                    ss[j] = ss[j] * w - sa * bb + val_s[t, h, pl.ds(v, 1), :] * kk
                    y_s[t, h, pl.ds(v, 1), :] = jnp.sum(ss[j] * rr, axis=0, keepdims=True)
            for j, v in enumerate(vs):
                sN_ref[h, v] = ss[j]
            return carry
        lax.fori_loop(0, RW_HEAD // RWS_ROWS, body, 0)

    for t in range(l_valid):
        r, kmod, v = rows[t]
        y_all = y_s[t].reshape(LANES, y_s.shape[-1]).T
        outs = []
        for sl in heads:
            y = y_all[:, sl]
            mean = jnp.mean(y, axis=-1, keepdims=True)
            yc = y - mean
            var = jnp.mean(yc * yc, axis=-1, keepdims=True)
            yn = yc * lax.rsqrt(var + GN_EPS) * gnw_ref[:, sl] + gnb_ref[:, sl]
            bonus = jnp.sum(r[:, sl] * kmod[:, sl] * rk_ref[:, sl], axis=-1, keepdims=True) * v[:, sl]
            outs.append(yn + bonus)
        o_ref[:, t, :] = jnp.concatenate(outs, axis=1).astype(o_ref.dtype)


def rwkv7_short(u_rw, shift_prev, s_prev_t, l, l_out, s_carry, mu, w0, w2, a0, a2, k_k, k_a, r_k, gn_w, gn_b, *, B, L,
                l_valid):
    assert B == LANES, "the batch must fill the lane dimension"
    H2 = LANES // RW_HEAD
    nblk = RW_WIDTH // LANES
    u3 = u_rw.reshape(B, L, RW_COLS)
    seg = lambda off: pl.BlockSpec((B, L, LANES), lambda hp, off=off: (0, 0, off + hp))
    lora = pl.BlockSpec((B, L, LANES), lambda hp: (0, 0, 3 * nblk))
    sh = lambda off: pl.BlockSpec((None, B, LANES), lambda hp, off=off: (l, 0, off + hp))
    sh_lora = pl.BlockSpec((None, B, LANES), lambda hp: (l, 0, 3 * nblk))
    mus = lambda off: pl.BlockSpec((1, LANES), lambda hp, off=off: (0, off + hp))
    mu_lora = pl.BlockSpec((1, LANES), lambda hp: (0, 3 * nblk))
    vec = pl.BlockSpec((1, LANES), lambda hp: (0, hp))
    mat = pl.BlockSpec((RW_RANK, LANES), lambda hp: (0, hp))
    s_blk = (H2, RW_HEAD, RW_HEAD, B)
    in_specs = [seg(0), seg(nblk), seg(2 * nblk), lora, sh(0), sh(nblk), sh(2 * nblk), sh_lora,
                mus(0), mus(nblk), mus(2 * nblk), mu_lora,
                vec, mat, vec, mat, vec, vec, vec, vec, vec,
                pl.BlockSpec((None,) + s_blk, lambda hp: (l, hp, 0, 0, 0))]
    row = lambda t: t.reshape(1, -1)
    args = [u3, u3, u3, u3, shift_prev, shift_prev, shift_prev, shift_prev, row(mu), row(mu), row(mu), row(mu),
            row(w0), w2, row(a0), a2, row(k_k), row(k_a), row(r_k), row(gn_w), row(gn_b), s_prev_t]
    aliases = {}
    if s_carry is not None:
        in_specs.append(pl.BlockSpec(memory_space=pl.ANY))
        args.append(s_carry)
        aliases = {len(args) - 1: 1}
    out, sN = pl.pallas_call(
        functools.partial(_rwkv_short_kernel, l_valid=l_valid, has_carry=s_carry is not None),
        grid=(RW_HEADS // H2,),
        in_specs=in_specs,
        out_specs=[pl.BlockSpec((B, L, LANES), lambda hp: (0, 0, hp)),
                   pl.BlockSpec((None,) + s_blk, lambda hp: (l_out, hp, 0, 0, 0))],
        out_shape=[jax.ShapeDtypeStruct((B, L, RW_WIDTH), BF16),
                   jax.ShapeDtypeStruct((DEPTH, RW_HEADS, RW_HEAD, RW_HEAD, B), F32)],
        scratch_shapes=[pltpu.VMEM((l_valid, RWS_VECS, H2, RW_HEAD, B), F32),
                        pltpu.VMEM((l_valid, H2, RW_HEAD, B), F32),
                        pltpu.VMEM((l_valid, H2, RW_HEAD, B), F32)],
        input_output_aliases=aliases,
        compiler_params=_cparams(1),
        name="rwkv7_short",
    )(*args)
    return out.reshape(B * L, RW_WIDTH), sN


ATTN_SEQS_PER_STEP = 8


def _attn_kernel(q_ref, kv_ref, o_ref):
    heads = [slice(h * XA_HEAD_DIM, (h + 1) * XA_HEAD_DIM) for h in range(XA_HEADS)]
    s = [_dot_nt(q_ref[0, :, hs], _bf(kv_ref[0, :, hs])) * (XA_HEAD_DIM ** -0.5) for hs in heads]
    e = [jnp.exp(t - jnp.max(t, axis=-1, keepdims=True)) for t in s]
    p = [t / jnp.sum(t, axis=-1, keepdims=True) for t in e]
    for h, hs in enumerate(heads):
        vs = slice(XA_WIDTH + h * XA_HEAD_DIM, XA_WIDTH + (h + 1) * XA_HEAD_DIM)
        o_ref[0, :, hs] = _dot(_bf(p[h]), _bf(kv_ref[0, :, vs])).astype(o_ref.dtype)


def _attn_short_kernel(q_ref, k_ref, v_ref, o_ref, *, bt):
    L = q_ref.shape[1]
    rows, cols = XA_HEADS * L, MEM_LEN * XA_HEADS
    own = (lax.broadcasted_iota(jnp.int32, (rows, cols), 0) // L
           == lax.broadcasted_iota(jnp.int32, (rows, cols), 1) % XA_HEADS)
    heads = [slice(h * XA_HEAD_DIM, (h + 1) * XA_HEAD_DIM) for h in range(XA_HEADS)]
    qs = [jnp.concatenate([q_ref[b, :, hs] for hs in heads], axis=0) for b in range(bt)]
    s = [_dot_nt(qs[b], _bf(k_ref[b].reshape(cols, XA_HEAD_DIM))) * (XA_HEAD_DIM ** -0.5) for b in range(bt)]
    s = [jnp.where(own, t, -jnp.inf) for t in s]
    e = [jnp.exp(t - jnp.max(t, axis=-1, keepdims=True)) for t in s]
    p = [t / jnp.sum(t, axis=-1, keepdims=True) for t in e]
    for b in range(bt):
        out = _dot(_bf(p[b]), _bf(v_ref[b].reshape(cols, XA_HEAD_DIM)))
        for h, hs in enumerate(heads):
            o_ref[b, :, hs] = out[h * L:(h + 1) * L].astype(o_ref.dtype)


def cross_attention(q, mem, l, *, B, L):
    q3 = q.reshape(B, L, XA_WIDTH)
    if l is None:
        bt, tq, body, mems = 1, min(L, 512), _attn_kernel, [mem]
        mem_specs = [pl.BlockSpec((bt, MEM_LEN, 2 * XA_WIDTH), lambda b, i: (b, 0, 0))]
    else:
        bt, tq, mems = ATTN_SEQS_PER_STEP, L, list(mem)
        body = functools.partial(_attn_short_kernel, bt=bt)
        mem_specs = [pl.BlockSpec((None, bt, MEM_LEN, XA_HEADS, XA_HEAD_DIM), lambda b, i: (l, b, 0, 0, 0))] * 2
    qo_spec = pl.BlockSpec((bt, tq, XA_WIDTH), lambda b, i: (b, i, 0))
    out = pl.pallas_call(
        body,
        grid=(B // bt, L // tq),
        in_specs=[qo_spec] + mem_specs,
        out_specs=qo_spec,
        out_shape=jax.ShapeDtypeStruct((B, L, XA_WIDTH), BF16),
        compiler_params=_cparams(2),
        name="cross_attention",
    )(q3, *mems)
    return out.reshape(B * L, XA_WIDTH)


ROUTER_GROUP_LANE = MOE_EXPERTS


MOE_TB = 512
MOE_TB_GATHER = 1024
MOE_SUB = 256
MOE_TE_LONG = 1024
MOE_TE_SHORT = 512
ROUTE_GROUP_LANE = 0
ROUTE_RANK_LANE = 1
COMB_PIECES = 3


def _router_kernel(x_ref, g_ref, w_ref, b_ref, xn_ref, comb_ref, route_ref, cnt_s):
    @pl.when(pl.program_id(0) == 0)
    def _():
        cnt_s[...] = jnp.zeros_like(cnt_s)

    xn = _rms(x_ref[...], g_ref[...])
    xn_ref[...] = _bf(xn)
    z = _dot(xn, w_ref[...], HI) + b_ref[...]
    tm = z.shape[0]
    lane = lax.broadcasted_iota(jnp.int32, z.shape, 1).astype(F32)
    big = float(LANES)
    neg = -jnp.inf
    first = lambda mask: jnp.min(jnp.where(mask, lane, big), axis=-1, keepdims=True)
    is_g = jnp.logical_and(lane >= ROUTER_GROUP_LANE, lane < ROUTER_GROUP_LANE + MOE_GROUPS)
    zg = jnp.where(is_g, z, neg)
    mg = jnp.max(zg, axis=-1, keepdims=True)
    grp = first(zg == mg) - ROUTER_GROUP_LANE
    p_grp = 1.0 / jnp.sum(jnp.exp(zg - mg), axis=-1, keepdims=True)
    lo = grp * MOE_PER_GROUP
    ze = jnp.where(jnp.logical_and(lane >= lo, lane < lo + MOE_PER_GROUP), z, neg)
    t1 = jnp.max(ze, axis=-1, keepdims=True)
    i1 = first(ze == t1)
    ze2 = jnp.where(lane == i1, neg, ze)
    t2 = jnp.max(ze2, axis=-1, keepdims=True)
    i2 = first(ze2 == t2)
    e2 = jnp.exp(t2 - t1)
    g1 = p_grp / (1.0 + e2)
    comb = jnp.where(lane == i1, g1, 0.0) + jnp.where(lane == i2, g1 * e2, 0.0)
    comb_ref[...] = _split3(comb)
    onehot = jnp.where(lane == grp, 1.0, 0.0)
    r = lax.broadcasted_iota(jnp.int32, (tm, tm), 0)
    c = lax.broadcasted_iota(jnp.int32, (tm, tm), 1)
    before = _dot(_bf(jnp.where(c < r, 1.0, 0.0)), _bf(onehot)) + cnt_s[...]
    rank = jnp.sum(onehot * before, axis=-1, keepdims=True)
    cnt_s[...] = cnt_s[...] + jnp.sum(onehot, axis=0, keepdims=True)
    route_ref[...] = jnp.where(lane == ROUTE_GROUP_LANE, grp, 0.0) + jnp.where(lane == ROUTE_RANK_LANE, rank, 0.0)


def moe_router(x, g, w_r1, b_r1, w_r2, b_r2):
    T, D = x.shape
    tm = MOE_TB
    pad = LANES - MOE_EXPERTS - MOE_GROUPS
    w = jnp.concatenate([w_r2, w_r1, jnp.zeros((D, pad), F32)], axis=1)
    b = jnp.concatenate([b_r2, b_r1, jnp.zeros((pad,), F32)]).reshape(1, LANES)
    return pl.pallas_call(
        _router_kernel,
        grid=(T // tm,),
        in_specs=[pl.BlockSpec((tm, D), lambda i: (i, 0)),
                  pl.BlockSpec((1, D), lambda i: (0, 0)),
                  pl.BlockSpec((D, LANES), lambda i: (0, 0)),
                  pl.BlockSpec((1, LANES), lambda i: (0, 0))],
        out_specs=[pl.BlockSpec((tm, D), lambda i: (i, 0)),
                   pl.BlockSpec((tm, COMB_PIECES * LANES), lambda i: (i, 0)),
                   pl.BlockSpec((tm, LANES), lambda i: (i, 0))],
        out_shape=[jax.ShapeDtypeStruct((T, D), BF16), jax.ShapeDtypeStruct((T, COMB_PIECES * LANES), BF16),
                   jax.ShapeDtypeStruct((T, LANES), F32)],
        scratch_shapes=[pltpu.VMEM((1, LANES), F32)],
        compiler_params=_cparams(1),
        name="moe_router",
    )(x, g.reshape(1, D), w, b)


def _moe_plan(route, T, te):
    i32 = jnp.int32
    grp = route[:, ROUTE_GROUP_LANE].astype(i32)
    rank = route[:, ROUTE_RANK_LANE].astype(i32)
    rows = T + MOE_GROUPS * te
    n_sub = rows // MOE_SUB
    n_tiles = rows // te
    onehot = (grp[:, None] == jnp.arange(MOE_GROUPS, dtype=i32)[None]).astype(i32)
    seg_rows = (onehot.sum(axis=0) + te - 1) // te * te
    seg_end = jnp.cumsum(seg_rows)
    seg_start = seg_end - seg_rows
    dest = seg_start[grp] + rank
    group_of = lambda row: jnp.minimum(jnp.sum(row[:, None] >= seg_end[None], axis=1), MOE_GROUPS - 1).astype(i32)
    sub_row = jnp.arange(n_sub, dtype=i32) * MOE_SUB
    sub_g = group_of(sub_row)
    sub_valid = sub_row < seg_end[-1]
    r0 = sub_row - seg_start[sub_g]

    def overlap(tb):
        blk_cnt = onehot.reshape(T // tb, tb, MOE_GROUPS).sum(axis=1)
        cum_blk = jnp.concatenate([jnp.zeros((1, MOE_GROUPS), i32), jnp.cumsum(blk_cnt, axis=0)])
        lo = cum_blk[:-1][:, sub_g].T
        hi = cum_blk[1:][:, sub_g].T
        return (lo < (r0 + MOE_SUB)[:, None]) & (hi > r0[:, None]) & sub_valid[:, None]

    g_mask = overlap(MOE_TB_GATHER)
    first_col = (jnp.arange(T // MOE_TB_GATHER) == 0)[None]
    g_mask = g_mask | (first_col & ~g_mask.any(axis=1, keepdims=True))

    def items(mask, ncol, nblk):
        n_items = n_sub + MOE_GROUPS * nblk
        flat = jnp.nonzero(mask.reshape(-1), size=n_items, fill_value=-1)[0].astype(i32)
        valid = flat >= 0
        flat = jnp.where(valid, flat, jnp.max(flat))
        major, minor = flat // ncol, flat % ncol
        prev = jnp.concatenate([jnp.full((1,), -1, i32), major[:-1]])
        nxt = jnp.concatenate([major[1:], jnp.full((1,), -1, i32)])
        nvalid = jnp.concatenate([valid[1:], jnp.zeros((1,), bool)])
        first = (major != prev) & valid
        last = ((major != nxt) | ~nvalid) & valid
        return major, minor, first.astype(i32), last.astype(i32), valid.astype(i32)

    g_sub, g_blk, g_first, _, g_valid = items(g_mask, T // MOE_TB_GATHER, T // MOE_TB_GATHER)
    s_blk, s_sub, s_first, s_last, s_valid = items(overlap(MOE_TB).T, n_sub, T // MOE_TB)
    tile_row = jnp.arange(n_tiles, dtype=i32) * te
    tile_valid = tile_row < seg_end[-1]
    tile_group = group_of(jnp.where(tile_valid, tile_row, seg_end[-1] - 1))
    return dict(dest=dest, rows=rows, gather=(g_sub, g_blk, g_first, g_valid),
                scatter=(s_sub, s_blk, s_first, s_last, s_valid),
                tile_group=tile_group, tile_valid=tile_valid.astype(i32))


def _moe_gather_kernel(sub_ref, blk_ref, first_ref, valid_ref, dest_ref, xn_ref, comb_ref, xs_ref, cs_ref):
    w = pl.program_id(0)

    @pl.when(valid_ref[w] == 1)
    def _():
        rows = sub_ref[w] * MOE_SUB + lax.broadcasted_iota(jnp.int32, (MOE_SUB, MOE_TB_GATHER), 0)
        hit = _bf(jnp.where(dest_ref[0] == rows, 1.0, 0.0))
        gx = _bf(_dot(hit, xn_ref[...]))
        g3 = _dot(hit, comb_ref[...])
        gc = g3[:, :LANES] + g3[:, LANES:2 * LANES] + g3[:, 2 * LANES:]

        @pl.when(first_ref[w] == 1)
        def _():
            xs_ref[...] = gx
            cs_ref[...] = gc

        @pl.when(first_ref[w] == 0)
        def _():
            xs_ref[...] = xs_ref[...] + gx
            cs_ref[...] = cs_ref[...] + gc


def moe_gather(xn, comb, plan):
    T, D = xn.shape
    g_sub, g_blk, g_first, g_valid = plan["gather"]
    rows = plan["rows"]
    tb = MOE_TB_GATHER
    dest3 = plan["dest"].reshape(T // tb, 1, tb)
    grid_spec = pltpu.PrefetchScalarGridSpec(
        num_scalar_prefetch=4,
        grid=(g_sub.shape[0],),
        in_specs=[pl.BlockSpec((1, 1, tb), lambda w, s, b, f, v: (b[w], 0, 0)),
                  pl.BlockSpec((tb, D), lambda w, s, b, f, v: (b[w], 0)),
                  pl.BlockSpec((tb, COMB_PIECES * LANES), lambda w, s, b, f, v: (b[w], 0))],
        out_specs=[pl.BlockSpec((MOE_SUB, D), lambda w, s, b, f, v: (s[w], 0)),
                   pl.BlockSpec((MOE_SUB, LANES), lambda w, s, b, f, v: (s[w], 0))])
    return pl.pallas_call(
        _moe_gather_kernel,
        grid_spec=grid_spec,
        out_shape=[jax.ShapeDtypeStruct((rows, D), BF16), jax.ShapeDtypeStruct((rows, LANES), F32)],
        compiler_params=_cparams(1),
        name="moe_gather",
    )(g_sub, g_blk, g_first, g_valid, dest3, xn, comb)


MOE_EXPERTS_PER_STEP = 2


def _moe_group_experts_kernel(tg_ref, tv_ref, xs_ref, cs_ref, wg_ref, wu_ref, wd_ref, y_ref, acc):
    i = pl.program_id(0)
    e = pl.program_id(1)

    @pl.when(e == 0)
    def _():
        acc[...] = jnp.zeros_like(acc)

    @pl.when(tv_ref[i] == 1)
    def _():
        xs = xs_ref[...]
        cs = cs_ref[...]
        lane = lax.broadcasted_iota(jnp.int32, cs.shape, 1)
        first = tg_ref[i] * MOE_PER_GROUP + e * MOE_EXPERTS_PER_STEP
        hids = []
        for j in range(MOE_EXPERTS_PER_STEP):
            ce = jnp.sum(jnp.where(lane == first + j, cs, 0.0), axis=-1, keepdims=True)
            hids.append(_bf(jax.nn.silu(_dot(xs, _bf(wg_ref[j]))) * _dot(xs, _bf(wu_ref[j])) * ce))
        hid = jnp.concatenate(hids, axis=1)
        acc[...] += _dot(hid, _bf(wd_ref[...].reshape(MOE_EXPERTS_PER_STEP * MOE_HIDDEN, wd_ref.shape[-1])))

    @pl.when(e == pl.num_programs(1) - 1)
    def _():
        y_ref[...] = _bf(acc[...])


def moe_group_experts(xs, cs, plan, wg, wu, wd, l, te):
    rows, D = xs.shape
    eps = MOE_EXPERTS_PER_STEP
    steps = MOE_PER_GROUP // eps

    def expert(i, e, tg, tv):
        return (l, tg[i] * steps + jnp.where(tv[i] == 1, e, steps - 1), 0, 0)

    w_in = pl.BlockSpec((None, eps, D, MOE_HIDDEN), expert)
    grid_spec = pltpu.PrefetchScalarGridSpec(
        num_scalar_prefetch=2,
        grid=(rows // te, steps),
        in_specs=[pl.BlockSpec((te, D), lambda i, e, tg, tv: (i, 0)),
                  pl.BlockSpec((te, LANES), lambda i, e, tg, tv: (i, 0)),
                  w_in, w_in,
                  pl.BlockSpec((None, eps, MOE_HIDDEN, D), expert)],
        out_specs=pl.BlockSpec((te, D), lambda i, e, tg, tv: (i, 0)),
        scratch_shapes=[pltpu.VMEM((te, D), F32)])
    return pl.pallas_call(
        _moe_group_experts_kernel,
        grid_spec=grid_spec,
        out_shape=jax.ShapeDtypeStruct((rows, D), BF16),
        compiler_params=_cparams(2),
        name="moe_group_experts",
    )(plan["tile_group"], plan["tile_valid"], xs, cs, wg, wu, wd)


def _moe_scatter_kernel(sub_ref, blk_ref, first_ref, last_ref, valid_ref, dest_ref, y_ref, x_ref, gf_ref, o_ref,
                        *, final_norm):
    w = pl.program_id(0)

    @pl.when(valid_ref[w] == 1)
    def _():
        cols = sub_ref[w] * MOE_SUB + lax.broadcasted_iota(jnp.int32, (MOE_TB, MOE_SUB), 1)
        pick = _bf(jnp.where(dest_ref[...] == cols, 1.0, 0.0))
        upd = _dot(pick, y_ref[...])

        @pl.when(first_ref[w] == 1)
        def _():
            o_ref[...] = x_ref[...] + upd

        @pl.when(first_ref[w] == 0)
        def _():
            o_ref[...] += upd

        if final_norm:
            @pl.when(last_ref[w] == 1)
            def _():
                o_ref[...] = _rms(o_ref[...], gf_ref[...])


def moe_scatter(ys, x, plan, g_final, *, final_norm):
    T, D = x.shape
    s_sub, s_blk, s_first, s_last, s_valid = plan["scatter"]
    dest_col = plan["dest"].reshape(T, 1)
    im = lambda f: (lambda w, s, b, fi, la, v: f(w, s, b))
    grid_spec = pltpu.PrefetchScalarGridSpec(
        num_scalar_prefetch=5,
        grid=(s_sub.shape[0],),
        in_specs=[pl.BlockSpec((MOE_TB, 1), im(lambda w, s, b: (b[w], 0))),
                  pl.BlockSpec((MOE_SUB, D), im(lambda w, s, b: (s[w], 0))),
                  pl.BlockSpec((MOE_TB, D), im(lambda w, s, b: (b[w], 0))),
                  pl.BlockSpec((1, D), im(lambda w, s, b: (0, 0)))],
        out_specs=pl.BlockSpec((MOE_TB, D), im(lambda w, s, b: (b[w], 0))))
    return pl.pallas_call(
        functools.partial(_moe_scatter_kernel, final_norm=final_norm),
        grid_spec=grid_spec,
        out_shape=jax.ShapeDtypeStruct((T, D), F32),
        compiler_params=_cparams(1),
        name="moe_scatter",
    )(s_sub, s_blk, s_first, s_last, s_valid, dest_col, ys, x, g_final.reshape(1, D))


def hmoe_block(x, p, l, *, final_norm):
    T = x.shape[0]
    te = MOE_TE_LONG if T >= 4 * MOE_TE_LONG else MOE_TE_SHORT
    xn, comb, route = moe_router(x, p["g_moe"][l], p["moe_wr1"][l], p["moe_br1"][l], p["moe_wr2"][l],
                                 p["moe_br2"][l])
    plan = _moe_plan(route, T, te)
    xs, cs = moe_gather(xn, comb, plan)
    ys = moe_group_experts(xs, cs, plan, p["moe_wg"], p["moe_wu"], p["moe_wd"], l, te)
    return moe_scatter(ys, x, plan, p["g_final"], final_norm=final_norm)


def _layer_weights(l, p):
    w_in = p["w_in"][l]
    c0 = RW_COLS
    c1 = c0 + ML_MAIN
    c2 = c1 + 2 * ML_HEADS
    c3 = c2 + POOL_WIDTH
    w_pl = jnp.concatenate([w_in[:, c2:c3], w_in[:, c1:c2], jnp.zeros((D_MODEL, LANES - 2 * ML_HEADS), F32)], axis=1)
    return dict(
        w_rw=_bf(w_in[:, :c0]), w_ml=_bf(w_in[:, c0:c1]), w_pl=_bf(w_pl), w_gate=_bf(w_in[:, c3:]),
        w_up_rwkv=_bf(p["w_up_rwkv"][l]), w_up_mlstm=_bf(p["w_up_mlstm"][l]), w_up_pool=_bf(p["w_up_pool"][l]),
        w_out=_bf(p["w_out"][l]), pool_w=_bf(p["pool_w"][l]),
        xa_wq=_bf(p["xa_wq"][l]), xa_wo=_bf(p["xa_wo"][l]),
        xa_wkv=_bf(jnp.concatenate([p["xa_wk"][l], p["xa_wv"][l]], axis=1)),
    )


def _trunk_layer(x, mem, mem_l, st, carry, *, l, sl, p, lw, B, L, l_valid, start, final_norm):
    g_mix = p["g_mix"][l]
    u_rw = norm_matmul(x, g_mix, lw["w_rw"], tn=TN_RWKV, tm=TM_WHOLE_SEGMENT, name="in_rwkv")
    u_ml = norm_matmul(x, g_mix, lw["w_ml"], tn=TN_MLSTM, tm=TM_WHOLE_SEGMENT, name="in_mlstm")
    u_pl = norm_matmul(x, g_mix, lw["w_pl"], tn=POOL_WIDTH + LANES, name="in_pool")
    gates = norm_matmul(x, g_mix, lw["w_gate"], tn=TN_GATES, tm=TM_WHOLE_SEGMENT, out_dtype=BF16, act="sigmoid",
                        name="in_gates")

    rwkv = rwkv7_short if L < CHUNK_ROWS else rwkv7_mix
    o_rw, rw_s = rwkv(u_rw, st["rw_shift"], st["rw_s"], sl, l, carry["rw_s"], p["rw_mu"][l], p["rw_w0"][l],
                      p["rw_w2"][l], p["rw_a0"][l], p["rw_a2"][l], p["rw_k_k"][l], p["rw_k_a"][l],
                      p["rw_r_k"][l], p["rw_gn_w"][l], p["rw_gn_b"][l], B=B, L=L, l_valid=l_valid)
    rw_shift = u_rw.reshape(B, L, RW_COLS)[:, l_valid - 1]
    o_ml, ml_c, ml_n, ml_m = mlstm_mix(u_ml, u_pl, st["ml_c"], st["ml_n"], st["ml_m"], sl, l, carry["ml_c"],
                                       p["ml_b_i"][l], p["ml_b_f"][l], p["ml_gn_w"][l], B=B, L=L, l_valid=l_valid)
    o_pl, pool_buf = pool_mix(u_pl, st["pool"], sl, lw["pool_w"], p["pool_scale"][l], B=B, L=L, l_valid=l_valid,
                              start=start)
    x = merge_out(o_rw, o_ml, o_pl, gates, lw["w_up_rwkv"], lw["w_up_mlstm"], lw["w_up_pool"], lw["w_out"], x)

    q = norm_matmul(x, p["g_xa"][l], lw["xa_wq"], tn=XA_WIDTH, out_dtype=BF16, name="xa_q")
    att = cross_attention(q, mem, mem_l, B=B, L=L)
    x = matmul_residual(att, lw["xa_wo"], x)

    x = hmoe_block(x, p, l, final_norm=final_norm)
    return x, dict(rw_s=rw_s, ml_c=ml_c), (rw_shift, ml_n, ml_m, pool_buf)


def kernel(x_prompt, x_sample, cache_mem_k, cache_mem_v, state_rwkv_s, state_rwkv_shift, state_mlstm_c, state_mlstm_n, state_mlstm_m, state_pool, mem_prompt, g_mix, w_in, rw_mu, rw_w0, rw_w2, rw_a0, rw_a2, rw_k_k, rw_k_a, rw_r_k, rw_gn_w, rw_gn_b, ml_b_i, ml_b_f, ml_gn_w, pool_w, pool_scale, w_up_rwkv, w_up_mlstm, w_up_pool, w_out, g_xa, g_mem, xa_wq, xa_wk, xa_wv, xa_wo, g_moe, moe_wr1, moe_br1, moe_wr2, moe_br2, moe_wg, moe_wu, moe_wd, g_final):
    p = dict(g_mix=g_mix, w_in=w_in, rw_mu=rw_mu, rw_w0=rw_w0, rw_w2=rw_w2, rw_a0=rw_a0, rw_a2=rw_a2, rw_k_k=rw_k_k,
             rw_k_a=rw_k_a, rw_r_k=rw_r_k, rw_gn_w=rw_gn_w, rw_gn_b=rw_gn_b, ml_b_i=ml_b_i, ml_b_f=ml_b_f,
             ml_gn_w=ml_gn_w, pool_w=pool_w, pool_scale=pool_scale, w_up_rwkv=w_up_rwkv, w_up_mlstm=w_up_mlstm,
             w_up_pool=w_up_pool, w_out=w_out, g_xa=g_xa, g_mem=g_mem, xa_wq=xa_wq, xa_wk=xa_wk, xa_wv=xa_wv,
             xa_wo=xa_wo, g_moe=g_moe, moe_wr1=moe_wr1, moe_br1=moe_br1, moe_wr2=moe_wr2, moe_br2=moe_br2,
             moe_wg=moe_wg, moe_wu=moe_wu, moe_wd=moe_wd, g_final=g_final)
    Bp, Lp, D = x_prompt.shape
    Bs, Ls, _ = x_sample.shape
    Ls_pad = -(-Ls // SUBLANES) * SUBLANES
    yp = x_prompt.reshape(Bp * Lp, D)
    ys = jnp.pad(x_sample, ((0, 0), (0, Ls_pad - Ls), (0, 0))).reshape(Bs * Ls_pad, D)
    zeros = lambda *s: jnp.zeros((1,) + s, F32)
    st_p = dict(rw_shift=zeros(Bp, RW_COLS), rw_s=zeros(Bp, RW_HEADS, RW_HEAD, RW_HEAD),
                ml_c=zeros(Bp, ML_HEADS, ML_DQK, ML_DV), ml_n=zeros(Bp, ML_HEADS, ML_DQK), ml_m=zeros(Bp, ML_HEADS),
                pool=zeros(Bp, POOL_BUF, POOL_WIDTH))
    batch_minor = (0, 2, 3, 4, 1)
    st_s = dict(rw_shift=state_rwkv_shift, rw_s=jnp.transpose(state_rwkv_s, batch_minor), ml_c=state_mlstm_c,
                ml_n=state_mlstm_n, ml_m=state_mlstm_m, pool=state_pool)
    carry_p = carry_s = dict(rw_s=None, ml_c=None)
    small_p = [[] for _ in range(6)]
    small_s = [[] for _ in range(4)]
    for l in range(DEPTH):
        lw = _layer_weights(l, p)
        final = l == DEPTH - 1
        kv = norm_matmul(mem_prompt.reshape(Bp * MEM_LEN, D), g_mem[l], lw["xa_wkv"], tn=2 * XA_WIDTH,
                         name="memory_kv")
        kv3 = kv.reshape(Bp, MEM_LEN, 2 * XA_WIDTH)
        yp, carry_p, small = _trunk_layer(yp, kv3, None, st_p, carry_p, l=l, sl=0, p=p, lw=lw, B=Bp, L=Lp,
                                          l_valid=Lp, start=0, final_norm=final)
        mk = kv[:, :XA_WIDTH].reshape(Bp, MEM_LEN, XA_HEADS, XA_HEAD_DIM)
        mv = kv[:, XA_WIDTH:].reshape(Bp, MEM_LEN, XA_HEADS, XA_HEAD_DIM)
        for acc, t in zip(small_p, small + (mk, mv)):
            acc.append(t)
        ys, carry_s, small = _trunk_layer(ys, (cache_mem_k, cache_mem_v), l, st_s, carry_s, l=l, sl=l, p=p, lw=lw,
                                          B=Bs, L=Ls_pad, l_valid=Ls, start=PAST_LEN, final_norm=final)
        for acc, t in zip(small_s, small):
            acc.append(t)
    y_prompt = yp.reshape(Bp, Lp, D)
    y_sample = ys.reshape(Bs, Ls_pad, D)[:, :Ls]
    p_sh, p_n, p_m, p_pool, p_mk, p_mv = [jnp.stack(t) for t in small_p]
    s_sh, s_n, s_m, s_pool = [jnp.stack(t) for t in small_s]
    return (y_prompt, y_sample, carry_p["rw_s"], p_sh, carry_p["ml_c"], p_n, p_m, p_pool, p_mk, p_mv,
            jnp.transpose(carry_s["rw_s"], (0, 4, 1, 2, 3)), s_sh, carry_s["ml_c"], s_n, s_m, s_pool)
```
